```python
import jax, jax.numpy as jnp
from jax import lax
import numpy as np

D_MODEL = 1024
BATCH = 4
SEQ = 4096
DEPTH = 1
DEC_BATCH = 32
DEC_SEQ = 32
PAST_LEN = 2048

CHUNK = 64
GDN_HEADS = 4
GDN_DK = 128
GDN_DV = 128
GDN_CONV = 4
ATT_HEADS = 8
ATT_DH = 64
BAND_CHUNKS = 8
WINDOW = BAND_CHUNKS * CHUNK
MAX_REL = 128
D_FF = 2816
FFN_CONV = 3
EPS = 1e-6

GDN_QKV = GDN_HEADS * (2 * GDN_DK + GDN_DV)
GDN_Z = GDN_HEADS * GDN_DV
ATT_W = ATT_HEADS * ATT_DH
MIX_WIDTH = GDN_HEADS * GDN_DV + ATT_W
IN_COLS = GDN_QKV + GDN_Z + 2 * GDN_HEADS + 3 * ATT_W

kernel_name = 'hybrid_gdn_bandattn_convffn_step'


def rmsnorm(x, w):
    xf = x.astype(jnp.float32)
    y = xf * lax.rsqrt(jnp.mean(xf * xf, axis=-1, keepdims=True) + EPS)
    return (y * w.astype(jnp.float32)).astype(x.dtype)


def l2norm(x):
    xf = x.astype(jnp.float32)
    return xf * lax.rsqrt(jnp.sum(xf * xf, axis=-1, keepdims=True) + EPS)


def causal_dwconv(x, prev, w):
    width = w.shape[0]
    t = x.shape[1]
    xp = jnp.concatenate([prev.astype(x.dtype), x], axis=1)
    out = sum(xp[:, i:i + t] * w[i].astype(x.dtype) for i in range(width))
    return out, xp[:, xp.shape[1] - (width - 1):]


def gated_delta_chunked(q, k, v, g, beta, s0, chunk):
    B, T, H, DK = q.shape
    DV = v.shape[-1]
    N = T // chunk
    f32 = jnp.float32

    def blocks(a):
        a = a.astype(f32).reshape((B, N, chunk) + a.shape[2:])
        return jnp.moveaxis(a, (1, 3), (0, 2))

    q = blocks(q) * (DK ** -0.5)
    k = blocks(k)
    v = blocks(v)
    g = blocks(g)
    beta = blocks(beta)
    G = jnp.cumsum(g, axis=-1)
    incl = jnp.tril(jnp.ones((chunk, chunk), bool))
    strict = jnp.tril(jnp.ones((chunk, chunk), bool), k=-1)
    diff = G[..., :, None] - G[..., None, :]
    gam = jnp.where(incl, jnp.exp(jnp.where(incl, diff, 0.0)), 0.0)
    kb = k * beta[..., None]
    a_mat = jnp.where(strict, jnp.einsum('nbhid,nbhjd->nbhij', kb, k) * gam, 0.0)
    eye = jnp.eye(chunk, dtype=f32)
    rhs = jnp.concatenate([v * beta[..., None], kb * jnp.exp(G)[..., None]], axis=-1)
    sol = lax.linalg.triangular_solve(eye + a_mat, rhs, left_side=True, lower=True, unit_diagonal=True)
    u_blk, w_blk = sol[..., :DV], sol[..., DV:]
    qk = jnp.where(incl, jnp.einsum('nbhid,nbhjd->nbhij', q, k) * gam, 0.0)
    qg = q * jnp.exp(G)[..., None]
    kg = k * jnp.exp(G[..., -1:] - G)[..., None]
    decay_last = jnp.exp(G[..., -1])

    def step(S, xs):
        qg_c, kg_c, u_c, w_c, qk_c, dl = xs
        v_new = u_c - jnp.einsum('bhck,bhkv->bhcv', w_c, S)
        o = jnp.einsum('bhck,bhkv->bhcv', qg_c, S) + jnp.einsum('bhij,bhjv->bhiv', qk_c, v_new)
        S = S * dl[..., None, None] + jnp.einsum('bhck,bhcv->bhkv', kg_c, v_new)
        return S, o

    S, o = lax.scan(step, s0.astype(f32), (qg, kg, u_blk, w_blk, qk, decay_last))
    o = jnp.moveaxis(o, (0, 2), (1, 3)).reshape(B, T, H, DV)
    return o, S


def gated_deltanet(qkv, z, b_raw, a_raw, conv_prev, s0, conv_w, a_log, dt_bias, norm_w, chunk):
    B, T, _ = qkv.shape
    qkv, conv_state = causal_dwconv(qkv, conv_prev, conv_w)
    qkv = jax.nn.silu(qkv)
    q, k, v = jnp.split(qkv, [GDN_HEADS * GDN_DK, 2 * GDN_HEADS * GDN_DK], axis=-1)
    q = l2norm(q.reshape(B, T, GDN_HEADS, GDN_DK))
    k = l2norm(k.reshape(B, T, GDN_HEADS, GDN_DK))
    v = v.reshape(B, T, GDN_HEADS, GDN_DV)
    beta = jax.nn.sigmoid(b_raw.astype(jnp.float32))
    g = -jnp.exp(a_log.astype(jnp.float32)) * jax.nn.softplus(a_raw.astype(jnp.float32) + dt_bias.astype(jnp.float32))
    o, S = gated_delta_chunked(q, k, v, g, beta, s0, chunk)
    o = rmsnorm(o.astype(qkv.dtype), norm_w) * jax.nn.silu(z.reshape(B, T, GDN_HEADS, GDN_DV))
    return o.reshape(B, T, GDN_Z), S.astype(s0.dtype), conv_state


def rel_bias_lookup(table, rel):
    idx = jnp.clip(rel, -MAX_REL, MAX_REL) + MAX_REL
    return table[:, idx].astype(jnp.float32)


def band_attention_prompt(q, k, v, table):
    B, T, H, DH = q.shape
    N = T // CHUNK
    nb = BAND_CHUNKS + 1
    qc = q.reshape(B, N, CHUNK, H, DH)
    pad = jnp.zeros((B, WINDOW, H, DH), k.dtype)
    kc = jnp.concatenate([pad, k], axis=1).reshape(B, N + BAND_CHUNKS, CHUNK, H, DH)
    vc = jnp.concatenate([pad, v], axis=1).reshape(B, N + BAND_CHUNKS, CHUNK, H, DH)
    kband = jnp.concatenate([kc[:, m:m + N] for m in range(nb)], axis=2)
    vband = jnp.concatenate([vc[:, m:m + N] for m in range(nb)], axis=2)
    s = jnp.einsum('bnqhd,bnkhd->bnhqk', qc, kband).astype(jnp.float32) * (DH ** -0.5)
    rel = WINDOW + jnp.arange(CHUNK)[:, None] - jnp.arange(nb * CHUNK)[None, :]
    s = s + rel_bias_lookup(table, rel)[None, None]
    key_pos = (jnp.arange(N)[:, None] - BAND_CHUNKS) * CHUNK + jnp.arange(nb * CHUNK)[None, :]
    s = jnp.where((key_pos >= 0)[None, :, None, None, :], s, -jnp.inf)
    p = jax.nn.softmax(s, axis=-1).astype(v.dtype)
    o = jnp.einsum('bnhqk,bnkhd->bnqhd', p, vband)
    return o.reshape(B, T, H, DH)


def band_attention_sample(q, k_new, v_new, k_cache, v_cache, table):
    T = q.shape[1]
    lc = k_cache.shape[1]
    kk = jnp.concatenate([k_cache.astype(k_new.dtype), k_new], axis=1)
    vv = jnp.concatenate([v_cache.astype(v_new.dtype), v_new], axis=1)
    s = jnp.einsum('bqhd,bkhd->bhqk', q, kk).astype(jnp.float32) * (ATT_DH ** -0.5)
    rel = lc + jnp.arange(T)[:, None] - jnp.arange(lc + T)[None, :]
    s = s + rel_bias_lookup(table, rel)[None]
    p = jax.nn.softmax(s, axis=-1).astype(vv.dtype)
    return jnp.einsum('bhqk,bkhd->bqhd', p, vv)


def hybrid_layer(x, band_cache, s0, qkv_prev, ffn_prev, gdn_chunk, lw):
    (norm_mix_pre, w_in, qkv_conv_w, a_log, dt_bias, gdn_norm_w, rel_bias,
     attn_norm_w, w_out, norm_mix_post, norm_ffn_pre, w_gate_up, ffn_conv_w,
     ffn_conv_b, w_down, norm_ffn_post) = lw
    B, T, _ = x.shape
    u = rmsnorm(x, norm_mix_pre)
    p = u @ w_in
    c0 = GDN_QKV
    c1 = c0 + GDN_Z
    c2 = c1 + GDN_HEADS
    c3 = c2 + GDN_HEADS
    c4 = c3 + ATT_W
    c5 = c4 + ATT_W
    qkv_a, z_a, b_a, a_a, q_b, k_b, v_b = jnp.split(p, [c0, c1, c2, c3, c4, c5], axis=-1)
    o_a, s_new, qkv_state = gated_deltanet(qkv_a, z_a, b_a, a_a, qkv_prev, s0, qkv_conv_w,
                                           a_log, dt_bias, gdn_norm_w, gdn_chunk)
    q_b = q_b.reshape(B, T, ATT_HEADS, ATT_DH)
    k_b = k_b.reshape(B, T, ATT_HEADS, ATT_DH)
    v_b = v_b.reshape(B, T, ATT_HEADS, ATT_DH)
    if band_cache is None:
        o_b = band_attention_prompt(q_b, k_b, v_b, rel_bias)
        keep = min(WINDOW, T)
        k_rows, v_rows = k_b[:, T - keep:], v_b[:, T - keep:]
    else:
        o_b = band_attention_sample(q_b, k_b, v_b, band_cache[0], band_cache[1], rel_bias)
        k_rows, v_rows = k_b, v_b
    o_b = rmsnorm(o_b, attn_norm_w).reshape(B, T, ATT_W)
    mix = jnp.concatenate([o_a, o_b], axis=-1) @ w_out
    x = x + rmsnorm(mix, norm_mix_post)
    u = rmsnorm(x, norm_ffn_pre)
    gate, up = jnp.split(u @ w_gate_up, 2, axis=-1)
    gate, ffn_state = causal_dwconv(gate, ffn_prev, ffn_conv_w)
    h = jax.nn.gelu(gate + ffn_conv_b.astype(gate.dtype), approximate=True) * up
    x = x + rmsnorm(h @ w_down, norm_ffn_post)
    return x, k_rows, v_rows, s_new, qkv_state, ffn_state


def setup_inputs(seed: int = 0) -> dict:
    key = jax.random.key(seed)
    ks = jax.random.split(key, 32)
    f32 = jnp.float32
    lc = min(WINDOW, PAST_LEN)

    def nrm(k, shape, scale):
        return jax.random.normal(k, shape, f32) * scale

    def gain(k, n):
        return 1.0 + 0.01 * jax.random.normal(k, (DEPTH, n), f32)

    dt = jnp.exp(jax.random.uniform(ks[9], (DEPTH, GDN_HEADS), f32, np.log(0.001), np.log(0.1)))
    return {
        'x_prompt': nrm(ks[0], (BATCH, SEQ, D_MODEL), 1.0),
        'x_sample': nrm(ks[1], (DEC_BATCH, DEC_SEQ, D_MODEL), 1.0),
        'cache_band_k': nrm(ks[2], (DEPTH, DEC_BATCH, lc, ATT_HEADS, ATT_DH), 1.0),
        'cache_band_v': nrm(ks[3], (DEPTH, DEC_BATCH, lc, ATT_HEADS, ATT_DH), 1.0),
        'state_delta': nrm(ks[4], (DEPTH, DEC_BATCH, GDN_HEADS, GDN_DK, GDN_DV), 0.1),
        'state_qkv_conv': nrm(ks[5], (DEPTH, DEC_BATCH, GDN_CONV - 1, GDN_QKV), 1.0),
        'state_ffn_conv': nrm(ks[6], (DEPTH, DEC_BATCH, FFN_CONV - 1, D_FF), 1.0),
        'norm_mix_pre': gain(ks[7], D_MODEL),
        'w_in': nrm(ks[8], (DEPTH, D_MODEL, IN_COLS), D_MODEL ** -0.5),
        'qkv_conv_w': nrm(ks[10], (DEPTH, GDN_CONV, GDN_QKV), GDN_CONV ** -0.5),
        'a_log': jnp.log(jax.random.uniform(ks[11], (DEPTH, GDN_HEADS), f32, 1.0, 16.0)),
        'dt_bias': dt + jnp.log(-jnp.expm1(-dt)),
        'gdn_norm_w': gain(ks[12], GDN_DV),
        'rel_bias': nrm(ks[13], (DEPTH, ATT_HEADS, 2 * MAX_REL + 1), 0.1),
        'attn_norm_w': gain(ks[14], ATT_DH),
        'w_out': nrm(ks[15], (DEPTH, MIX_WIDTH, D_MODEL), MIX_WIDTH ** -0.5),
        'norm_mix_post': gain(ks[16], D_MODEL),
        'norm_ffn_pre': gain(ks[17], D_MODEL),
        'w_gate_up': nrm(ks[18], (DEPTH, D_MODEL, 2 * D_FF), D_MODEL ** -0.5),
        'ffn_conv_w': nrm(ks[19], (DEPTH, FFN_CONV, D_FF), FFN_CONV ** -0.5),
        'ffn_conv_b': nrm(ks[20], (DEPTH, D_FF), 0.01),
        'w_down': nrm(ks[21], (DEPTH, D_FF, D_MODEL), D_FF ** -0.5),
        'norm_ffn_post': gain(ks[22], D_MODEL),
    }


def _stack(outs, i):
    return jnp.stack([o[i] for o in outs], axis=0)


def reference(x_prompt, x_sample, cache_band_k, cache_band_v, state_delta, state_qkv_conv,
              state_ffn_conv, norm_mix_pre, w_in, qkv_conv_w, a_log, dt_bias, gdn_norm_w,
              rel_bias, attn_norm_w, w_out, norm_mix_post, norm_ffn_pre, w_gate_up,
              ffn_conv_w, ffn_conv_b, w_down, norm_ffn_post):
    weights = (norm_mix_pre, w_in, qkv_conv_w, a_log, dt_bias, gdn_norm_w, rel_bias,
               attn_norm_w, w_out, norm_mix_post, norm_ffn_pre, w_gate_up, ffn_conv_w,
               ffn_conv_b, w_down, norm_ffn_post)
    xp, xs = x_prompt, x_sample
    bp = xp.shape[0]
    outs_p, outs_s = [], []
    for l in range(DEPTH):
        lw = tuple(w[l] for w in weights)
        s0_p = jnp.zeros((bp, GDN_HEADS, GDN_DK, GDN_DV), xp.dtype)
        qkv0_p = jnp.zeros((bp, GDN_CONV - 1, GDN_QKV), xp.dtype)
        ffn0_p = jnp.zeros((bp, FFN_CONV - 1, D_FF), xp.dtype)
        xp, kp, vp, sp, cqp, cfp = hybrid_layer(xp, None, s0_p, qkv0_p, ffn0_p, CHUNK, lw)
        xs, ksm, vsm, ssm, cqs, cfs = hybrid_layer(
            xs, (cache_band_k[l], cache_band_v[l]), state_delta[l], state_qkv_conv[l],
            state_ffn_conv[l], xs.shape[1], lw)
        outs_p.append((kp, vp, sp, cqp, cfp))
        outs_s.append((ksm, vsm, ssm, cqs, cfs))
    return (xp, xs,
            _stack(outs_p, 0), _stack(outs_p, 1), _stack(outs_p, 2), _stack(outs_p, 3), _stack(outs_p, 4),
            _stack(outs_s, 0), _stack(outs_s, 1), _stack(outs_s, 2), _stack(outs_s, 3), _stack(outs_s, 4))
```

```python
import functools

import jax
import jax.numpy as jnp
from jax import lax
from jax.experimental import pallas as pl
from jax.experimental.pallas import tpu as pltpu

F32 = jnp.float32
BF16 = jnp.bfloat16

EPS = 1e-6
CHUNK = 64
GDN_HEADS = 4
GDN_DK = 128
GDN_DV = 128
GDN_CONV = 4
ATT_HEADS = 8
ATT_DH = 64
BAND_CHUNKS = 8
WINDOW = BAND_CHUNKS * CHUNK
MAX_REL = 128
FFN_CONV = 3

GDN_QK = GDN_HEADS * GDN_DK
GDN_QKV = GDN_HEADS * (2 * GDN_DK + GDN_DV)
GDN_Z = GDN_HEADS * GDN_DV
ATT_W = ATT_HEADS * ATT_DH
LANES = 128
SUBLANES = 8
VMEM_LIMIT = 56 * 1024 * 1024
ROW_TILE = 512
FFN_COLS = 256


def _dot(a, b):
    return jnp.dot(a, b, preferred_element_type=F32)


def _dot_nt(a, b):
    return lax.dot_general(a, b, (((1,), (1,)), ((), ())), preferred_element_type=F32)


def _dot_tn(a, b):
    return lax.dot_general(a, b, (((0,), (0,)), ((), ())), preferred_element_type=F32)


def _split3(x):
    x1 = x.astype(BF16)
    r1 = x - x1.astype(F32)
    x2 = r1.astype(BF16)
    x3 = (r1 - x2.astype(F32)).astype(BF16)
    return x1, x2, x3


def _dot_exact_lhs(a16, x):
    x1, x2, x3 = _split3(x)
    return _dot(a16, x1) + _dot(a16, x2) + _dot(a16, x3)


def _dot_exact_rhs(x, b16):
    x1, x2, x3 = _split3(x)
    return _dot(x1, b16) + _dot(x2, b16) + _dot(x3, b16)


def _sigmoid(x):
    return 1.0 / (1.0 + jnp.exp(-x))


def _softplus(x):
    return jnp.maximum(x, 0.0) + jnp.log(1.0 + jnp.exp(-jnp.abs(x)))


def _rms(x):
    return x * lax.rsqrt(jnp.mean(x * x, axis=-1, keepdims=True) + EPS)


def _gelu_tanh(x):
    c = 0.7978845608028654
    return 0.5 * x * (1.0 + jnp.tanh(c * (x + 0.044715 * (x * x * x))))


def _const_spec(shape):
    n = len(shape)
    return pl.BlockSpec(shape, lambda *_: (0,) * n, pipeline_mode=pl.Buffered(1))


def _inproj_kernel(x_ref, nw_ref, wa_ref, wg_ref, wb_ref,
                   qkv_ref, z_ref, ba_ref, q_ref, k_ref, v_ref, kf_ref, vf_ref,
                   *, keep_all, n_tiles):
    x = x_ref[0]
    u = (_rms(x) * nw_ref[...]).astype(BF16)
    qkv_ref[0] = _dot(u, wa_ref[:, :GDN_QKV])
    z_ref[0] = _dot(u, wa_ref[:, GDN_QKV:])
    ba_ref[0] = _dot(u, wg_ref[...])
    q_ref[0] = _dot(u, wb_ref[:, :ATT_W]).astype(BF16)
    k = _dot(u, wb_ref[:, ATT_W:2 * ATT_W])
    v = _dot(u, wb_ref[:, 2 * ATT_W:])
    k_ref[0] = k.astype(BF16)
    v_ref[0] = v.astype(BF16)

    def keep():
        kf_ref[0] = k
        vf_ref[0] = v

    if keep_all:
        keep()
    else:
        pl.when(pl.program_id(1) == n_tiles - 1)(keep)


def _inproj(x, nw, wa, wg, wb, *, keep_all):
    B, T, D = x.shape
    tm = ROW_TILE
    nt = T // tm
    keep_rows = T if keep_all else tm
    row = lambda w: pl.BlockSpec((1, tm, w), lambda b, i: (b, i, 0))
    keep = (pl.BlockSpec((1, tm, ATT_W), lambda b, i: (b, i, 0)) if keep_all
            else pl.BlockSpec((1, tm, ATT_W), lambda b, i: (b, 0, 0)))
    return pl.pallas_call(
        functools.partial(_inproj_kernel, keep_all=keep_all, n_tiles=nt),
        grid=(B, nt),
        in_specs=[row(D), _const_spec(nw.shape), _const_spec(wa.shape), _const_spec(wg.shape),
                  _const_spec(wb.shape)],
        out_specs=[row(GDN_QKV), row(GDN_Z), row(LANES), row(ATT_W), row(ATT_W), row(ATT_W), keep, keep],
        out_shape=[jax.ShapeDtypeStruct((B, T, GDN_QKV), F32),
                   jax.ShapeDtypeStruct((B, T, GDN_Z), F32),
                   jax.ShapeDtypeStruct((B, T, LANES), F32),
                   jax.ShapeDtypeStruct((B, T, ATT_W), BF16),
                   jax.ShapeDtypeStruct((B, T, ATT_W), BF16),
                   jax.ShapeDtypeStruct((B, T, ATT_W), BF16),
                   jax.ShapeDtypeStruct((B, keep_rows, ATT_W), F32),
                   jax.ShapeDtypeStruct((B, keep_rows, ATT_W), F32)],
        compiler_params=pltpu.CompilerParams(
            dimension_semantics=("arbitrary", "arbitrary"), vmem_limit_bytes=VMEM_LIMIT),
    )(x, nw, wa, wg, wb)


def _gdn_kernel(qkv_ref, z_ref, ba_ref, prev_ref, s0_ref, cw_ref, alog_ref, dtb_ref, gnw_ref,
                o_ref, sout_ref, xbuf, s_scr, *, C, cps, n_steps):
    j = pl.program_id(1)
    R = C * cps
    pad = SUBLANES

    @pl.when(j == 0)
    def _():
        xbuf[0:pad, :] = prev_ref[0]
        s_scr[...] = s0_ref[0]

    xbuf[pad:pad + R, :] = qkv_ref[0]

    ri = lax.broadcasted_iota(jnp.int32, (C, C), 0)
    ci = lax.broadcasted_iota(jnp.int32, (C, C), 1)
    incl = ri >= ci
    strict = ri > ci
    tril16 = jnp.where(incl, 1.0, 0.0).astype(BF16)
    n_sq = C.bit_length() - 2
    cw = cw_ref[...]
    alog = alog_ref[...]
    dtb = dtb_ref[...]
    gnw = gnw_ref[...]

    for c in range(cps):
        r0 = c * C
        conv = cw[0:1] * xbuf[pad - 3 + r0:pad - 3 + r0 + C, :]
        for i in range(1, GDN_CONV):
            conv = conv + cw[i:i + 1] * xbuf[pad - 3 + i + r0:pad - 3 + i + r0 + C, :]
        act = conv * _sigmoid(conv)

        ba = ba_ref[0, r0:r0 + C, :]
        sig = _sigmoid(ba)
        g = -jnp.exp(alog) * _softplus(ba + dtb)
        G = _dot_exact_lhs(tril16, g)
        GT = G.T

        for h in range(GDN_HEADS):
            q = act[:, h * GDN_DK:(h + 1) * GDN_DK]
            k = act[:, GDN_QK + h * GDN_DK:GDN_QK + (h + 1) * GDN_DK]
            v = act[:, 2 * GDN_QK + h * GDN_DV:2 * GDN_QK + (h + 1) * GDN_DV]
            qn = q * lax.rsqrt(jnp.sum(q * q, axis=-1, keepdims=True) + EPS) * (GDN_DK ** -0.5)
            kn = k * lax.rsqrt(jnp.sum(k * k, axis=-1, keepdims=True) + EPS)
            beta = sig[:, h:h + 1]
            Gc = G[:, GDN_HEADS + h:GDN_HEADS + h + 1]
            Gr = GT[GDN_HEADS + h:GDN_HEADS + h + 1, :]
            Gl = Gc[C - 1:C, :]
            eG = jnp.exp(Gc)
            gam = jnp.where(incl, jnp.exp(jnp.where(incl, Gc - Gr, 0.0)), 0.0)
            kb = kn * beta
            kn16 = kn.astype(BF16)
            A = jnp.where(strict, _dot_nt(kb.astype(BF16), kn16) * gam, 0.0)
            QK = _dot_nt(qn.astype(BF16), kn16) * gam

            N = -A
            P = A
            for _ in range(n_sq):
                P16 = P.astype(BF16)
                P = _dot(P16, P16)
                N = N + P + _dot(N.astype(BF16), P.astype(BF16))
            rhs = jnp.concatenate([v * beta, kb * eG], axis=-1)
            sol = rhs + _dot(N.astype(BF16), rhs.astype(BF16))
            u = sol[:, :GDN_DV]
            w = sol[:, GDN_DV:]

            S = s_scr[h]
            wq = jnp.concatenate([w, qn * eG], axis=0).astype(BF16)
            r = _dot(wq, S.astype(BF16))
            v_new = u - r[:C]
            vn16 = v_new.astype(BF16)
            o = r[C:] + _dot(QK.astype(BF16), vn16)
            kg = kn * jnp.exp(Gl - Gc)
            s_scr[h] = S * jnp.exp(Gl) + _dot_tn(kg.astype(BF16), vn16)

            zh = z_ref[0, r0:r0 + C, h * GDN_DV:(h + 1) * GDN_DV]
            o_ref[0, r0:r0 + C, h * GDN_DV:(h + 1) * GDN_DV] = (
                _rms(o) * gnw * (zh * _sigmoid(zh))).astype(o_ref.dtype)

    xbuf[0:pad, :] = xbuf[R:R + pad, :]

    @pl.when(j == n_steps - 1)
    def _():
        sout_ref[0] = s_scr[...]


def _gdn(qkv, z, ba, prev, s0, cw, alog, dtb, gnw, *, C, cps):
    B, T, _ = qkv.shape
    R = C * cps
    n_steps = T // R
    row = lambda w: pl.BlockSpec((1, R, w), lambda b, j: (b, j, 0))
    st = pl.BlockSpec((1, GDN_HEADS, GDN_DK, GDN_DV), lambda b, j: (b, 0, 0, 0))
    return pl.pallas_call(
        functools.partial(_gdn_kernel, C=C, cps=cps, n_steps=n_steps),
        grid=(B, n_steps),
        in_specs=[row(GDN_QKV), row(GDN_Z), row(LANES),
                  pl.BlockSpec((1, SUBLANES, GDN_QKV), lambda b, j: (b, 0, 0)), st,
                  _const_spec(cw.shape), _const_spec(alog.shape), _const_spec(dtb.shape),
                  _const_spec(gnw.shape)],
        out_specs=[row(GDN_Z), st],
        out_shape=[jax.ShapeDtypeStruct((B, T, GDN_Z), BF16),
                   jax.ShapeDtypeStruct((B, GDN_HEADS, GDN_DK, GDN_DV), F32)],
        scratch_shapes=[pltpu.VMEM((SUBLANES + R, GDN_QKV), F32),
                        pltpu.VMEM((GDN_HEADS, GDN_DK, GDN_DV), F32)],
        compiler_params=pltpu.CompilerParams(
            dimension_semantics=("arbitrary", "arbitrary"), vmem_limit_bytes=VMEM_LIMIT),
    )(qkv, z, ba, prev, s0, cw, alog, dtb, gnw)


def _head_norm(o, nw, seg16):
    ms = _dot_exact_rhs(o * o, seg16)
    return o * lax.rsqrt(ms + EPS) * nw


def _attn_pair(qp, blocks, bias_ref, p):
    rows = qp.shape[0]
    lane_head = lax.broadcasted_iota(jnp.int32, (rows, LANES), 1) // ATT_DH
    o_pair = None
    for e in range(2):
        h = 2 * p + e
        qe = jnp.where(lane_head == e, qp, jnp.zeros_like(qp))
        ss = []
        for k16, _, bias_fn, mask in blocks:
            s = _dot_nt(qe, k16) * (ATT_DH ** -0.5) + bias_fn(h)
            if mask is not None:
                s = jnp.where(mask, s, -jnp.inf)
            ss.append(s)
        m = functools.reduce(jnp.maximum, [jnp.max(s, axis=-1, keepdims=True) for s in ss])
        ps = [jnp.exp(s - m) for s in ss]
        l = functools.reduce(jnp.add, [jnp.sum(pe, axis=-1, keepdims=True) for pe in ps])
        acc = functools.reduce(jnp.add, [_dot(pe.astype(BF16), blk[1]) for pe, blk in zip(ps, blocks)])
        oe = acc * (1.0 / l)
        o_pair = oe if e == 0 else jnp.where(lane_head == 0, o_pair, oe)
    return o_pair


def _attn_prompt_kernel(q_ref, k_ref, v_ref, bias_ref, nw_ref, seg_ref, o_ref, kpad, vpad, *, qt, T):
    j = pl.program_id(1)
    nb = BAND_CHUNKS + 1

    @pl.when(j == 0)
    def _():
        kpad[0:WINDOW, :] = jnp.zeros((WINDOW, ATT_W), BF16)
        vpad[0:WINDOW, :] = jnp.zeros((WINDOW, ATT_W), BF16)
        kpad[WINDOW:WINDOW + T, :] = k_ref[0]
        vpad[WINDOW:WINDOW + T, :] = v_ref[0]

    band_chunk = lax.broadcasted_iota(jnp.int32, (CHUNK, nb * CHUNK), 1) // CHUNK
    for c in range(qt):
        n = j * qt + c
        row0 = pl.multiple_of(n * CHUNK, CHUNK)
        mask = band_chunk >= BAND_CHUNKS - n
        outs = []
        for p in range(ATT_HEADS // 2):
            kw = kpad[pl.ds(row0, nb * CHUNK), p * LANES:(p + 1) * LANES]
            vw = vpad[pl.ds(row0, nb * CHUNK), p * LANES:(p + 1) * LANES]
            qp = q_ref[0, c * CHUNK:(c + 1) * CHUNK, p * LANES:(p + 1) * LANES]
            outs.append(_attn_pair(qp, [(kw, vw, lambda h: bias_ref[h], mask)], bias_ref, p))
        o = jnp.concatenate(outs, axis=-1)
        o_ref[0, c * CHUNK:(c + 1) * CHUNK, :] = _head_norm(o, nw_ref[...], seg_ref[...]).astype(o_ref.dtype)


def _attn_prompt(q, k, v, bias, nw, seg, *, qt):
    B, T, _ = q.shape
    n_steps = T // (qt * CHUNK)
    full = pl.BlockSpec((1, T, ATT_W), lambda b, j: (b, 0, 0))
    row = pl.BlockSpec((1, qt * CHUNK, ATT_W), lambda b, j: (b, j, 0))
    return pl.pallas_call(
        functools.partial(_attn_prompt_kernel, qt=qt, T=T),
        grid=(B, n_steps),
        in_specs=[row, full, full, _const_spec(bias.shape), _const_spec(nw.shape), _const_spec(seg.shape)],
        out_specs=row,
        out_shape=jax.ShapeDtypeStruct((B, T, ATT_W), BF16),
        scratch_shapes=[pltpu.VMEM((WINDOW + T, ATT_W), BF16), pltpu.VMEM((WINDOW + T, ATT_W), BF16)],
        compiler_params=pltpu.CompilerParams(
            dimension_semantics=("arbitrary", "arbitrary"), vmem_limit_bytes=VMEM_LIMIT),
    )(q, k, v, bias, nw, seg)


def _attn_sample_kernel(q_ref, kc_ref, vc_ref, kn_ref, vn_ref, bc_ref, bn_ref, nw_ref, seg_ref, o_ref):
    outs = []
    for p in range(ATT_HEADS // 2):
        cols = slice(p * LANES, (p + 1) * LANES)
        blocks = [(kc_ref[0, :, cols].astype(BF16), vc_ref[0, :, cols].astype(BF16), lambda h: bc_ref[h], None),
                  (kn_ref[0, :, cols], vn_ref[0, :, cols], lambda h: bn_ref[h], None)]
        outs.append(_attn_pair(q_ref[0, :, cols], blocks, None, p))
    o = jnp.concatenate(outs, axis=-1)
    o_ref[0] = _head_norm(o, nw_ref[...], seg_ref[...]).astype(o_ref.dtype)


def _attn_sample(q, kc, vc, kn, vn, bias_c, bias_n, nw, seg):
    B, T, _ = q.shape
    lc = kc.shape[1]
    new = pl.BlockSpec((1, T, ATT_W), lambda b: (b, 0, 0))
    cache = pl.BlockSpec((1, lc, ATT_W), lambda b: (b, 0, 0))
    return pl.pallas_call(
        _attn_sample_kernel,
        grid=(B,),
        in_specs=[new, cache, cache, new, new, _const_spec(bias_c.shape), _const_spec(bias_n.shape),
                  _const_spec(nw.shape), _const_spec(seg.shape)],
        out_specs=new,
        out_shape=jax.ShapeDtypeStruct((B, T, ATT_W), BF16),
        compiler_params=pltpu.CompilerParams(
            dimension_semantics=("arbitrary",), vmem_limit_bytes=VMEM_LIMIT),
    )(q, kc, vc, kn, vn, bias_c, bias_n, nw, seg)


def _mix_ffn_kernel(x_ref, oa_ref, ob_ref, wo_ref, nmp_ref, nfp_ref, wg_ref, wu_ref, cw_ref, cb_ref,
                    wd_ref, nfo_ref, prev_ref, y_ref, st_ref, gbuf, carry, acc,
                    *, S, R, tps, d_ff):
    t = pl.program_id(0)
    tm = S * R
    pad = SUBLANES
    half = oa_ref.shape[-1]
    mix = _dot(oa_ref[...], wo_ref[:half, :]) + _dot(ob_ref[...], wo_ref[half:, :])
    x1 = x_ref[...] + _rms(mix) * nmp_ref[...]
    u2 = (_rms(x1) * nfp_ref[...]).astype(BF16)

    if tps > 1:
        @pl.when(t % tps == 0)
        def _():
            carry[...] = prev_ref[0]

    fc = FFN_COLS
    for c in range(d_ff // fc):
        cols = slice(c * fc, (c + 1) * fc)
        g = _dot(u2, wg_ref[:, cols])
        up = _dot(u2, wu_ref[:, cols])
        if tps > 1:
            gbuf[0, 0:pad, :] = carry[:, cols]
        else:
            gbuf[:, 0:pad, :] = prev_ref[:, :, cols]
        gbuf[:, pad:pad + R, :] = g.reshape(S, R, fc)
        cw = cw_ref[:, cols]
        conv = cb_ref[:, cols] + cw[0:1] * gbuf[:, pad - 2:pad - 2 + R, :]
        for i in range(1, FFN_CONV):
            conv = conv + cw[i:i + 1] * gbuf[:, pad - 2 + i:pad - 2 + i + R, :]
        hid = (_gelu_tanh(conv) * up.reshape(S, R, fc)).reshape(tm, fc).astype(BF16)
        last = gbuf[:, R:R + pad, :]
        st_ref[:, :, cols] = last
        if tps > 1:
            carry[:, cols] = last[0]
        part = _dot(hid, wd_ref[cols, :])
        if c == 0:
            acc[...] = part
        else:
            acc[...] += part
    y_ref[...] = x1 + _rms(acc[...]) * nfo_ref[...]


def _mix_ffn(x, oa, ob, wo, nmp, nfp, wg, wu, cw, cb, wd, nfo, prev, *, S, R, tps):
    M, D = x.shape
    d_ff = wg.shape[1]
    tm = S * R
    n_seq = prev.shape[0]
    row = lambda w: pl.BlockSpec((tm, w), lambda t: (t, 0))
    st = pl.BlockSpec((S, SUBLANES, d_ff), lambda t: (t // tps, 0, 0))
    consts = [wo, nmp, nfp, wg, wu, cw, cb, wd, nfo]
    return pl.pallas_call(
        functools.partial(_mix_ffn_kernel, S=S, R=R, tps=tps, d_ff=d_ff),
        grid=(M // tm,),
        in_specs=[row(D), row(oa.shape[1]), row(ob.shape[1])] + [_const_spec(a.shape) for a in consts] + [st],
        out_specs=[row(D), st],
        out_shape=[jax.ShapeDtypeStruct((M, D), F32),
                   jax.ShapeDtypeStruct((n_seq, SUBLANES, d_ff), F32)],
        scratch_shapes=[pltpu.VMEM((S, SUBLANES + R, FFN_COLS), F32),
                        pltpu.VMEM((SUBLANES, d_ff), F32),
                        pltpu.VMEM((tm, D), F32)],
        compiler_params=pltpu.CompilerParams(
            dimension_semantics=("arbitrary",), vmem_limit_bytes=VMEM_LIMIT),
    )(x, oa, ob, *consts, prev)


def _pad_rows_front(a, rows):
    return jnp.pad(a, ((0, 0), (rows - a.shape[1], 0), (0, 0)))


def _lane_row(vals, offset):
    return jnp.zeros((1, LANES), F32).at[0, offset:offset + vals.shape[0]].set(vals.astype(F32))


def _layer(xp, xs, cache_k, cache_v, s_delta, s_qkv, s_ffn, lw):
    (norm_mix_pre, w_in, qkv_conv_w, a_log, dt_bias, gdn_norm_w, rel_bias, attn_norm_w, w_out,
     norm_mix_post, norm_ffn_pre, w_gate_up, ffn_conv_w, ffn_conv_b, w_down, norm_ffn_post) = lw
    Bp, Tp, D = xp.shape
    Bs, Ts, _ = xs.shape
    d_ff = w_down.shape[0]

    c1 = GDN_QKV + GDN_Z
    c2 = c1 + 2 * GDN_HEADS
    wa = w_in[:, :c1].astype(BF16)
    wg_in = jnp.pad(w_in[:, c1:c2], ((0, 0), (0, LANES - 2 * GDN_HEADS))).astype(BF16)
    wb = w_in[:, c2:].astype(BF16)
    nmix = norm_mix_pre.reshape(1, D)
    cw_qkv = jnp.pad(qkv_conv_w, ((0, SUBLANES - GDN_CONV), (0, 0)))
    alog = _lane_row(a_log, GDN_HEADS)
    dtb = _lane_row(dt_bias, GDN_HEADS)
    gnw = gdn_norm_w.reshape(1, GDN_DV)
    anw = jnp.tile(attn_norm_w, ATT_HEADS).reshape(1, ATT_W)
    lane_head = jnp.arange(ATT_W) // ATT_DH
    seg = jnp.where(lane_head[:, None] == lane_head[None, :], 1.0 / ATT_DH, 0.0).astype(BF16)
    nb = BAND_CHUNKS + 1
    rel = WINDOW + jnp.arange(CHUNK)[:, None] - jnp.arange(nb * CHUNK)[None, :]
    bias = rel_bias[:, jnp.clip(rel, -MAX_REL, MAX_REL) + MAX_REL].astype(F32)
    wo = w_out.astype(BF16)
    w_gate = w_gate_up[:, :d_ff].astype(BF16)
    w_up = w_gate_up[:, d_ff:].astype(BF16)
    wd = w_down.astype(BF16)
    cw_ffn = jnp.pad(ffn_conv_w, ((0, SUBLANES - FFN_CONV), (0, 0)))
    cb = ffn_conv_b.reshape(1, d_ff)
    nmp = norm_mix_post.reshape(1, D)
    nfp = norm_ffn_pre.reshape(1, D)
    nfo = norm_ffn_post.reshape(1, D)

    def group(x, keep_all, gdn_prev, gdn_s0, gdn_c, gdn_cps, ffn_prev, S, R, tps, attn):
        B, T, _ = x.shape
        xi = x if not keep_all else x.reshape(1, B * T, D)
        qkv, z, ba, q, k, v, kf, vf = _inproj(xi, nmix, wa, wg_in, wb, keep_all=keep_all)
        rs = lambda a: a.reshape(B, T, a.shape[-1])
        qkv, z, ba, q, k, v = map(rs, (qkv, z, ba, q, k, v))
        oa, s_new = _gdn(qkv, z, ba, _pad_rows_front(gdn_prev, SUBLANES), gdn_s0, cw_qkv, alog, dtb, gnw,
                         C=gdn_c, cps=gdn_cps)
        ob = attn(q, k, v)
        y, st = _mix_ffn(x.reshape(B * T, D), oa.reshape(B * T, GDN_Z), ob.reshape(B * T, ATT_W),
                         wo, nmp, nfp, w_gate, w_up, cw_ffn, cb, wd, nfo,
                         _pad_rows_front(ffn_prev, SUBLANES), S=S, R=R, tps=tps)
        keep = kf.shape[1] if not keep_all else T
        k_rows = kf.reshape(B, keep, ATT_HEADS, ATT_DH)
        v_rows = vf.reshape(B, keep, ATT_HEADS, ATT_DH)
        qkv_state = qkv[:, T - (GDN_CONV - 1):, :]
        ffn_state = st[:, SUBLANES - (FFN_CONV - 1):, :]
        return y.reshape(B, T, D), k_rows, v_rows, s_new, qkv_state, ffn_state

    out_p = group(
        xp, False, jnp.zeros((Bp, GDN_CONV - 1, GDN_QKV), F32),
        jnp.zeros((Bp, GDN_HEADS, GDN_DK, GDN_DV), F32), CHUNK, 4,
        jnp.zeros((Bp, FFN_CONV - 1, d_ff), F32), 1, ROW_TILE, Tp // ROW_TILE,
        lambda q, k, v: _attn_prompt(q, k, v, bias, anw, seg, qt=4))

    lc = cache_k.shape[1]
    kc = cache_k.reshape(Bs, lc, ATT_W)
    vc = cache_v.reshape(Bs, lc, ATT_W)
    bias_c = bias[:, :Ts, :lc]
    bias_n = bias[:, :Ts, lc:lc + Ts]
    out_s = group(
        xs, True, s_qkv, s_delta, Ts, 1, s_ffn, ROW_TILE // Ts, Ts, 1,
        lambda q, k, v: _attn_sample(q, kc, vc, k, v, bias_c, bias_n, anw, seg))
    return out_p, out_s


def kernel(x_prompt, x_sample, cache_band_k, cache_band_v, state_delta, state_qkv_conv, state_ffn_conv, norm_mix_pre, w_in, qkv_conv_w, a_log, dt_bias, gdn_norm_w, rel_bias, attn_norm_w, w_out, norm_mix_post, norm_ffn_pre, w_gate_up, ffn_conv_w, ffn_conv_b, w_down, norm_ffn_post):
    weights = (norm_mix_pre, w_in, qkv_conv_w, a_log, dt_bias, gdn_norm_w, rel_bias, attn_norm_w, w_out,
               norm_mix_post, norm_ffn_pre, w_gate_up, ffn_conv_w, ffn_conv_b, w_down, norm_ffn_post)
    depth = w_in.shape[0]
    xp, xs = x_prompt, x_sample
    outs_p, outs_s = [], []
    for l in range(depth):
        lw = tuple(w[l] for w in weights)
        op, os_ = _layer(xp, xs, cache_band_k[l], cache_band_v[l], state_delta[l], state_qkv_conv[l],
                         state_ffn_conv[l], lw)
        xp, xs = op[0], os_[0]
        outs_p.append(op[1:])
        outs_s.append(os_[1:])
    stack = lambda outs, i: jnp.stack([o[i] for o in outs], axis=0)
    return (xp, xs) + tuple(stack(outs_p, i) for i in range(5)) + tuple(stack(outs_s, i) for i in range(5))
```

```python
import functools

import jax
import jax.numpy as jnp
from jax import lax
from jax.experimental import pallas as pl
from jax.experimental.pallas import tpu as pltpu

F32 = jnp.float32
BF16 = jnp.bfloat16

EPS = 1e-6
CHUNK = 64
GDN_HEADS = 4
GDN_DK = 128
GDN_DV = 128
GDN_CONV = 4
ATT_HEADS = 8
ATT_DH = 64
BAND_CHUNKS = 8
WINDOW = BAND_CHUNKS * CHUNK
MAX_REL = 128
FFN_CONV = 3

GDN_QK = GDN_HEADS * GDN_DK
GDN_QKV = GDN_HEADS * (2 * GDN_DK + GDN_DV)
GDN_Z = GDN_HEADS * GDN_DV
ATT_W = ATT_HEADS * ATT_DH
LANES = 128
SUBLANES = 8
VMEM_LIMIT = 56 * 1024 * 1024
ROW_TILE = 512
FFN_COLS = 256


def _dot(a, b):
    return jnp.dot(a, b, preferred_element_type=F32)


def _dot_nt(a, b):
    return lax.dot_general(a, b, (((1,), (1,)), ((), ())), preferred_element_type=F32)


def _dot_tn(a, b):
    return lax.dot_general(a, b, (((0,), (0,)), ((), ())), preferred_element_type=F32)


def _split3(x):
    x1 = x.astype(BF16)
    r1 = x - x1.astype(F32)
    x2 = r1.astype(BF16)
    x3 = (r1 - x2.astype(F32)).astype(BF16)
    return x1, x2, x3


def _dot_exact_lhs(a16, x):
    x1, x2, x3 = _split3(x)
    return _dot(a16, x1) + _dot(a16, x2) + _dot(a16, x3)


def _dot_exact_rhs(x, b16):
    x1, x2, x3 = _split3(x)
    return _dot(x1, b16) + _dot(x2, b16) + _dot(x3, b16)


def _sigmoid(x):
    return 1.0 / (1.0 + jnp.exp(-x))


def _softplus(x):
    return jnp.maximum(x, 0.0) + jnp.log(1.0 + jnp.exp(-jnp.abs(x)))


def _rms(x):
    return x * lax.rsqrt(jnp.mean(x * x, axis=-1, keepdims=True) + EPS)


def _gelu_tanh(x):
    c = 0.7978845608028654
    return 0.5 * x * (1.0 + jnp.tanh(c * (x + 0.044715 * (x * x * x))))


def _const_spec(shape):
    n = len(shape)
    return pl.BlockSpec(shape, lambda *_: (0,) * n, pipeline_mode=pl.Buffered(1))


def _inproj_kernel(x_ref, nw_ref, wa_ref, wg_ref, wb_ref,
                   qkv_ref, z_ref, ba_ref, q_ref, k_ref, v_ref, kf_ref, vf_ref,
                   *, keep_all, n_tiles):
    x = x_ref[0]
    u = (_rms(x) * nw_ref[...]).astype(BF16)
    qkv_ref[0] = _dot(u, wa_ref[:, :GDN_QKV])
    z_ref[0] = _dot(u, wa_ref[:, GDN_QKV:])
    ba_ref[0] = _dot(u, wg_ref[...])
    q_ref[0] = _dot(u, wb_ref[:, :ATT_W]).astype(BF16)
    k = _dot(u, wb_ref[:, ATT_W:2 * ATT_W])
    v = _dot(u, wb_ref[:, 2 * ATT_W:])
    k_ref[0] = k.astype(BF16)
    v_ref[0] = v.astype(BF16)

    def keep():
        kf_ref[0] = k
        vf_ref[0] = v

    if keep_all:
        keep()
    else:
        pl.when(pl.program_id(1) == n_tiles - 1)(keep)


def _inproj(x, nw, wa, wg, wb, *, keep_all):
    B, T, D = x.shape
    tm = ROW_TILE
    nt = T // tm
    keep_rows = T if keep_all else tm
    row = lambda w: pl.BlockSpec((1, tm, w), lambda b, i: (b, i, 0))
    keep = (pl.BlockSpec((1, tm, ATT_W), lambda b, i: (b, i, 0)) if keep_all
            else pl.BlockSpec((1, tm, ATT_W), lambda b, i: (b, 0, 0)))
    return pl.pallas_call(
        functools.partial(_inproj_kernel, keep_all=keep_all, n_tiles=nt),
        name="inproj",
        grid=(B, nt),
        in_specs=[row(D), _const_spec(nw.shape), _const_spec(wa.shape), _const_spec(wg.shape),
                  _const_spec(wb.shape)],
        out_specs=[row(GDN_QKV), row(GDN_Z), row(LANES), row(ATT_W), row(ATT_W), row(ATT_W), keep, keep],
        out_shape=[jax.ShapeDtypeStruct((B, T, GDN_QKV), F32),
                   jax.ShapeDtypeStruct((B, T, GDN_Z), F32),
                   jax.ShapeDtypeStruct((B, T, LANES), F32),
                   jax.ShapeDtypeStruct((B, T, ATT_W), BF16),
                   jax.ShapeDtypeStruct((B, T, ATT_W), BF16),
                   jax.ShapeDtypeStruct((B, T, ATT_W), BF16),
                   jax.ShapeDtypeStruct((B, keep_rows, ATT_W), F32),
                   jax.ShapeDtypeStruct((B, keep_rows, ATT_W), F32)],
        compiler_params=pltpu.CompilerParams(
            dimension_semantics=("arbitrary", "arbitrary"), vmem_limit_bytes=VMEM_LIMIT),
    )(x, nw, wa, wg, wb)


def _bmm(a, b):
    return lax.dot_general(a, b, (((2,), (1,)), ((0,), (0,))), preferred_element_type=F32)


def _bmm_nt(a, b):
    return lax.dot_general(a, b, (((2,), (2,)), ((0,), (0,))), preferred_element_type=F32)


def _bmm_tn(a, b):
    return lax.dot_general(a, b, (((1,), (1,)), ((0,), (0,))), preferred_element_type=F32)


def _gdn_kernel(qkv_ref, z_ref, ba_ref, prev_ref, s0_ref, cw_ref, alog_ref, dtb_ref, gnw_ref,
                o_ref, sout_ref, xbuf, s_scr, *, C, cps, bs, n_steps):
    j = pl.program_id(1)
    R = C * cps
    H = GDN_HEADS
    nh = bs * H
    pad = SUBLANES

    @pl.when(j == 0)
    def _():
        xbuf[:, 0:pad, :] = prev_ref[...]
        s_scr[...] = s0_ref[...].reshape(nh, GDN_DK, GDN_DV)

    xbuf[:, pad:pad + R, :] = qkv_ref[...]

    ri = lax.broadcasted_iota(jnp.int32, (C, C), 0)
    ci = lax.broadcasted_iota(jnp.int32, (C, C), 1)
    incl = ri >= ci
    strict = ri > ci
    rr = lax.broadcasted_iota(jnp.int32, (R, R), 0)
    cc = lax.broadcasted_iota(jnp.int32, (R, R), 1)
    cum16 = jnp.where((rr >= cc) & (rr // C == cc // C), 1.0, 0.0).astype(BF16)
    cw = cw_ref[...]
    alog = alog_ref[...]
    dtb = dtb_ref[...]

    acts, sigs, Gs, GTs = [], [], [], []
    for s in range(bs):
        conv = cw[0:1] * xbuf[s, pad - 3:pad - 3 + R, :]
        for i in range(1, GDN_CONV):
            conv = conv + cw[i:i + 1] * xbuf[s, pad - 3 + i:pad - 3 + i + R, :]
        acts.append(conv * _sigmoid(conv))
        ba = ba_ref[s]
        sigs.append(_sigmoid(ba))
        G = _dot_exact_lhs(cum16, -jnp.exp(alog) * _softplus(ba + dtb))
        Gs.append(G)
        GTs.append(G.T)
    xbuf[:, 0:pad, :] = xbuf[:, R:R + pad, :]

    order = [(c, s, h) for c in range(cps) for s in range(bs) for h in range(H)]
    rows = lambda c: slice(c * C, (c + 1) * C)

    def tiles(slabs, col0, width):
        return jnp.stack([slabs[s][rows(c), col0 + h * width:col0 + (h + 1) * width] for c, s, h in order])

    q = tiles(acts, 0, GDN_DK)
    k = tiles(acts, GDN_QK, GDN_DK)
    v = tiles(acts, 2 * GDN_QK, GDN_DV)
    beta = tiles(sigs, 0, 1)
    Gc = tiles(Gs, H, 1)
    Gr = jnp.stack([GTs[s][H + h:H + h + 1, rows(c)] for c, s, h in order])
    Gl = Gc[:, C - 1:C, :]

    qn = q * lax.rsqrt(jnp.sum(q * q, axis=-1, keepdims=True) + EPS) * (GDN_DK ** -0.5)
    kn = k * lax.rsqrt(jnp.sum(k * k, axis=-1, keepdims=True) + EPS)
    eG = jnp.exp(Gc)
    gam = jnp.where(incl, jnp.exp(jnp.where(incl, Gc - Gr, 0.0)), 0.0)
    kb = kn * beta
    kn16 = kn.astype(BF16)
    aq = _bmm_nt(jnp.concatenate([kb, qn], axis=1).astype(BF16), kn16)
    A = jnp.where(strict, aq[:, :C] * gam, 0.0)
    QK16 = (aq[:, C:] * gam).astype(BF16)

    n_joint = C.bit_length() - 2
    A16 = A.astype(BF16)
    N = -A
    Q = _bmm(A16, A16)
    for it in range(n_joint):
        Q16 = Q.astype(BF16)
        if it == n_joint - 1:
            N = N + Q + _bmm(N.astype(BF16), Q16)
        else:
            nq = _bmm(jnp.concatenate([N, Q], axis=1).astype(BF16), Q16)
            N = N + Q + nq[:, :C]
            Q = nq[:, C:]
    rhs = jnp.concatenate([v * beta, kb * eG], axis=-1)
    sol = rhs + _bmm(N.astype(BF16), rhs.astype(BF16))
    u = sol[:, :, :GDN_DV]
    wq16 = jnp.concatenate([sol[:, :, GDN_DV:], qn * eG], axis=1).astype(BF16)
    kg16 = (kn * jnp.exp(Gl - Gc)).astype(BF16)
    dl = jnp.exp(Gl)

    S = s_scr[...]
    o_parts = []
    for c in range(cps):
        sl = slice(c * nh, (c + 1) * nh)
        r = _bmm(wq16[sl], S.astype(BF16))
        vn16 = (u[sl] - r[:, :C]).astype(BF16)
        o_parts.append(r[:, C:] + _bmm(QK16[sl], vn16))
        S = S * dl[sl] + _bmm_tn(kg16[sl], vn16)
    s_scr[...] = S

    on = _rms(jnp.concatenate(o_parts, axis=0)) * gnw_ref[...]
    gates = []
    for s in range(bs):
        zs = z_ref[s]
        gates.append(zs * _sigmoid(zs))
    for idx, (c, s, h) in enumerate(order):
        cols = slice(h * GDN_DV, (h + 1) * GDN_DV)
        o_ref[s, rows(c), cols] = (on[idx] * gates[s][rows(c), cols]).astype(o_ref.dtype)

    @pl.when(j == n_steps - 1)
    def _():
        sout_ref[...] = S.reshape(bs, H, GDN_DK, GDN_DV)


def _gdn(qkv, z, ba, prev, s0, cw, alog, dtb, gnw, *, C, cps, bs):
    B, T, _ = qkv.shape
    R = C * cps
    n_steps = T // R
    row = lambda w: pl.BlockSpec((bs, R, w), lambda b, j: (b, j, 0))
    st = pl.BlockSpec((bs, GDN_HEADS, GDN_DK, GDN_DV), lambda b, j: (b, 0, 0, 0))
    return pl.pallas_call(
        functools.partial(_gdn_kernel, C=C, cps=cps, bs=bs, n_steps=n_steps),
        name="gdn",
        grid=(B // bs, n_steps),
        in_specs=[row(GDN_QKV), row(GDN_Z), row(LANES),
                  pl.BlockSpec((bs, SUBLANES, GDN_QKV), lambda b, j: (b, 0, 0)), st,
                  _const_spec(cw.shape), _const_spec(alog.shape), _const_spec(dtb.shape),
                  _const_spec(gnw.shape)],
        out_specs=[row(GDN_Z), st],
        out_shape=[jax.ShapeDtypeStruct((B, T, GDN_Z), BF16),
                   jax.ShapeDtypeStruct((B, GDN_HEADS, GDN_DK, GDN_DV), F32)],
        scratch_shapes=[pltpu.VMEM((bs, SUBLANES + R, GDN_QKV), F32),
                        pltpu.VMEM((bs * GDN_HEADS, GDN_DK, GDN_DV), F32)],
        compiler_params=pltpu.CompilerParams(
            dimension_semantics=("arbitrary", "arbitrary"), vmem_limit_bytes=VMEM_LIMIT),
    )(qkv, z, ba, prev, s0, cw, alog, dtb, gnw)


def _head_norm(o, nw, seg16):
    ms = _dot_exact_rhs(o * o, seg16)
    return o * lax.rsqrt(ms + EPS) * nw


def _attn_pair(qp, blocks, bias_ref, p):
    rows = qp.shape[0]
    lane_head = lax.broadcasted_iota(jnp.int32, (rows, LANES), 1) // ATT_DH
    o_pair = None
    for e in range(2):
        h = 2 * p + e
        qe = jnp.where(lane_head == e, qp, jnp.zeros_like(qp))
        ss = []
        for k16, _, bias_fn, mask in blocks:
            s = _dot_nt(qe, k16) * (ATT_DH ** -0.5) + bias_fn(h)
            if mask is not None:
                s = jnp.where(mask, s, -jnp.inf)
            ss.append(s)
        m = functools.reduce(jnp.maximum, [jnp.max(s, axis=-1, keepdims=True) for s in ss])
        ps = [jnp.exp(s - m) for s in ss]
        l = functools.reduce(jnp.add, [jnp.sum(pe, axis=-1, keepdims=True) for pe in ps])
        acc = functools.reduce(jnp.add, [_dot(pe.astype(BF16), blk[1]) for pe, blk in zip(ps, blocks)])
        oe = acc * (1.0 / l)
        o_pair = oe if e == 0 else jnp.where(lane_head == 0, o_pair, oe)
    return o_pair


def _attn_prompt_kernel(q_ref, k_ref, v_ref, bias_ref, nw_ref, seg_ref, o_ref, kpad, vpad, *, qt, T):
    j = pl.program_id(1)
    nb = BAND_CHUNKS + 1

    @pl.when(j == 0)
    def _():
        kpad[0:WINDOW, :] = jnp.zeros((WINDOW, ATT_W), BF16)
        vpad[0:WINDOW, :] = jnp.zeros((WINDOW, ATT_W), BF16)
        kpad[WINDOW:WINDOW + T, :] = k_ref[0]
        vpad[WINDOW:WINDOW + T, :] = v_ref[0]

    band_chunk = lax.broadcasted_iota(jnp.int32, (CHUNK, nb * CHUNK), 1) // CHUNK
    for c in range(qt):
        n = j * qt + c
        row0 = pl.multiple_of(n * CHUNK, CHUNK)
        mask = band_chunk >= BAND_CHUNKS - n
        outs = []
        for p in range(ATT_HEADS // 2):
            kw = kpad[pl.ds(row0, nb * CHUNK), p * LANES:(p + 1) * LANES]
            vw = vpad[pl.ds(row0, nb * CHUNK), p * LANES:(p + 1) * LANES]
            qp = q_ref[0, c * CHUNK:(c + 1) * CHUNK, p * LANES:(p + 1) * LANES]
            outs.append(_attn_pair(qp, [(kw, vw, lambda h: bias_ref[h], mask)], bias_ref, p))
        o = jnp.concatenate(outs, axis=-1)
        o_ref[0, c * CHUNK:(c + 1) * CHUNK, :] = _head_norm(o, nw_ref[...], seg_ref[...]).astype(o_ref.dtype)


def _attn_prompt(q, k, v, bias, nw, seg, *, qt):
    B, T, _ = q.shape
    n_steps = T // (qt * CHUNK)
    full = pl.BlockSpec((1, T, ATT_W), lambda b, j: (b, 0, 0))
    row = pl.BlockSpec((1, qt * CHUNK, ATT_W), lambda b, j: (b, j, 0))
    return pl.pallas_call(
        functools.partial(_attn_prompt_kernel, qt=qt, T=T),
        name="attn_prompt",
        grid=(B, n_steps),
        in_specs=[row, full, full, _const_spec(bias.shape), _const_spec(nw.shape), _const_spec(seg.shape)],
        out_specs=row,
        out_shape=jax.ShapeDtypeStruct((B, T, ATT_W), BF16),
        scratch_shapes=[pltpu.VMEM((WINDOW + T, ATT_W), BF16), pltpu.VMEM((WINDOW + T, ATT_W), BF16)],
        compiler_params=pltpu.CompilerParams(
            dimension_semantics=("arbitrary", "arbitrary"), vmem_limit_bytes=VMEM_LIMIT),
    )(q, k, v, bias, nw, seg)


def _attn_sample_kernel(q_ref, kc_ref, vc_ref, kn_ref, vn_ref, bc_ref, bn_ref, nw_ref, seg_ref, o_ref):
    outs = []
    for p in range(ATT_HEADS // 2):
        cols = slice(p * LANES, (p + 1) * LANES)
        blocks = [(kc_ref[0, :, cols].astype(BF16), vc_ref[0, :, cols].astype(BF16), lambda h: bc_ref[h], None),
                  (kn_ref[0, :, cols], vn_ref[0, :, cols], lambda h: bn_ref[h], None)]
        outs.append(_attn_pair(q_ref[0, :, cols], blocks, None, p))
    o = jnp.concatenate(outs, axis=-1)
    o_ref[0] = _head_norm(o, nw_ref[...], seg_ref[...]).astype(o_ref.dtype)


def _attn_sample(q, kc, vc, kn, vn, bias_c, bias_n, nw, seg):
    B, T, _ = q.shape
    lc = kc.shape[1]
    new = pl.BlockSpec((1, T, ATT_W), lambda b: (b, 0, 0))
    cache = pl.BlockSpec((1, lc, ATT_W), lambda b: (b, 0, 0))
    return pl.pallas_call(
        _attn_sample_kernel,
        name="attn_sample",
        grid=(B,),
        in_specs=[new, cache, cache, new, new, _const_spec(bias_c.shape), _const_spec(bias_n.shape),
                  _const_spec(nw.shape), _const_spec(seg.shape)],
        out_specs=new,
        out_shape=jax.ShapeDtypeStruct((B, T, ATT_W), BF16),
        compiler_params=pltpu.CompilerParams(
            dimension_semantics=("arbitrary",), vmem_limit_bytes=VMEM_LIMIT),
    )(q, kc, vc, kn, vn, bias_c, bias_n, nw, seg)


def _mix_ffn_kernel(x_ref, oa_ref, ob_ref, wo_ref, nmp_ref, nfp_ref, wg_ref, wu_ref, cw_ref, cb_ref,
                    wd_ref, nfo_ref, prev_ref, y_ref, st_ref, gbuf, carry, acc,
                    *, S, R, tps, d_ff):
    t = pl.program_id(0)
    tm = S * R
    pad = SUBLANES
    half = oa_ref.shape[-1]
    mix = _dot(oa_ref[...], wo_ref[:half, :]) + _dot(ob_ref[...], wo_ref[half:, :])
    x1 = x_ref[...] + _rms(mix) * nmp_ref[...]
    u2 = (_rms(x1) * nfp_ref[...]).astype(BF16)

    if tps > 1:
        @pl.when(t % tps == 0)
        def _():
            carry[...] = prev_ref[0]

    fc = FFN_COLS
    for c in range(d_ff // fc):
        cols = slice(c * fc, (c + 1) * fc)
        g = _dot(u2, wg_ref[:, cols])
        up = _dot(u2, wu_ref[:, cols])
        if tps > 1:
            gbuf[0, 0:pad, :] = carry[:, cols]
        else:
            gbuf[:, 0:pad, :] = prev_ref[:, :, cols]
        gbuf[:, pad:pad + R, :] = g.reshape(S, R, fc)
        cw = cw_ref[:, cols]
        conv = cb_ref[:, cols] + cw[0:1] * gbuf[:, pad - 2:pad - 2 + R, :]
        for i in range(1, FFN_CONV):
            conv = conv + cw[i:i + 1] * gbuf[:, pad - 2 + i:pad - 2 + i + R, :]
        hid = (_gelu_tanh(conv) * up.reshape(S, R, fc)).reshape(tm, fc).astype(BF16)
        last = gbuf[:, R:R + pad, :]
        st_ref[:, :, cols] = last
        if tps > 1:
            carry[:, cols] = last[0]
        part = _dot(hid, wd_ref[cols, :])
        if c == 0:
            acc[...] = part
        else:
            acc[...] += part
    y_ref[...] = x1 + _rms(acc[...]) * nfo_ref[...]


def _mix_ffn(x, oa, ob, wo, nmp, nfp, wg, wu, cw, cb, wd, nfo, prev, *, S, R, tps):
    M, D = x.shape
    d_ff = wg.shape[1]
    tm = S * R
    n_seq = prev.shape[0]
    row = lambda w: pl.BlockSpec((tm, w), lambda t: (t, 0))
    st = pl.BlockSpec((S, SUBLANES, d_ff), lambda t: (t // tps, 0, 0))
    consts = [wo, nmp, nfp, wg, wu, cw, cb, wd, nfo]
    return pl.pallas_call(
        functools.partial(_mix_ffn_kernel, S=S, R=R, tps=tps, d_ff=d_ff),
        name="mix_ffn",
        grid=(M // tm,),
        in_specs=[row(D), row(oa.shape[1]), row(ob.shape[1])] + [_const_spec(a.shape) for a in consts] + [st],
        out_specs=[row(D), st],
        out_shape=[jax.ShapeDtypeStruct((M, D), F32),
                   jax.ShapeDtypeStruct((n_seq, SUBLANES, d_ff), F32)],
        scratch_shapes=[pltpu.VMEM((S, SUBLANES + R, FFN_COLS), F32),
                        pltpu.VMEM((SUBLANES, d_ff), F32),
                        pltpu.VMEM((tm, D), F32)],
        compiler_params=pltpu.CompilerParams(
            dimension_semantics=("arbitrary",), vmem_limit_bytes=VMEM_LIMIT),
    )(x, oa, ob, *consts, prev)


def _pad_rows_front(a, rows):
    return jnp.pad(a, ((0, 0), (rows - a.shape[1], 0), (0, 0)))


def _lane_row(vals, offset):
    return jnp.zeros((1, LANES), F32).at[0, offset:offset + vals.shape[0]].set(vals.astype(F32))


def _layer(xp, xs, cache_k, cache_v, s_delta, s_qkv, s_ffn, lw):
    (norm_mix_pre, w_in, qkv_conv_w, a_log, dt_bias, gdn_norm_w, rel_bias, attn_norm_w, w_out,
     norm_mix_post, norm_ffn_pre, w_gate_up, ffn_conv_w, ffn_conv_b, w_down, norm_ffn_post) = lw
    Bp, Tp, D = xp.shape
    Bs, Ts, _ = xs.shape
    d_ff = w_down.shape[0]

    c1 = GDN_QKV + GDN_Z
    c2 = c1 + 2 * GDN_HEADS
    wa = w_in[:, :c1].astype(BF16)
    wg_in = jnp.pad(w_in[:, c1:c2], ((0, 0), (0, LANES - 2 * GDN_HEADS))).astype(BF16)
    wb = w_in[:, c2:].astype(BF16)
    nmix = norm_mix_pre.reshape(1, D)
    cw_qkv = jnp.pad(qkv_conv_w, ((0, SUBLANES - GDN_CONV), (0, 0)))
    alog = _lane_row(a_log, GDN_HEADS)
    dtb = _lane_row(dt_bias, GDN_HEADS)
    gnw = gdn_norm_w.reshape(1, GDN_DV)
    anw = jnp.tile(attn_norm_w, ATT_HEADS).reshape(1, ATT_W)
    lane_head = jnp.arange(ATT_W) // ATT_DH
    seg = jnp.where(lane_head[:, None] == lane_head[None, :], 1.0 / ATT_DH, 0.0).astype(BF16)
    nb = BAND_CHUNKS + 1
    rel = WINDOW + jnp.arange(CHUNK)[:, None] - jnp.arange(nb * CHUNK)[None, :]
    bias = rel_bias[:, jnp.clip(rel, -MAX_REL, MAX_REL) + MAX_REL].astype(F32)
    wo = w_out.astype(BF16)
    w_gate = w_gate_up[:, :d_ff].astype(BF16)
    w_up = w_gate_up[:, d_ff:].astype(BF16)
    wd = w_down.astype(BF16)
    cw_ffn = jnp.pad(ffn_conv_w, ((0, SUBLANES - FFN_CONV), (0, 0)))
    cb = ffn_conv_b.reshape(1, d_ff)
    nmp = norm_mix_post.reshape(1, D)
    nfp = norm_ffn_pre.reshape(1, D)
    nfo = norm_ffn_post.reshape(1, D)

    def group(x, keep_all, gdn_prev, gdn_s0, gdn_c, gdn_cps, gdn_bs, ffn_prev, S, R, tps, attn):
        B, T, _ = x.shape
        xi = x if not keep_all else x.reshape(1, B * T, D)
        qkv, z, ba, q, k, v, kf, vf = _inproj(xi, nmix, wa, wg_in, wb, keep_all=keep_all)
        rs = lambda a: a.reshape(B, T, a.shape[-1])
        qkv, z, ba, q, k, v = map(rs, (qkv, z, ba, q, k, v))
        oa, s_new = _gdn(qkv, z, ba, _pad_rows_front(gdn_prev, SUBLANES), gdn_s0, cw_qkv, alog, dtb, gnw,
                         C=gdn_c, cps=gdn_cps, bs=gdn_bs)
        ob = attn(q, k, v)
        y, st = _mix_ffn(x.reshape(B * T, D), oa.reshape(B * T, GDN_Z), ob.reshape(B * T, ATT_W),
                         wo, nmp, nfp, w_gate, w_up, cw_ffn, cb, wd, nfo,
                         _pad_rows_front(ffn_prev, SUBLANES), S=S, R=R, tps=tps)
        keep = kf.shape[1] if not keep_all else T
        k_rows = kf.reshape(B, keep, ATT_HEADS, ATT_DH)
        v_rows = vf.reshape(B, keep, ATT_HEADS, ATT_DH)
        qkv_state = qkv[:, T - (GDN_CONV - 1):, :]
        ffn_state = st[:, SUBLANES - (FFN_CONV - 1):, :]
        return y.reshape(B, T, D), k_rows, v_rows, s_new, qkv_state, ffn_state

    out_p = group(
        xp, False, jnp.zeros((Bp, GDN_CONV - 1, GDN_QKV), F32),
        jnp.zeros((Bp, GDN_HEADS, GDN_DK, GDN_DV), F32), CHUNK, 4, 1,
        jnp.zeros((Bp, FFN_CONV - 1, d_ff), F32), 1, ROW_TILE, Tp // ROW_TILE,
        lambda q, k, v: _attn_prompt(q, k, v, bias, anw, seg, qt=4))

    lc = cache_k.shape[1]
    kc = cache_k.reshape(Bs, lc, ATT_W)
    vc = cache_v.reshape(Bs, lc, ATT_W)
    bias_c = bias[:, :Ts, :lc]
    bias_n = bias[:, :Ts, lc:lc + Ts]
    out_s = group(
        xs, True, s_qkv, s_delta, Ts, 1, 4, s_ffn, ROW_TILE // Ts, Ts, 1,
        lambda q, k, v: _attn_sample(q, kc, vc, k, v, bias_c, bias_n, anw, seg))
    return out_p, out_s


def kernel(x_prompt, x_sample, cache_band_k, cache_band_v, state_delta, state_qkv_conv, state_ffn_conv, norm_mix_pre, w_in, qkv_conv_w, a_log, dt_bias, gdn_norm_w, rel_bias, attn_norm_w, w_out, norm_mix_post, norm_ffn_pre, w_gate_up, ffn_conv_w, ffn_conv_b, w_down, norm_ffn_post):
    weights = (norm_mix_pre, w_in, qkv_conv_w, a_log, dt_bias, gdn_norm_w, rel_bias, attn_norm_w, w_out,
               norm_mix_post, norm_ffn_pre, w_gate_up, ffn_conv_w, ffn_conv_b, w_down, norm_ffn_post)
    depth = w_in.shape[0]
    xp, xs = x_prompt, x_sample
    outs_p, outs_s = [], []
    for l in range(depth):
        lw = tuple(w[l] for w in weights)
        op, os_ = _layer(xp, xs, cache_band_k[l], cache_band_v[l], state_delta[l], state_qkv_conv[l],
                         state_ffn_conv[l], lw)
        xp, xs = op[0], os_[0]
        outs_p.append(op[1:])
        outs_s.append(os_[1:])
    stack = lambda outs, i: jnp.stack([o[i] for o in outs], axis=0)
    return (xp, xs) + tuple(stack(outs_p, i) for i in range(5)) + tuple(stack(outs_s, i) for i in range(5))
```

```python
import functools

import jax
import jax.numpy as jnp
from jax import lax
from jax.experimental import pallas as pl
from jax.experimental.pallas import tpu as pltpu

F32 = jnp.float32
BF16 = jnp.bfloat16

EPS = 1e-6
CHUNK = 64
GDN_HEADS = 4
GDN_DK = 128
GDN_DV = 128
GDN_CONV = 4
ATT_HEADS = 8
ATT_DH = 64
BAND_CHUNKS = 8
WINDOW = BAND_CHUNKS * CHUNK
MAX_REL = 128
FFN_CONV = 3

GDN_QK = GDN_HEADS * GDN_DK
GDN_QKV = GDN_HEADS * (2 * GDN_DK + GDN_DV)
GDN_Z = GDN_HEADS * GDN_DV
ATT_W = ATT_HEADS * ATT_DH
LANES = 128
SUBLANES = 8
VMEM_LIMIT = 56 * 1024 * 1024
ROW_TILE = 512
FFN_COLS = 256
LOG2E = 1.4426950408889634
Q_SCALE = ATT_DH ** -0.5 * LOG2E


def _dot(a, b):
    return jnp.dot(a, b, preferred_element_type=F32)


def _dot_nt(a, b):
    return lax.dot_general(a, b, (((1,), (1,)), ((), ())), preferred_element_type=F32)


def _dot_tn(a, b):
    return lax.dot_general(a, b, (((0,), (0,)), ((), ())), preferred_element_type=F32)


def _split3(x):
    x1 = x.astype(BF16)
    r1 = x - x1.astype(F32)
    x2 = r1.astype(BF16)
    x3 = (r1 - x2.astype(F32)).astype(BF16)
    return x1, x2, x3


def _dot_exact_lhs(a16, x):
    x1, x2, x3 = _split3(x)
    return _dot(a16, x1) + _dot(a16, x2) + _dot(a16, x3)


def _dot_exact_rhs(x, b16):
    x1, x2, x3 = _split3(x)
    return _dot(x1, b16) + _dot(x2, b16) + _dot(x3, b16)


def _sigmoid(x):
    return 1.0 / (1.0 + jnp.exp(-x))


def _softplus(x):
    return jnp.maximum(x, 0.0) + jnp.log(1.0 + jnp.exp(-jnp.abs(x)))


def _rms(x):
    return x * lax.rsqrt(jnp.mean(x * x, axis=-1, keepdims=True) + EPS)


def _gelu_tanh(x):
    c = 0.7978845608028654
    return 0.5 * x * (1.0 + jnp.tanh(c * (x + 0.044715 * (x * x * x))))


def _const_spec(shape):
    n = len(shape)
    return pl.BlockSpec(shape, lambda *_: (0,) * n, pipeline_mode=pl.Buffered(1))


def _inproj_kernel(x_ref, nw_ref, wa_ref, wg_ref, wqk_ref, wv_ref, wvt_ref,
                   qkv_ref, z_ref, ba_ref, q_ref, k_ref, vt_ref, kf_ref, vf_ref,
                   *, keep_all, n_tiles):
    x = x_ref[0]
    tm = x.shape[0]
    u = (_rms(x) * nw_ref[...]).astype(BF16)
    qkv_ref[0] = _dot(u, wa_ref[:, :GDN_QKV])
    z_ref[0] = _dot(u, wa_ref[:, GDN_QKV:])
    ba_ref[0] = _dot(u, wg_ref[...])
    q_ref[0] = (_dot(u, wqk_ref[:, :ATT_W]) * Q_SCALE).astype(BF16)
    k = _dot(u, wqk_ref[:, ATT_W:])
    k_ref[0] = k.astype(BF16)
    vt = _dot_nt(wvt_ref[...], u)
    for jb in range(tm // LANES):
        vt_ref[0, jb] = vt[:, jb * LANES:(jb + 1) * LANES].astype(BF16)

    def keep():
        kf_ref[0] = k
        vf_ref[0] = _dot(u, wv_ref[...])

    if keep_all:
        keep()
    else:
        pl.when(pl.program_id(1) == n_tiles - 1)(keep)


def _inproj(x, nw, wa, wg, wqk, wv, wvt, *, keep_all):
    B, T, D = x.shape
    tm = ROW_TILE
    nt = T // tm
    keep_rows = T if keep_all else tm
    row = lambda w: pl.BlockSpec((1, tm, w), lambda b, i: (b, i, 0))
    keep = (pl.BlockSpec((1, tm, ATT_W), lambda b, i: (b, i, 0)) if keep_all
            else pl.BlockSpec((1, tm, ATT_W), lambda b, i: (b, 0, 0)))
    consts = [nw, wa, wg, wqk, wv, wvt]
    return pl.pallas_call(
        functools.partial(_inproj_kernel, keep_all=keep_all, n_tiles=nt),
        name="inproj",
        grid=(B, nt),
        in_specs=[row(D)] + [_const_spec(a.shape) for a in consts],
        out_specs=[row(GDN_QKV), row(GDN_Z), row(LANES), row(ATT_W), row(ATT_W),
                   pl.BlockSpec((1, tm // LANES, ATT_W, LANES), lambda b, i: (b, i, 0, 0)), keep, keep],
        out_shape=[jax.ShapeDtypeStruct((B, T, GDN_QKV), F32),
                   jax.ShapeDtypeStruct((B, T, GDN_Z), F32),
                   jax.ShapeDtypeStruct((B, T, LANES), F32),
                   jax.ShapeDtypeStruct((B, T, ATT_W), BF16),
                   jax.ShapeDtypeStruct((B, T, ATT_W), BF16),
                   jax.ShapeDtypeStruct((B, T // LANES, ATT_W, LANES), BF16),
                   jax.ShapeDtypeStruct((B, keep_rows, ATT_W), F32),
                   jax.ShapeDtypeStruct((B, keep_rows, ATT_W), F32)],
        compiler_params=pltpu.CompilerParams(
            dimension_semantics=("arbitrary", "arbitrary"), vmem_limit_bytes=VMEM_LIMIT),
    )(x, *consts)


def _bmm(a, b):
    return lax.dot_general(a, b, (((2,), (1,)), ((0,), (0,))), preferred_element_type=F32)


def _bmm_nt(a, b):
    return lax.dot_general(a, b, (((2,), (2,)), ((0,), (0,))), preferred_element_type=F32)


def _bmm_tn(a, b):
    return lax.dot_general(a, b, (((1,), (1,)), ((0,), (0,))), preferred_element_type=F32)


def _gdn_kernel(qkv_ref, z_ref, ba_ref, prev_ref, s0_ref, cw_ref, alog_ref, dtb_ref, gnw_ref,
                o_ref, sout_ref, xbuf, s_scr, *, C, cps, bs, n_steps):
    j = pl.program_id(1)
    R = C * cps
    H = GDN_HEADS
    nh = bs * H
    pad = SUBLANES

    @pl.when(j == 0)
    def _():
        xbuf[:, 0:pad, :] = prev_ref[...]
        s_scr[...] = s0_ref[...].reshape(nh, GDN_DK, GDN_DV)

    xbuf[:, pad:pad + R, :] = qkv_ref[...]

    ri = lax.broadcasted_iota(jnp.int32, (C, C), 0)
    ci = lax.broadcasted_iota(jnp.int32, (C, C), 1)
    incl = ri >= ci
    strict = ri > ci
    rr = lax.broadcasted_iota(jnp.int32, (R, R), 0)
    cc = lax.broadcasted_iota(jnp.int32, (R, R), 1)
    cum16 = jnp.where((rr >= cc) & (rr // C == cc // C), 1.0, 0.0).astype(BF16)
    cw = cw_ref[...]
    alog = alog_ref[...]
    dtb = dtb_ref[...]

    acts, sigs, Gs, GTs = [], [], [], []
    for s in range(bs):
        conv = cw[0:1] * xbuf[s, pad - 3:pad - 3 + R, :]
        for i in range(1, GDN_CONV):
            conv = conv + cw[i:i + 1] * xbuf[s, pad - 3 + i:pad - 3 + i + R, :]
        acts.append(conv * _sigmoid(conv))
        ba = ba_ref[s]
        sigs.append(_sigmoid(ba))
        G = _dot_exact_lhs(cum16, -jnp.exp(alog) * _softplus(ba + dtb))
        Gs.append(G)
        GTs.append(G.T)
    xbuf[:, 0:pad, :] = xbuf[:, R:R + pad, :]

    order = [(c, s, h) for c in range(cps) for s in range(bs) for h in range(H)]
    rows = lambda c: slice(c * C, (c + 1) * C)

    def tiles(slabs, col0, width):
        return jnp.stack([slabs[s][rows(c), col0 + h * width:col0 + (h + 1) * width] for c, s, h in order])

    q = tiles(acts, 0, GDN_DK)
    k = tiles(acts, GDN_QK, GDN_DK)
    v = tiles(acts, 2 * GDN_QK, GDN_DV)
    beta = tiles(sigs, 0, 1)
    Gc = tiles(Gs, H, 1)
    Gr = jnp.stack([GTs[s][H + h:H + h + 1, rows(c)] for c, s, h in order])
    Gl = Gc[:, C - 1:C, :]

    qn = q * lax.rsqrt(jnp.sum(q * q, axis=-1, keepdims=True) + EPS) * (GDN_DK ** -0.5)
    kn = k * lax.rsqrt(jnp.sum(k * k, axis=-1, keepdims=True) + EPS)
    eG = jnp.exp(Gc)
    gam = jnp.where(incl, jnp.exp(jnp.where(incl, Gc - Gr, 0.0)), 0.0)
    kb = kn * beta
    kn16 = kn.astype(BF16)
    aq = _bmm_nt(jnp.concatenate([kb, qn], axis=1).astype(BF16), kn16)
    A = jnp.where(strict, aq[:, :C] * gam, 0.0)
    QK16 = (aq[:, C:] * gam).astype(BF16)

    n_joint = C.bit_length() - 2
    A16 = A.astype(BF16)
    N = -A
    Q = _bmm(A16, A16)
    for it in range(n_joint):
        Q16 = Q.astype(BF16)
        if it == n_joint - 1:
            N = N + Q + _bmm(N.astype(BF16), Q16)
        else:
            nq = _bmm(jnp.concatenate([N, Q], axis=1).astype(BF16), Q16)
            N = N + Q + nq[:, :C]
            Q = nq[:, C:]
    rhs = jnp.concatenate([v * beta, kb * eG], axis=-1)
    sol = rhs + _bmm(N.astype(BF16), rhs.astype(BF16))
    u = sol[:, :, :GDN_DV]
    wq16 = jnp.concatenate([sol[:, :, GDN_DV:], qn * eG], axis=1).astype(BF16)
    kg16 = (kn * jnp.exp(Gl - Gc)).astype(BF16)
    dl = jnp.exp(Gl)

    S = s_scr[...]
    o_parts = []
    for c in range(cps):
        sl = slice(c * nh, (c + 1) * nh)
        r = _bmm(wq16[sl], S.astype(BF16))
        vn16 = (u[sl] - r[:, :C]).astype(BF16)
        o_parts.append(r[:, C:] + _bmm(QK16[sl], vn16))
        S = S * dl[sl] + _bmm_tn(kg16[sl], vn16)
    s_scr[...] = S

    on = _rms(jnp.concatenate(o_parts, axis=0)) * gnw_ref[...]
    gates = []
    for s in range(bs):
        zs = z_ref[s]
        gates.append(zs * _sigmoid(zs))
    for idx, (c, s, h) in enumerate(order):
        cols = slice(h * GDN_DV, (h + 1) * GDN_DV)
        o_ref[s, rows(c), cols] = (on[idx] * gates[s][rows(c), cols]).astype(o_ref.dtype)

    @pl.when(j == n_steps - 1)
    def _():
        sout_ref[...] = S.reshape(bs, H, GDN_DK, GDN_DV)


def _gdn(qkv, z, ba, prev, s0, cw, alog, dtb, gnw, *, C, cps, bs):
    B, T, _ = qkv.shape
    R = C * cps
    n_steps = T // R
    row = lambda w: pl.BlockSpec((bs, R, w), lambda b, j: (b, j, 0))
    st = pl.BlockSpec((bs, GDN_HEADS, GDN_DK, GDN_DV), lambda b, j: (b, 0, 0, 0))
    return pl.pallas_call(
        functools.partial(_gdn_kernel, C=C, cps=cps, bs=bs, n_steps=n_steps),
        name="gdn",
        grid=(B // bs, n_steps),
        in_specs=[row(GDN_QKV), row(GDN_Z), row(LANES),
                  pl.BlockSpec((bs, SUBLANES, GDN_QKV), lambda b, j: (b, 0, 0)), st,
                  _const_spec(cw.shape), _const_spec(alog.shape), _const_spec(dtb.shape),
                  _const_spec(gnw.shape)],
        out_specs=[row(GDN_Z), st],
        out_shape=[jax.ShapeDtypeStruct((B, T, GDN_Z), BF16),
                   jax.ShapeDtypeStruct((B, GDN_HEADS, GDN_DK, GDN_DV), F32)],
        scratch_shapes=[pltpu.VMEM((bs, SUBLANES + R, GDN_QKV), F32),
                        pltpu.VMEM((bs * GDN_HEADS, GDN_DK, GDN_DV), F32)],
        compiler_params=pltpu.CompilerParams(
            dimension_semantics=("arbitrary", "arbitrary"), vmem_limit_bytes=VMEM_LIMIT),
    )(qkv, z, ba, prev, s0, cw, alog, dtb, gnw)


def _head_norm(o, nw, seg16):
    ms = _dot_exact_rhs(o * o, seg16)
    return o * lax.rsqrt(ms + EPS) * nw


def _attn_pair(qp, blocks, p):
    rows = qp.shape[0]
    lane_head = lax.broadcasted_iota(jnp.int32, (rows, LANES), 1) // ATT_DH
    o_pair = None
    for e in range(2):
        h = 2 * p + e
        qe = jnp.where(lane_head == e, qp, jnp.zeros_like(qp))
        ss = [_dot_nt(qe, k16) + bias_fn(h) for k16, _, bias_fn in blocks]
        m = functools.reduce(jnp.maximum, [jnp.max(s, axis=-1, keepdims=True) for s in ss])
        ps = [jnp.exp2(s - m) for s in ss]
        l = functools.reduce(jnp.add, [jnp.sum(pe, axis=-1, keepdims=True) for pe in ps])
        acc = functools.reduce(jnp.add, [_dot(pe.astype(BF16), blk[1]) for pe, blk in zip(ps, blocks)])
        oe = acc * (1.0 / l)
        o_pair = oe if e == 0 else jnp.where(lane_head == 0, o_pair, oe)
    return o_pair


def _attn_prompt_kernel(q_ref, k_ref, vt_ref, bias_ref, nw_ref, o_ref, *, qt):
    j = pl.program_id(1)
    tq = 2 * CHUNK
    n_kb = BAND_CHUNKS // 2 + 1
    n_pairs = ATT_HEADS // 2
    lane_head = lax.broadcasted_iota(jnp.int32, (tq, LANES), 1) // ATT_DH
    for i in range(qt):
        m = j * qt + i
        q = q_ref[0, i * tq:(i + 1) * tq, :]
        wt = []
        for p in range(n_pairs):
            qp = q[:, p * LANES:(p + 1) * LANES]
            zero = jnp.zeros_like(qp)
            wt.append(jnp.concatenate([jnp.where(lane_head == 0, qp, zero),
                                       jnp.where(lane_head == 1, qp, zero)], axis=0))
        wt = jnp.stack(wt)
        s_parts, v_parts = [], []
        for jb in range(n_kb):
            first = m - (n_kb - 1) + jb
            blk = jnp.maximum(first, 0)
            kb = k_ref[0, blk]
            kj = jnp.stack([kb[:, p * LANES:(p + 1) * LANES] for p in range(n_pairs)])
            s = _bmm_nt(kj, wt) + bias_ref[:, jb * tq:(jb + 1) * tq, :]
            if jb < n_kb - 1:
                s = jnp.where(first >= 0, s, -jnp.inf)
            s_parts.append(s)
            v_parts.append(vt_ref[0, blk].reshape(n_pairs, LANES, tq))
        st = jnp.concatenate(s_parts, axis=1)
        pt = jnp.exp2(st - jnp.max(st, axis=1, keepdims=True))
        inv = 1.0 / jnp.sum(pt, axis=1, keepdims=True)
        pt16 = pt.astype(BF16)
        acc = _bmm(v_parts[0], pt16[:, 0:tq, :])
        for jb in range(1, n_kb):
            acc = acc + _bmm(v_parts[jb], pt16[:, jb * tq:(jb + 1) * tq, :])
        ot = jnp.concatenate([acc[:, :ATT_DH, :tq] * inv[:, :, :tq],
                              acc[:, ATT_DH:, tq:] * inv[:, :, tq:]], axis=1)
        oh = ot.reshape(ATT_HEADS, ATT_DH, tq)
        on = oh * lax.rsqrt(jnp.mean(oh * oh, axis=1, keepdims=True) + EPS) * nw_ref[...]
        on = on.reshape(n_pairs, LANES, tq)
        for p in range(n_pairs):
            o_ref[0, i * tq:(i + 1) * tq, p * LANES:(p + 1) * LANES] = on[p].T.astype(o_ref.dtype)


def _attn_prompt(q, k, vt, bias, nw, *, qt):
    B, T, _ = q.shape
    tq = 2 * CHUNK
    n_steps = T // (qt * tq)
    k4 = k.reshape(B, T // tq, tq, ATT_W)
    row = pl.BlockSpec((1, qt * tq, ATT_W), lambda b, j: (b, j, 0))
    return pl.pallas_call(
        functools.partial(_attn_prompt_kernel, qt=qt),
        name="attn_prompt",
        grid=(B, n_steps),
        in_specs=[row,
                  pl.BlockSpec((1, T // tq, tq, ATT_W), lambda b, j: (b, 0, 0, 0)),
                  pl.BlockSpec((1, T // tq, ATT_W, tq), lambda b, j: (b, 0, 0, 0)),
                  _const_spec(bias.shape), _const_spec(nw.shape)],
        out_specs=row,
        out_shape=jax.ShapeDtypeStruct((B, T, ATT_W), BF16),
        compiler_params=pltpu.CompilerParams(
            dimension_semantics=("arbitrary", "arbitrary"), vmem_limit_bytes=VMEM_LIMIT),
    )(q, k4, vt, bias, nw)


def _attn_sample_kernel(q_ref, kc_ref, vc_ref, kn_ref, vn_ref, bc_ref, bn_ref, nw_ref, seg_ref, o_ref):
    outs = []
    for p in range(ATT_HEADS // 2):
        cols = slice(p * LANES, (p + 1) * LANES)
        blocks = [(kc_ref[0, :, cols].astype(BF16), vc_ref[0, :, cols].astype(BF16), lambda h: bc_ref[h]),
                  (kn_ref[0, :, cols], vn_ref[0, :, cols].astype(BF16), lambda h: bn_ref[h])]
        outs.append(_attn_pair(q_ref[0, :, cols], blocks, p))
    o = jnp.concatenate(outs, axis=-1)
    o_ref[0] = _head_norm(o, nw_ref[...], seg_ref[...]).astype(o_ref.dtype)


def _attn_sample(q, kc, vc, kn, vn, bias_c, bias_n, nw, seg):
    B, T, _ = q.shape
    lc = kc.shape[1]
    new = pl.BlockSpec((1, T, ATT_W), lambda b: (b, 0, 0))
    cache = pl.BlockSpec((1, lc, ATT_W), lambda b: (b, 0, 0))
    return pl.pallas_call(
        _attn_sample_kernel,
        name="attn_sample",
        grid=(B,),
        in_specs=[new, cache, cache, new, new, _const_spec(bias_c.shape), _const_spec(bias_n.shape),
                  _const_spec(nw.shape), _const_spec(seg.shape)],
        out_specs=new,
        out_shape=jax.ShapeDtypeStruct((B, T, ATT_W), BF16),
        compiler_params=pltpu.CompilerParams(
            dimension_semantics=("arbitrary",), vmem_limit_bytes=VMEM_LIMIT),
    )(q, kc, vc, kn, vn, bias_c, bias_n, nw, seg)


def _mix_ffn_kernel(x_ref, oa_ref, ob_ref, wo_ref, nmp_ref, nfp_ref, wg_ref, wu_ref, cw_ref, cb_ref,
                    wd_ref, nfo_ref, prev_ref, y_ref, st_ref, gbuf, carry, acc,
                    *, S, R, tps, d_ff):
    t = pl.program_id(0)
    tm = S * R
    pad = SUBLANES
    half = oa_ref.shape[-1]
    mix = _dot(oa_ref[...], wo_ref[:half, :]) + _dot(ob_ref[...], wo_ref[half:, :])
    x1 = x_ref[...] + _rms(mix) * nmp_ref[...]
    u2 = (_rms(x1) * nfp_ref[...]).astype(BF16)

    if tps > 1:
        @pl.when(t % tps == 0)
        def _():
            carry[...] = prev_ref[0]

    fc = FFN_COLS
    for c in range(d_ff // fc):
        cols = slice(c * fc, (c + 1) * fc)
        g = _dot(u2, wg_ref[:, cols])
        up = _dot(u2, wu_ref[:, cols])
        if tps > 1:
            gbuf[0, 0:pad, :] = carry[:, cols]
        else:
            gbuf[:, 0:pad, :] = prev_ref[:, :, cols]
        gbuf[:, pad:pad + R, :] = g.reshape(S, R, fc)
        cw = cw_ref[:, cols]
        conv = cb_ref[:, cols] + cw[0:1] * gbuf[:, pad - 2:pad - 2 + R, :]
        for i in range(1, FFN_CONV):
            conv = conv + cw[i:i + 1] * gbuf[:, pad - 2 + i:pad - 2 + i + R, :]
        hid = (_gelu_tanh(conv) * up.reshape(S, R, fc)).reshape(tm, fc).astype(BF16)
        last = gbuf[:, R:R + pad, :]
        st_ref[:, :, cols] = last
        if tps > 1:
            carry[:, cols] = last[0]
        part = _dot(hid, wd_ref[cols, :])
        if c == 0:
            acc[...] = part
        else:
            acc[...] += part
    y_ref[...] = x1 + _rms(acc[...]) * nfo_ref[...]


def _mix_ffn(x, oa, ob, wo, nmp, nfp, wg, wu, cw, cb, wd, nfo, prev, *, S, R, tps):
    M, D = x.shape
    d_ff = wg.shape[1]
    tm = S * R
    n_seq = prev.shape[0]
    row = lambda w: pl.BlockSpec((tm, w), lambda t: (t, 0))
    st = pl.BlockSpec((S, SUBLANES, d_ff), lambda t: (t // tps, 0, 0))
    consts = [wo, nmp, nfp, wg, wu, cw, cb, wd, nfo]
    return pl.pallas_call(
        functools.partial(_mix_ffn_kernel, S=S, R=R, tps=tps, d_ff=d_ff),
        name="mix_ffn",
        grid=(M // tm,),
        in_specs=[row(D), row(oa.shape[1]), row(ob.shape[1])] + [_const_spec(a.shape) for a in consts] + [st],
        out_specs=[row(D), st],
        out_shape=[jax.ShapeDtypeStruct((M, D), F32),
                   jax.ShapeDtypeStruct((n_seq, SUBLANES, d_ff), F32)],
        scratch_shapes=[pltpu.VMEM((S, SUBLANES + R, FFN_COLS), F32),
                        pltpu.VMEM((SUBLANES, d_ff), F32),
                        pltpu.VMEM((tm, D), F32)],
        compiler_params=pltpu.CompilerParams(
            dimension_semantics=("arbitrary",), vmem_limit_bytes=VMEM_LIMIT),
    )(x, oa, ob, *consts, prev)


def _pad_rows_front(a, rows):
    return jnp.pad(a, ((0, 0), (rows - a.shape[1], 0), (0, 0)))


def _lane_row(vals, offset):
    return jnp.zeros((1, LANES), F32).at[0, offset:offset + vals.shape[0]].set(vals.astype(F32))


def _toeplitz_bias(table):
    n_heads = table.shape[0]
    nk = (BAND_CHUNKS + 2) * CHUNK
    nq = 2 * CHUNK
    period = nk + nq
    z = jnp.arange(period)
    per_dist = table[:, jnp.clip(WINDOW + (nq - 1) - z, -MAX_REL, MAX_REL) + MAX_REL]
    skew = jnp.tile(per_dist, (1, nk + 1))[:, :nk * (period + 1)].reshape(n_heads, nk, period + 1)
    return skew[:, :, :nq][:, :, ::-1]


def _prompt_bias(toep):
    n_heads, nk, nq = toep.shape
    key_in_band = jnp.arange(nk)[:, None] - (jnp.arange(nq)[None, :] // CHUNK) * CHUNK
    valid = (key_in_band >= 0) & (key_in_band < (BAND_CHUNKS + 1) * CHUNK)
    b = jnp.where(valid[None], toep, -jnp.inf)
    return b.reshape(n_heads // 2, 2, nk, nq).transpose(0, 2, 1, 3).reshape(n_heads // 2, nk, 2 * nq)


def _layer(xp, xs, cache_k, cache_v, s_delta, s_qkv, s_ffn, lw):
    (norm_mix_pre, w_in, qkv_conv_w, a_log, dt_bias, gdn_norm_w, rel_bias, attn_norm_w, w_out,
     norm_mix_post, norm_ffn_pre, w_gate_up, ffn_conv_w, ffn_conv_b, w_down, norm_ffn_post) = lw
    Bp, Tp, D = xp.shape
    Bs, Ts, _ = xs.shape
    d_ff = w_down.shape[0]

    c1 = GDN_QKV + GDN_Z
    c2 = c1 + 2 * GDN_HEADS
    wa = w_in[:, :c1].astype(BF16)
    wg_in = jnp.pad(w_in[:, c1:c2], ((0, 0), (0, LANES - 2 * GDN_HEADS))).astype(BF16)
    wqk = w_in[:, c2:c2 + 2 * ATT_W].astype(BF16)
    wv = w_in[:, c2 + 2 * ATT_W:].astype(BF16)
    wvt = wv.T
    nmix = norm_mix_pre.reshape(1, D)
    cw_qkv = jnp.pad(qkv_conv_w, ((0, SUBLANES - GDN_CONV), (0, 0)))
    alog = _lane_row(a_log, GDN_HEADS)
    dtb = _lane_row(dt_bias, GDN_HEADS)
    gnw = gdn_norm_w.reshape(1, GDN_DV)
    anw = jnp.tile(attn_norm_w, ATT_HEADS).reshape(1, ATT_W)
    lane_head = jnp.arange(ATT_W) // ATT_DH
    seg = jnp.where(lane_head[:, None] == lane_head[None, :], 1.0 / ATT_DH, 0.0).astype(BF16)
    anw_col = jnp.broadcast_to(attn_norm_w.astype(F32)[:, None], (ATT_DH, 2 * CHUNK))
    toep = _toeplitz_bias(rel_bias.astype(F32)) * LOG2E
    bias_p = _prompt_bias(toep)
    wo = w_out.astype(BF16)
    w_gate = w_gate_up[:, :d_ff].astype(BF16)
    w_up = w_gate_up[:, d_ff:].astype(BF16)
    wd = w_down.astype(BF16)
    cw_ffn = jnp.pad(ffn_conv_w, ((0, SUBLANES - FFN_CONV), (0, 0)))
    cb = ffn_conv_b.reshape(1, d_ff)
    nmp = norm_mix_post.reshape(1, D)
    nfp = norm_ffn_pre.reshape(1, D)
    nfo = norm_ffn_post.reshape(1, D)

    def group(x, keep_all, gdn_prev, gdn_s0, gdn_c, gdn_cps, gdn_bs, ffn_prev, S, R, tps, attn):
        B, T, _ = x.shape
        xi = x if not keep_all else x.reshape(1, B * T, D)
        qkv, z, ba, q, k, vt, kf, vf = _inproj(xi, nmix, wa, wg_in, wqk, wv, wvt, keep_all=keep_all)
        rs = lambda a: a.reshape(B, T, a.shape[-1])
        qkv, z, ba, q, k = map(rs, (qkv, z, ba, q, k))
        oa, s_new = _gdn(qkv, z, ba, _pad_rows_front(gdn_prev, SUBLANES), gdn_s0, cw_qkv, alog, dtb, gnw,
                         C=gdn_c, cps=gdn_cps, bs=gdn_bs)
        ob = attn(q, k, vt, vf)
        y, st = _mix_ffn(x.reshape(B * T, D), oa.reshape(B * T, GDN_Z), ob.reshape(B * T, ATT_W),
                         wo, nmp, nfp, w_gate, w_up, cw_ffn, cb, wd, nfo,
                         _pad_rows_front(ffn_prev, SUBLANES), S=S, R=R, tps=tps)
        keep = kf.shape[1] if not keep_all else T
        k_rows = kf.reshape(B, keep, ATT_HEADS, ATT_DH)
        v_rows = vf.reshape(B, keep, ATT_HEADS, ATT_DH)
        qkv_state = qkv[:, T - (GDN_CONV - 1):, :]
        ffn_state = st[:, SUBLANES - (FFN_CONV - 1):, :]
        return y.reshape(B, T, D), k_rows, v_rows, s_new, qkv_state, ffn_state

    out_p = group(
        xp, False, jnp.zeros((Bp, GDN_CONV - 1, GDN_QKV), F32),
        jnp.zeros((Bp, GDN_HEADS, GDN_DK, GDN_DV), F32), CHUNK, 4, 1,
        jnp.zeros((Bp, FFN_CONV - 1, d_ff), F32), 1, ROW_TILE, Tp // ROW_TILE,
        lambda q, k, vt, vf: _attn_prompt(q, k, vt, bias_p, anw_col, qt=2))

    lc = cache_k.shape[1]
    assert lc == WINDOW and Ts <= 2 * CHUNK
    kc = cache_k.reshape(Bs, lc, ATT_W)
    vc = cache_v.reshape(Bs, lc, ATT_W)
    bias_s = jnp.swapaxes(toep[:, :lc + Ts, :Ts], 1, 2)
    bias_c = bias_s[:, :, :lc]
    bias_n = bias_s[:, :, lc:]
    out_s = group(
        xs, True, s_qkv, s_delta, Ts, 1, 4, s_ffn, ROW_TILE // Ts, Ts, 1,
        lambda q, k, vt, vf: _attn_sample(q, kc, vc, k, vf.reshape(Bs, Ts, ATT_W), bias_c, bias_n, anw, seg))
    return out_p, out_s


def kernel(x_prompt, x_sample, cache_band_k, cache_band_v, state_delta, state_qkv_conv, state_ffn_conv, norm_mix_pre, w_in, qkv_conv_w, a_log, dt_bias, gdn_norm_w, rel_bias, attn_norm_w, w_out, norm_mix_post, norm_ffn_pre, w_gate_up, ffn_conv_w, ffn_conv_b, w_down, norm_ffn_post):
    weights = (norm_mix_pre, w_in, qkv_conv_w, a_log, dt_bias, gdn_norm_w, rel_bias, attn_norm_w, w_out,
               norm_mix_post, norm_ffn_pre, w_gate_up, ffn_conv_w, ffn_conv_b, w_down, norm_ffn_post)
    depth = w_in.shape[0]
    xp, xs = x_prompt, x_sample
    outs_p, outs_s = [], []
    for l in range(depth):
        lw = tuple(w[l] for w in weights)
        op, os_ = _layer(xp, xs, cache_band_k[l], cache_band_v[l], state_delta[l], state_qkv_conv[l],
                         state_ffn_conv[l], lw)
        xp, xs = op[0], os_[0]
        outs_p.append(op[1:])
        outs_s.append(os_[1:])
    stack = lambda outs, i: jnp.stack([o[i] for o in outs], axis=0)
    return (xp, xs) + tuple(stack(outs_p, i) for i in range(5)) + tuple(stack(outs_s, i) for i in range(5))
```

```python
import functools

import jax
import jax.numpy as jnp
from jax import lax
from jax.experimental import pallas as pl
from jax.experimental.pallas import tpu as pltpu

F32 = jnp.float32
BF16 = jnp.bfloat16

EPS = 1e-6
CHUNK = 64
GDN_HEADS = 4
GDN_DK = 128
GDN_DV = 128
GDN_CONV = 4
ATT_HEADS = 8
ATT_DH = 64
BAND_CHUNKS = 8
WINDOW = BAND_CHUNKS * CHUNK
MAX_REL = 128
FFN_CONV = 3

GDN_QK = GDN_HEADS * GDN_DK
GDN_QKV = GDN_HEADS * (2 * GDN_DK + GDN_DV)
GDN_Z = GDN_HEADS * GDN_DV
ATT_W = ATT_HEADS * ATT_DH
LANES = 128
SUBLANES = 8
VMEM_LIMIT = 56 * 1024 * 1024
ROW_TILE = 512
FFN_COLS = 256
LOG2E = 1.4426950408889634
Q_SCALE = ATT_DH ** -0.5 * LOG2E


def _dot(a, b):
    return jnp.dot(a, b, preferred_element_type=F32)


def _dot_nt(a, b):
    return lax.dot_general(a, b, (((1,), (1,)), ((), ())), preferred_element_type=F32)


def _dot_tn(a, b):
    return lax.dot_general(a, b, (((0,), (0,)), ((), ())), preferred_element_type=F32)


def _split3(x):
    x1 = x.astype(BF16)
    r1 = x - x1.astype(F32)
    x2 = r1.astype(BF16)
    x3 = (r1 - x2.astype(F32)).astype(BF16)
    return x1, x2, x3


def _dot_exact_lhs(a16, x):
    x1, x2, x3 = _split3(x)
    return _dot(a16, x1) + _dot(a16, x2) + _dot(a16, x3)


def _dot_exact_rhs(x, b16):
    x1, x2, x3 = _split3(x)
    return _dot(x1, b16) + _dot(x2, b16) + _dot(x3, b16)


def _sigmoid(x):
    return 1.0 / (1.0 + jnp.exp(-x))


def _softplus(x):
    return jnp.maximum(x, 0.0) + jnp.log(1.0 + jnp.exp(-jnp.abs(x)))


def _rms(x):
    return x * lax.rsqrt(jnp.mean(x * x, axis=-1, keepdims=True) + EPS)


def _gelu_tanh(x):
    c = 0.7978845608028654
    return 0.5 * x * (1.0 + jnp.tanh(c * (x + 0.044715 * (x * x * x))))


def _const_spec(shape):
    n = len(shape)
    return pl.BlockSpec(shape, lambda *_: (0,) * n, pipeline_mode=pl.Buffered(1))


def _inproj_kernel(x_ref, nw_ref, wa_ref, wg_ref, wqk_ref, wv_ref, wvt_ref,
                   qkv_ref, z_ref, ba_ref, q_ref, k_ref, vt_ref, kf_ref, vf_ref,
                   *, keep_all, n_tiles):
    x = x_ref[0]
    tm = x.shape[0]
    u = (_rms(x) * nw_ref[...]).astype(BF16)
    qkv_ref[0] = _dot(u, wa_ref[:, :GDN_QKV])
    z_ref[0] = _dot(u, wa_ref[:, GDN_QKV:])
    ba_ref[0] = _dot(u, wg_ref[...])
    q_ref[0] = (_dot(u, wqk_ref[:, :ATT_W]) * Q_SCALE).astype(BF16)
    k = _dot(u, wqk_ref[:, ATT_W:])
    k_ref[0] = k.astype(BF16)
    vt = _dot_nt(wvt_ref[...], u)
    for jb in range(tm // LANES):
        vt_ref[0, jb] = vt[:, jb * LANES:(jb + 1) * LANES].astype(BF16)

    def keep():
        kf_ref[0] = k
        vf_ref[0] = _dot(u, wv_ref[...])

    if keep_all:
        keep()
    else:
        pl.when(pl.program_id(1) == n_tiles - 1)(keep)


def _inproj(x, nw, wa, wg, wqk, wv, wvt, *, keep_all):
    B, T, D = x.shape
    tm = ROW_TILE
    nt = T // tm
    keep_rows = T if keep_all else tm
    row = lambda w: pl.BlockSpec((1, tm, w), lambda b, i: (b, i, 0))
    keep = (pl.BlockSpec((1, tm, ATT_W), lambda b, i: (b, i, 0)) if keep_all
            else pl.BlockSpec((1, tm, ATT_W), lambda b, i: (b, 0, 0)))
    consts = [nw, wa, wg, wqk, wv, wvt]
    return pl.pallas_call(
        functools.partial(_inproj_kernel, keep_all=keep_all, n_tiles=nt),
        name="inproj",
        grid=(B, nt),
        in_specs=[row(D)] + [_const_spec(a.shape) for a in consts],
        out_specs=[row(GDN_QKV), row(GDN_Z), row(LANES), row(ATT_W), row(ATT_W),
                   pl.BlockSpec((1, tm // LANES, ATT_W, LANES), lambda b, i: (b, i, 0, 0)), keep, keep],
        out_shape=[jax.ShapeDtypeStruct((B, T, GDN_QKV), F32),
                   jax.ShapeDtypeStruct((B, T, GDN_Z), F32),
                   jax.ShapeDtypeStruct((B, T, LANES), F32),
                   jax.ShapeDtypeStruct((B, T, ATT_W), BF16),
                   jax.ShapeDtypeStruct((B, T, ATT_W), BF16),
                   jax.ShapeDtypeStruct((B, T // LANES, ATT_W, LANES), BF16),
                   jax.ShapeDtypeStruct((B, keep_rows, ATT_W), F32),
                   jax.ShapeDtypeStruct((B, keep_rows, ATT_W), F32)],
        compiler_params=pltpu.CompilerParams(
            dimension_semantics=("arbitrary", "arbitrary"), vmem_limit_bytes=VMEM_LIMIT),
    )(x, *consts)


def _bmm(a, b):
    return lax.dot_general(a, b, (((2,), (1,)), ((0,), (0,))), preferred_element_type=F32)


def _bmm_nt(a, b):
    return lax.dot_general(a, b, (((2,), (2,)), ((0,), (0,))), preferred_element_type=F32)


def _bmm_tn(a, b):
    return lax.dot_general(a, b, (((1,), (1,)), ((0,), (0,))), preferred_element_type=F32)


def _gdn_kernel(qkv_ref, z_ref, ba_ref, prev_ref, s0_ref, cw_ref, alog_ref, dtb_ref, gnw_ref,
                o_ref, sout_ref, xbuf, s_scr, *, C, cps, bs, n_steps):
    j = pl.program_id(1)
    R = C * cps
    H = GDN_HEADS
    nh = bs * H
    pad = SUBLANES

    @pl.when(j == 0)
    def _():
        xbuf[:, 0:pad, :] = prev_ref[...]
        s_scr[...] = s0_ref[...].reshape(nh, GDN_DK, GDN_DV)

    xbuf[:, pad:pad + R, :] = qkv_ref[...]

    ri = lax.broadcasted_iota(jnp.int32, (C, C), 0)
    ci = lax.broadcasted_iota(jnp.int32, (C, C), 1)
    incl = ri >= ci
    strict = ri > ci
    rr = lax.broadcasted_iota(jnp.int32, (R, R), 0)
    cc = lax.broadcasted_iota(jnp.int32, (R, R), 1)
    cum16 = jnp.where((rr >= cc) & (rr // C == cc // C), 1.0, 0.0).astype(BF16)
    cw = cw_ref[...]
    alog = alog_ref[...]
    dtb = dtb_ref[...]

    acts, sigs, Gs, GTs = [], [], [], []
    for s in range(bs):
        conv = cw[0:1] * xbuf[s, pad - 3:pad - 3 + R, :]
        for i in range(1, GDN_CONV):
            conv = conv + cw[i:i + 1] * xbuf[s, pad - 3 + i:pad - 3 + i + R, :]
        acts.append(conv * _sigmoid(conv))
        ba = ba_ref[s]
        sigs.append(_sigmoid(ba))
        G = _dot_exact_lhs(cum16, -jnp.exp(alog) * _softplus(ba + dtb))
        Gs.append(G)
        GTs.append(G.T)
    xbuf[:, 0:pad, :] = xbuf[:, R:R + pad, :]

    order = [(c, s, h) for c in range(cps) for s in range(bs) for h in range(H)]
    rows = lambda c: slice(c * C, (c + 1) * C)

    def tiles(slabs, col0, width):
        return jnp.stack([slabs[s][rows(c), col0 + h * width:col0 + (h + 1) * width] for c, s, h in order])

    q = tiles(acts, 0, GDN_DK)
    k = tiles(acts, GDN_QK, GDN_DK)
    v = tiles(acts, 2 * GDN_QK, GDN_DV)
    beta = tiles(sigs, 0, 1)
    Gc = tiles(Gs, H, 1)
    Gr = jnp.stack([GTs[s][H + h:H + h + 1, rows(c)] for c, s, h in order])
    Gl = Gc[:, C - 1:C, :]

    qn = q * lax.rsqrt(jnp.sum(q * q, axis=-1, keepdims=True) + EPS) * (GDN_DK ** -0.5)
    kn = k * lax.rsqrt(jnp.sum(k * k, axis=-1, keepdims=True) + EPS)
    eG = jnp.exp(Gc)
    gam = jnp.where(incl, jnp.exp(jnp.where(incl, Gc - Gr, 0.0)), 0.0)
    kb = kn * beta
    kn16 = kn.astype(BF16)
    aq = _bmm_nt(jnp.concatenate([kb, qn], axis=1).astype(BF16), kn16)
    A = jnp.where(strict, aq[:, :C] * gam, 0.0)
    QK16 = (aq[:, C:] * gam).astype(BF16)

    n_joint = C.bit_length() - 2
    A16 = A.astype(BF16)
    N = -A
    Q = _bmm(A16, A16)
    for it in range(n_joint):
        Q16 = Q.astype(BF16)
        if it == n_joint - 1:
            N = N + Q + _bmm(N.astype(BF16), Q16)
        else:
            nq = _bmm(jnp.concatenate([N, Q], axis=1).astype(BF16), Q16)
            N = N + Q + nq[:, :C]
            Q = nq[:, C:]
    rhs = jnp.concatenate([v * beta, kb * eG], axis=-1)
    sol = rhs + _bmm(N.astype(BF16), rhs.astype(BF16))
    u = sol[:, :, :GDN_DV]
    wq16 = jnp.concatenate([sol[:, :, GDN_DV:], qn * eG], axis=1).astype(BF16)
    kg16 = (kn * jnp.exp(Gl - Gc)).astype(BF16)
    dl = jnp.exp(Gl)

    S = s_scr[...]
    o_parts = []
    for c in range(cps):
        sl = slice(c * nh, (c + 1) * nh)
        r = _bmm(wq16[sl], S.astype(BF16))
        vn16 = (u[sl] - r[:, :C]).astype(BF16)
        o_parts.append(r[:, C:] + _bmm(QK16[sl], vn16))
        S = S * dl[sl] + _bmm_tn(kg16[sl], vn16)
    s_scr[...] = S

    on = _rms(jnp.concatenate(o_parts, axis=0)) * gnw_ref[...]
    gates = []
    for s in range(bs):
        zs = z_ref[s]
        gates.append(zs * _sigmoid(zs))
    for idx, (c, s, h) in enumerate(order):
        cols = slice(h * GDN_DV, (h + 1) * GDN_DV)
        o_ref[s, rows(c), cols] = (on[idx] * gates[s][rows(c), cols]).astype(o_ref.dtype)

    @pl.when(j == n_steps - 1)
    def _():
        sout_ref[...] = S.reshape(bs, H, GDN_DK, GDN_DV)


def _gdn(qkv, z, ba, prev, s0, cw, alog, dtb, gnw, *, C, cps, bs):
    B, T, _ = qkv.shape
    R = C * cps
    n_steps = T // R
    row = lambda w: pl.BlockSpec((bs, R, w), lambda b, j: (b, j, 0))
    st = pl.BlockSpec((bs, GDN_HEADS, GDN_DK, GDN_DV), lambda b, j: (b, 0, 0, 0))
    return pl.pallas_call(
        functools.partial(_gdn_kernel, C=C, cps=cps, bs=bs, n_steps=n_steps),
        name="gdn",
        grid=(B // bs, n_steps),
        in_specs=[row(GDN_QKV), row(GDN_Z), row(LANES),
                  pl.BlockSpec((bs, SUBLANES, GDN_QKV), lambda b, j: (b, 0, 0)), st,
                  _const_spec(cw.shape), _const_spec(alog.shape), _const_spec(dtb.shape),
                  _const_spec(gnw.shape)],
        out_specs=[row(GDN_Z), st],
        out_shape=[jax.ShapeDtypeStruct((B, T, GDN_Z), BF16),
                   jax.ShapeDtypeStruct((B, GDN_HEADS, GDN_DK, GDN_DV), F32)],
        scratch_shapes=[pltpu.VMEM((bs, SUBLANES + R, GDN_QKV), F32),
                        pltpu.VMEM((bs * GDN_HEADS, GDN_DK, GDN_DV), F32)],
        compiler_params=pltpu.CompilerParams(
            dimension_semantics=("arbitrary", "arbitrary"), vmem_limit_bytes=VMEM_LIMIT),
    )(qkv, z, ba, prev, s0, cw, alog, dtb, gnw)


def _head_norm(o, nw, seg16):
    ms = _dot_exact_rhs(o * o, seg16)
    return o * lax.rsqrt(ms + EPS) * nw


def _attn_pair(qp, blocks, p):
    rows = qp.shape[0]
    lane_head = lax.broadcasted_iota(jnp.int32, (rows, LANES), 1) // ATT_DH
    o_pair = None
    for e in range(2):
        h = 2 * p + e
        qe = jnp.where(lane_head == e, qp, jnp.zeros_like(qp))
        ss = [_dot_nt(qe, k16) + bias_fn(h) for k16, _, bias_fn in blocks]
        m = functools.reduce(jnp.maximum, [jnp.max(s, axis=-1, keepdims=True) for s in ss])
        ps = [jnp.exp2(s - m) for s in ss]
        l = functools.reduce(jnp.add, [jnp.sum(pe, axis=-1, keepdims=True) for pe in ps])
        acc = functools.reduce(jnp.add, [_dot(pe.astype(BF16), blk[1]) for pe, blk in zip(ps, blocks)])
        oe = acc * (1.0 / l)
        o_pair = oe if e == 0 else jnp.where(lane_head == 0, o_pair, oe)
    return o_pair


def _attn_prompt_kernel(q_ref, k_ref, vt_ref, bias_ref, nw_ref, o_ref, *, qt):
    j = pl.program_id(1)
    tq = 2 * CHUNK
    n_kb = BAND_CHUNKS // 2 + 1
    n_pairs = ATT_HEADS // 2
    lane_head = lax.broadcasted_iota(jnp.int32, (tq, LANES), 1) // ATT_DH
    for i in range(qt):
        m = j * qt + i
        q = q_ref[0, i * tq:(i + 1) * tq, :]
        wt = []
        for p in range(n_pairs):
            qp = q[:, p * LANES:(p + 1) * LANES]
            zero = jnp.zeros_like(qp)
            wt.append(jnp.concatenate([jnp.where(lane_head == 0, qp, zero),
                                       jnp.where(lane_head == 1, qp, zero)], axis=0))
        wt = jnp.stack(wt)
        s_parts, v_parts = [], []
        for jb in range(n_kb):
            first = m - (n_kb - 1) + jb
            blk = jnp.maximum(first, 0)
            kb = k_ref[0, blk]
            kj = jnp.stack([kb[:, p * LANES:(p + 1) * LANES] for p in range(n_pairs)])
            s = _bmm_nt(kj, wt) + bias_ref[:, jb * tq:(jb + 1) * tq, :]
            if jb < n_kb - 1:
                s = jnp.where(first >= 0, s, -jnp.inf)
            s_parts.append(s)
            v_parts.append(vt_ref[0, blk].reshape(n_pairs, LANES, tq))
        st = jnp.concatenate(s_parts, axis=1)
        pt = jnp.exp2(st - jnp.max(st, axis=1, keepdims=True))
        inv = 1.0 / jnp.sum(pt, axis=1, keepdims=True)
        pt16 = pt.astype(BF16)
        acc = _bmm(v_parts[0], pt16[:, 0:tq, :])
        for jb in range(1, n_kb):
            acc = acc + _bmm(v_parts[jb], pt16[:, jb * tq:(jb + 1) * tq, :])
        ot = jnp.concatenate([acc[:, :ATT_DH, :tq] * inv[:, :, :tq],
                              acc[:, ATT_DH:, tq:] * inv[:, :, tq:]], axis=1)
        oh = ot.reshape(ATT_HEADS, ATT_DH, tq)
        on = oh * lax.rsqrt(jnp.mean(oh * oh, axis=1, keepdims=True) + EPS) * nw_ref[...]
        on = on.reshape(n_pairs, LANES, tq)
        for p in range(n_pairs):
            o_ref[0, i * tq:(i + 1) * tq, p * LANES:(p + 1) * LANES] = on[p].T.astype(o_ref.dtype)


def _attn_prompt(q, k, vt, bias, nw, *, qt):
    B, T, _ = q.shape
    tq = 2 * CHUNK
    n_steps = T // (qt * tq)
    k4 = k.reshape(B, T // tq, tq, ATT_W)
    row = pl.BlockSpec((1, qt * tq, ATT_W), lambda b, j: (b, j, 0))
    return pl.pallas_call(
        functools.partial(_attn_prompt_kernel, qt=qt),
        name="attn_prompt",
        grid=(B, n_steps),
        in_specs=[row,
                  pl.BlockSpec((1, T // tq, tq, ATT_W), lambda b, j: (b, 0, 0, 0)),
                  pl.BlockSpec((1, T // tq, ATT_W, tq), lambda b, j: (b, 0, 0, 0)),
                  _const_spec(bias.shape), _const_spec(nw.shape)],
        out_specs=row,
        out_shape=jax.ShapeDtypeStruct((B, T, ATT_W), BF16),
        compiler_params=pltpu.CompilerParams(
            dimension_semantics=("arbitrary", "arbitrary"), vmem_limit_bytes=VMEM_LIMIT),
    )(q, k4, vt, bias, nw)


def _attn_sample_kernel(q_ref, kc_ref, vc_ref, kn_ref, vn_ref, bc_ref, bn_ref, nw_ref, seg_ref, o_ref):
    outs = []
    for p in range(ATT_HEADS // 2):
        cols = slice(p * LANES, (p + 1) * LANES)
        blocks = [(kc_ref[0, :, cols].astype(BF16), vc_ref[0, :, cols].astype(BF16), lambda h: bc_ref[h]),
                  (kn_ref[0, :, cols], vn_ref[0, :, cols].astype(BF16), lambda h: bn_ref[h])]
        outs.append(_attn_pair(q_ref[0, :, cols], blocks, p))
    o = jnp.concatenate(outs, axis=-1)
    o_ref[0] = _head_norm(o, nw_ref[...], seg_ref[...]).astype(o_ref.dtype)


def _attn_sample(q, kc, vc, kn, vn, bias_c, bias_n, nw, seg):
    B, T, _ = q.shape
    lc = kc.shape[1]
    new = pl.BlockSpec((1, T, ATT_W), lambda b: (b, 0, 0))
    cache = pl.BlockSpec((1, lc, ATT_W), lambda b: (b, 0, 0))
    return pl.pallas_call(
        _attn_sample_kernel,
        name="attn_sample",
        grid=(B,),
        in_specs=[new, cache, cache, new, new, _const_spec(bias_c.shape), _const_spec(bias_n.shape),
                  _const_spec(nw.shape), _const_spec(seg.shape)],
        out_specs=new,
        out_shape=jax.ShapeDtypeStruct((B, T, ATT_W), BF16),
        compiler_params=pltpu.CompilerParams(
            dimension_semantics=("arbitrary",), vmem_limit_bytes=VMEM_LIMIT),
    )(q, kc, vc, kn, vn, bias_c, bias_n, nw, seg)


def _delayed(g, k, S, R, hist_row):
    rolled = pltpu.roll(g, k, axis=0)
    sub = lax.broadcasted_iota(jnp.int32, (SUBLANES, g.shape[1]), 0)
    parts = []
    for s in range(S):
        head = rolled[s * R:s * R + SUBLANES]
        for i in range(k):
            head = jnp.where(sub == i, hist_row(s, i), head)
        parts += [head, rolled[s * R + SUBLANES:(s + 1) * R]]
    return jnp.concatenate(parts, axis=0)


def _mix_ffn_kernel(x_ref, oa_ref, ob_ref, wo_ref, nmp_ref, nfp_ref, wg_ref, wu_ref, cw_ref, cb_ref,
                    wd_ref, nfo_ref, prev_ref, y_ref, st_ref, carry,
                    *, S, R, tps, d_ff):
    t = pl.program_id(0)
    hist = FFN_CONV - 1
    half = oa_ref.shape[-1]
    mix = _dot(oa_ref[...], wo_ref[:half, :]) + _dot(ob_ref[...], wo_ref[half:, :])
    x1 = x_ref[...] + _rms(mix) * nmp_ref[...]
    u2 = (_rms(x1) * nfp_ref[...]).astype(BF16)

    if tps > 1:
        @pl.when(t % tps == 0)
        def _():
            for i in range(hist):
                carry[i:i + 1, :] = prev_ref[0, i]

    fc = FFN_COLS
    n_fc = d_ff // fc
    acc = None
    proj = lambda c: (_dot(u2, wg_ref[:, c * fc:(c + 1) * fc]), _dot(u2, wu_ref[:, c * fc:(c + 1) * fc]))
    nxt = proj(0)
    for c in range(n_fc):
        cols = slice(c * fc, (c + 1) * fc)
        g, up = nxt
        if c + 1 < n_fc:
            nxt = proj(c + 1)
        if tps > 1:
            history = lambda s, i: carry[i:i + 1, cols]
        else:
            history = lambda s, i: prev_ref[0, i, s:s + 1, cols]
        cw = cw_ref[:, cols]
        conv = cb_ref[:, cols] + cw[hist:hist + 1] * g
        for k in range(1, FFN_CONV):
            conv = conv + cw[hist - k:hist - k + 1] * _delayed(
                g, k, S, R, lambda s, i, k=k: history(s, hist - k + i))
        for s in range(S):
            for i in range(hist):
                row = g[(s + 1) * R - hist + i:(s + 1) * R - hist + i + 1]
                if tps > 1:
                    carry[i:i + 1, cols] = row
                else:
                    st_ref[0, i, s:s + 1, cols] = row
        hid = (_gelu_tanh(conv) * up).astype(BF16)
        part = _dot(hid, wd_ref[cols, :])
        acc = part if acc is None else acc + part
    if tps > 1:
        for i in range(hist):
            st_ref[0, i] = carry[i:i + 1, :]
    y_ref[...] = x1 + _rms(acc) * nfo_ref[...]


def _mix_ffn(x, oa, ob, wo, nmp, nfp, wg, wu, cw, cb, wd, nfo, prev, *, S, R):
    M, D = x.shape
    d_ff = wg.shape[1]
    tm = S * R
    n_seq = prev.shape[0]
    hist = FFN_CONV - 1
    tps = M // (n_seq * R)
    assert S == 1 or tps == 1
    grouped = (n_seq // S, hist, S, d_ff)
    prev_g = prev.reshape(n_seq // S, S, hist, d_ff).transpose(0, 2, 1, 3)
    row = lambda w: pl.BlockSpec((tm, w), lambda t: (t, 0))
    st = pl.BlockSpec((1,) + grouped[1:], lambda t: (t // tps, 0, 0, 0))
    consts = [wo, nmp, nfp, wg, wu, cw, cb, wd, nfo]
    y, st_g = pl.pallas_call(
        functools.partial(_mix_ffn_kernel, S=S, R=R, tps=tps, d_ff=d_ff),
        name="mix_ffn",
        grid=(M // tm,),
        in_specs=[row(D), row(oa.shape[1]), row(ob.shape[1])] + [_const_spec(a.shape) for a in consts] + [st],
        out_specs=[row(D), st],
        out_shape=[jax.ShapeDtypeStruct((M, D), F32), jax.ShapeDtypeStruct(grouped, F32)],
        scratch_shapes=[pltpu.VMEM((hist, d_ff), F32)],
        compiler_params=pltpu.CompilerParams(
            dimension_semantics=("arbitrary",), vmem_limit_bytes=VMEM_LIMIT),
    )(x, oa, ob, *consts, prev_g)
    return y, st_g.transpose(0, 2, 1, 3).reshape(n_seq, hist, d_ff)


def _pad_rows_front(a, rows):
    return jnp.pad(a, ((0, 0), (rows - a.shape[1], 0), (0, 0)))


def _lane_row(vals, offset):
    return jnp.zeros((1, LANES), F32).at[0, offset:offset + vals.shape[0]].set(vals.astype(F32))


def _toeplitz_bias(table):
    n_heads = table.shape[0]
    nk = (BAND_CHUNKS + 2) * CHUNK
    nq = 2 * CHUNK
    period = nk + nq
    z = jnp.arange(period)
    per_dist = table[:, jnp.clip(WINDOW + (nq - 1) - z, -MAX_REL, MAX_REL) + MAX_REL]
    skew = jnp.tile(per_dist, (1, nk + 1))[:, :nk * (period + 1)].reshape(n_heads, nk, period + 1)
    return skew[:, :, :nq][:, :, ::-1]


def _prompt_bias(toep):
    n_heads, nk, nq = toep.shape
    key_in_band = jnp.arange(nk)[:, None] - (jnp.arange(nq)[None, :] // CHUNK) * CHUNK
    valid = (key_in_band >= 0) & (key_in_band < (BAND_CHUNKS + 1) * CHUNK)
    b = jnp.where(valid[None], toep, -jnp.inf)
    return b.reshape(n_heads // 2, 2, nk, nq).transpose(0, 2, 1, 3).reshape(n_heads // 2, nk, 2 * nq)


def _layer(xp, xs, cache_k, cache_v, s_delta, s_qkv, s_ffn, lw):
    (norm_mix_pre, w_in, qkv_conv_w, a_log, dt_bias, gdn_norm_w, rel_bias, attn_norm_w, w_out,
     norm_mix_post, norm_ffn_pre, w_gate_up, ffn_conv_w, ffn_conv_b, w_down, norm_ffn_post) = lw
    Bp, Tp, D = xp.shape
    Bs, Ts, _ = xs.shape
    d_ff = w_down.shape[0]

    c1 = GDN_QKV + GDN_Z
    c2 = c1 + 2 * GDN_HEADS
    wa = w_in[:, :c1].astype(BF16)
    wg_in = jnp.pad(w_in[:, c1:c2], ((0, 0), (0, LANES - 2 * GDN_HEADS))).astype(BF16)
    wqk = w_in[:, c2:c2 + 2 * ATT_W].astype(BF16)
    wv = w_in[:, c2 + 2 * ATT_W:].astype(BF16)
    wvt = wv.T
    nmix = norm_mix_pre.reshape(1, D)
    cw_qkv = jnp.pad(qkv_conv_w, ((0, SUBLANES - GDN_CONV), (0, 0)))
    alog = _lane_row(a_log, GDN_HEADS)
    dtb = _lane_row(dt_bias, GDN_HEADS)
    gnw = gdn_norm_w.reshape(1, GDN_DV)
    anw = jnp.tile(attn_norm_w, ATT_HEADS).reshape(1, ATT_W)
    lane_head = jnp.arange(ATT_W) // ATT_DH
    seg = jnp.where(lane_head[:, None] == lane_head[None, :], 1.0 / ATT_DH, 0.0).astype(BF16)
    anw_col = jnp.broadcast_to(attn_norm_w.astype(F32)[:, None], (ATT_DH, 2 * CHUNK))
    toep = _toeplitz_bias(rel_bias.astype(F32)) * LOG2E
    bias_p = _prompt_bias(toep)
    wo = w_out.astype(BF16)
    w_gate = w_gate_up[:, :d_ff].astype(BF16)
    w_up = w_gate_up[:, d_ff:].astype(BF16)
    wd = w_down.astype(BF16)
    cw_ffn = jnp.pad(ffn_conv_w, ((0, SUBLANES - FFN_CONV), (0, 0)))
    cb = ffn_conv_b.reshape(1, d_ff)
    nmp = norm_mix_post.reshape(1, D)
    nfp = norm_ffn_pre.reshape(1, D)
    nfo = norm_ffn_post.reshape(1, D)

    def group(x, keep_all, gdn_prev, gdn_s0, gdn_c, gdn_cps, gdn_bs, ffn_prev, S, R, attn):
        B, T, _ = x.shape
        xi = x if not keep_all else x.reshape(1, B * T, D)
        qkv, z, ba, q, k, vt, kf, vf = _inproj(xi, nmix, wa, wg_in, wqk, wv, wvt, keep_all=keep_all)
        rs = lambda a: a.reshape(B, T, a.shape[-1])
        qkv, z, ba, q, k = map(rs, (qkv, z, ba, q, k))
        oa, s_new = _gdn(qkv, z, ba, _pad_rows_front(gdn_prev, SUBLANES), gdn_s0, cw_qkv, alog, dtb, gnw,
                         C=gdn_c, cps=gdn_cps, bs=gdn_bs)
        ob = attn(q, k, vt, vf)
        y, ffn_state = _mix_ffn(x.reshape(B * T, D), oa.reshape(B * T, GDN_Z), ob.reshape(B * T, ATT_W),
                                wo, nmp, nfp, w_gate, w_up, cw_ffn, cb, wd, nfo, ffn_prev,
                                S=S, R=R)
        keep = kf.shape[1] if not keep_all else T
        k_rows = kf.reshape(B, keep, ATT_HEADS, ATT_DH)
        v_rows = vf.reshape(B, keep, ATT_HEADS, ATT_DH)
        qkv_state = qkv[:, T - (GDN_CONV - 1):, :]
        return y.reshape(B, T, D), k_rows, v_rows, s_new, qkv_state, ffn_state

    out_p = group(
        xp, False, jnp.zeros((Bp, GDN_CONV - 1, GDN_QKV), F32),
        jnp.zeros((Bp, GDN_HEADS, GDN_DK, GDN_DV), F32), CHUNK, 4, 1,
        jnp.zeros((Bp, FFN_CONV - 1, d_ff), F32), 1, ROW_TILE,
        lambda q, k, vt, vf: _attn_prompt(q, k, vt, bias_p, anw_col, qt=2))

    lc = cache_k.shape[1]
    assert lc == WINDOW and Ts <= 2 * CHUNK
    kc = cache_k.reshape(Bs, lc, ATT_W)
    vc = cache_v.reshape(Bs, lc, ATT_W)
    bias_s = jnp.swapaxes(toep[:, :lc + Ts, :Ts], 1, 2)
    bias_c = bias_s[:, :, :lc]
    bias_n = bias_s[:, :, lc:]
    out_s = group(
        xs, True, s_qkv, s_delta, Ts, 1, 4, s_ffn, ROW_TILE // Ts, Ts,
        lambda q, k, vt, vf: _attn_sample(q, kc, vc, k, vf.reshape(Bs, Ts, ATT_W), bias_c, bias_n, anw, seg))
    return out_p, out_s


def kernel(x_prompt, x_sample, cache_band_k, cache_band_v, state_delta, state_qkv_conv, state_ffn_conv, norm_mix_pre, w_in, qkv_conv_w, a_log, dt_bias, gdn_norm_w, rel_bias, attn_norm_w, w_out, norm_mix_post, norm_ffn_pre, w_gate_up, ffn_conv_w, ffn_conv_b, w_down, norm_ffn_post):
    weights = (norm_mix_pre, w_in, qkv_conv_w, a_log, dt_bias, gdn_norm_w, rel_bias, attn_norm_w, w_out,
               norm_mix_post, norm_ffn_pre, w_gate_up, ffn_conv_w, ffn_conv_b, w_down, norm_ffn_post)
    depth = w_in.shape[0]
    xp, xs = x_prompt, x_sample
    outs_p, outs_s = [], []
    for l in range(depth):
        lw = tuple(w[l] for w in weights)
        op, os_ = _layer(xp, xs, cache_band_k[l], cache_band_v[l], state_delta[l], state_qkv_conv[l],
                         state_ffn_conv[l], lw)
        xp, xs = op[0], os_[0]
        outs_p.append(op[1:])
        outs_s.append(os_[1:])
    stack = lambda outs, i: jnp.stack([o[i] for o in outs], axis=0)
    return (xp, xs) + tuple(stack(outs_p, i) for i in range(5)) + tuple(stack(outs_s, i) for i in range(5))
```

```python
import functools

import jax
import jax.numpy as jnp
from jax import lax
from jax.experimental import pallas as pl
from jax.experimental.pallas import tpu as pltpu

F32 = jnp.float32
BF16 = jnp.bfloat16

EPS = 1e-6
CHUNK = 64
GDN_HEADS = 4
GDN_DK = 128
GDN_DV = 128
GDN_CONV = 4
ATT_HEADS = 8
ATT_DH = 64
BAND_CHUNKS = 8
WINDOW = BAND_CHUNKS * CHUNK
MAX_REL = 128
FFN_CONV = 3

GDN_QK = GDN_HEADS * GDN_DK
GDN_QKV = GDN_HEADS * (2 * GDN_DK + GDN_DV)
GDN_Z = GDN_HEADS * GDN_DV
ATT_W = ATT_HEADS * ATT_DH
LANES = 128
SUBLANES = 8
VMEM_LIMIT = 56 * 1024 * 1024
ROW_TILE = 512
FFN_COLS = 256
LOG2E = 1.4426950408889634
Q_SCALE = ATT_DH ** -0.5 * LOG2E


def _dot(a, b):
    return jnp.dot(a, b, preferred_element_type=F32)


def _dot_nt(a, b):
    return lax.dot_general(a, b, (((1,), (1,)), ((), ())), preferred_element_type=F32)


def _dot_tn(a, b):
    return lax.dot_general(a, b, (((0,), (0,)), ((), ())), preferred_element_type=F32)


def _split3(x):
    x1 = x.astype(BF16)
    r1 = x - x1.astype(F32)
    x2 = r1.astype(BF16)
    x3 = (r1 - x2.astype(F32)).astype(BF16)
    return x1, x2, x3


def _dot_exact_lhs(a16, x):
    x1, x2, x3 = _split3(x)
    return _dot(a16, x1) + _dot(a16, x2) + _dot(a16, x3)


def _dot_exact_rhs(x, b16):
    x1, x2, x3 = _split3(x)
    return _dot(x1, b16) + _dot(x2, b16) + _dot(x3, b16)


def _sigmoid(x):
    return 1.0 / (1.0 + jnp.exp(-x))


def _softplus(x):
    return jnp.maximum(x, 0.0) + jnp.log(1.0 + jnp.exp(-jnp.abs(x)))


def _rms(x):
    return x * lax.rsqrt(jnp.mean(x * x, axis=-1, keepdims=True) + EPS)


def _gelu_tanh(x):
    c = 0.7978845608028654
    return 0.5 * x * (1.0 + jnp.tanh(c * (x + 0.044715 * (x * x * x))))


def _const_spec(shape):
    n = len(shape)
    return pl.BlockSpec(shape, lambda *_: (0,) * n, pipeline_mode=pl.Buffered(1))


def _inproj_kernel(x_ref, nw_ref, wa_ref, wg_ref, wqk_ref, wv_ref, wvt_ref,
                   qkv_ref, z_ref, ba_ref, q_ref, k_ref, vt_ref, kf_ref, vf_ref,
                   *, keep_all, n_tiles):
    x = x_ref[0]
    tm = x.shape[0]
    u = (_rms(x) * nw_ref[...]).astype(BF16)
    qkv_ref[0] = _dot(u, wa_ref[:, :GDN_QKV])
    z_ref[0] = _dot(u, wa_ref[:, GDN_QKV:])
    ba_ref[0] = _dot(u, wg_ref[...])
    q_ref[0] = (_dot(u, wqk_ref[:, :ATT_W]) * Q_SCALE).astype(BF16)
    k = _dot(u, wqk_ref[:, ATT_W:])
    k_ref[0] = k.astype(BF16)
    vt = _dot_nt(wvt_ref[...], u)
    for jb in range(tm // LANES):
        vt_ref[0, jb] = vt[:, jb * LANES:(jb + 1) * LANES].astype(BF16)

    def keep():
        kf_ref[0] = k
        vf_ref[0] = _dot(u, wv_ref[...])

    if keep_all:
        keep()
    else:
        pl.when(pl.program_id(1) == n_tiles - 1)(keep)


def _inproj(x, nw, wa, wg, wqk, wv, wvt, *, keep_all):
    B, T, D = x.shape
    tm = ROW_TILE
    nt = T // tm
    keep_rows = T if keep_all else tm
    row = lambda w: pl.BlockSpec((1, tm, w), lambda b, i: (b, i, 0))
    keep = (pl.BlockSpec((1, tm, ATT_W), lambda b, i: (b, i, 0)) if keep_all
            else pl.BlockSpec((1, tm, ATT_W), lambda b, i: (b, 0, 0)))
    consts = [nw, wa, wg, wqk, wv, wvt]
    return pl.pallas_call(
        functools.partial(_inproj_kernel, keep_all=keep_all, n_tiles=nt),
        name="inproj",
        grid=(B, nt),
        in_specs=[row(D)] + [_const_spec(a.shape) for a in consts],
        out_specs=[row(GDN_QKV), row(GDN_Z), row(LANES), row(ATT_W), row(ATT_W),
                   pl.BlockSpec((1, tm // LANES, ATT_W, LANES), lambda b, i: (b, i, 0, 0)), keep, keep],
        out_shape=[jax.ShapeDtypeStruct((B, T, GDN_QKV), F32),
                   jax.ShapeDtypeStruct((B, T, GDN_Z), F32),
                   jax.ShapeDtypeStruct((B, T, LANES), F32),
                   jax.ShapeDtypeStruct((B, T, ATT_W), BF16),
                   jax.ShapeDtypeStruct((B, T, ATT_W), BF16),
                   jax.ShapeDtypeStruct((B, T // LANES, ATT_W, LANES), BF16),
                   jax.ShapeDtypeStruct((B, keep_rows, ATT_W), F32),
                   jax.ShapeDtypeStruct((B, keep_rows, ATT_W), F32)],
        compiler_params=pltpu.CompilerParams(
            dimension_semantics=("arbitrary", "arbitrary"), vmem_limit_bytes=VMEM_LIMIT),
    )(x, *consts)


def _bmm(a, b):
    return lax.dot_general(a, b, (((2,), (1,)), ((0,), (0,))), preferred_element_type=F32)


def _bmm_nt(a, b):
    return lax.dot_general(a, b, (((2,), (2,)), ((0,), (0,))), preferred_element_type=F32)


def _bmm_tn(a, b):
    return lax.dot_general(a, b, (((1,), (1,)), ((0,), (0,))), preferred_element_type=F32)


def _gdn_kernel(qkv_ref, z_ref, ba_ref, prev_ref, s0_ref, cw_ref, alog_ref, dtb_ref, gnw_ref,
                o_ref, sout_ref, xbuf, s_scr, *, C, cps, bs, n_steps):
    j = pl.program_id(1)
    R = C * cps
    H = GDN_HEADS
    nh = bs * H
    pad = SUBLANES

    @pl.when(j == 0)
    def _():
        xbuf[:, 0:pad, :] = prev_ref[...]
        s_scr[...] = s0_ref[...].reshape(nh, GDN_DK, GDN_DV)

    xbuf[:, pad:pad + R, :] = qkv_ref[...]

    ri = lax.broadcasted_iota(jnp.int32, (C, C), 0)
    ci = lax.broadcasted_iota(jnp.int32, (C, C), 1)
    incl = ri >= ci
    strict = ri > ci
    rr = lax.broadcasted_iota(jnp.int32, (R, R), 0)
    cc = lax.broadcasted_iota(jnp.int32, (R, R), 1)
    cum16 = jnp.where((rr >= cc) & (rr // C == cc // C), 1.0, 0.0).astype(BF16)
    cw = cw_ref[...]
    alog = alog_ref[...]
    dtb = dtb_ref[...]

    acts, sigs, Gs, GTs = [], [], [], []
    for s in range(bs):
        conv = cw[0:1] * xbuf[s, pad - 3:pad - 3 + R, :]
        for i in range(1, GDN_CONV):
            conv = conv + cw[i:i + 1] * xbuf[s, pad - 3 + i:pad - 3 + i + R, :]
        acts.append(conv * _sigmoid(conv))
        ba = ba_ref[s]
        sigs.append(_sigmoid(ba))
        G = _dot_exact_lhs(cum16, -jnp.exp(alog) * _softplus(ba + dtb))
        Gs.append(G)
        GTs.append(G.T)
    xbuf[:, 0:pad, :] = xbuf[:, R:R + pad, :]

    order = [(c, s, h) for c in range(cps) for s in range(bs) for h in range(H)]
    rows = lambda c: slice(c * C, (c + 1) * C)

    def tiles(slabs, col0, width):
        return jnp.stack([slabs[s][rows(c), col0 + h * width:col0 + (h + 1) * width] for c, s, h in order])

    q = tiles(acts, 0, GDN_DK)
    k = tiles(acts, GDN_QK, GDN_DK)
    v = tiles(acts, 2 * GDN_QK, GDN_DV)
    beta = tiles(sigs, 0, 1)
    Gc = tiles(Gs, H, 1)
    Gr = jnp.stack([GTs[s][H + h:H + h + 1, rows(c)] for c, s, h in order])
    Gl = Gc[:, C - 1:C, :]

    qn = q * lax.rsqrt(jnp.sum(q * q, axis=-1, keepdims=True) + EPS) * (GDN_DK ** -0.5)
    kn = k * lax.rsqrt(jnp.sum(k * k, axis=-1, keepdims=True) + EPS)
    eG = jnp.exp(Gc)
    gam = jnp.where(incl, jnp.exp(jnp.where(incl, Gc - Gr, 0.0)), 0.0)
    kb = kn * beta
    kn16 = kn.astype(BF16)
    aq = _bmm_nt(jnp.concatenate([kb, qn], axis=1).astype(BF16), kn16)
    A = jnp.where(strict, aq[:, :C] * gam, 0.0)
    QK16 = (aq[:, C:] * gam).astype(BF16)

    n_joint = C.bit_length() - 2
    A16 = A.astype(BF16)
    N = -A
    Q = _bmm(A16, A16)
    for it in range(n_joint):
        Q16 = Q.astype(BF16)
        if it == n_joint - 1:
            N = N + Q + _bmm(N.astype(BF16), Q16)
        else:
            nq = _bmm(jnp.concatenate([N, Q], axis=1).astype(BF16), Q16)
            N = N + Q + nq[:, :C]
            Q = nq[:, C:]
    rhs = jnp.concatenate([v * beta, kb * eG], axis=-1)
    sol = rhs + _bmm(N.astype(BF16), rhs.astype(BF16))
    u = sol[:, :, :GDN_DV]
    wq16 = jnp.concatenate([sol[:, :, GDN_DV:], qn * eG], axis=1).astype(BF16)
    kg16 = (kn * jnp.exp(Gl - Gc)).astype(BF16)
    dl = jnp.exp(Gl)

    S = s_scr[...]
    o_parts = []
    for c in range(cps):
        sl = slice(c * nh, (c + 1) * nh)
        r = _bmm(wq16[sl], S.astype(BF16))
        vn16 = (u[sl] - r[:, :C]).astype(BF16)
        o_parts.append(r[:, C:] + _bmm(QK16[sl], vn16))
        S = S * dl[sl] + _bmm_tn(kg16[sl], vn16)
    s_scr[...] = S

    on = _rms(jnp.concatenate(o_parts, axis=0)) * gnw_ref[...]
    gates = []
    for s in range(bs):
        zs = z_ref[s]
        gates.append(zs * _sigmoid(zs))
    for idx, (c, s, h) in enumerate(order):
        cols = slice(h * GDN_DV, (h + 1) * GDN_DV)
        o_ref[s, rows(c), cols] = (on[idx] * gates[s][rows(c), cols]).astype(o_ref.dtype)

    @pl.when(j == n_steps - 1)
    def _():
        sout_ref[...] = S.reshape(bs, H, GDN_DK, GDN_DV)


def _gdn(qkv, z, ba, prev, s0, cw, alog, dtb, gnw, *, C, cps, bs):
    B, T, _ = qkv.shape
    R = C * cps
    n_steps = T // R
    row = lambda w: pl.BlockSpec((bs, R, w), lambda b, j: (b, j, 0))
    st = pl.BlockSpec((bs, GDN_HEADS, GDN_DK, GDN_DV), lambda b, j: (b, 0, 0, 0))
    return pl.pallas_call(
        functools.partial(_gdn_kernel, C=C, cps=cps, bs=bs, n_steps=n_steps),
        name="gdn",
        grid=(B // bs, n_steps),
        in_specs=[row(GDN_QKV), row(GDN_Z), row(LANES),
                  pl.BlockSpec((bs, SUBLANES, GDN_QKV), lambda b, j: (b, 0, 0)), st,
                  _const_spec(cw.shape), _const_spec(alog.shape), _const_spec(dtb.shape),
                  _const_spec(gnw.shape)],
        out_specs=[row(GDN_Z), st],
        out_shape=[jax.ShapeDtypeStruct((B, T, GDN_Z), BF16),
                   jax.ShapeDtypeStruct((B, GDN_HEADS, GDN_DK, GDN_DV), F32)],
        scratch_shapes=[pltpu.VMEM((bs, SUBLANES + R, GDN_QKV), F32),
                        pltpu.VMEM((bs * GDN_HEADS, GDN_DK, GDN_DV), F32)],
        compiler_params=pltpu.CompilerParams(
            dimension_semantics=("arbitrary", "arbitrary"), vmem_limit_bytes=VMEM_LIMIT),
    )(qkv, z, ba, prev, s0, cw, alog, dtb, gnw)


def _attn_prompt_kernel(q_ref, k_ref, vt_ref, bias_ref, nw_ref, o_ref, *, qt):
    j = pl.program_id(1)
    tq = 2 * CHUNK
    n_kb = BAND_CHUNKS // 2 + 1
    n_pairs = ATT_HEADS // 2
    lane_head = lax.broadcasted_iota(jnp.int32, (tq, LANES), 1) // ATT_DH
    for i in range(qt):
        m = j * qt + i
        q = q_ref[0, i * tq:(i + 1) * tq, :]
        wt = []
        for p in range(n_pairs):
            qp = q[:, p * LANES:(p + 1) * LANES]
            zero = jnp.zeros_like(qp)
            wt.append(jnp.concatenate([jnp.where(lane_head == 0, qp, zero),
                                       jnp.where(lane_head == 1, qp, zero)], axis=0))
        wt = jnp.stack(wt)
        s_parts, v_parts = [], []
        for jb in range(n_kb):
            first = m - (n_kb - 1) + jb
            blk = jnp.maximum(first, 0)
            kb = k_ref[0, blk]
            kj = jnp.stack([kb[:, p * LANES:(p + 1) * LANES] for p in range(n_pairs)])
            s = _bmm_nt(kj, wt) + bias_ref[:, jb * tq:(jb + 1) * tq, :]
            if jb < n_kb - 1:
                s = jnp.where(first >= 0, s, -jnp.inf)
            s_parts.append(s)
            v_parts.append(vt_ref[0, blk].reshape(n_pairs, LANES, tq))
        st = jnp.concatenate(s_parts, axis=1)
        pt = jnp.exp2(st - jnp.max(st, axis=1, keepdims=True))
        inv = 1.0 / jnp.sum(pt, axis=1, keepdims=True)
        pt16 = pt.astype(BF16)
        acc = _bmm(v_parts[0], pt16[:, 0:tq, :])
        for jb in range(1, n_kb):
            acc = acc + _bmm(v_parts[jb], pt16[:, jb * tq:(jb + 1) * tq, :])
        ot = jnp.concatenate([acc[:, :ATT_DH, :tq] * inv[:, :, :tq],
                              acc[:, ATT_DH:, tq:] * inv[:, :, tq:]], axis=1)
        oh = ot.reshape(ATT_HEADS, ATT_DH, tq)
        on = oh * lax.rsqrt(jnp.mean(oh * oh, axis=1, keepdims=True) + EPS) * nw_ref[...]
        on = on.reshape(n_pairs, LANES, tq)
        for p in range(n_pairs):
            o_ref[0, i * tq:(i + 1) * tq, p * LANES:(p + 1) * LANES] = on[p].T.astype(o_ref.dtype)


def _attn_prompt(q, k, vt, bias, nw, *, qt):
    B, T, _ = q.shape
    tq = 2 * CHUNK
    n_steps = T // (qt * tq)
    k4 = k.reshape(B, T // tq, tq, ATT_W)
    row = pl.BlockSpec((1, qt * tq, ATT_W), lambda b, j: (b, j, 0))
    return pl.pallas_call(
        functools.partial(_attn_prompt_kernel, qt=qt),
        name="attn_prompt",
        grid=(B, n_steps),
        in_specs=[row,
                  pl.BlockSpec((1, T // tq, tq, ATT_W), lambda b, j: (b, 0, 0, 0)),
                  pl.BlockSpec((1, T // tq, ATT_W, tq), lambda b, j: (b, 0, 0, 0)),
                  _const_spec(bias.shape), _const_spec(nw.shape)],
        out_specs=row,
        out_shape=jax.ShapeDtypeStruct((B, T, ATT_W), BF16),
        compiler_params=pltpu.CompilerParams(
            dimension_semantics=("arbitrary", "arbitrary"), vmem_limit_bytes=VMEM_LIMIT),
    )(q, k4, vt, bias, nw)


def _attn_sample_kernel(q_ref, kc_ref, vc_ref, kn_ref, vn_ref, bc_ref, bn_ref, nw_ref, o_ref, *, n_seq):
    by_head = lambda a: jnp.stack([a[:, h * ATT_DH:(h + 1) * ATT_DH] for h in range(ATT_HEADS)])
    for s in range(n_seq):
        kc = jnp.swapaxes(kc_ref[s], 0, 1).astype(BF16)
        vc = jnp.swapaxes(vc_ref[s], 0, 1).astype(BF16)
        q = by_head(q_ref[s])
        kn = by_head(kn_ref[s])
        vn = by_head(vn_ref[s]).astype(BF16)
        s_c = _bmm_nt(q, kc) + bc_ref[...]
        s_n = _bmm_nt(q, kn) + bn_ref[...]
        m = jnp.maximum(jnp.max(s_c, axis=-1, keepdims=True), jnp.max(s_n, axis=-1, keepdims=True))
        p_c = jnp.exp2(s_c - m)
        p_n = jnp.exp2(s_n - m)
        l = jnp.sum(p_c, axis=-1, keepdims=True) + jnp.sum(p_n, axis=-1, keepdims=True)
        o = (_bmm(p_c.astype(BF16), vc) + _bmm(p_n.astype(BF16), vn)) * (1.0 / l)
        on = _rms(o) * nw_ref[...]
        o_ref[s] = jnp.concatenate([on[h] for h in range(ATT_HEADS)], axis=-1).astype(o_ref.dtype)


def _attn_sample(q, kc, vc, kn, vn, bias_c, bias_n, nw, *, n_seq):
    B, T, _ = q.shape
    new = pl.BlockSpec((n_seq, T, ATT_W), lambda b: (b, 0, 0))
    cache = pl.BlockSpec((n_seq,) + kc.shape[1:], lambda b: (b, 0, 0, 0))
    return pl.pallas_call(
        functools.partial(_attn_sample_kernel, n_seq=n_seq),
        name="attn_sample",
        grid=(B // n_seq,),
        in_specs=[new, cache, cache, new, new, _const_spec(bias_c.shape), _const_spec(bias_n.shape),
                  _const_spec(nw.shape)],
        out_specs=new,
        out_shape=jax.ShapeDtypeStruct((B, T, ATT_W), BF16),
        compiler_params=pltpu.CompilerParams(
            dimension_semantics=("arbitrary",), vmem_limit_bytes=VMEM_LIMIT),
    )(q, kc, vc, kn, vn, bias_c, bias_n, nw)


def _delayed(g, k, S, R, hist_row):
    rolled = pltpu.roll(g, k, axis=0)
    sub = lax.broadcasted_iota(jnp.int32, (SUBLANES, g.shape[1]), 0)
    parts = []
    for s in range(S):
        head = rolled[s * R:s * R + SUBLANES]
        for i in range(k):
            head = jnp.where(sub == i, hist_row(s, i), head)
        parts += [head, rolled[s * R + SUBLANES:(s + 1) * R]]
    return jnp.concatenate(parts, axis=0)


def _mix_ffn_kernel(x_ref, oa_ref, ob_ref, wo_ref, nmp_ref, nfp_ref, wg_ref, wu_ref, cw_ref, cb_ref,
                    wd_ref, nfo_ref, prev_ref, y_ref, st_ref, carry,
                    *, S, R, tps, d_ff):
    t = pl.program_id(0)
    hist = FFN_CONV - 1
    half = oa_ref.shape[-1]
    mix = _dot(oa_ref[...], wo_ref[:half, :]) + _dot(ob_ref[...], wo_ref[half:, :])
    x1 = x_ref[...] + _rms(mix) * nmp_ref[...]
    u2 = (_rms(x1) * nfp_ref[...]).astype(BF16)

    if tps > 1:
        @pl.when(t % tps == 0)
        def _():
            for i in range(hist):
                carry[i:i + 1, :] = prev_ref[0, i]

    fc = FFN_COLS
    n_fc = d_ff // fc
    acc = None
    proj = lambda c: (_dot(u2, wg_ref[:, c * fc:(c + 1) * fc]), _dot(u2, wu_ref[:, c * fc:(c + 1) * fc]))
    nxt = proj(0)
    for c in range(n_fc):
        cols = slice(c * fc, (c + 1) * fc)
        g, up = nxt
        if c + 1 < n_fc:
            nxt = proj(c + 1)
        if tps > 1:
            history = lambda s, i: carry[i:i + 1, cols]
        else:
            history = lambda s, i: prev_ref[0, i, s:s + 1, cols]
        cw = cw_ref[:, cols]
        conv = cb_ref[:, cols] + cw[hist:hist + 1] * g
        for k in range(1, FFN_CONV):
            conv = conv + cw[hist - k:hist - k + 1] * _delayed(
                g, k, S, R, lambda s, i, k=k: history(s, hist - k + i))
        for s in range(S):
            for i in range(hist):
                row = g[(s + 1) * R - hist + i:(s + 1) * R - hist + i + 1]
                if tps > 1:
                    carry[i:i + 1, cols] = row
                else:
                    st_ref[0, i, s:s + 1, cols] = row
        hid = (_gelu_tanh(conv) * up).astype(BF16)
        part = _dot(hid, wd_ref[cols, :])
        acc = part if acc is None else acc + part
    if tps > 1:
        for i in range(hist):
            st_ref[0, i] = carry[i:i + 1, :]
    y_ref[...] = x1 + _rms(acc) * nfo_ref[...]


def _mix_ffn(x, oa, ob, wo, nmp, nfp, wg, wu, cw, cb, wd, nfo, prev, *, S, R):
    M, D = x.shape
    d_ff = wg.shape[1]
    tm = S * R
    n_seq = prev.shape[0]
    hist = FFN_CONV - 1
    tps = M // (n_seq * R)
    assert S == 1 or tps == 1
    grouped = (n_seq // S, hist, S, d_ff)
    prev_g = prev.reshape(n_seq // S, S, hist, d_ff).transpose(0, 2, 1, 3)
    row = lambda w: pl.BlockSpec((tm, w), lambda t: (t, 0))
    st = pl.BlockSpec((1,) + grouped[1:], lambda t: (t // tps, 0, 0, 0))
    consts = [wo, nmp, nfp, wg, wu, cw, cb, wd, nfo]
    y, st_g = pl.pallas_call(
        functools.partial(_mix_ffn_kernel, S=S, R=R, tps=tps, d_ff=d_ff),
        name="mix_ffn",
        grid=(M // tm,),
        in_specs=[row(D), row(oa.shape[1]), row(ob.shape[1])] + [_const_spec(a.shape) for a in consts] + [st],
        out_specs=[row(D), st],
        out_shape=[jax.ShapeDtypeStruct((M, D), F32), jax.ShapeDtypeStruct(grouped, F32)],
        scratch_shapes=[pltpu.VMEM((hist, d_ff), F32)],
        compiler_params=pltpu.CompilerParams(
            dimension_semantics=("arbitrary",), vmem_limit_bytes=VMEM_LIMIT),
    )(x, oa, ob, *consts, prev_g)
    return y, st_g.transpose(0, 2, 1, 3).reshape(n_seq, hist, d_ff)


def _pad_rows_front(a, rows):
    return jnp.pad(a, ((0, 0), (rows - a.shape[1], 0), (0, 0)))


def _lane_row(vals, offset):
    return jnp.zeros((1, LANES), F32).at[0, offset:offset + vals.shape[0]].set(vals.astype(F32))


def _bias_kernel(r_ref, bp_ref, bc_ref, bn_ref, *, lc, ts):
    tq = 2 * CHUNK
    key = lax.broadcasted_iota(jnp.int32, (tq, tq), 0)
    query_chunk = lax.broadcasted_iota(jnp.int32, (tq, tq), 1) // CHUNK
    for h in range(r_ref.shape[0]):
        p, e = divmod(h, 2)
        for jb in range(r_ref.shape[1]):
            row = jnp.broadcast_to(r_ref[h, jb], (tq, 2 * tq))
            blk = pltpu.roll(row, 0, 1, stride=1, stride_axis=0)[:, :tq]
            key_in_band = jb * tq + key - query_chunk * CHUNK
            valid = (key_in_band >= 0) & (key_in_band < (BAND_CHUNKS + 1) * CHUNK)
            bp_ref[p, jb * tq:(jb + 1) * tq, e * tq:(e + 1) * tq] = jnp.where(valid, blk, -jnp.inf)
            by_query = blk.T
            if (jb + 1) * tq <= lc:
                bc_ref[h, :, jb * tq:(jb + 1) * tq] = by_query[:ts]
            else:
                bn_ref[h] = by_query[:ts, :ts]


def _band_biases(table, lc, ts):
    n_heads = table.shape[0]
    tq = 2 * CHUNK
    n_kb = BAND_CHUNKS // 2 + 1
    assert lc == (n_kb - 1) * tq and ts <= tq
    y = jnp.arange(2 * tq)
    dist = WINDOW - tq * jnp.arange(n_kb)[:, None] + jnp.where(y < tq, y, y - 2 * tq)[None, :]
    rows = table[:, jnp.clip(dist, -MAX_REL, MAX_REL) + MAX_REL].reshape(n_heads, n_kb, 1, 2 * tq)
    return pl.pallas_call(
        functools.partial(_bias_kernel, lc=lc, ts=ts),
        name="band_bias",
        out_shape=[jax.ShapeDtypeStruct((n_heads // 2, n_kb * tq, 2 * tq), F32),
                   jax.ShapeDtypeStruct((n_heads, ts, lc), F32),
                   jax.ShapeDtypeStruct((n_heads, ts, ts), F32)],
    )(rows)


def _layer(xp, xs, cache_k, cache_v, s_delta, s_qkv, s_ffn, lw):
    (norm_mix_pre, w_in, qkv_conv_w, a_log, dt_bias, gdn_norm_w, rel_bias, attn_norm_w, w_out,
     norm_mix_post, norm_ffn_pre, w_gate_up, ffn_conv_w, ffn_conv_b, w_down, norm_ffn_post) = lw
    Bp, Tp, D = xp.shape
    Bs, Ts, _ = xs.shape
    d_ff = w_down.shape[0]

    c1 = GDN_QKV + GDN_Z
    c2 = c1 + 2 * GDN_HEADS
    wa = w_in[:, :c1].astype(BF16)
    wg_in = jnp.pad(w_in[:, c1:c2], ((0, 0), (0, LANES - 2 * GDN_HEADS))).astype(BF16)
    wqk = w_in[:, c2:c2 + 2 * ATT_W].astype(BF16)
    wv = w_in[:, c2 + 2 * ATT_W:].astype(BF16)
    wvt = wv.T
    nmix = norm_mix_pre.reshape(1, D)
    cw_qkv = jnp.pad(qkv_conv_w, ((0, SUBLANES - GDN_CONV), (0, 0)))
    alog = _lane_row(a_log, GDN_HEADS)
    dtb = _lane_row(dt_bias, GDN_HEADS)
    gnw = gdn_norm_w.reshape(1, GDN_DV)
    anw = attn_norm_w.astype(F32).reshape(1, ATT_DH)
    anw_col = jnp.broadcast_to(attn_norm_w.astype(F32)[:, None], (ATT_DH, 2 * CHUNK))
    lc = cache_k.shape[1]
    bias_p, bias_c, bias_n = _band_biases(rel_bias.astype(F32) * LOG2E, lc, Ts)
    wo = w_out.astype(BF16)
    w_gate = w_gate_up[:, :d_ff].astype(BF16)
    w_up = w_gate_up[:, d_ff:].astype(BF16)
    wd = w_down.astype(BF16)
    cw_ffn = jnp.pad(ffn_conv_w, ((0, SUBLANES - FFN_CONV), (0, 0)))
    cb = ffn_conv_b.reshape(1, d_ff)
    nmp = norm_mix_post.reshape(1, D)
    nfp = norm_ffn_pre.reshape(1, D)
    nfo = norm_ffn_post.reshape(1, D)

    def group(x, keep_all, gdn_prev, gdn_s0, gdn_c, gdn_cps, gdn_bs, ffn_prev, S, R, attn):
        B, T, _ = x.shape
        xi = x if not keep_all else x.reshape(1, B * T, D)
        qkv, z, ba, q, k, vt, kf, vf = _inproj(xi, nmix, wa, wg_in, wqk, wv, wvt, keep_all=keep_all)
        rs = lambda a: a.reshape(B, T, a.shape[-1])
        qkv, z, ba, q, k = map(rs, (qkv, z, ba, q, k))
        oa, s_new = _gdn(qkv, z, ba, _pad_rows_front(gdn_prev, SUBLANES), gdn_s0, cw_qkv, alog, dtb, gnw,
                         C=gdn_c, cps=gdn_cps, bs=gdn_bs)
        ob = attn(q, k, vt, vf)
        y, ffn_state = _mix_ffn(x.reshape(B * T, D), oa.reshape(B * T, GDN_Z), ob.reshape(B * T, ATT_W),
                                wo, nmp, nfp, w_gate, w_up, cw_ffn, cb, wd, nfo, ffn_prev,
                                S=S, R=R)
        keep = kf.shape[1] if not keep_all else T
        k_rows = kf.reshape(B, keep, ATT_HEADS, ATT_DH)
        v_rows = vf.reshape(B, keep, ATT_HEADS, ATT_DH)
        qkv_state = qkv[:, T - (GDN_CONV - 1):, :]
        return y.reshape(B, T, D), k_rows, v_rows, s_new, qkv_state, ffn_state

    out_p = group(
        xp, False, jnp.zeros((Bp, GDN_CONV - 1, GDN_QKV), F32),
        jnp.zeros((Bp, GDN_HEADS, GDN_DK, GDN_DV), F32), CHUNK, 4, 1,
        jnp.zeros((Bp, FFN_CONV - 1, d_ff), F32), 1, ROW_TILE,
        lambda q, k, vt, vf: _attn_prompt(q, k, vt, bias_p, anw_col, qt=2))

    lc = cache_k.shape[1]
    assert lc == WINDOW and Ts <= 2 * CHUNK
    out_s = group(
        xs, True, s_qkv, s_delta, Ts, 1, 4, s_ffn, ROW_TILE // Ts, Ts,
        lambda q, k, vt, vf: _attn_sample(q, cache_k, cache_v, k, vf.reshape(Bs, Ts, ATT_W), bias_c, bias_n, anw,
                                          n_seq=2))
    return out_p, out_s


def kernel(x_prompt, x_sample, cache_band_k, cache_band_v, state_delta, state_qkv_conv, state_ffn_conv, norm_mix_pre, w_in, qkv_conv_w, a_log, dt_bias, gdn_norm_w, rel_bias, attn_norm_w, w_out, norm_mix_post, norm_ffn_pre, w_gate_up, ffn_conv_w, ffn_conv_b, w_down, norm_ffn_post):
    weights = (norm_mix_pre, w_in, qkv_conv_w, a_log, dt_bias, gdn_norm_w, rel_bias, attn_norm_w, w_out,
               norm_mix_post, norm_ffn_pre, w_gate_up, ffn_conv_w, ffn_conv_b, w_down, norm_ffn_post)
    depth = w_in.shape[0]
    xp, xs = x_prompt, x_sample
    outs_p, outs_s = [], []
    for l in range(depth):
        lw = tuple(w[l] for w in weights)
        op, os_ = _layer(xp, xs, cache_band_k[l], cache_band_v[l], state_delta[l], state_qkv_conv[l],
                         state_ffn_conv[l], lw)
        xp, xs = op[0], os_[0]
        outs_p.append(op[1:])
        outs_s.append(os_[1:])
    stack = lambda outs, i: jnp.stack([o[i] for o in outs], axis=0)
    return (xp, xs) + tuple(stack(outs_p, i) for i in range(5)) + tuple(stack(outs_s, i) for i in range(5))
```

```python
import functools

import jax
import jax.numpy as jnp
from jax import lax
from jax.experimental import pallas as pl
from jax.experimental.pallas import tpu as pltpu

F32 = jnp.float32
BF16 = jnp.bfloat16

EPS = 1e-6
CHUNK = 64
GDN_HEADS = 4
GDN_DK = 128
GDN_DV = 128
GDN_CONV = 4
ATT_HEADS = 8
ATT_DH = 64
BAND_CHUNKS = 8
WINDOW = BAND_CHUNKS * CHUNK
MAX_REL = 128
FFN_CONV = 3

GDN_QK = GDN_HEADS * GDN_DK
GDN_QKV = GDN_HEADS * (2 * GDN_DK + GDN_DV)
GDN_Z = GDN_HEADS * GDN_DV
ATT_W = ATT_HEADS * ATT_DH
LANES = 128
SUBLANES = 8
VMEM_LIMIT = 56 * 1024 * 1024
ROW_TILE = 512
FFN_COLS = 256
LOG2E = 1.4426950408889634
Q_SCALE = ATT_DH ** -0.5 * LOG2E


def _dot(a, b):
    return jnp.dot(a, b, preferred_element_type=F32)


def _dot_nt(a, b):
    return lax.dot_general(a, b, (((1,), (1,)), ((), ())), preferred_element_type=F32)


def _dot_tn(a, b):
    return lax.dot_general(a, b, (((0,), (0,)), ((), ())), preferred_element_type=F32)


def _split3(x):
    x1 = x.astype(BF16)
    r1 = x - x1.astype(F32)
    x2 = r1.astype(BF16)
    x3 = (r1 - x2.astype(F32)).astype(BF16)
    return x1, x2, x3


def _dot_exact_lhs(a16, x):
    x1, x2, x3 = _split3(x)
    return _dot(a16, x1) + _dot(a16, x2) + _dot(a16, x3)


def _dot_exact_rhs(x, b16):
    x1, x2, x3 = _split3(x)
    return _dot(x1, b16) + _dot(x2, b16) + _dot(x3, b16)


def _sigmoid(x):
    return 1.0 / (1.0 + jnp.exp(-x))


def _softplus(x):
    return jnp.maximum(x, 0.0) + jnp.log(1.0 + jnp.exp(-jnp.abs(x)))


def _rms(x):
    return x * lax.rsqrt(jnp.mean(x * x, axis=-1, keepdims=True) + EPS)


def _gelu_tanh(x):
    c = 0.7978845608028654
    return 0.5 * x * (1.0 + jnp.tanh(c * (x + 0.044715 * (x * x * x))))


def _const_spec(shape):
    n = len(shape)
    return pl.BlockSpec(shape, lambda *_: (0,) * n, pipeline_mode=pl.Buffered(1))


def _inproj_kernel(x_ref, nw_ref, wa_ref, wg_ref, wqk_ref, wv_ref, wkt_ref, wvt_ref,
                   qkv_ref, z_ref, ba_ref, q_ref, k_ref, vt_ref, kf_ref, vf_ref,
                   *, keep_all, n_tiles):
    x = x_ref[0]
    tm = x.shape[0]
    u = (_rms(x) * nw_ref[...]).astype(BF16)
    qkv_ref[0] = _dot(u, wa_ref[:, :GDN_QKV])
    z_ref[0] = _dot(u, wa_ref[:, GDN_QKV:])
    ba_ref[0] = _dot(u, wg_ref[...])
    q_ref[0] = (_dot(u, wqk_ref[:, :ATT_W]) * Q_SCALE).astype(BF16)
    k = _dot(u, wqk_ref[:, ATT_W:])
    k_ref[0] = k.astype(BF16)
    vt = _dot_nt(wvt_ref[...], u)
    for jb in range(tm // LANES):
        vt_ref[0, jb] = vt[:, jb * LANES:(jb + 1) * LANES].astype(BF16)

    if keep_all:
        kf_ref[0] = k
        vf_ref[0] = _dot(u, wv_ref[...])
    else:
        @pl.when(pl.program_id(1) == n_tiles - 1)
        def _():
            kf_ref[0] = _dot_nt(wkt_ref[...], u).reshape(ATT_HEADS, ATT_DH, tm)
            vf_ref[0] = vt.reshape(ATT_HEADS, ATT_DH, tm)


def _inproj(x, nw, wa, wg, wqk, wv, wkt, wvt, *, keep_all):
    B, T, D = x.shape
    tm = ROW_TILE
    nt = T // tm
    row = lambda w: pl.BlockSpec((1, tm, w), lambda b, i: (b, i, 0))
    if keep_all:
        keep = row(ATT_W)
        keep_shape = (B, T, ATT_W)
    else:
        keep = pl.BlockSpec((1, ATT_HEADS, ATT_DH, tm), lambda b, i: (b, 0, 0, 0))
        keep_shape = (B, ATT_HEADS, ATT_DH, tm)
    consts = [nw, wa, wg, wqk, wv, wkt, wvt]
    return pl.pallas_call(
        functools.partial(_inproj_kernel, keep_all=keep_all, n_tiles=nt),
        name="inproj",
        grid=(B, nt),
        in_specs=[row(D)] + [_const_spec(a.shape) for a in consts],
        out_specs=[row(GDN_QKV), row(GDN_Z), row(LANES), row(ATT_W), row(ATT_W),
                   pl.BlockSpec((1, tm // LANES, ATT_W, LANES), lambda b, i: (b, i, 0, 0)), keep, keep],
        out_shape=[jax.ShapeDtypeStruct((B, T, GDN_QKV), F32),
                   jax.ShapeDtypeStruct((B, T, GDN_Z), F32),
                   jax.ShapeDtypeStruct((B, T, LANES), F32),
                   jax.ShapeDtypeStruct((B, T, ATT_W), BF16),
                   jax.ShapeDtypeStruct((B, T, ATT_W), BF16),
                   jax.ShapeDtypeStruct((B, T // LANES, ATT_W, LANES), BF16),
                   jax.ShapeDtypeStruct(keep_shape, F32),
                   jax.ShapeDtypeStruct(keep_shape, F32)],
        compiler_params=pltpu.CompilerParams(
            dimension_semantics=("arbitrary", "arbitrary"), vmem_limit_bytes=VMEM_LIMIT),
    )(x, *consts)


def _bmm(a, b):
    return lax.dot_general(a, b, (((2,), (1,)), ((0,), (0,))), preferred_element_type=F32)


def _bmm_nt(a, b):
    return lax.dot_general(a, b, (((2,), (2,)), ((0,), (0,))), preferred_element_type=F32)


def _bmm_tn(a, b):
    return lax.dot_general(a, b, (((1,), (1,)), ((0,), (0,))), preferred_element_type=F32)


def _gdn_kernel(qkv_ref, z_ref, ba_ref, prev_ref, s0_ref, cw_ref, alog_ref, dtb_ref, gnw_ref,
                o_ref, sout_ref, xbuf, s_scr, *, C, cps, bs, n_steps):
    j = pl.program_id(1)
    R = C * cps
    H = GDN_HEADS
    nh = bs * H
    pad = SUBLANES

    @pl.when(j == 0)
    def _():
        xbuf[:, 0:pad, :] = prev_ref[...]
        s_scr[...] = s0_ref[...].reshape(nh, GDN_DK, GDN_DV)

    xbuf[:, pad:pad + R, :] = qkv_ref[...]

    ri = lax.broadcasted_iota(jnp.int32, (C, C), 0)
    ci = lax.broadcasted_iota(jnp.int32, (C, C), 1)
    incl = ri >= ci
    strict = ri > ci
    rr = lax.broadcasted_iota(jnp.int32, (R, R), 0)
    cc = lax.broadcasted_iota(jnp.int32, (R, R), 1)
    cum16 = jnp.where((rr >= cc) & (rr // C == cc // C), 1.0, 0.0).astype(BF16)
    cw = cw_ref[...]
    alog = alog_ref[...]
    dtb = dtb_ref[...]

    acts, sigs, Gs, GTs = [], [], [], []
    for s in range(bs):
        conv = cw[0:1] * xbuf[s, pad - 3:pad - 3 + R, :]
        for i in range(1, GDN_CONV):
            conv = conv + cw[i:i + 1] * xbuf[s, pad - 3 + i:pad - 3 + i + R, :]
        acts.append(conv * _sigmoid(conv))
        ba = ba_ref[s]
        sigs.append(_sigmoid(ba))
        G = _dot_exact_lhs(cum16, -jnp.exp(alog) * _softplus(ba + dtb))
        Gs.append(G)
        GTs.append(G.T)
    xbuf[:, 0:pad, :] = xbuf[:, R:R + pad, :]

    order = [(c, s, h) for c in range(cps) for s in range(bs) for h in range(H)]
    rows = lambda c: slice(c * C, (c + 1) * C)

    def tiles(slabs, col0, width):
        return jnp.stack([slabs[s][rows(c), col0 + h * width:col0 + (h + 1) * width] for c, s, h in order])

    q = tiles(acts, 0, GDN_DK)
    k = tiles(acts, GDN_QK, GDN_DK)
    v = tiles(acts, 2 * GDN_QK, GDN_DV)
    beta = tiles(sigs, 0, 1)
    Gc = tiles(Gs, H, 1)
    Gr = jnp.stack([GTs[s][H + h:H + h + 1, rows(c)] for c, s, h in order])
    Gl = Gc[:, C - 1:C, :]

    qn = q * lax.rsqrt(jnp.sum(q * q, axis=-1, keepdims=True) + EPS) * (GDN_DK ** -0.5)
    kn = k * lax.rsqrt(jnp.sum(k * k, axis=-1, keepdims=True) + EPS)
    eG = jnp.exp(Gc)
    gam = jnp.where(incl, jnp.exp(jnp.where(incl, Gc - Gr, 0.0)), 0.0)
    kb = kn * beta
    kn16 = kn.astype(BF16)
    aq = _bmm_nt(jnp.concatenate([kb, qn], axis=1).astype(BF16), kn16)
    A = jnp.where(strict, aq[:, :C] * gam, 0.0)
    QK16 = (aq[:, C:] * gam).astype(BF16)

    n_joint = C.bit_length() - 2
    A16 = A.astype(BF16)
    N = -A
    Q = _bmm(A16, A16)
    for it in range(n_joint):
        Q16 = Q.astype(BF16)
        if it == n_joint - 1:
            N = N + Q + _bmm(N.astype(BF16), Q16)
        else:
            nq = _bmm(jnp.concatenate([N, Q], axis=1).astype(BF16), Q16)
            N = N + Q + nq[:, :C]
            Q = nq[:, C:]
    rhs = jnp.concatenate([v * beta, kb * eG], axis=-1)
    sol = rhs + _bmm(N.astype(BF16), rhs.astype(BF16))
    u = sol[:, :, :GDN_DV]
    wq16 = jnp.concatenate([sol[:, :, GDN_DV:], qn * eG], axis=1).astype(BF16)
    kg16 = (kn * jnp.exp(Gl - Gc)).astype(BF16)
    dl = jnp.exp(Gl)

    S = s_scr[...]
    o_parts = []
    for c in range(cps):
        sl = slice(c * nh, (c + 1) * nh)
        r = _bmm(wq16[sl], S.astype(BF16))
        vn16 = (u[sl] - r[:, :C]).astype(BF16)
        o_parts.append(r[:, C:] + _bmm(QK16[sl], vn16))
        S = S * dl[sl] + _bmm_tn(kg16[sl], vn16)
    s_scr[...] = S

    on = _rms(jnp.concatenate(o_parts, axis=0)) * gnw_ref[...]
    gates = []
    for s in range(bs):
        zs = z_ref[s]
        gates.append(zs * _sigmoid(zs))
    for idx, (c, s, h) in enumerate(order):
        cols = slice(h * GDN_DV, (h + 1) * GDN_DV)
        o_ref[s, rows(c), cols] = (on[idx] * gates[s][rows(c), cols]).astype(o_ref.dtype)

    @pl.when(j == n_steps - 1)
    def _():
        sout_ref[...] = S.reshape(bs, H, GDN_DK, GDN_DV)


def _gdn(qkv, z, ba, prev, s0, cw, alog, dtb, gnw, *, C, cps, bs):
    B, T, _ = qkv.shape
    R = C * cps
    n_steps = T // R
    row = lambda w: pl.BlockSpec((bs, R, w), lambda b, j: (b, j, 0))
    st = pl.BlockSpec((bs, GDN_HEADS, GDN_DK, GDN_DV), lambda b, j: (b, 0, 0, 0))
    return pl.pallas_call(
        functools.partial(_gdn_kernel, C=C, cps=cps, bs=bs, n_steps=n_steps),
        name="gdn",
        grid=(B // bs, n_steps),
        in_specs=[row(GDN_QKV), row(GDN_Z), row(LANES),
                  pl.BlockSpec((bs, SUBLANES, GDN_QKV), lambda b, j: (b, 0, 0)), st,
                  _const_spec(cw.shape), _const_spec(alog.shape), _const_spec(dtb.shape),
                  _const_spec(gnw.shape)],
        out_specs=[row(GDN_Z), st],
        out_shape=[jax.ShapeDtypeStruct((B, T, GDN_Z), BF16),
                   jax.ShapeDtypeStruct((B, GDN_HEADS, GDN_DK, GDN_DV), F32)],
        scratch_shapes=[pltpu.VMEM((bs, SUBLANES + R, GDN_QKV), F32),
                        pltpu.VMEM((bs * GDN_HEADS, GDN_DK, GDN_DV), F32)],
        compiler_params=pltpu.CompilerParams(
            dimension_semantics=("arbitrary", "arbitrary"), vmem_limit_bytes=VMEM_LIMIT),
    )(qkv, z, ba, prev, s0, cw, alog, dtb, gnw)


def _attn_prompt_kernel(q_ref, k_ref, vt_ref, bias_ref, nw_ref, o_ref, *, qt):
    j = pl.program_id(1)
    tq = 2 * CHUNK
    n_kb = BAND_CHUNKS // 2 + 1
    n_pairs = ATT_HEADS // 2
    lane_head = lax.broadcasted_iota(jnp.int32, (tq, LANES), 1) // ATT_DH
    for i in range(qt):
        m = j * qt + i
        q = q_ref[0, i * tq:(i + 1) * tq, :]
        wt = []
        for p in range(n_pairs):
            qp = q[:, p * LANES:(p + 1) * LANES]
            zero = jnp.zeros_like(qp)
            wt.append(jnp.concatenate([jnp.where(lane_head == 0, qp, zero),
                                       jnp.where(lane_head == 1, qp, zero)], axis=0))
        wt = jnp.stack(wt)
        s_parts, v_parts = [], []
        for jb in range(n_kb):
            first = m - (n_kb - 1) + jb
            blk = jnp.maximum(first, 0)
            kb = k_ref[0, blk]
            kj = jnp.stack([kb[:, p * LANES:(p + 1) * LANES] for p in range(n_pairs)])
            s = _bmm_nt(kj, wt) + bias_ref[:, jb * tq:(jb + 1) * tq, :]
            if jb < n_kb - 1:
                s = jnp.where(first >= 0, s, -jnp.inf)
            s_parts.append(s)
            v_parts.append(vt_ref[0, blk].reshape(n_pairs, LANES, tq))
        st = jnp.concatenate(s_parts, axis=1)
        pt = jnp.exp2(st - jnp.max(st, axis=1, keepdims=True))
        inv = 1.0 / jnp.sum(pt, axis=1, keepdims=True)
        pt16 = pt.astype(BF16)
        acc = _bmm(v_parts[0], pt16[:, 0:tq, :])
        for jb in range(1, n_kb):
            acc = acc + _bmm(v_parts[jb], pt16[:, jb * tq:(jb + 1) * tq, :])
        ot = jnp.concatenate([acc[:, :ATT_DH, :tq] * inv[:, :, :tq],
                              acc[:, ATT_DH:, tq:] * inv[:, :, tq:]], axis=1)
        oh = ot.reshape(ATT_HEADS, ATT_DH, tq)
        on = oh * lax.rsqrt(jnp.mean(oh * oh, axis=1, keepdims=True) + EPS) * nw_ref[...]
        on = on.reshape(n_pairs, LANES, tq)
        for p in range(n_pairs):
            o_ref[0, i * tq:(i + 1) * tq, p * LANES:(p + 1) * LANES] = on[p].T.astype(o_ref.dtype)


def _attn_prompt(q, k, vt, bias, nw, *, qt):
    B, T, _ = q.shape
    tq = 2 * CHUNK
    n_steps = T // (qt * tq)
    k4 = k.reshape(B, T // tq, tq, ATT_W)
    row = pl.BlockSpec((1, qt * tq, ATT_W), lambda b, j: (b, j, 0))
    return pl.pallas_call(
        functools.partial(_attn_prompt_kernel, qt=qt),
        name="attn_prompt",
        grid=(B, n_steps),
        in_specs=[row,
                  pl.BlockSpec((1, T // tq, tq, ATT_W), lambda b, j: (b, 0, 0, 0)),
                  pl.BlockSpec((1, T // tq, ATT_W, tq), lambda b, j: (b, 0, 0, 0)),
                  _const_spec(bias.shape), _const_spec(nw.shape)],
        out_specs=row,
        out_shape=jax.ShapeDtypeStruct((B, T, ATT_W), BF16),
        compiler_params=pltpu.CompilerParams(
            dimension_semantics=("arbitrary", "arbitrary"), vmem_limit_bytes=VMEM_LIMIT),
    )(q, k4, vt, bias, nw)


def _attn_sample_kernel(q_ref, kc_ref, vc_ref, kn_ref, vn_ref, bc_ref, bn_ref, nw_ref, o_ref, *, n_seq):
    by_head = lambda a: jnp.stack([a[:, h * ATT_DH:(h + 1) * ATT_DH] for h in range(ATT_HEADS)])
    for s in range(n_seq):
        kc = kc_ref[s].astype(BF16)
        vc = vc_ref[s].astype(BF16)
        q = by_head(q_ref[s])
        kn = by_head(kn_ref[s])
        vn = by_head(vn_ref[s]).astype(BF16)
        s_c = _bmm(q, kc) + bc_ref[...]
        s_n = _bmm_nt(q, kn) + bn_ref[...]
        m = jnp.maximum(jnp.max(s_c, axis=-1, keepdims=True), jnp.max(s_n, axis=-1, keepdims=True))
        p_c = jnp.exp2(s_c - m)
        p_n = jnp.exp2(s_n - m)
        l = jnp.sum(p_c, axis=-1, keepdims=True) + jnp.sum(p_n, axis=-1, keepdims=True)
        o = (_bmm_nt(p_c.astype(BF16), vc) + _bmm(p_n.astype(BF16), vn)) * (1.0 / l)
        on = _rms(o) * nw_ref[...]
        o_ref[s] = jnp.concatenate([on[h] for h in range(ATT_HEADS)], axis=-1).astype(o_ref.dtype)


def _attn_sample(q, kc, vc, kn, vn, bias_c, bias_n, nw, *, n_seq):
    B, T, _ = q.shape
    new = pl.BlockSpec((n_seq, T, ATT_W), lambda b: (b, 0, 0))
    cache = pl.BlockSpec((n_seq,) + kc.shape[1:], lambda b: (b, 0, 0, 0))
    return pl.pallas_call(
        functools.partial(_attn_sample_kernel, n_seq=n_seq),
        name="attn_sample",
        grid=(B // n_seq,),
        in_specs=[new, cache, cache, new, new, _const_spec(bias_c.shape), _const_spec(bias_n.shape),
                  _const_spec(nw.shape)],
        out_specs=new,
        out_shape=jax.ShapeDtypeStruct((B, T, ATT_W), BF16),
        compiler_params=pltpu.CompilerParams(
            dimension_semantics=("arbitrary",), vmem_limit_bytes=VMEM_LIMIT),
    )(q, kc, vc, kn, vn, bias_c, bias_n, nw)


def _delayed(g, k, S, R, hist_row):
    rolled = pltpu.roll(g, k, axis=0)
    sub = lax.broadcasted_iota(jnp.int32, (SUBLANES, g.shape[1]), 0)
    parts = []
    for s in range(S):
        head = rolled[s * R:s * R + SUBLANES]
        for i in range(k):
            head = jnp.where(sub == i, hist_row(s, i), head)
        parts += [head, rolled[s * R + SUBLANES:(s + 1) * R]]
    return jnp.concatenate(parts, axis=0)


def _mix_ffn_kernel(x_ref, oa_ref, ob_ref, wo_ref, nmp_ref, nfp_ref, wg_ref, wu_ref, cw_ref, cb_ref,
                    wd_ref, nfo_ref, prev_ref, y_ref, st_ref, carry,
                    *, S, R, tps, d_ff):
    t = pl.program_id(0)
    hist = FFN_CONV - 1
    half = oa_ref.shape[-1]
    mix = _dot(oa_ref[...], wo_ref[:half, :]) + _dot(ob_ref[...], wo_ref[half:, :])
    x1 = x_ref[...] + _rms(mix) * nmp_ref[...]
    u2 = (_rms(x1) * nfp_ref[...]).astype(BF16)

    if tps > 1:
        @pl.when(t % tps == 0)
        def _():
            for i in range(hist):
                carry[i:i + 1, :] = prev_ref[0, i]

    fc = FFN_COLS
    n_fc = d_ff // fc
    acc = None
    proj = lambda c: (_dot(u2, wg_ref[:, c * fc:(c + 1) * fc]), _dot(u2, wu_ref[:, c * fc:(c + 1) * fc]))
    nxt = proj(0)
    for c in range(n_fc):
        cols = slice(c * fc, (c + 1) * fc)
        g, up = nxt
        if c + 1 < n_fc:
            nxt = proj(c + 1)
        if tps > 1:
            history = lambda s, i: carry[i:i + 1, cols]
        else:
            history = lambda s, i: prev_ref[0, i, s:s + 1, cols]
        cw = cw_ref[:, cols]
        conv = cb_ref[:, cols] + cw[hist:hist + 1] * g
        for k in range(1, FFN_CONV):
            conv = conv + cw[hist - k:hist - k + 1] * _delayed(
                g, k, S, R, lambda s, i, k=k: history(s, hist - k + i))
        for s in range(S):
            for i in range(hist):
                row = g[(s + 1) * R - hist + i:(s + 1) * R - hist + i + 1]
                if tps > 1:
                    carry[i:i + 1, cols] = row
                else:
                    st_ref[0, i, s:s + 1, cols] = row
        hid = (_gelu_tanh(conv) * up).astype(BF16)
        part = _dot(hid, wd_ref[cols, :])
        acc = part if acc is None else acc + part
    if tps > 1:
        for i in range(hist):
            st_ref[0, i] = carry[i:i + 1, :]
    y_ref[...] = x1 + _rms(acc) * nfo_ref[...]


def _mix_ffn(x, oa, ob, wo, nmp, nfp, wg, wu, cw, cb, wd, nfo, prev, *, S, R):
    M, D = x.shape
    d_ff = wg.shape[1]
    tm = S * R
    n_seq = prev.shape[0]
    hist = FFN_CONV - 1
    tps = M // (n_seq * R)
    assert S == 1 or tps == 1
    grouped = (n_seq // S, hist, S, d_ff)
    prev_g = prev.reshape(n_seq // S, S, hist, d_ff).transpose(0, 2, 1, 3)
    row = lambda w: pl.BlockSpec((tm, w), lambda t: (t, 0))
    st = pl.BlockSpec((1,) + grouped[1:], lambda t: (t // tps, 0, 0, 0))
    consts = [wo, nmp, nfp, wg, wu, cw, cb, wd, nfo]
    y, st_g = pl.pallas_call(
        functools.partial(_mix_ffn_kernel, S=S, R=R, tps=tps, d_ff=d_ff),
        name="mix_ffn",
        grid=(M // tm,),
        in_specs=[row(D), row(oa.shape[1]), row(ob.shape[1])] + [_const_spec(a.shape) for a in consts] + [st],
        out_specs=[row(D), st],
        out_shape=[jax.ShapeDtypeStruct((M, D), F32), jax.ShapeDtypeStruct(grouped, F32)],
        scratch_shapes=[pltpu.VMEM((hist, d_ff), F32)],
        compiler_params=pltpu.CompilerParams(
            dimension_semantics=("arbitrary",), vmem_limit_bytes=VMEM_LIMIT),
    )(x, oa, ob, *consts, prev_g)
    return y, st_g.transpose(0, 2, 1, 3).reshape(n_seq, hist, d_ff)


def _pad_rows_front(a, rows):
    return jnp.pad(a, ((0, 0), (rows - a.shape[1], 0), (0, 0)))


def _lane_row(vals, offset):
    return jnp.zeros((1, LANES), F32).at[0, offset:offset + vals.shape[0]].set(vals.astype(F32))


def _bias_kernel(r_ref, bp_ref, bc_ref, bn_ref, *, lc, ts):
    tq = 2 * CHUNK
    key = lax.broadcasted_iota(jnp.int32, (tq, tq), 0)
    query_chunk = lax.broadcasted_iota(jnp.int32, (tq, tq), 1) // CHUNK
    for h in range(r_ref.shape[0]):
        p, e = divmod(h, 2)
        for jb in range(r_ref.shape[1]):
            row = jnp.broadcast_to(r_ref[h, jb], (tq, 2 * tq))
            blk = pltpu.roll(row, 0, 1, stride=1, stride_axis=0)[:, :tq]
            key_in_band = jb * tq + key - query_chunk * CHUNK
            valid = (key_in_band >= 0) & (key_in_band < (BAND_CHUNKS + 1) * CHUNK)
            bp_ref[p, jb * tq:(jb + 1) * tq, e * tq:(e + 1) * tq] = jnp.where(valid, blk, -jnp.inf)
            by_query = blk.T
            if (jb + 1) * tq <= lc:
                bc_ref[h, :, jb * tq:(jb + 1) * tq] = by_query[:ts]
            else:
                bn_ref[h] = by_query[:ts, :ts]


def _band_biases(table, lc, ts):
    n_heads = table.shape[0]
    tq = 2 * CHUNK
    n_kb = BAND_CHUNKS // 2 + 1
    assert lc == (n_kb - 1) * tq and ts <= tq
    y = jnp.arange(2 * tq)
    dist = WINDOW - tq * jnp.arange(n_kb)[:, None] + jnp.where(y < tq, y, y - 2 * tq)[None, :]
    rows = table[:, jnp.clip(dist, -MAX_REL, MAX_REL) + MAX_REL].reshape(n_heads, n_kb, 1, 2 * tq)
    return pl.pallas_call(
        functools.partial(_bias_kernel, lc=lc, ts=ts),
        name="band_bias",
        out_shape=[jax.ShapeDtypeStruct((n_heads // 2, n_kb * tq, 2 * tq), F32),
                   jax.ShapeDtypeStruct((n_heads, ts, lc), F32),
                   jax.ShapeDtypeStruct((n_heads, ts, ts), F32)],
    )(rows)


def _layer(xp, xs, cache_k, cache_v, s_delta, s_qkv, s_ffn, lw):
    (norm_mix_pre, w_in, qkv_conv_w, a_log, dt_bias, gdn_norm_w, rel_bias, attn_norm_w, w_out,
     norm_mix_post, norm_ffn_pre, w_gate_up, ffn_conv_w, ffn_conv_b, w_down, norm_ffn_post) = lw
    Bp, Tp, D = xp.shape
    Bs, Ts, _ = xs.shape
    d_ff = w_down.shape[0]

    c1 = GDN_QKV + GDN_Z
    c2 = c1 + 2 * GDN_HEADS
    wa = w_in[:, :c1].astype(BF16)
    wg_in = jnp.pad(w_in[:, c1:c2], ((0, 0), (0, LANES - 2 * GDN_HEADS))).astype(BF16)
    wqk = w_in[:, c2:c2 + 2 * ATT_W].astype(BF16)
    wv = w_in[:, c2 + 2 * ATT_W:].astype(BF16)
    wkt = wqk[:, ATT_W:].T
    wvt = wv.T
    nmix = norm_mix_pre.reshape(1, D)
    cw_qkv = jnp.pad(qkv_conv_w, ((0, SUBLANES - GDN_CONV), (0, 0)))
    alog = _lane_row(a_log, GDN_HEADS)
    dtb = _lane_row(dt_bias, GDN_HEADS)
    gnw = gdn_norm_w.reshape(1, GDN_DV)
    anw = attn_norm_w.astype(F32).reshape(1, ATT_DH)
    anw_col = jnp.broadcast_to(attn_norm_w.astype(F32)[:, None], (ATT_DH, 2 * CHUNK))
    lc = cache_k.shape[1]
    bias_p, bias_c, bias_n = _band_biases(rel_bias.astype(F32) * LOG2E, lc, Ts)
    wo = w_out.astype(BF16)
    w_gate = w_gate_up[:, :d_ff].astype(BF16)
    w_up = w_gate_up[:, d_ff:].astype(BF16)
    wd = w_down.astype(BF16)
    cw_ffn = jnp.pad(ffn_conv_w, ((0, SUBLANES - FFN_CONV), (0, 0)))
    cb = ffn_conv_b.reshape(1, d_ff)
    nmp = norm_mix_post.reshape(1, D)
    nfp = norm_ffn_pre.reshape(1, D)
    nfo = norm_ffn_post.reshape(1, D)

    def group(x, keep_all, gdn_prev, gdn_s0, gdn_c, gdn_cps, gdn_bs, ffn_prev, S, R, attn):
        B, T, _ = x.shape
        xi = x if not keep_all else x.reshape(1, B * T, D)
        qkv, z, ba, q, k, vt, kf, vf = _inproj(xi, nmix, wa, wg_in, wqk, wv, wkt, wvt, keep_all=keep_all)
        rs = lambda a: a.reshape(B, T, a.shape[-1])
        qkv, z, ba, q, k = map(rs, (qkv, z, ba, q, k))
        oa, s_new = _gdn(qkv, z, ba, _pad_rows_front(gdn_prev, SUBLANES), gdn_s0, cw_qkv, alog, dtb, gnw,
                         C=gdn_c, cps=gdn_cps, bs=gdn_bs)
        ob = attn(q, k, vt, vf)
        y, ffn_state = _mix_ffn(x.reshape(B * T, D), oa.reshape(B * T, GDN_Z), ob.reshape(B * T, ATT_W),
                                wo, nmp, nfp, w_gate, w_up, cw_ffn, cb, wd, nfo, ffn_prev,
                                S=S, R=R)
        if keep_all:
            k_rows = kf.reshape(B, T, ATT_HEADS, ATT_DH)
            v_rows = vf.reshape(B, T, ATT_HEADS, ATT_DH)
        else:
            k_rows = jnp.transpose(kf, (0, 3, 1, 2))
            v_rows = jnp.transpose(vf, (0, 3, 1, 2))
        qkv_state = qkv[:, T - (GDN_CONV - 1):, :]
        return y.reshape(B, T, D), k_rows, v_rows, s_new, qkv_state, ffn_state

    out_p = group(
        xp, False, jnp.zeros((Bp, GDN_CONV - 1, GDN_QKV), F32),
        jnp.zeros((Bp, GDN_HEADS, GDN_DK, GDN_DV), F32), CHUNK, 4, 1,
        jnp.zeros((Bp, FFN_CONV - 1, d_ff), F32), 1, ROW_TILE,
        lambda q, k, vt, vf: _attn_prompt(q, k, vt, bias_p, anw_col, qt=2))

    kc_t = jnp.transpose(cache_k, (0, 2, 3, 1))
    vc_t = jnp.transpose(cache_v, (0, 2, 3, 1))
    out_s = group(
        xs, True, s_qkv, s_delta, Ts, 1, 4, s_ffn, ROW_TILE // Ts, Ts,
        lambda q, k, vt, vf: _attn_sample(q, kc_t, vc_t, k, vf.reshape(Bs, Ts, ATT_W), bias_c, bias_n, anw,
                                          n_seq=2))
    return out_p, out_s


def kernel(x_prompt, x_sample, cache_band_k, cache_band_v, state_delta, state_qkv_conv, state_ffn_conv, norm_mix_pre, w_in, qkv_conv_w, a_log, dt_bias, gdn_norm_w, rel_bias, attn_norm_w, w_out, norm_mix_post, norm_ffn_pre, w_gate_up, ffn_conv_w, ffn_conv_b, w_down, norm_ffn_post):
    weights = (norm_mix_pre, w_in, qkv_conv_w, a_log, dt_bias, gdn_norm_w, rel_bias, attn_norm_w, w_out,
               norm_mix_post, norm_ffn_pre, w_gate_up, ffn_conv_w, ffn_conv_b, w_down, norm_ffn_post)
    depth = w_in.shape[0]
    xp, xs = x_prompt, x_sample
    outs_p, outs_s = [], []
    for l in range(depth):
        lw = tuple(w[l] for w in weights)
        op, os_ = _layer(xp, xs, cache_band_k[l], cache_band_v[l], state_delta[l], state_qkv_conv[l],
                         state_ffn_conv[l], lw)
        xp, xs = op[0], os_[0]
        outs_p.append(op[1:])
        outs_s.append(os_[1:])
    stack = lambda outs, i: jnp.stack([o[i] for o in outs], axis=0)
    return (xp, xs) + tuple(stack(outs_p, i) for i in range(5)) + tuple(stack(outs_s, i) for i in range(5))
```

```python
import functools

import jax
import jax.numpy as jnp
from jax import lax
from jax.experimental import pallas as pl
from jax.experimental.pallas import tpu as pltpu

F32 = jnp.float32
BF16 = jnp.bfloat16

EPS = 1e-6
CHUNK = 64
GDN_HEADS = 4
GDN_DK = 128
GDN_DV = 128
GDN_CONV = 4
ATT_HEADS = 8
ATT_DH = 64
BAND_CHUNKS = 8
WINDOW = BAND_CHUNKS * CHUNK
MAX_REL = 128
FFN_CONV = 3

GDN_QK = GDN_HEADS * GDN_DK
GDN_QKV = GDN_HEADS * (2 * GDN_DK + GDN_DV)
GDN_Z = GDN_HEADS * GDN_DV
ATT_W = ATT_HEADS * ATT_DH
LANES = 128
SUBLANES = 8
VMEM_LIMIT = 56 * 1024 * 1024
ROW_TILE = 512
FFN_COLS = 256
LOG2E = 1.4426950408889634
Q_SCALE = ATT_DH ** -0.5 * LOG2E


def _dot(a, b):
    return jnp.dot(a, b, preferred_element_type=F32)


def _dot_nt(a, b):
    return lax.dot_general(a, b, (((1,), (1,)), ((), ())), preferred_element_type=F32)


def _dot_tn(a, b):
    return lax.dot_general(a, b, (((0,), (0,)), ((), ())), preferred_element_type=F32)


def _split3(x):
    x1 = x.astype(BF16)
    r1 = x - x1.astype(F32)
    x2 = r1.astype(BF16)
    x3 = (r1 - x2.astype(F32)).astype(BF16)
    return x1, x2, x3


def _dot_exact_lhs(a16, x):
    x1, x2, x3 = _split3(x)
    return _dot(a16, x1) + _dot(a16, x2) + _dot(a16, x3)


def _dot_exact_rhs(x, b16):
    x1, x2, x3 = _split3(x)
    return _dot(x1, b16) + _dot(x2, b16) + _dot(x3, b16)


def _sigmoid(x):
    return 1.0 / (1.0 + jnp.exp(-x))


def _softplus(x):
    return jnp.maximum(x, 0.0) + jnp.log(1.0 + jnp.exp(-jnp.abs(x)))


def _rms(x):
    return x * lax.rsqrt(jnp.mean(x * x, axis=-1, keepdims=True) + EPS)


def _gelu_tanh(x):
    c = 0.7978845608028654
    return 0.5 * x * (1.0 + jnp.tanh(c * (x + 0.044715 * (x * x * x))))


def _const_spec(shape):
    n = len(shape)
    return pl.BlockSpec(shape, lambda *_: (0,) * n, pipeline_mode=pl.Buffered(1))


def _inproj_kernel(x_ref, nw_ref, wa_ref, wg_ref, wqk_ref, wv_ref, wkt_ref, wvt_ref, cw_ref, prev_ref,
                   act_ref, cst_ref, z_ref, ba_ref, q_ref, k_ref, vt_ref, kf_ref, vf_ref, carry,
                   *, keep_all, n_tiles, S, R, tps):
    x = x_ref[0]
    tm = x.shape[0]
    u = (_rms(x) * nw_ref[...]).astype(BF16)

    hist = GDN_CONV - 1
    if tps > 1:
        @pl.when(pl.program_id(1) % tps == 0)
        def _():
            for r in range(hist):
                carry[r:r + 1, :] = prev_ref[0, r]
    cw = cw_ref[...]
    slabs = [slice(c * GDN_QK, (c + 1) * GDN_QK) for c in range(GDN_QKV // GDN_QK)]

    def conv_silu(pre, cols):
        if tps > 1:
            history = lambda s, r, cols=cols: carry[r:r + 1, cols]
        else:
            history = lambda s, r, cols=cols: prev_ref[0, r, s:s + 1, cols]
        conv = cw[hist:hist + 1, cols] * pre
        for k in range(1, GDN_CONV):
            conv = conv + cw[hist - k:hist - k + 1, cols] * _delayed(
                pre, k, S, R, lambda s, r, k=k, history=history: history(s, hist - k + r))
        act_ref[0, :, cols] = conv * _sigmoid(conv)
        for s in range(S):
            for r in range(hist):
                row = pre[(s + 1) * R - hist + r:(s + 1) * R - hist + r + 1]
                if tps > 1:
                    carry[r:r + 1, cols] = row
                else:
                    cst_ref[0, r, s:s + 1, cols] = row

    pre = [_dot(u, wa_ref[:, slabs[0]]), _dot(u, wa_ref[:, slabs[1]])]
    conv_silu(pre[0], slabs[0])
    pre.append(_dot(u, wa_ref[:, slabs[2]]))
    conv_silu(pre[1], slabs[1])
    z_ref[0] = _dot(u, wa_ref[:, GDN_QKV:])
    ba_ref[0] = _dot(u, wg_ref[...])
    q_ref[0] = (_dot(u, wqk_ref[:, :ATT_W]) * Q_SCALE).astype(BF16)
    conv_silu(pre[2], slabs[2])
    if tps > 1:
        for r in range(hist):
            cst_ref[0, r] = carry[r:r + 1, :]
    k = _dot(u, wqk_ref[:, ATT_W:])
    k_ref[0] = k.astype(BF16)
    vt = _dot_nt(wvt_ref[...], u)
    for jb in range(tm // LANES):
        vt_ref[0, jb] = vt[:, jb * LANES:(jb + 1) * LANES].astype(BF16)

    if keep_all:
        kf_ref[0] = k
        vf_ref[0] = _dot(u, wv_ref[...])
    else:
        @pl.when(pl.program_id(1) == n_tiles - 1)
        def _():
            kf_ref[0] = _dot_nt(wkt_ref[...], u).reshape(ATT_HEADS, ATT_DH, tm)
            vf_ref[0] = vt.reshape(ATT_HEADS, ATT_DH, tm)


def _inproj(x, nw, wa, wg, wqk, wv, wkt, wvt, cw, prev, *, keep_all):
    B, T, D = x.shape
    tm = ROW_TILE
    nt = T // tm
    n_seq = prev.shape[0]
    hist = GDN_CONV - 1
    R = min(B * T // n_seq, tm)
    S = tm // R
    tps = B * T // (n_seq * R)
    assert S == 1 or tps == 1
    grouped = (n_seq // S, hist, S, GDN_QKV)
    prev_g = prev.reshape(n_seq // S, S, hist, GDN_QKV).transpose(0, 2, 1, 3)
    groups_per_b = nt // tps
    st = pl.BlockSpec((1,) + grouped[1:], lambda b, i: (b * groups_per_b + i // tps, 0, 0, 0))
    row = lambda w: pl.BlockSpec((1, tm, w), lambda b, i: (b, i, 0))
    if keep_all:
        keep = row(ATT_W)
        keep_shape = (B, T, ATT_W)
    else:
        keep = pl.BlockSpec((1, ATT_HEADS, ATT_DH, tm), lambda b, i: (b, 0, 0, 0))
        keep_shape = (B, ATT_HEADS, ATT_DH, tm)
    consts = [nw, wa, wg, wqk, wv, wkt, wvt, cw]
    act, cst, *rest = pl.pallas_call(
        functools.partial(_inproj_kernel, keep_all=keep_all, n_tiles=nt, S=S, R=R, tps=tps),
        name="inproj",
        grid=(B, nt),
        in_specs=[row(D)] + [_const_spec(a.shape) for a in consts] + [st],
        out_specs=[row(GDN_QKV), st, row(GDN_Z), row(LANES), row(ATT_W), row(ATT_W),
                   pl.BlockSpec((1, tm // LANES, ATT_W, LANES), lambda b, i: (b, i, 0, 0)), keep, keep],
        out_shape=[jax.ShapeDtypeStruct((B, T, GDN_QKV), F32),
                   jax.ShapeDtypeStruct(grouped, F32),
                   jax.ShapeDtypeStruct((B, T, GDN_Z), F32),
                   jax.ShapeDtypeStruct((B, T, LANES), F32),
                   jax.ShapeDtypeStruct((B, T, ATT_W), BF16),
                   jax.ShapeDtypeStruct((B, T, ATT_W), BF16),
                   jax.ShapeDtypeStruct((B, T // LANES, ATT_W, LANES), BF16),
                   jax.ShapeDtypeStruct(keep_shape, F32),
                   jax.ShapeDtypeStruct(keep_shape, F32)],
        scratch_shapes=[pltpu.VMEM((hist, GDN_QKV), F32)],
        compiler_params=pltpu.CompilerParams(
            dimension_semantics=("arbitrary", "arbitrary"), vmem_limit_bytes=VMEM_LIMIT),
    )(x, *consts, prev_g)
    return (act, cst.transpose(0, 2, 1, 3).reshape(n_seq, hist, GDN_QKV), *rest)


def _bmm(a, b):
    return lax.dot_general(a, b, (((2,), (1,)), ((0,), (0,))), preferred_element_type=F32)


def _bmm_nt(a, b):
    return lax.dot_general(a, b, (((2,), (2,)), ((0,), (0,))), preferred_element_type=F32)


def _bmm_tn(a, b):
    return lax.dot_general(a, b, (((1,), (1,)), ((0,), (0,))), preferred_element_type=F32)


def _gdn_kernel(act_ref, z_ref, ba_ref, s0_ref, alog_ref, dtb_ref, gnw_ref,
                o_ref, sout_ref, s_scr, *, C, cps, bs, n_steps):
    j = pl.program_id(1)
    R = C * cps
    H = GDN_HEADS
    nh = bs * H

    @pl.when(j == 0)
    def _():
        s_scr[...] = s0_ref[...].reshape(nh, GDN_DK, GDN_DV)

    ri = lax.broadcasted_iota(jnp.int32, (C, C), 0)
    ci = lax.broadcasted_iota(jnp.int32, (C, C), 1)
    incl = ri >= ci
    strict = ri > ci
    rr = lax.broadcasted_iota(jnp.int32, (R, R), 0)
    cc = lax.broadcasted_iota(jnp.int32, (R, R), 1)
    cum16 = jnp.where((rr >= cc) & (rr // C == cc // C), 1.0, 0.0).astype(BF16)
    alog = alog_ref[...]
    dtb = dtb_ref[...]

    acts, sigs, Gs, GTs = [], [], [], []
    for s in range(bs):
        acts.append(act_ref.at[s])
        ba = ba_ref[s]
        sigs.append(_sigmoid(ba))
        G = _dot_exact_lhs(cum16, -jnp.exp(alog) * _softplus(ba + dtb))
        Gs.append(G)
        GTs.append(G.T)

    order = [(c, s, h) for c in range(cps) for s in range(bs) for h in range(H)]
    rows = lambda c: slice(c * C, (c + 1) * C)

    def tiles(slabs, col0, width):
        return jnp.stack([slabs[s][rows(c), col0 + h * width:col0 + (h + 1) * width] for c, s, h in order])

    q = tiles(acts, 0, GDN_DK)
    k = tiles(acts, GDN_QK, GDN_DK)
    v = tiles(acts, 2 * GDN_QK, GDN_DV)
    beta = tiles(sigs, 0, 1)
    Gc = tiles(Gs, H, 1)
    Gr = jnp.stack([GTs[s][H + h:H + h + 1, rows(c)] for c, s, h in order])
    Gl = Gc[:, C - 1:C, :]

    qn = q * lax.rsqrt(jnp.sum(q * q, axis=-1, keepdims=True) + EPS) * (GDN_DK ** -0.5)
    kn = k * lax.rsqrt(jnp.sum(k * k, axis=-1, keepdims=True) + EPS)
    eG = jnp.exp(Gc)
    gam = jnp.where(incl, jnp.exp(jnp.where(incl, Gc - Gr, 0.0)), 0.0)
    kb = kn * beta
    kn16 = kn.astype(BF16)
    aq = _bmm_nt(jnp.concatenate([kb, qn], axis=1).astype(BF16), kn16)
    A = jnp.where(strict, aq[:, :C] * gam, 0.0)
    QK16 = (aq[:, C:] * gam).astype(BF16)

    n_joint = C.bit_length() - 2
    A16 = A.astype(BF16)
    N = -A
    Q = _bmm(A16, A16)
    for it in range(n_joint):
        Q16 = Q.astype(BF16)
        if it == n_joint - 1:
            N = N + Q + _bmm(N.astype(BF16), Q16)
        else:
            nq = _bmm(jnp.concatenate([N, Q], axis=1).astype(BF16), Q16)
            N = N + Q + nq[:, :C]
            Q = nq[:, C:]
    rhs = jnp.concatenate([v * beta, kb * eG], axis=-1)
    sol = rhs + _bmm(N.astype(BF16), rhs.astype(BF16))
    u = sol[:, :, :GDN_DV]
    wq16 = jnp.concatenate([sol[:, :, GDN_DV:], qn * eG], axis=1).astype(BF16)
    kg16 = (kn * jnp.exp(Gl - Gc)).astype(BF16)
    dl = jnp.exp(Gl)

    S = s_scr[...]
    o_parts = []
    for c in range(cps):
        sl = slice(c * nh, (c + 1) * nh)
        r = _bmm(wq16[sl], S.astype(BF16))
        vn16 = (u[sl] - r[:, :C]).astype(BF16)
        o_parts.append(r[:, C:] + _bmm(QK16[sl], vn16))
        S = S * dl[sl] + _bmm_tn(kg16[sl], vn16)
    s_scr[...] = S

    on = _rms(jnp.concatenate(o_parts, axis=0)) * gnw_ref[...]
    gates = []
    for s in range(bs):
        zs = z_ref[s]
        gates.append(zs * _sigmoid(zs))
    for idx, (c, s, h) in enumerate(order):
        cols = slice(h * GDN_DV, (h + 1) * GDN_DV)
        o_ref[s, rows(c), cols] = (on[idx] * gates[s][rows(c), cols]).astype(o_ref.dtype)

    @pl.when(j == n_steps - 1)
    def _():
        sout_ref[...] = S.reshape(bs, H, GDN_DK, GDN_DV)


def _gdn(act, z, ba, s0, alog, dtb, gnw, *, C, cps, bs):
    B, T, _ = act.shape
    R = C * cps
    n_steps = T // R
    row = lambda w: pl.BlockSpec((bs, R, w), lambda b, j: (b, j, 0))
    st = pl.BlockSpec((bs, GDN_HEADS, GDN_DK, GDN_DV), lambda b, j: (b, 0, 0, 0))
    return pl.pallas_call(
        functools.partial(_gdn_kernel, C=C, cps=cps, bs=bs, n_steps=n_steps),
        name="gdn",
        grid=(B // bs, n_steps),
        in_specs=[row(GDN_QKV), row(GDN_Z), row(LANES), st,
                  _const_spec(alog.shape), _const_spec(dtb.shape), _const_spec(gnw.shape)],
        out_specs=[row(GDN_Z), st],
        out_shape=[jax.ShapeDtypeStruct((B, T, GDN_Z), BF16),
                   jax.ShapeDtypeStruct((B, GDN_HEADS, GDN_DK, GDN_DV), F32)],
        scratch_shapes=[pltpu.VMEM((bs * GDN_HEADS, GDN_DK, GDN_DV), F32)],
        compiler_params=pltpu.CompilerParams(
            dimension_semantics=("arbitrary", "arbitrary"), vmem_limit_bytes=VMEM_LIMIT),
    )(act, z, ba, s0, alog, dtb, gnw)


def _attn_prompt_kernel(q_ref, k_ref, vt_ref, bias_ref, nw_ref, o_ref, *, qt):
    j = pl.program_id(1)
    tq = 2 * CHUNK
    n_kb = BAND_CHUNKS // 2 + 1
    n_pairs = ATT_HEADS // 2
    lane_head = lax.broadcasted_iota(jnp.int32, (tq, LANES), 1) // ATT_DH
    for i in range(qt):
        m = j * qt + i
        q = q_ref[0, i * tq:(i + 1) * tq, :]
        wt = []
        for p in range(n_pairs):
            qp = q[:, p * LANES:(p + 1) * LANES]
            zero = jnp.zeros_like(qp)
            wt.append(jnp.concatenate([jnp.where(lane_head == 0, qp, zero),
                                       jnp.where(lane_head == 1, qp, zero)], axis=0))
        wt = jnp.stack(wt)
        s_parts, v_parts = [], []
        for jb in range(n_kb):
            first = m - (n_kb - 1) + jb
            blk = jnp.maximum(first, 0)
            kb = k_ref[0, blk]
            kj = jnp.stack([kb[:, p * LANES:(p + 1) * LANES] for p in range(n_pairs)])
            s = _bmm_nt(kj, wt) + bias_ref[:, jb * tq:(jb + 1) * tq, :]
            if jb < n_kb - 1:
                s = jnp.where(first >= 0, s, -jnp.inf)
            s_parts.append(s)
            v_parts.append(vt_ref[0, blk].reshape(n_pairs, LANES, tq))
        st = jnp.concatenate(s_parts, axis=1)
        pt = jnp.exp2(st - jnp.max(st, axis=1, keepdims=True))
        inv = 1.0 / jnp.sum(pt, axis=1, keepdims=True)
        pt16 = pt.astype(BF16)
        acc = _bmm(v_parts[0], pt16[:, 0:tq, :])
        for jb in range(1, n_kb):
            acc = acc + _bmm(v_parts[jb], pt16[:, jb * tq:(jb + 1) * tq, :])
        ot = jnp.concatenate([acc[:, :ATT_DH, :tq] * inv[:, :, :tq],
                              acc[:, ATT_DH:, tq:] * inv[:, :, tq:]], axis=1)
        oh = ot.reshape(ATT_HEADS, ATT_DH, tq)
        on = oh * lax.rsqrt(jnp.mean(oh * oh, axis=1, keepdims=True) + EPS) * nw_ref[...]
        on = on.reshape(n_pairs, LANES, tq)
        for p in range(n_pairs):
            o_ref[0, i * tq:(i + 1) * tq, p * LANES:(p + 1) * LANES] = on[p].T.astype(o_ref.dtype)


def _attn_prompt(q, k, vt, bias, nw, *, qt):
    B, T, _ = q.shape
    tq = 2 * CHUNK
    n_steps = T // (qt * tq)
    k4 = k.reshape(B, T // tq, tq, ATT_W)
    row = pl.BlockSpec((1, qt * tq, ATT_W), lambda b, j: (b, j, 0))
    return pl.pallas_call(
        functools.partial(_attn_prompt_kernel, qt=qt),
        name="attn_prompt",
        grid=(B, n_steps),
        in_specs=[row,
                  pl.BlockSpec((1, T // tq, tq, ATT_W), lambda b, j: (b, 0, 0, 0)),
                  pl.BlockSpec((1, T // tq, ATT_W, tq), lambda b, j: (b, 0, 0, 0)),
                  _const_spec(bias.shape), _const_spec(nw.shape)],
        out_specs=row,
        out_shape=jax.ShapeDtypeStruct((B, T, ATT_W), BF16),
        compiler_params=pltpu.CompilerParams(
            dimension_semantics=("arbitrary", "arbitrary"), vmem_limit_bytes=VMEM_LIMIT),
    )(q, k4, vt, bias, nw)


def _attn_sample_kernel(q_ref, kc_ref, vc_ref, kn_ref, vn_ref, bc_ref, bn_ref, nw_ref, o_ref, *, n_seq):
    by_head = lambda a: jnp.stack([a[:, h * ATT_DH:(h + 1) * ATT_DH] for h in range(ATT_HEADS)])
    for s in range(n_seq):
        kc = kc_ref[s].astype(BF16)
        vc = vc_ref[s].astype(BF16)
        q = by_head(q_ref[s])
        kn = by_head(kn_ref[s])
        vn = by_head(vn_ref[s]).astype(BF16)
        s_c = _bmm(q, kc) + bc_ref[...]
        s_n = _bmm_nt(q, kn) + bn_ref[...]
        m = jnp.maximum(jnp.max(s_c, axis=-1, keepdims=True), jnp.max(s_n, axis=-1, keepdims=True))
        p_c = jnp.exp2(s_c - m)
        p_n = jnp.exp2(s_n - m)
        l = jnp.sum(p_c, axis=-1, keepdims=True) + jnp.sum(p_n, axis=-1, keepdims=True)
        o = (_bmm_nt(p_c.astype(BF16), vc) + _bmm(p_n.astype(BF16), vn)) * (1.0 / l)
        on = _rms(o) * nw_ref[...]
        o_ref[s] = jnp.concatenate([on[h] for h in range(ATT_HEADS)], axis=-1).astype(o_ref.dtype)


def _attn_sample(q, kc, vc, kn, vn, bias_c, bias_n, nw, *, n_seq):
    B, T, _ = q.shape
    new = pl.BlockSpec((n_seq, T, ATT_W), lambda b: (b, 0, 0))
    cache = pl.BlockSpec((n_seq,) + kc.shape[1:], lambda b: (b, 0, 0, 0))
    return pl.pallas_call(
        functools.partial(_attn_sample_kernel, n_seq=n_seq),
        name="attn_sample",
        grid=(B // n_seq,),
        in_specs=[new, cache, cache, new, new, _const_spec(bias_c.shape), _const_spec(bias_n.shape),
                  _const_spec(nw.shape)],
        out_specs=new,
        out_shape=jax.ShapeDtypeStruct((B, T, ATT_W), BF16),
        compiler_params=pltpu.CompilerParams(
            dimension_semantics=("arbitrary",), vmem_limit_bytes=VMEM_LIMIT),
    )(q, kc, vc, kn, vn, bias_c, bias_n, nw)


def _delayed(g, k, S, R, hist_row):
    rolled = pltpu.roll(g, k, axis=0)
    sub = lax.broadcasted_iota(jnp.int32, (SUBLANES, g.shape[1]), 0)
    parts = []
    for s in range(S):
        head = rolled[s * R:s * R + SUBLANES]
        for i in range(k):
            head = jnp.where(sub == i, hist_row(s, i), head)
        parts += [head, rolled[s * R + SUBLANES:(s + 1) * R]]
    return jnp.concatenate(parts, axis=0)


def _mix_ffn_kernel(x_ref, oa_ref, ob_ref, wo_ref, nmp_ref, nfp_ref, wg_ref, wu_ref, cw_ref, cb_ref,
                    wd_ref, nfo_ref, prev_ref, y_ref, st_ref, carry,
                    *, S, R, tps, d_ff):
    t = pl.program_id(0)
    hist = FFN_CONV - 1
    half = oa_ref.shape[-1]
    mix = _dot(oa_ref[...], wo_ref[:half, :]) + _dot(ob_ref[...], wo_ref[half:, :])
    x1 = x_ref[...] + _rms(mix) * nmp_ref[...]
    u2 = (_rms(x1) * nfp_ref[...]).astype(BF16)

    if tps > 1:
        @pl.when(t % tps == 0)
        def _():
            for i in range(hist):
                carry[i:i + 1, :] = prev_ref[0, i]

    fc = FFN_COLS
    n_fc = d_ff // fc
    acc = None
    proj = lambda c: (_dot(u2, wg_ref[:, c * fc:(c + 1) * fc]), _dot(u2, wu_ref[:, c * fc:(c + 1) * fc]))
    nxt = proj(0)
    for c in range(n_fc):
        cols = slice(c * fc, (c + 1) * fc)
        g, up = nxt
        if c + 1 < n_fc:
            nxt = proj(c + 1)
        if tps > 1:
            history = lambda s, i: carry[i:i + 1, cols]
        else:
            history = lambda s, i: prev_ref[0, i, s:s + 1, cols]
        cw = cw_ref[:, cols]
        conv = cb_ref[:, cols] + cw[hist:hist + 1] * g
        for k in range(1, FFN_CONV):
            conv = conv + cw[hist - k:hist - k + 1] * _delayed(
                g, k, S, R, lambda s, i, k=k: history(s, hist - k + i))
        for s in range(S):
            for i in range(hist):
                row = g[(s + 1) * R - hist + i:(s + 1) * R - hist + i + 1]
                if tps > 1:
                    carry[i:i + 1, cols] = row
                else:
                    st_ref[0, i, s:s + 1, cols] = row
        hid = (_gelu_tanh(conv) * up).astype(BF16)
        part = _dot(hid, wd_ref[cols, :])
        acc = part if acc is None else acc + part
    if tps > 1:
        for i in range(hist):
            st_ref[0, i] = carry[i:i + 1, :]
    y_ref[...] = x1 + _rms(acc) * nfo_ref[...]


def _mix_ffn(x, oa, ob, wo, nmp, nfp, wg, wu, cw, cb, wd, nfo, prev, *, S, R):
    M, D = x.shape
    d_ff = wg.shape[1]
    tm = S * R
    n_seq = prev.shape[0]
    hist = FFN_CONV - 1
    tps = M // (n_seq * R)
    assert S == 1 or tps == 1
    grouped = (n_seq // S, hist, S, d_ff)
    prev_g = prev.reshape(n_seq // S, S, hist, d_ff).transpose(0, 2, 1, 3)
    row = lambda w: pl.BlockSpec((tm, w), lambda t: (t, 0))
    st = pl.BlockSpec((1,) + grouped[1:], lambda t: (t // tps, 0, 0, 0))
    consts = [wo, nmp, nfp, wg, wu, cw, cb, wd, nfo]
    y, st_g = pl.pallas_call(
        functools.partial(_mix_ffn_kernel, S=S, R=R, tps=tps, d_ff=d_ff),
        name="mix_ffn",
        grid=(M // tm,),
        in_specs=[row(D), row(oa.shape[1]), row(ob.shape[1])] + [_const_spec(a.shape) for a in consts] + [st],
        out_specs=[row(D), st],
        out_shape=[jax.ShapeDtypeStruct((M, D), F32), jax.ShapeDtypeStruct(grouped, F32)],
        scratch_shapes=[pltpu.VMEM((hist, d_ff), F32)],
        compiler_params=pltpu.CompilerParams(
            dimension_semantics=("arbitrary",), vmem_limit_bytes=VMEM_LIMIT),
    )(x, oa, ob, *consts, prev_g)
    return y, st_g.transpose(0, 2, 1, 3).reshape(n_seq, hist, d_ff)


def _pad_rows_front(a, rows):
    return jnp.pad(a, ((0, 0), (rows - a.shape[1], 0), (0, 0)))


def _lane_row(vals, offset):
    return jnp.zeros((1, LANES), F32).at[0, offset:offset + vals.shape[0]].set(vals.astype(F32))


def _bias_kernel(r_ref, bp_ref, bc_ref, bn_ref, *, lc, ts):
    tq = 2 * CHUNK
    key = lax.broadcasted_iota(jnp.int32, (tq, tq), 0)
    query_chunk = lax.broadcasted_iota(jnp.int32, (tq, tq), 1) // CHUNK
    for h in range(r_ref.shape[0]):
        p, e = divmod(h, 2)
        for jb in range(r_ref.shape[1]):
            row = jnp.broadcast_to(r_ref[h, jb], (tq, 2 * tq))
            blk = pltpu.roll(row, 0, 1, stride=1, stride_axis=0)[:, :tq]
            key_in_band = jb * tq + key - query_chunk * CHUNK
            valid = (key_in_band >= 0) & (key_in_band < (BAND_CHUNKS + 1) * CHUNK)
            bp_ref[p, jb * tq:(jb + 1) * tq, e * tq:(e + 1) * tq] = jnp.where(valid, blk, -jnp.inf)
            by_query = blk.T
            if (jb + 1) * tq <= lc:
                bc_ref[h, :, jb * tq:(jb + 1) * tq] = by_query[:ts]
            else:
                bn_ref[h] = by_query[:ts, :ts]


def _band_biases(table, lc, ts):
    n_heads = table.shape[0]
    tq = 2 * CHUNK
    n_kb = BAND_CHUNKS // 2 + 1
    assert lc == (n_kb - 1) * tq and ts <= tq
    starts = [WINDOW - tq * jb + MAX_REL - half * tq for jb in range(n_kb) for half in (0, 1)]
    pad_l = max(0, -min(starts))
    pad_r = max(0, max(starts) + tq - table.shape[1])
    ext = jnp.pad(table, ((0, 0), (pad_l, pad_r)), mode="edge")
    rows = jnp.concatenate([ext[:, s + pad_l:s + pad_l + tq] for s in starts], axis=1)
    rows = rows.reshape(n_heads, n_kb, 1, 2 * tq)
    return pl.pallas_call(
        functools.partial(_bias_kernel, lc=lc, ts=ts),
        name="band_bias",
        out_shape=[jax.ShapeDtypeStruct((n_heads // 2, n_kb * tq, 2 * tq), F32),
                   jax.ShapeDtypeStruct((n_heads, ts, lc), F32),
                   jax.ShapeDtypeStruct((n_heads, ts, ts), F32)],
    )(rows)


def _layer(xp, xs, cache_k, cache_v, s_delta, s_qkv, s_ffn, lw):
    (norm_mix_pre, w_in, qkv_conv_w, a_log, dt_bias, gdn_norm_w, rel_bias, attn_norm_w, w_out,
     norm_mix_post, norm_ffn_pre, w_gate_up, ffn_conv_w, ffn_conv_b, w_down, norm_ffn_post) = lw
    Bp, Tp, D = xp.shape
    Bs, Ts, _ = xs.shape
    d_ff = w_down.shape[0]

    c1 = GDN_QKV + GDN_Z
    c2 = c1 + 2 * GDN_HEADS
    wa = w_in[:, :c1].astype(BF16)
    wg_in = jnp.pad(w_in[:, c1:c2], ((0, 0), (0, LANES - 2 * GDN_HEADS))).astype(BF16)
    wqk = w_in[:, c2:c2 + 2 * ATT_W].astype(BF16)
    wv = w_in[:, c2 + 2 * ATT_W:].astype(BF16)
    wkt = wqk[:, ATT_W:].T
    wvt = wv.T
    nmix = norm_mix_pre.reshape(1, D)
    cw_qkv = jnp.pad(qkv_conv_w, ((0, SUBLANES - GDN_CONV), (0, 0)))
    alog = _lane_row(a_log, GDN_HEADS)
    dtb = _lane_row(dt_bias, GDN_HEADS)
    gnw = gdn_norm_w.reshape(1, GDN_DV)
    anw = attn_norm_w.astype(F32).reshape(1, ATT_DH)
    anw_col = jnp.broadcast_to(attn_norm_w.astype(F32)[:, None], (ATT_DH, 2 * CHUNK))
    lc = cache_k.shape[1]
    bias_p, bias_c, bias_n = _band_biases(rel_bias.astype(F32) * LOG2E, lc, Ts)
    wo = w_out.astype(BF16)
    w_gate = w_gate_up[:, :d_ff].astype(BF16)
    w_up = w_gate_up[:, d_ff:].astype(BF16)
    wd = w_down.astype(BF16)
    cw_ffn = jnp.pad(ffn_conv_w, ((0, SUBLANES - FFN_CONV), (0, 0)))
    cb = ffn_conv_b.reshape(1, d_ff)
    nmp = norm_mix_post.reshape(1, D)
    nfp = norm_ffn_pre.reshape(1, D)
    nfo = norm_ffn_post.reshape(1, D)

    def group(x, keep_all, gdn_prev, gdn_s0, gdn_c, gdn_cps, gdn_bs, ffn_prev, S, R, attn):
        B, T, _ = x.shape
        xi = x if not keep_all else x.reshape(1, B * T, D)
        act, qkv_state, z, ba, q, k, vt, kf, vf = _inproj(xi, nmix, wa, wg_in, wqk, wv, wkt, wvt, cw_qkv, gdn_prev,
                                                          keep_all=keep_all)
        rs = lambda a: a.reshape(B, T, a.shape[-1])
        act, z, ba, q, k = map(rs, (act, z, ba, q, k))
        oa, s_new = _gdn(act, z, ba, gdn_s0, alog, dtb, gnw, C=gdn_c, cps=gdn_cps, bs=gdn_bs)
        ob = attn(q, k, vt, vf)
        y, ffn_state = _mix_ffn(x.reshape(B * T, D), oa.reshape(B * T, GDN_Z), ob.reshape(B * T, ATT_W),
                                wo, nmp, nfp, w_gate, w_up, cw_ffn, cb, wd, nfo, ffn_prev,
                                S=S, R=R)
        if keep_all:
            k_rows = kf.reshape(B, T, ATT_HEADS, ATT_DH)
            v_rows = vf.reshape(B, T, ATT_HEADS, ATT_DH)
        else:
            k_rows = jnp.transpose(kf, (0, 3, 1, 2))
            v_rows = jnp.transpose(vf, (0, 3, 1, 2))
        return y.reshape(B, T, D), k_rows, v_rows, s_new, qkv_state, ffn_state

    out_p = group(
        xp, False, jnp.zeros((Bp, GDN_CONV - 1, GDN_QKV), F32),
        jnp.zeros((Bp, GDN_HEADS, GDN_DK, GDN_DV), F32), CHUNK, 4, 4,
        jnp.zeros((Bp, FFN_CONV - 1, d_ff), F32), 1, ROW_TILE,
        lambda q, k, vt, vf: _attn_prompt(q, k, vt, bias_p, anw_col, qt=2))

    kc_t = jnp.transpose(cache_k, (0, 2, 3, 1))
    vc_t = jnp.transpose(cache_v, (0, 2, 3, 1))
    out_s = group(
        xs, True, s_qkv, s_delta, Ts, 1, 4, s_ffn, ROW_TILE // Ts, Ts,
        lambda q, k, vt, vf: _attn_sample(q, kc_t, vc_t, k, vf.reshape(Bs, Ts, ATT_W), bias_c, bias_n, anw,
                                          n_seq=2))
    return out_p, out_s


def kernel(x_prompt, x_sample, cache_band_k, cache_band_v, state_delta, state_qkv_conv, state_ffn_conv, norm_mix_pre, w_in, qkv_conv_w, a_log, dt_bias, gdn_norm_w, rel_bias, attn_norm_w, w_out, norm_mix_post, norm_ffn_pre, w_gate_up, ffn_conv_w, ffn_conv_b, w_down, norm_ffn_post):
    weights = (norm_mix_pre, w_in, qkv_conv_w, a_log, dt_bias, gdn_norm_w, rel_bias, attn_norm_w, w_out,
               norm_mix_post, norm_ffn_pre, w_gate_up, ffn_conv_w, ffn_conv_b, w_down, norm_ffn_post)
    depth = w_in.shape[0]
    xp, xs = x_prompt, x_sample
    outs_p, outs_s = [], []
    for l in range(depth):
        lw = tuple(w[l] for w in weights)
        op, os_ = _layer(xp, xs, cache_band_k[l], cache_band_v[l], state_delta[l], state_qkv_conv[l],
                         state_ffn_conv[l], lw)
        xp, xs = op[0], os_[0]
        outs_p.append(op[1:])
        outs_s.append(os_[1:])
    stack = lambda outs, i: jnp.stack([o[i] for o in outs], axis=0)
    return (xp, xs) + tuple(stack(outs_p, i) for i in range(5)) + tuple(stack(outs_s, i) for i in range(5))
```

```python
import functools

import jax
import jax.numpy as jnp
from jax import lax
from jax.experimental import pallas as pl
from jax.experimental.pallas import tpu as pltpu

F32 = jnp.float32
BF16 = jnp.bfloat16

EPS = 1e-6
CHUNK = 64
GDN_HEADS = 4
GDN_DK = 128
GDN_DV = 128
GDN_CONV = 4
ATT_HEADS = 8
ATT_DH = 64
BAND_CHUNKS = 8
WINDOW = BAND_CHUNKS * CHUNK
MAX_REL = 128
FFN_CONV = 3

GDN_QK = GDN_HEADS * GDN_DK
GDN_QKV = GDN_HEADS * (2 * GDN_DK + GDN_DV)
GDN_Z = GDN_HEADS * GDN_DV
ATT_W = ATT_HEADS * ATT_DH
LANES = 128
SUBLANES = 8
VMEM_LIMIT = 56 * 1024 * 1024
ROW_TILE = 512
FFN_COLS = 1408
LOG2E = 1.4426950408889634
Q_SCALE = ATT_DH ** -0.5 * LOG2E


def _dot(a, b):
    return jnp.dot(a, b, preferred_element_type=F32)


def _dot_nt(a, b):
    return lax.dot_general(a, b, (((1,), (1,)), ((), ())), preferred_element_type=F32)


def _dot_tn(a, b):
    return lax.dot_general(a, b, (((0,), (0,)), ((), ())), preferred_element_type=F32)


def _split3(x):
    x1 = x.astype(BF16)
    r1 = x - x1.astype(F32)
    x2 = r1.astype(BF16)
    x3 = (r1 - x2.astype(F32)).astype(BF16)
    return x1, x2, x3


def _dot_exact_lhs(a16, x):
    x1, x2, x3 = _split3(x)
    return _dot(a16, x1) + _dot(a16, x2) + _dot(a16, x3)


def _dot_exact_rhs(x, b16):
    x1, x2, x3 = _split3(x)
    return _dot(x1, b16) + _dot(x2, b16) + _dot(x3, b16)


def _sigmoid(x):
    return 0.5 + 0.5 * jnp.tanh(0.5 * x)


def _silu(x):
    h = 0.5 * x
    return h + h * jnp.tanh(h)


def _softplus(x):
    return jnp.maximum(x, 0.0) + jnp.log(1.0 + jnp.exp(-jnp.abs(x)))


def _rms(x):
    return x * lax.rsqrt(jnp.mean(x * x, axis=-1, keepdims=True) + EPS)


def _gelu_tanh(x):
    c = 0.7978845608028654
    return 0.5 * x * (1.0 + jnp.tanh(c * (x + 0.044715 * (x * x * x))))


def _const_spec(shape):
    n = len(shape)
    return pl.BlockSpec(shape, lambda *_: (0,) * n, pipeline_mode=pl.Buffered(1))


def _inproj_kernel(x_ref, nw_ref, wa_ref, wg_ref, wqk_ref, wv_ref, wkt_ref, wvt_ref, cw_ref, prev_ref,
                   act_ref, cst_ref, z_ref, ba_ref, q_ref, k_ref, vt_ref, kf_ref, vf_ref, carry,
                   *, keep_all, n_tiles, S, R, tps):
    x = x_ref[0]
    tm = x.shape[0]
    u = (_rms(x) * nw_ref[...]).astype(BF16)

    hist = GDN_CONV - 1
    if tps > 1:
        @pl.when(pl.program_id(1) % tps == 0)
        def _():
            for r in range(hist):
                carry[r:r + 1, :] = prev_ref[0, r]
    cw = cw_ref[...]
    slabs = [slice(c * GDN_QK, (c + 1) * GDN_QK) for c in range(GDN_QKV // GDN_QK)]

    def conv_silu(pre, cols):
        if tps > 1:
            history = lambda s, r, cols=cols: carry[r:r + 1, cols]
        else:
            history = lambda s, r, cols=cols: prev_ref[0, r, s:s + 1, cols]
        conv = cw[hist:hist + 1, cols] * pre
        for k in range(1, GDN_CONV):
            conv = conv + cw[hist - k:hist - k + 1, cols] * _delayed(
                pre, k, S, R, lambda s, r, k=k, history=history: history(s, hist - k + r))
        act_ref[0, :, cols] = _silu(conv)
        for s in range(S):
            for r in range(hist):
                row = pre[(s + 1) * R - hist + r:(s + 1) * R - hist + r + 1]
                if tps > 1:
                    carry[r:r + 1, cols] = row
                else:
                    cst_ref[0, r, s:s + 1, cols] = row

    pre = [_dot(u, wa_ref[:, slabs[0]]), _dot(u, wa_ref[:, slabs[1]])]
    conv_silu(pre[0], slabs[0])
    pre.append(_dot(u, wa_ref[:, slabs[2]]))
    conv_silu(pre[1], slabs[1])
    z_ref[0] = _dot(u, wa_ref[:, GDN_QKV:])
    ba_ref[0] = _dot(u, wg_ref[...])
    q_ref[0] = (_dot(u, wqk_ref[:, :ATT_W]) * Q_SCALE).astype(BF16)
    conv_silu(pre[2], slabs[2])
    if tps > 1:
        for r in range(hist):
            cst_ref[0, r] = carry[r:r + 1, :]
    k = _dot(u, wqk_ref[:, ATT_W:])
    k_ref[0] = k.astype(BF16)
    vt = _dot_nt(wvt_ref[...], u)
    for jb in range(tm // LANES):
        vt_ref[0, jb] = vt[:, jb * LANES:(jb + 1) * LANES].astype(BF16)

    if keep_all:
        kf_ref[0] = k
        vf_ref[0] = _dot(u, wv_ref[...])
    else:
        @pl.when(pl.program_id(1) == n_tiles - 1)
        def _():
            kf_ref[0] = _dot_nt(wkt_ref[...], u).reshape(ATT_HEADS, ATT_DH, tm)
            vf_ref[0] = vt.reshape(ATT_HEADS, ATT_DH, tm)


def _inproj(x, nw, wa, wg, wqk, wv, wkt, wvt, cw, prev, *, keep_all):
    B, T, D = x.shape
    tm = ROW_TILE
    nt = T // tm
    n_seq = prev.shape[0]
    hist = GDN_CONV - 1
    R = min(B * T // n_seq, tm)
    S = tm // R
    tps = B * T // (n_seq * R)
    assert S == 1 or tps == 1
    grouped = (n_seq // S, hist, S, GDN_QKV)
    prev_g = prev.reshape(n_seq // S, S, hist, GDN_QKV).transpose(0, 2, 1, 3)
    groups_per_b = nt // tps
    st = pl.BlockSpec((1,) + grouped[1:], lambda b, i: (b * groups_per_b + i // tps, 0, 0, 0))
    row = lambda w: pl.BlockSpec((1, tm, w), lambda b, i: (b, i, 0))
    if keep_all:
        keep = row(ATT_W)
        keep_shape = (B, T, ATT_W)
    else:
        keep = pl.BlockSpec((1, ATT_HEADS, ATT_DH, tm), lambda b, i: (b, 0, 0, 0))
        keep_shape = (B, ATT_HEADS, ATT_DH, tm)
    consts = [nw, wa, wg, wqk, wv, wkt, wvt, cw]
    act, cst, *rest = pl.pallas_call(
        functools.partial(_inproj_kernel, keep_all=keep_all, n_tiles=nt, S=S, R=R, tps=tps),
        name="inproj",
        grid=(B, nt),
        in_specs=[row(D)] + [_const_spec(a.shape) for a in consts] + [st],
        out_specs=[row(GDN_QKV), st, row(GDN_Z), row(LANES), row(ATT_W), row(ATT_W),
                   pl.BlockSpec((1, tm // LANES, ATT_W, LANES), lambda b, i: (b, i, 0, 0)), keep, keep],
        out_shape=[jax.ShapeDtypeStruct((B, T, GDN_QKV), F32),
                   jax.ShapeDtypeStruct(grouped, F32),
                   jax.ShapeDtypeStruct((B, T, GDN_Z), F32),
                   jax.ShapeDtypeStruct((B, T, LANES), F32),
                   jax.ShapeDtypeStruct((B, T, ATT_W), BF16),
                   jax.ShapeDtypeStruct((B, T, ATT_W), BF16),
                   jax.ShapeDtypeStruct((B, T // LANES, ATT_W, LANES), BF16),
                   jax.ShapeDtypeStruct(keep_shape, F32),
                   jax.ShapeDtypeStruct(keep_shape, F32)],
        scratch_shapes=[pltpu.VMEM((hist, GDN_QKV), F32)],
        compiler_params=pltpu.CompilerParams(
            dimension_semantics=("arbitrary", "arbitrary"), vmem_limit_bytes=VMEM_LIMIT),
    )(x, *consts, prev_g)
    return (act, cst.transpose(0, 2, 1, 3).reshape(n_seq, hist, GDN_QKV), *rest)


def _bmm(a, b):
    return lax.dot_general(a, b, (((2,), (1,)), ((0,), (0,))), preferred_element_type=F32)


def _bmm_nt(a, b):
    return lax.dot_general(a, b, (((2,), (2,)), ((0,), (0,))), preferred_element_type=F32)


def _bmm_tn(a, b):
    return lax.dot_general(a, b, (((1,), (1,)), ((0,), (0,))), preferred_element_type=F32)


def _gdn_kernel(act_ref, z_ref, ba_ref, s0_ref, alog_ref, dtb_ref, gnw_ref,
                o_ref, sout_ref, s_scr, *, C, cps, bs, n_steps):
    j = pl.program_id(1)
    R = C * cps
    H = GDN_HEADS
    nh = bs * H

    @pl.when(j == 0)
    def _():
        s_scr[...] = s0_ref[...].reshape(nh, GDN_DK, GDN_DV)

    ri = lax.broadcasted_iota(jnp.int32, (C, C), 0)
    ci = lax.broadcasted_iota(jnp.int32, (C, C), 1)
    incl = ri >= ci
    strict = ri > ci
    rr = lax.broadcasted_iota(jnp.int32, (R, R), 0)
    cc = lax.broadcasted_iota(jnp.int32, (R, R), 1)
    cum16 = jnp.where((rr >= cc) & (rr // C == cc // C), 1.0, 0.0).astype(BF16)
    alog = alog_ref[...]
    dtb = dtb_ref[...]

    acts, sigs, Gs, GTs = [], [], [], []
    for s in range(bs):
        acts.append(act_ref.at[s])
        ba = ba_ref[s]
        sigs.append(_sigmoid(ba))
        G = _dot_exact_lhs(cum16, -jnp.exp(alog) * _softplus(ba + dtb))
        Gs.append(G)
        GTs.append(G.T)

    order = [(c, s, h) for c in range(cps) for s in range(bs) for h in range(H)]
    rows = lambda c: slice(c * C, (c + 1) * C)

    def tiles(slabs, col0, width):
        return jnp.stack([slabs[s][rows(c), col0 + h * width:col0 + (h + 1) * width] for c, s, h in order])

    q = tiles(acts, 0, GDN_DK)
    k = tiles(acts, GDN_QK, GDN_DK)
    v = tiles(acts, 2 * GDN_QK, GDN_DV)
    beta = tiles(sigs, 0, 1)
    Gc = tiles(Gs, H, 1)
    Gr = jnp.stack([GTs[s][H + h:H + h + 1, rows(c)] for c, s, h in order])
    Gl = Gc[:, C - 1:C, :]

    qn = q * lax.rsqrt(jnp.sum(q * q, axis=-1, keepdims=True) + EPS) * (GDN_DK ** -0.5)
    kn = k * lax.rsqrt(jnp.sum(k * k, axis=-1, keepdims=True) + EPS)
    eG = jnp.exp(Gc)
    gam = jnp.where(incl, jnp.exp(jnp.where(incl, Gc - Gr, 0.0)), 0.0)
    kb = kn * beta
    kn16 = kn.astype(BF16)
    aq = _bmm_nt(jnp.concatenate([kb, qn], axis=1).astype(BF16), kn16)
    A = jnp.where(strict, aq[:, :C] * gam, 0.0)
    QK16 = (aq[:, C:] * gam).astype(BF16)

    n_joint = C.bit_length() - 2
    A16 = A.astype(BF16)
    N = -A
    Q = _bmm(A16, A16)
    for it in range(n_joint):
        Q16 = Q.astype(BF16)
        if it == n_joint - 1:
            N = N + Q + _bmm(N.astype(BF16), Q16)
        else:
            nq = _bmm(jnp.concatenate([N, Q], axis=1).astype(BF16), Q16)
            N = N + Q + nq[:, :C]
            Q = nq[:, C:]
    rhs = jnp.concatenate([v * beta, kb * eG], axis=-1)
    sol = rhs + _bmm(N.astype(BF16), rhs.astype(BF16))
    u = sol[:, :, :GDN_DV]
    wq16 = jnp.concatenate([sol[:, :, GDN_DV:], qn * eG], axis=1).astype(BF16)
    kg16 = (kn * jnp.exp(Gl - Gc)).astype(BF16)
    dl = jnp.exp(Gl)

    S = s_scr[...]
    o_parts = []
    for c in range(cps):
        sl = slice(c * nh, (c + 1) * nh)
        r = _bmm(wq16[sl], S.astype(BF16))
        vn16 = (u[sl] - r[:, :C]).astype(BF16)
        o_parts.append(r[:, C:] + _bmm(QK16[sl], vn16))
        S = S * dl[sl] + _bmm_tn(kg16[sl], vn16)
    s_scr[...] = S

    on = _rms(jnp.concatenate(o_parts, axis=0)) * gnw_ref[...]
    gates = []
    for s in range(bs):
        zs = z_ref[s]
        gates.append(_silu(zs))
    for idx, (c, s, h) in enumerate(order):
        cols = slice(h * GDN_DV, (h + 1) * GDN_DV)
        o_ref[s, rows(c), cols] = (on[idx] * gates[s][rows(c), cols]).astype(o_ref.dtype)

    @pl.when(j == n_steps - 1)
    def _():
        sout_ref[...] = S.reshape(bs, H, GDN_DK, GDN_DV)


def _gdn(act, z, ba, s0, alog, dtb, gnw, *, C, cps, bs):
    B, T, _ = act.shape
    R = C * cps
    n_steps = T // R
    row = lambda w: pl.BlockSpec((bs, R, w), lambda b, j: (b, j, 0))
    st = pl.BlockSpec((bs, GDN_HEADS, GDN_DK, GDN_DV), lambda b, j: (b, 0, 0, 0))
    return pl.pallas_call(
        functools.partial(_gdn_kernel, C=C, cps=cps, bs=bs, n_steps=n_steps),
        name="gdn",
        grid=(B // bs, n_steps),
        in_specs=[row(GDN_QKV), row(GDN_Z), row(LANES), st,
                  _const_spec(alog.shape), _const_spec(dtb.shape), _const_spec(gnw.shape)],
        out_specs=[row(GDN_Z), st],
        out_shape=[jax.ShapeDtypeStruct((B, T, GDN_Z), BF16),
                   jax.ShapeDtypeStruct((B, GDN_HEADS, GDN_DK, GDN_DV), F32)],
        scratch_shapes=[pltpu.VMEM((bs * GDN_HEADS, GDN_DK, GDN_DV), F32)],
        compiler_params=pltpu.CompilerParams(
            dimension_semantics=("arbitrary", "arbitrary"), vmem_limit_bytes=VMEM_LIMIT),
    )(act, z, ba, s0, alog, dtb, gnw)


def _attn_prompt_kernel(q_ref, k_ref, vt_ref, bias_ref, nw_ref, o_ref, *, qt):
    j = pl.program_id(1)
    tq = 2 * CHUNK
    n_kb = BAND_CHUNKS // 2 + 1
    n_pairs = ATT_HEADS // 2
    lane_head = lax.broadcasted_iota(jnp.int32, (tq, LANES), 1) // ATT_DH
    for i in range(qt):
        m = j * qt + i
        q = q_ref[0, i * tq:(i + 1) * tq, :]
        wt = []
        for p in range(n_pairs):
            qp = q[:, p * LANES:(p + 1) * LANES]
            zero = jnp.zeros_like(qp)
            wt.append(jnp.concatenate([jnp.where(lane_head == 0, qp, zero),
                                       jnp.where(lane_head == 1, qp, zero)], axis=0))
        wt = jnp.stack(wt)
        s_parts, v_parts = [], []
        for jb in range(n_kb):
            first = m - (n_kb - 1) + jb
            blk = jnp.maximum(first, 0)
            kb = k_ref[0, blk]
            kj = jnp.stack([kb[:, p * LANES:(p + 1) * LANES] for p in range(n_pairs)])
            s = _bmm_nt(kj, wt) + bias_ref[:, jb * tq:(jb + 1) * tq, :]
            if jb < n_kb - 1:
                s = jnp.where(first >= 0, s, -jnp.inf)
            s_parts.append(s)
            v_parts.append(vt_ref[0, blk].reshape(n_pairs, LANES, tq))
        st = jnp.concatenate(s_parts, axis=1)
        pt = jnp.exp2(st - jnp.max(st, axis=1, keepdims=True))
        inv = 1.0 / jnp.sum(pt, axis=1, keepdims=True)
        pt16 = pt.astype(BF16)
        acc = _bmm(v_parts[0], pt16[:, 0:tq, :])
        for jb in range(1, n_kb):
            acc = acc + _bmm(v_parts[jb], pt16[:, jb * tq:(jb + 1) * tq, :])
        ot = jnp.concatenate([acc[:, :ATT_DH, :tq] * inv[:, :, :tq],
                              acc[:, ATT_DH:, tq:] * inv[:, :, tq:]], axis=1)
        oh = ot.reshape(ATT_HEADS, ATT_DH, tq)
        on = oh * lax.rsqrt(jnp.mean(oh * oh, axis=1, keepdims=True) + EPS) * nw_ref[...]
        on = on.reshape(n_pairs, LANES, tq)
        for p in range(n_pairs):
            o_ref[0, i * tq:(i + 1) * tq, p * LANES:(p + 1) * LANES] = on[p].T.astype(o_ref.dtype)


def _attn_prompt(q, k, vt, bias, nw, *, qt):
    B, T, _ = q.shape
    tq = 2 * CHUNK
    n_steps = T // (qt * tq)
    k4 = k.reshape(B, T // tq, tq, ATT_W)
    row = pl.BlockSpec((1, qt * tq, ATT_W), lambda b, j: (b, j, 0))
    return pl.pallas_call(
        functools.partial(_attn_prompt_kernel, qt=qt),
        name="attn_prompt",
        grid=(B, n_steps),
        in_specs=[row,
                  pl.BlockSpec((1, T // tq, tq, ATT_W), lambda b, j: (b, 0, 0, 0)),
                  pl.BlockSpec((1, T // tq, ATT_W, tq), lambda b, j: (b, 0, 0, 0)),
                  _const_spec(bias.shape), _const_spec(nw.shape)],
        out_specs=row,
        out_shape=jax.ShapeDtypeStruct((B, T, ATT_W), BF16),
        compiler_params=pltpu.CompilerParams(
            dimension_semantics=("arbitrary", "arbitrary"), vmem_limit_bytes=VMEM_LIMIT),
    )(q, k4, vt, bias, nw)


def _attn_sample_kernel(q_ref, kc_ref, vc_ref, kn_ref, vn_ref, bc_ref, bn_ref, nw_ref, o_ref, *, n_seq):
    by_head = lambda a: jnp.stack([a[:, h * ATT_DH:(h + 1) * ATT_DH] for h in range(ATT_HEADS)])
    for s in range(n_seq):
        kc = kc_ref[s].astype(BF16)
        vc = vc_ref[s].astype(BF16)
        q = by_head(q_ref[s])
        kn = by_head(kn_ref[s])
        vn = by_head(vn_ref[s]).astype(BF16)
        s_c = _bmm(q, kc) + bc_ref[...]
        s_n = _bmm_nt(q, kn) + bn_ref[...]
        m = jnp.maximum(jnp.max(s_c, axis=-1, keepdims=True), jnp.max(s_n, axis=-1, keepdims=True))
        p_c = jnp.exp2(s_c - m)
        p_n = jnp.exp2(s_n - m)
        l = jnp.sum(p_c, axis=-1, keepdims=True) + jnp.sum(p_n, axis=-1, keepdims=True)
        o = (_bmm_nt(p_c.astype(BF16), vc) + _bmm(p_n.astype(BF16), vn)) * (1.0 / l)
        on = _rms(o) * nw_ref[...]
        o_ref[s] = jnp.concatenate([on[h] for h in range(ATT_HEADS)], axis=-1).astype(o_ref.dtype)


def _attn_sample(q, kc, vc, kn, vn, bias_c, bias_n, nw, *, n_seq):
    B, T, _ = q.shape
    new = pl.BlockSpec((n_seq, T, ATT_W), lambda b: (b, 0, 0))
    cache = pl.BlockSpec((n_seq,) + kc.shape[1:], lambda b: (b, 0, 0, 0))
    return pl.pallas_call(
        functools.partial(_attn_sample_kernel, n_seq=n_seq),
        name="attn_sample",
        grid=(B // n_seq,),
        in_specs=[new, cache, cache, new, new, _const_spec(bias_c.shape), _const_spec(bias_n.shape),
                  _const_spec(nw.shape)],
        out_specs=new,
        out_shape=jax.ShapeDtypeStruct((B, T, ATT_W), BF16),
        compiler_params=pltpu.CompilerParams(
            dimension_semantics=("arbitrary",), vmem_limit_bytes=VMEM_LIMIT),
    )(q, kc, vc, kn, vn, bias_c, bias_n, nw)


def _delayed(g, k, S, R, hist_row):
    rolled = pltpu.roll(g, k, axis=0)
    sub = lax.broadcasted_iota(jnp.int32, (SUBLANES, g.shape[1]), 0)
    parts = []
    for s in range(S):
        head = rolled[s * R:s * R + SUBLANES]
        for i in range(k):
            head = jnp.where(sub == i, hist_row(s, i), head)
        parts += [head, rolled[s * R + SUBLANES:(s + 1) * R]]
    return jnp.concatenate(parts, axis=0)


def _mix_ffn_kernel(x_ref, oa_ref, ob_ref, wo_ref, nmp_ref, nfp_ref, wg_ref, wu_ref, cw_ref, cb_ref,
                    wd_ref, nfo_ref, prev_ref, y_ref, st_ref, carry,
                    *, S, R, tps, d_ff):
    t = pl.program_id(0)
    hist = FFN_CONV - 1
    half = oa_ref.shape[-1]
    mix = _dot(oa_ref[...], wo_ref[:half, :]) + _dot(ob_ref[...], wo_ref[half:, :])
    x1 = x_ref[...] + _rms(mix) * nmp_ref[...]
    u2 = (_rms(x1) * nfp_ref[...]).astype(BF16)

    if tps > 1:
        @pl.when(t % tps == 0)
        def _():
            for i in range(hist):
                carry[i:i + 1, :] = prev_ref[0, i]

    fc = FFN_COLS
    n_fc = d_ff // fc
    acc = None
    proj = lambda c: (_dot(u2, wg_ref[:, c * fc:(c + 1) * fc]), _dot(u2, wu_ref[:, c * fc:(c + 1) * fc]))
    nxt = proj(0)
    for c in range(n_fc):
        cols = slice(c * fc, (c + 1) * fc)
        g, up = nxt
        if c + 1 < n_fc:
            nxt = proj(c + 1)
        if tps > 1:
            history = lambda s, i: carry[i:i + 1, cols]
        else:
            history = lambda s, i: prev_ref[0, i, s:s + 1, cols]
        cw = cw_ref[:, cols]
        conv = cb_ref[:, cols] + cw[hist:hist + 1] * g
        for k in range(1, FFN_CONV):
            conv = conv + cw[hist - k:hist - k + 1] * _delayed(
                g, k, S, R, lambda s, i, k=k: history(s, hist - k + i))
        for s in range(S):
            for i in range(hist):
                row = g[(s + 1) * R - hist + i:(s + 1) * R - hist + i + 1]
                if tps > 1:
                    carry[i:i + 1, cols] = row
                else:
                    st_ref[0, i, s:s + 1, cols] = row
        hid = (_gelu_tanh(conv) * up).astype(BF16)
        part = _dot(hid, wd_ref[cols, :])
        acc = part if acc is None else acc + part
    if tps > 1:
        for i in range(hist):
            st_ref[0, i] = carry[i:i + 1, :]
    y_ref[...] = x1 + _rms(acc) * nfo_ref[...]


def _mix_ffn(x, oa, ob, wo, nmp, nfp, wg, wu, cw, cb, wd, nfo, prev, *, S, R):
    M, D = x.shape
    d_ff = wg.shape[1]
    tm = S * R
    n_seq = prev.shape[0]
    hist = FFN_CONV - 1
    tps = M // (n_seq * R)
    assert S == 1 or tps == 1
    grouped = (n_seq // S, hist, S, d_ff)
    prev_g = prev.reshape(n_seq // S, S, hist, d_ff).transpose(0, 2, 1, 3)
    row = lambda w: pl.BlockSpec((tm, w), lambda t: (t, 0))
    st = pl.BlockSpec((1,) + grouped[1:], lambda t: (t // tps, 0, 0, 0))
    consts = [wo, nmp, nfp, wg, wu, cw, cb, wd, nfo]
    y, st_g = pl.pallas_call(
        functools.partial(_mix_ffn_kernel, S=S, R=R, tps=tps, d_ff=d_ff),
        name="mix_ffn",
        grid=(M // tm,),
        in_specs=[row(D), row(oa.shape[1]), row(ob.shape[1])] + [_const_spec(a.shape) for a in consts] + [st],
        out_specs=[row(D), st],
        out_shape=[jax.ShapeDtypeStruct((M, D), F32), jax.ShapeDtypeStruct(grouped, F32)],
        scratch_shapes=[pltpu.VMEM((hist, d_ff), F32)],
        compiler_params=pltpu.CompilerParams(
            dimension_semantics=("arbitrary",), vmem_limit_bytes=VMEM_LIMIT),
    )(x, oa, ob, *consts, prev_g)
    return y, st_g.transpose(0, 2, 1, 3).reshape(n_seq, hist, d_ff)


def _pad_rows_front(a, rows):
    return jnp.pad(a, ((0, 0), (rows - a.shape[1], 0), (0, 0)))


def _lane_row(vals, offset):
    return jnp.zeros((1, LANES), F32).at[0, offset:offset + vals.shape[0]].set(vals.astype(F32))


def _bias_kernel(r_ref, bp_ref, bc_ref, bn_ref, *, lc, ts):
    tq = 2 * CHUNK
    key = lax.broadcasted_iota(jnp.int32, (tq, tq), 0)
    query_chunk = lax.broadcasted_iota(jnp.int32, (tq, tq), 1) // CHUNK
    for h in range(r_ref.shape[0]):
        p, e = divmod(h, 2)
        for jb in range(r_ref.shape[1]):
            row = jnp.broadcast_to(r_ref[h, jb], (tq, 2 * tq))
            blk = pltpu.roll(row, 0, 1, stride=1, stride_axis=0)[:, :tq]
            key_in_band = jb * tq + key - query_chunk * CHUNK
            valid = (key_in_band >= 0) & (key_in_band < (BAND_CHUNKS + 1) * CHUNK)
            bp_ref[p, jb * tq:(jb + 1) * tq, e * tq:(e + 1) * tq] = jnp.where(valid, blk, -jnp.inf)
            by_query = blk.T
            if (jb + 1) * tq <= lc:
                bc_ref[h, :, jb * tq:(jb + 1) * tq] = by_query[:ts]
            else:
                bn_ref[h] = by_query[:ts, :ts]


def _band_biases(table, lc, ts):
    n_heads = table.shape[0]
    tq = 2 * CHUNK
    n_kb = BAND_CHUNKS // 2 + 1
    assert lc == (n_kb - 1) * tq and ts <= tq
    starts = [WINDOW - tq * jb + MAX_REL - half * tq for jb in range(n_kb) for half in (0, 1)]
    pad_l = max(0, -min(starts))
    pad_r = max(0, max(starts) + tq - table.shape[1])
    ext = jnp.pad(table, ((0, 0), (pad_l, pad_r)), mode="edge")
    rows = jnp.concatenate([ext[:, s + pad_l:s + pad_l + tq] for s in starts], axis=1)
    rows = rows.reshape(n_heads, n_kb, 1, 2 * tq)
    return pl.pallas_call(
        functools.partial(_bias_kernel, lc=lc, ts=ts),
        name="band_bias",
        out_shape=[jax.ShapeDtypeStruct((n_heads // 2, n_kb * tq, 2 * tq), F32),
                   jax.ShapeDtypeStruct((n_heads, ts, lc), F32),
                   jax.ShapeDtypeStruct((n_heads, ts, ts), F32)],
    )(rows)


def _layer(xp, xs, cache_k, cache_v, s_delta, s_qkv, s_ffn, lw):
    (norm_mix_pre, w_in, qkv_conv_w, a_log, dt_bias, gdn_norm_w, rel_bias, attn_norm_w, w_out,
     norm_mix_post, norm_ffn_pre, w_gate_up, ffn_conv_w, ffn_conv_b, w_down, norm_ffn_post) = lw
    Bp, Tp, D = xp.shape
    Bs, Ts, _ = xs.shape
    d_ff = w_down.shape[0]

    c1 = GDN_QKV + GDN_Z
    c2 = c1 + 2 * GDN_HEADS
    wa = w_in[:, :c1].astype(BF16)
    wg_in = jnp.pad(w_in[:, c1:c2], ((0, 0), (0, LANES - 2 * GDN_HEADS))).astype(BF16)
    wqk = w_in[:, c2:c2 + 2 * ATT_W].astype(BF16)
    wv = w_in[:, c2 + 2 * ATT_W:].astype(BF16)
    wkt = wqk[:, ATT_W:].T
    wvt = wv.T
    nmix = norm_mix_pre.reshape(1, D)
    cw_qkv = jnp.pad(qkv_conv_w, ((0, SUBLANES - GDN_CONV), (0, 0)))
    alog = _lane_row(a_log, GDN_HEADS)
    dtb = _lane_row(dt_bias, GDN_HEADS)
    gnw = gdn_norm_w.reshape(1, GDN_DV)
    anw = attn_norm_w.astype(F32).reshape(1, ATT_DH)
    anw_col = jnp.broadcast_to(attn_norm_w.astype(F32)[:, None], (ATT_DH, 2 * CHUNK))
    lc = cache_k.shape[1]
    bias_p, bias_c, bias_n = _band_biases(rel_bias.astype(F32) * LOG2E, lc, Ts)
    wo = w_out.astype(BF16)
    w_gate = w_gate_up[:, :d_ff].astype(BF16)
    w_up = w_gate_up[:, d_ff:].astype(BF16)
    wd = w_down.astype(BF16)
    cw_ffn = jnp.pad(ffn_conv_w, ((0, SUBLANES - FFN_CONV), (0, 0)))
    cb = ffn_conv_b.reshape(1, d_ff)
    nmp = norm_mix_post.reshape(1, D)
    nfp = norm_ffn_pre.reshape(1, D)
    nfo = norm_ffn_post.reshape(1, D)

    def group(x, keep_all, gdn_prev, gdn_s0, gdn_c, gdn_cps, gdn_bs, ffn_prev, S, R, attn):
        B, T, _ = x.shape
        xi = x if not keep_all else x.reshape(1, B * T, D)
        act, qkv_state, z, ba, q, k, vt, kf, vf = _inproj(xi, nmix, wa, wg_in, wqk, wv, wkt, wvt, cw_qkv, gdn_prev,
                                                          keep_all=keep_all)
        rs = lambda a: a.reshape(B, T, a.shape[-1])
        act, z, ba, q, k = map(rs, (act, z, ba, q, k))
        oa, s_new = _gdn(act, z, ba, gdn_s0, alog, dtb, gnw, C=gdn_c, cps=gdn_cps, bs=gdn_bs)
        ob = attn(q, k, vt, vf)
        y, ffn_state = _mix_ffn(x.reshape(B * T, D), oa.reshape(B * T, GDN_Z), ob.reshape(B * T, ATT_W),
                                wo, nmp, nfp, w_gate, w_up, cw_ffn, cb, wd, nfo, ffn_prev,
                                S=S, R=R)
        if keep_all:
            k_rows = kf.reshape(B, T, ATT_HEADS, ATT_DH)
            v_rows = vf.reshape(B, T, ATT_HEADS, ATT_DH)
        else:
            k_rows = jnp.transpose(kf, (0, 3, 1, 2))
            v_rows = jnp.transpose(vf, (0, 3, 1, 2))
        return y.reshape(B, T, D), k_rows, v_rows, s_new, qkv_state, ffn_state

    out_p = group(
        xp, False, jnp.zeros((Bp, GDN_CONV - 1, GDN_QKV), F32),
        jnp.zeros((Bp, GDN_HEADS, GDN_DK, GDN_DV), F32), CHUNK, 4, 4,
        jnp.zeros((Bp, FFN_CONV - 1, d_ff), F32), 1, ROW_TILE,
        lambda q, k, vt, vf: _attn_prompt(q, k, vt, bias_p, anw_col, qt=2))

    kc_t = jnp.transpose(cache_k, (0, 2, 3, 1))
    vc_t = jnp.transpose(cache_v, (0, 2, 3, 1))
    out_s = group(
        xs, True, s_qkv, s_delta, Ts, 1, 4, s_ffn, ROW_TILE // Ts, Ts,
        lambda q, k, vt, vf: _attn_sample(q, kc_t, vc_t, k, vf.reshape(Bs, Ts, ATT_W), bias_c, bias_n, anw,
                                          n_seq=2))
    return out_p, out_s


def kernel(x_prompt, x_sample, cache_band_k, cache_band_v, state_delta, state_qkv_conv, state_ffn_conv, norm_mix_pre, w_in, qkv_conv_w, a_log, dt_bias, gdn_norm_w, rel_bias, attn_norm_w, w_out, norm_mix_post, norm_ffn_pre, w_gate_up, ffn_conv_w, ffn_conv_b, w_down, norm_ffn_post):
    weights = (norm_mix_pre, w_in, qkv_conv_w, a_log, dt_bias, gdn_norm_w, rel_bias, attn_norm_w, w_out,
               norm_mix_post, norm_ffn_pre, w_gate_up, ffn_conv_w, ffn_conv_b, w_down, norm_ffn_post)
    depth = w_in.shape[0]
    xp, xs = x_prompt, x_sample
    outs_p, outs_s = [], []
    for l in range(depth):
        lw = tuple(w[l] for w in weights)
        op, os_ = _layer(xp, xs, cache_band_k[l], cache_band_v[l], state_delta[l], state_qkv_conv[l],
                         state_ffn_conv[l], lw)
        xp, xs = op[0], os_[0]
        outs_p.append(op[1:])
        outs_s.append(os_[1:])
    stack = lambda outs, i: jnp.stack([o[i] for o in outs], axis=0)
    return (xp, xs) + tuple(stack(outs_p, i) for i in range(5)) + tuple(stack(outs_s, i) for i in range(5))
```

```python
import functools

import jax
import jax.numpy as jnp
from jax import lax
from jax.experimental import pallas as pl
from jax.experimental.pallas import tpu as pltpu

F32 = jnp.float32
BF16 = jnp.bfloat16

EPS = 1e-6
CHUNK = 64
GDN_HEADS = 4
GDN_DK = 128
GDN_DV = 128
GDN_CONV = 4
ATT_HEADS = 8
ATT_DH = 64
BAND_CHUNKS = 8
WINDOW = BAND_CHUNKS * CHUNK
MAX_REL = 128
FFN_CONV = 3

GDN_QK = GDN_HEADS * GDN_DK
GDN_QKV = GDN_HEADS * (2 * GDN_DK + GDN_DV)
GDN_Z = GDN_HEADS * GDN_DV
ATT_W = ATT_HEADS * ATT_DH
LANES = 128
SUBLANES = 8
VMEM_LIMIT = 56 * 1024 * 1024
ROW_TILE = 512
FFN_COLS = 2816
_FLAT_BIAS_BLOCKS = tuple(
    jb for jb in range(BAND_CHUNKS // 2 + 1)
    if jb * 2 * CHUNK >= CHUNK and (jb + 1) * 2 * CHUNK <= (BAND_CHUNKS + 1) * CHUNK
    and WINDOW - (jb + 1) * 2 * CHUNK + 1 >= MAX_REL)
LOG2E = 1.4426950408889634
Q_SCALE = ATT_DH ** -0.5 * LOG2E


def _dot(a, b):
    return jnp.dot(a, b, preferred_element_type=F32)


def _dot_nt(a, b):
    return lax.dot_general(a, b, (((1,), (1,)), ((), ())), preferred_element_type=F32)


def _dot_tn(a, b):
    return lax.dot_general(a, b, (((0,), (0,)), ((), ())), preferred_element_type=F32)


def _split3(x):
    x1 = x.astype(BF16)
    r1 = x - x1.astype(F32)
    x2 = r1.astype(BF16)
    x3 = (r1 - x2.astype(F32)).astype(BF16)
    return x1, x2, x3


def _dot_exact_lhs(a16, x):
    x1, x2, x3 = _split3(x)
    return _dot(a16, x1) + _dot(a16, x2) + _dot(a16, x3)


def _dot_exact_rhs(x, b16):
    x1, x2, x3 = _split3(x)
    return _dot(x1, b16) + _dot(x2, b16) + _dot(x3, b16)


def _sigmoid(x):
    return 0.5 + 0.5 * jnp.tanh(0.5 * x)


def _silu(x):
    h = 0.5 * x
    return h + h * jnp.tanh(h)


def _softplus(x):
    return jnp.maximum(x, 0.0) + jnp.log(1.0 + jnp.exp(-jnp.abs(x)))


def _rms(x):
    return x * lax.rsqrt(jnp.mean(x * x, axis=-1, keepdims=True) + EPS)


def _gelu_tanh(x):
    c = 0.7978845608028654
    return 0.5 * x * (1.0 + jnp.tanh(c * (x + 0.044715 * (x * x * x))))


def _const_spec(shape):
    n = len(shape)
    return pl.BlockSpec(shape, lambda *_: (0,) * n, pipeline_mode=pl.Buffered(1))


def _inproj_kernel(x_ref, nw_ref, w_ref, wv_ref, wkt_ref, wvt_ref, cw_ref, prev_ref,
                   act_ref, cst_ref, z_ref, ba_ref, q_ref, k_ref, vt_ref, kf_ref, vf_ref, carry,
                   *, keep_all, n_tiles, S, R, tps):
    x = x_ref[0]
    tm = x.shape[0]
    u = (_rms(x) * nw_ref[...]).astype(BF16)

    hist = GDN_CONV - 1
    if tps > 1:
        @pl.when(pl.program_id(1) % tps == 0)
        def _():
            for r in range(hist):
                carry[r:r + 1, :] = prev_ref[0, r]
    cw = cw_ref[...]
    slabs = [slice(c * GDN_QK, (c + 1) * GDN_QK) for c in range(GDN_QKV // GDN_QK)]

    def conv_silu(pre, cols):
        if tps > 1:
            history = lambda s, r, cols=cols: carry[r:r + 1, cols]
        else:
            history = lambda s, r, cols=cols: prev_ref[0, r, s:s + 1, cols]
        conv = cw[hist:hist + 1, cols] * pre
        for k in range(1, GDN_CONV):
            conv = conv + cw[hist - k:hist - k + 1, cols] * _delayed(
                pre, k, S, R, lambda s, r, k=k, history=history: history(s, hist - k + r))
        act_ref[0, :, cols] = _silu(conv)
        for s in range(S):
            for r in range(hist):
                row = pre[(s + 1) * R - hist + r:(s + 1) * R - hist + r + 1]
                if tps > 1:
                    carry[r:r + 1, cols] = row
                else:
                    cst_ref[0, r, s:s + 1, cols] = row

    p = _dot(u, w_ref[...])
    for cols in slabs:
        conv_silu(p[:, cols], cols)
    if tps > 1:
        for r in range(hist):
            cst_ref[0, r] = carry[r:r + 1, :]
    c0 = GDN_QKV + GDN_Z
    c1 = c0 + LANES
    z_ref[0] = p[:, GDN_QKV:c0]
    ba_ref[0] = p[:, c0:c1]
    q_ref[0] = (p[:, c1:c1 + ATT_W] * Q_SCALE).astype(BF16)
    k = p[:, c1 + ATT_W:]
    k_ref[0] = k.astype(BF16)
    vt = _dot_nt(wvt_ref[...], u)
    for jb in range(tm // LANES):
        vt_ref[0, jb] = vt[:, jb * LANES:(jb + 1) * LANES].astype(BF16)

    if keep_all:
        kf_ref[0] = k
        vf_ref[0] = _dot(u, wv_ref[...])
    else:
        @pl.when(pl.program_id(1) == n_tiles - 1)
        def _():
            kf_ref[0] = _dot_nt(wkt_ref[...], u).reshape(ATT_HEADS, ATT_DH, tm)
            vf_ref[0] = vt.reshape(ATT_HEADS, ATT_DH, tm)


def _inproj(x, nw, w, wv, wkt, wvt, cw, prev, *, keep_all):
    B, T, D = x.shape
    tm = ROW_TILE
    nt = T // tm
    n_seq = prev.shape[0]
    hist = GDN_CONV - 1
    R = min(B * T // n_seq, tm)
    S = tm // R
    tps = B * T // (n_seq * R)
    assert S == 1 or tps == 1
    grouped = (n_seq // S, hist, S, GDN_QKV)
    prev_g = prev.reshape(n_seq // S, S, hist, GDN_QKV).transpose(0, 2, 1, 3)
    groups_per_b = nt // tps
    st = pl.BlockSpec((1,) + grouped[1:], lambda b, i: (b * groups_per_b + i // tps, 0, 0, 0))
    row = lambda w: pl.BlockSpec((1, tm, w), lambda b, i: (b, i, 0))
    if keep_all:
        keep = row(ATT_W)
        keep_shape = (B, T, ATT_W)
    else:
        keep = pl.BlockSpec((1, ATT_HEADS, ATT_DH, tm), lambda b, i: (b, 0, 0, 0))
        keep_shape = (B, ATT_HEADS, ATT_DH, tm)
    consts = [nw, w, wv, wkt, wvt, cw]
    act, cst, *rest = pl.pallas_call(
        functools.partial(_inproj_kernel, keep_all=keep_all, n_tiles=nt, S=S, R=R, tps=tps),
        name="inproj",
        grid=(B, nt),
        in_specs=[row(D)] + [_const_spec(a.shape) for a in consts] + [st],
        out_specs=[row(GDN_QKV), st, row(GDN_Z), row(LANES), row(ATT_W), row(ATT_W),
                   pl.BlockSpec((1, tm // LANES, ATT_W, LANES), lambda b, i: (b, i, 0, 0)), keep, keep],
        out_shape=[jax.ShapeDtypeStruct((B, T, GDN_QKV), F32),
                   jax.ShapeDtypeStruct(grouped, F32),
                   jax.ShapeDtypeStruct((B, T, GDN_Z), F32),
                   jax.ShapeDtypeStruct((B, T, LANES), F32),
                   jax.ShapeDtypeStruct((B, T, ATT_W), BF16),
                   jax.ShapeDtypeStruct((B, T, ATT_W), BF16),
                   jax.ShapeDtypeStruct((B, T // LANES, ATT_W, LANES), BF16),
                   jax.ShapeDtypeStruct(keep_shape, F32),
                   jax.ShapeDtypeStruct(keep_shape, F32)],
        scratch_shapes=[pltpu.VMEM((hist, GDN_QKV), F32)],
        compiler_params=pltpu.CompilerParams(
            dimension_semantics=("arbitrary", "arbitrary"), vmem_limit_bytes=VMEM_LIMIT),
    )(x, *consts, prev_g)
    return (act, cst.transpose(0, 2, 1, 3).reshape(n_seq, hist, GDN_QKV), *rest)


def _bmm(a, b):
    return lax.dot_general(a, b, (((2,), (1,)), ((0,), (0,))), preferred_element_type=F32)


def _bmm_nt(a, b):
    return lax.dot_general(a, b, (((2,), (2,)), ((0,), (0,))), preferred_element_type=F32)


def _bmm_tn(a, b):
    return lax.dot_general(a, b, (((1,), (1,)), ((0,), (0,))), preferred_element_type=F32)


def _gdn_kernel(act_ref, z_ref, ba_ref, s0_ref, alog_ref, dtb_ref, gnw_ref,
                o_ref, sout_ref, s_scr, *, C, cps, bs, n_steps):
    j = pl.program_id(1)
    R = C * cps
    H = GDN_HEADS
    nh = bs * H

    @pl.when(j == 0)
    def _():
        s_scr[...] = s0_ref[...].reshape(nh, GDN_DK, GDN_DV)

    ri = lax.broadcasted_iota(jnp.int32, (C, C), 0)
    ci = lax.broadcasted_iota(jnp.int32, (C, C), 1)
    incl = ri >= ci
    strict = ri > ci
    rr = lax.broadcasted_iota(jnp.int32, (R, R), 0)
    cc = lax.broadcasted_iota(jnp.int32, (R, R), 1)
    cum16 = jnp.where((rr >= cc) & (rr // C == cc // C), 1.0, 0.0).astype(BF16)
    alog = alog_ref[...]
    dtb = dtb_ref[...]

    acts, sigs, Gs, GTs = [], [], [], []
    for s in range(bs):
        acts.append(act_ref.at[s])
        ba = ba_ref[s]
        sigs.append(_sigmoid(ba))
        G = _dot_exact_lhs(cum16, -jnp.exp(alog) * _softplus(ba + dtb))
        Gs.append(G)
        GTs.append(G.T)

    order = [(c, s, h) for c in range(cps) for s in range(bs) for h in range(H)]
    rows = lambda c: slice(c * C, (c + 1) * C)

    def tiles(slabs, col0, width):
        return jnp.stack([slabs[s][rows(c), col0 + h * width:col0 + (h + 1) * width] for c, s, h in order])

    q = tiles(acts, 0, GDN_DK)
    k = tiles(acts, GDN_QK, GDN_DK)
    v = tiles(acts, 2 * GDN_QK, GDN_DV)
    beta = tiles(sigs, 0, 1)
    Gc = tiles(Gs, H, 1)
    Gr = jnp.stack([GTs[s][H + h:H + h + 1, rows(c)] for c, s, h in order])
    Gl = Gc[:, C - 1:C, :]

    qn = q * lax.rsqrt(jnp.sum(q * q, axis=-1, keepdims=True) + EPS) * (GDN_DK ** -0.5)
    kn = k * lax.rsqrt(jnp.sum(k * k, axis=-1, keepdims=True) + EPS)
    eG = jnp.exp(Gc)
    gam = jnp.where(incl, jnp.exp(jnp.where(incl, Gc - Gr, 0.0)), 0.0)
    kb = kn * beta
    kn16 = kn.astype(BF16)
    aq = _bmm_nt(jnp.concatenate([kb, qn], axis=1).astype(BF16), kn16)
    A = jnp.where(strict, aq[:, :C] * gam, 0.0)
    QK16 = (aq[:, C:] * gam).astype(BF16)

    n_joint = C.bit_length() - 2
    A16 = A.astype(BF16)
    N = -A
    Q = _bmm(A16, A16)
    for it in range(n_joint):
        Q16 = Q.astype(BF16)
        if it == n_joint - 1:
            N = N + Q + _bmm(N.astype(BF16), Q16)
        else:
            nq = _bmm(jnp.concatenate([N, Q], axis=1).astype(BF16), Q16)
            N = N + Q + nq[:, :C]
            Q = nq[:, C:]
    rhs = jnp.concatenate([v * beta, kb * eG], axis=-1)
    sol = rhs + _bmm(N.astype(BF16), rhs.astype(BF16))
    u = sol[:, :, :GDN_DV]
    wq16 = jnp.concatenate([sol[:, :, GDN_DV:], qn * eG], axis=1).astype(BF16)
    kg16 = (kn * jnp.exp(Gl - Gc)).astype(BF16)
    dl = jnp.exp(Gl)

    S = s_scr[...]
    o_parts = []
    for c in range(cps):
        sl = slice(c * nh, (c + 1) * nh)
        r = _bmm(wq16[sl], S.astype(BF16))
        vn16 = (u[sl] - r[:, :C]).astype(BF16)
        o_parts.append(r[:, C:] + _bmm(QK16[sl], vn16))
        S = S * dl[sl] + _bmm_tn(kg16[sl], vn16)
    s_scr[...] = S

    on = _rms(jnp.concatenate(o_parts, axis=0)) * gnw_ref[...]
    gates = []
    for s in range(bs):
        zs = z_ref[s]
        gates.append(_silu(zs))
    for idx, (c, s, h) in enumerate(order):
        cols = slice(h * GDN_DV, (h + 1) * GDN_DV)
        o_ref[s, rows(c), cols] = (on[idx] * gates[s][rows(c), cols]).astype(o_ref.dtype)

    @pl.when(j == n_steps - 1)
    def _():
        sout_ref[...] = S.reshape(bs, H, GDN_DK, GDN_DV)


def _gdn(act, z, ba, s0, alog, dtb, gnw, *, C, cps, bs):
    B, T, _ = act.shape
    R = C * cps
    n_steps = T // R
    row = lambda w: pl.BlockSpec((bs, R, w), lambda b, j: (b, j, 0))
    st = pl.BlockSpec((bs, GDN_HEADS, GDN_DK, GDN_DV), lambda b, j: (b, 0, 0, 0))
    return pl.pallas_call(
        functools.partial(_gdn_kernel, C=C, cps=cps, bs=bs, n_steps=n_steps),
        name="gdn",
        grid=(B // bs, n_steps),
        in_specs=[row(GDN_QKV), row(GDN_Z), row(LANES), st,
                  _const_spec(alog.shape), _const_spec(dtb.shape), _const_spec(gnw.shape)],
        out_specs=[row(GDN_Z), st],
        out_shape=[jax.ShapeDtypeStruct((B, T, GDN_Z), BF16),
                   jax.ShapeDtypeStruct((B, GDN_HEADS, GDN_DK, GDN_DV), F32)],
        scratch_shapes=[pltpu.VMEM((bs * GDN_HEADS, GDN_DK, GDN_DV), F32)],
        compiler_params=pltpu.CompilerParams(
            dimension_semantics=("arbitrary", "arbitrary"), vmem_limit_bytes=VMEM_LIMIT),
    )(act, z, ba, s0, alog, dtb, gnw)


def _attn_prompt_kernel(q_ref, k_ref, vt_ref, bias_ref, nw_ref, o_ref, *, qt):
    j = pl.program_id(1)
    tq = 2 * CHUNK
    n_kb = BAND_CHUNKS // 2 + 1
    n_pairs = ATT_HEADS // 2
    lane_head = lax.broadcasted_iota(jnp.int32, (tq, LANES), 1) // ATT_DH

    def tile(i, masked):
        m = j * qt + i
        q = q_ref[0, i * tq:(i + 1) * tq, :]
        wt = []
        for p in range(n_pairs):
            qp = q[:, p * LANES:(p + 1) * LANES]
            zero = jnp.zeros_like(qp)
            wt.append(jnp.concatenate([jnp.where(lane_head == 0, qp, zero),
                                       jnp.where(lane_head == 1, qp, zero)], axis=0))
        wt = jnp.stack(wt)
        s_parts, v_parts = [], []
        for jb in range(n_kb):
            first = m - (n_kb - 1) + jb
            blk = jnp.maximum(first, 0)
            kb = k_ref[0, blk]
            kj = jnp.stack([kb[:, p * LANES:(p + 1) * LANES] for p in range(n_pairs)])
            s = _bmm_nt(kj, wt)
            if jb not in _FLAT_BIAS_BLOCKS:
                s = s + bias_ref[:, jb * tq:(jb + 1) * tq, :]
            if masked and jb < n_kb - 1:
                s = jnp.where(first >= 0, s, -jnp.inf)
            s_parts.append(s)
            v_parts.append(vt_ref[0, blk].reshape(n_pairs, LANES, tq))
        st = jnp.concatenate(s_parts, axis=1)
        pt = jnp.exp2(st - jnp.max(st, axis=1, keepdims=True))
        inv = 1.0 / jnp.sum(pt, axis=1, keepdims=True)
        pt16 = pt.astype(BF16)
        acc = _bmm(v_parts[0], pt16[:, 0:tq, :])
        for jb in range(1, n_kb):
            acc = acc + _bmm(v_parts[jb], pt16[:, jb * tq:(jb + 1) * tq, :])
        ot = jnp.concatenate([acc[:, :ATT_DH, :tq] * inv[:, :, :tq],
                              acc[:, ATT_DH:, tq:] * inv[:, :, tq:]], axis=1)
        oh = ot.reshape(ATT_HEADS, ATT_DH, tq)
        on = oh * lax.rsqrt(jnp.mean(oh * oh, axis=1, keepdims=True) + EPS) * nw_ref[...]
        on = on.reshape(n_pairs, LANES, tq)
        for p in range(n_pairs):
            o_ref[0, i * tq:(i + 1) * tq, p * LANES:(p + 1) * LANES] = on[p].T.astype(o_ref.dtype)

    near_start = j * qt < n_kb - 1

    @pl.when(near_start)
    def _():
        for i in range(qt):
            tile(i, True)

    @pl.when(jnp.logical_not(near_start))
    def _():
        for i in range(qt):
            tile(i, False)


def _attn_prompt(q, k, vt, bias, nw, *, qt):
    B, T, _ = q.shape
    tq = 2 * CHUNK
    n_steps = T // (qt * tq)
    k4 = k.reshape(B, T // tq, tq, ATT_W)
    row = pl.BlockSpec((1, qt * tq, ATT_W), lambda b, j: (b, j, 0))
    return pl.pallas_call(
        functools.partial(_attn_prompt_kernel, qt=qt),
        name="attn_prompt",
        grid=(B, n_steps),
        in_specs=[row,
                  pl.BlockSpec((1, T // tq, tq, ATT_W), lambda b, j: (b, 0, 0, 0)),
                  pl.BlockSpec((1, T // tq, ATT_W, tq), lambda b, j: (b, 0, 0, 0)),
                  _const_spec(bias.shape), _const_spec(nw.shape)],
        out_specs=row,
        out_shape=jax.ShapeDtypeStruct((B, T, ATT_W), BF16),
        compiler_params=pltpu.CompilerParams(
            dimension_semantics=("arbitrary", "arbitrary"), vmem_limit_bytes=VMEM_LIMIT),
    )(q, k4, vt, bias, nw)


def _attn_sample_kernel(q_ref, kc_ref, vc_ref, kn_ref, vn_ref, bc_ref, bn_ref, nw_ref, o_ref, *, n_seq):
    by_head = lambda a: jnp.stack([a[:, h * ATT_DH:(h + 1) * ATT_DH] for h in range(ATT_HEADS)])
    for s in range(n_seq):
        kc = kc_ref[s].astype(BF16)
        vc = vc_ref[s].astype(BF16)
        q = by_head(q_ref[s])
        kn = by_head(kn_ref[s])
        vn = by_head(vn_ref[s]).astype(BF16)
        s_c = _bmm(q, kc) + bc_ref[...]
        s_n = _bmm_nt(q, kn) + bn_ref[...]
        m = jnp.maximum(jnp.max(s_c, axis=-1, keepdims=True), jnp.max(s_n, axis=-1, keepdims=True))
        p_c = jnp.exp2(s_c - m)
        p_n = jnp.exp2(s_n - m)
        l = jnp.sum(p_c, axis=-1, keepdims=True) + jnp.sum(p_n, axis=-1, keepdims=True)
        o = (_bmm_nt(p_c.astype(BF16), vc) + _bmm(p_n.astype(BF16), vn)) * (1.0 / l)
        on = _rms(o) * nw_ref[...]
        o_ref[s] = jnp.concatenate([on[h] for h in range(ATT_HEADS)], axis=-1).astype(o_ref.dtype)


def _attn_sample(q, kc, vc, kn, vn, bias_c, bias_n, nw, *, n_seq):
    B, T, _ = q.shape
    new = pl.BlockSpec((n_seq, T, ATT_W), lambda b: (b, 0, 0))
    cache = pl.BlockSpec((n_seq,) + kc.shape[1:], lambda b: (b, 0, 0, 0))
    return pl.pallas_call(
        functools.partial(_attn_sample_kernel, n_seq=n_seq),
        name="attn_sample",
        grid=(B // n_seq,),
        in_specs=[new, cache, cache, new, new, _const_spec(bias_c.shape), _const_spec(bias_n.shape),
                  _const_spec(nw.shape)],
        out_specs=new,
        out_shape=jax.ShapeDtypeStruct((B, T, ATT_W), BF16),
        compiler_params=pltpu.CompilerParams(
            dimension_semantics=("arbitrary",), vmem_limit_bytes=VMEM_LIMIT),
    )(q, kc, vc, kn, vn, bias_c, bias_n, nw)


def _delayed(g, k, S, R, hist_row):
    rolled = pltpu.roll(g, k, axis=0)
    sub = lax.broadcasted_iota(jnp.int32, (SUBLANES, g.shape[1]), 0)
    parts = []
    for s in range(S):
        head = rolled[s * R:s * R + SUBLANES]
        for i in range(k):
            head = jnp.where(sub == i, hist_row(s, i), head)
        parts += [head, rolled[s * R + SUBLANES:(s + 1) * R]]
    return jnp.concatenate(parts, axis=0)


def _mix_ffn_kernel(x_ref, oa_ref, ob_ref, wo_ref, nmp_ref, nfp_ref, wg_ref, wu_ref, cw_ref, cb_ref,
                    wd_ref, nfo_ref, prev_ref, y_ref, st_ref, carry,
                    *, S, R, tps, d_ff):
    t = pl.program_id(0)
    hist = FFN_CONV - 1
    half = oa_ref.shape[-1]
    mix = _dot(oa_ref[...], wo_ref[:half, :]) + _dot(ob_ref[...], wo_ref[half:, :])
    x1 = x_ref[...] + _rms(mix) * nmp_ref[...]
    u2 = (_rms(x1) * nfp_ref[...]).astype(BF16)

    if tps > 1:
        @pl.when(t % tps == 0)
        def _():
            for i in range(hist):
                carry[i:i + 1, :] = prev_ref[0, i]

    fc = FFN_COLS
    n_fc = d_ff // fc
    acc = None
    proj = lambda c: (_dot(u2, wg_ref[:, c * fc:(c + 1) * fc]), _dot(u2, wu_ref[:, c * fc:(c + 1) * fc]))
    nxt = proj(0)
    for c in range(n_fc):
        cols = slice(c * fc, (c + 1) * fc)
        g, up = nxt
        if c + 1 < n_fc:
            nxt = proj(c + 1)
        if tps > 1:
            history = lambda s, i: carry[i:i + 1, cols]
        else:
            history = lambda s, i: prev_ref[0, i, s:s + 1, cols]
        cw = cw_ref[:, cols]
        conv = cb_ref[:, cols] + cw[hist:hist + 1] * g
        for k in range(1, FFN_CONV):
            conv = conv + cw[hist - k:hist - k + 1] * _delayed(
                g, k, S, R, lambda s, i, k=k: history(s, hist - k + i))
        for s in range(S):
            for i in range(hist):
                row = g[(s + 1) * R - hist + i:(s + 1) * R - hist + i + 1]
                if tps > 1:
                    carry[i:i + 1, cols] = row
                else:
                    st_ref[0, i, s:s + 1, cols] = row
        hid = (_gelu_tanh(conv) * up).astype(BF16)
        part = _dot(hid, wd_ref[cols, :])
        acc = part if acc is None else acc + part
    if tps > 1:
        for i in range(hist):
            st_ref[0, i] = carry[i:i + 1, :]
    y_ref[...] = x1 + _rms(acc) * nfo_ref[...]


def _mix_ffn(x, oa, ob, wo, nmp, nfp, wg, wu, cw, cb, wd, nfo, prev, *, S, R):
    M, D = x.shape
    d_ff = wg.shape[1]
    tm = S * R
    n_seq = prev.shape[0]
    hist = FFN_CONV - 1
    tps = M // (n_seq * R)
    assert S == 1 or tps == 1
    grouped = (n_seq // S, hist, S, d_ff)
    prev_g = prev.reshape(n_seq // S, S, hist, d_ff).transpose(0, 2, 1, 3)
    row = lambda w: pl.BlockSpec((tm, w), lambda t: (t, 0))
    st = pl.BlockSpec((1,) + grouped[1:], lambda t: (t // tps, 0, 0, 0))
    consts = [wo, nmp, nfp, wg, wu, cw, cb, wd, nfo]
    y, st_g = pl.pallas_call(
        functools.partial(_mix_ffn_kernel, S=S, R=R, tps=tps, d_ff=d_ff),
        name="mix_ffn",
        grid=(M // tm,),
        in_specs=[row(D), row(oa.shape[1]), row(ob.shape[1])] + [_const_spec(a.shape) for a in consts] + [st],
        out_specs=[row(D), st],
        out_shape=[jax.ShapeDtypeStruct((M, D), F32), jax.ShapeDtypeStruct(grouped, F32)],
        scratch_shapes=[pltpu.VMEM((hist, d_ff), F32)],
        compiler_params=pltpu.CompilerParams(
            dimension_semantics=("arbitrary",), vmem_limit_bytes=VMEM_LIMIT),
    )(x, oa, ob, *consts, prev_g)
    return y, st_g.transpose(0, 2, 1, 3).reshape(n_seq, hist, d_ff)


def _pad_rows_front(a, rows):
    return jnp.pad(a, ((0, 0), (rows - a.shape[1], 0), (0, 0)))


def _lane_row(vals, offset):
    return jnp.zeros((1, LANES), F32).at[0, offset:offset + vals.shape[0]].set(vals.astype(F32))


def _bias_kernel(r_ref, bp_ref, bc_ref, bn_ref, *, lc, ts):
    tq = 2 * CHUNK
    key = lax.broadcasted_iota(jnp.int32, (tq, tq), 0)
    query_chunk = lax.broadcasted_iota(jnp.int32, (tq, tq), 1) // CHUNK
    for h in range(r_ref.shape[0]):
        p, e = divmod(h, 2)
        for jb in range(r_ref.shape[1]):
            row = jnp.broadcast_to(r_ref[h, jb], (tq, 2 * tq))
            blk = pltpu.roll(row, 0, 1, stride=1, stride_axis=0)[:, :tq]
            key_in_band = jb * tq + key - query_chunk * CHUNK
            valid = (key_in_band >= 0) & (key_in_band < (BAND_CHUNKS + 1) * CHUNK)
            far = r_ref[h, 0][:, 0:1]
            bp_ref[p, jb * tq:(jb + 1) * tq, e * tq:(e + 1) * tq] = jnp.where(valid, blk - far, -jnp.inf)
            by_query = blk.T
            if (jb + 1) * tq <= lc:
                bc_ref[h, :, jb * tq:(jb + 1) * tq] = by_query[:ts]
            else:
                bn_ref[h] = by_query[:ts, :ts]


def _band_biases(table, lc, ts):
    n_heads = table.shape[0]
    tq = 2 * CHUNK
    n_kb = BAND_CHUNKS // 2 + 1
    assert lc == (n_kb - 1) * tq and ts <= tq
    starts = [WINDOW - tq * jb + MAX_REL - half * tq for jb in range(n_kb) for half in (0, 1)]
    pad_l = max(0, -min(starts))
    pad_r = max(0, max(starts) + tq - table.shape[1])
    ext = jnp.pad(table, ((0, 0), (pad_l, pad_r)), mode="edge")
    rows = jnp.concatenate([ext[:, s + pad_l:s + pad_l + tq] for s in starts], axis=1)
    rows = rows.reshape(n_heads, n_kb, 1, 2 * tq)
    return pl.pallas_call(
        functools.partial(_bias_kernel, lc=lc, ts=ts),
        name="band_bias",
        out_shape=[jax.ShapeDtypeStruct((n_heads // 2, n_kb * tq, 2 * tq), F32),
                   jax.ShapeDtypeStruct((n_heads, ts, lc), F32),
                   jax.ShapeDtypeStruct((n_heads, ts, ts), F32)],
    )(rows)


def _layer(xp, xs, cache_k, cache_v, s_delta, s_qkv, s_ffn, lw):
    (norm_mix_pre, w_in, qkv_conv_w, a_log, dt_bias, gdn_norm_w, rel_bias, attn_norm_w, w_out,
     norm_mix_post, norm_ffn_pre, w_gate_up, ffn_conv_w, ffn_conv_b, w_down, norm_ffn_post) = lw
    Bp, Tp, D = xp.shape
    Bs, Ts, _ = xs.shape
    d_ff = w_down.shape[0]

    c1 = GDN_QKV + GDN_Z
    c2 = c1 + 2 * GDN_HEADS
    w16 = w_in.astype(BF16)
    w_proj = jnp.concatenate([w16[:, :c1], jnp.pad(w16[:, c1:c2], ((0, 0), (0, LANES - 2 * GDN_HEADS))),
                              w16[:, c2:c2 + 2 * ATT_W]], axis=1)
    wv = w16[:, c2 + 2 * ATT_W:]
    wkt = w16[:, c2 + ATT_W:c2 + 2 * ATT_W].T
    wvt = wv.T
    nmix = norm_mix_pre.reshape(1, D)
    cw_qkv = jnp.pad(qkv_conv_w, ((0, SUBLANES - GDN_CONV), (0, 0)))
    alog = _lane_row(a_log, GDN_HEADS)
    dtb = _lane_row(dt_bias, GDN_HEADS)
    gnw = gdn_norm_w.reshape(1, GDN_DV)
    anw = attn_norm_w.astype(F32).reshape(1, ATT_DH)
    anw_col = jnp.broadcast_to(attn_norm_w.astype(F32)[:, None], (ATT_DH, 2 * CHUNK))
    lc = cache_k.shape[1]
    bias_p, bias_c, bias_n = _band_biases(rel_bias.astype(F32) * LOG2E, lc, Ts)
    wo = w_out.astype(BF16)
    w_gate = w_gate_up[:, :d_ff].astype(BF16)
    w_up = w_gate_up[:, d_ff:].astype(BF16)
    wd = w_down.astype(BF16)
    cw_ffn = jnp.pad(ffn_conv_w, ((0, SUBLANES - FFN_CONV), (0, 0)))
    cb = ffn_conv_b.reshape(1, d_ff)
    nmp = norm_mix_post.reshape(1, D)
    nfp = norm_ffn_pre.reshape(1, D)
    nfo = norm_ffn_post.reshape(1, D)

    def group(x, keep_all, gdn_prev, gdn_s0, gdn_c, gdn_cps, gdn_bs, ffn_prev, S, R, attn):
        B, T, _ = x.shape
        xi = x if not keep_all else x.reshape(1, B * T, D)
        act, qkv_state, z, ba, q, k, vt, kf, vf = _inproj(xi, nmix, w_proj, wv, wkt, wvt, cw_qkv, gdn_prev,
                                                          keep_all=keep_all)
        rs = lambda a: a.reshape(B, T, a.shape[-1])
        act, z, ba, q, k = map(rs, (act, z, ba, q, k))
        oa, s_new = _gdn(act, z, ba, gdn_s0, alog, dtb, gnw, C=gdn_c, cps=gdn_cps, bs=gdn_bs)
        ob = attn(q, k, vt, vf)
        y, ffn_state = _mix_ffn(x.reshape(B * T, D), oa.reshape(B * T, GDN_Z), ob.reshape(B * T, ATT_W),
                                wo, nmp, nfp, w_gate, w_up, cw_ffn, cb, wd, nfo, ffn_prev,
                                S=S, R=R)
        if keep_all:
            k_rows = kf.reshape(B, T, ATT_HEADS, ATT_DH)
            v_rows = vf.reshape(B, T, ATT_HEADS, ATT_DH)
        else:
            k_rows = jnp.transpose(kf, (0, 3, 1, 2))
            v_rows = jnp.transpose(vf, (0, 3, 1, 2))
        return y.reshape(B, T, D), k_rows, v_rows, s_new, qkv_state, ffn_state

    out_p = group(
        xp, False, jnp.zeros((Bp, GDN_CONV - 1, GDN_QKV), F32),
        jnp.zeros((Bp, GDN_HEADS, GDN_DK, GDN_DV), F32), CHUNK, 4, 4,
        jnp.zeros((Bp, FFN_CONV - 1, d_ff), F32), 1, ROW_TILE,
        lambda q, k, vt, vf: _attn_prompt(q, k, vt, bias_p, anw_col, qt=2))

    kc_t = jnp.transpose(cache_k, (0, 2, 3, 1))
    vc_t = jnp.transpose(cache_v, (0, 2, 3, 1))
    out_s = group(
        xs, True, s_qkv, s_delta, Ts, 1, 4, s_ffn, ROW_TILE // Ts, Ts,
        lambda q, k, vt, vf: _attn_sample(q, kc_t, vc_t, k, vf.reshape(Bs, Ts, ATT_W), bias_c, bias_n, anw,
                                          n_seq=2))
    return out_p, out_s


def kernel(x_prompt, x_sample, cache_band_k, cache_band_v, state_delta, state_qkv_conv, state_ffn_conv, norm_mix_pre, w_in, qkv_conv_w, a_log, dt_bias, gdn_norm_w, rel_bias, attn_norm_w, w_out, norm_mix_post, norm_ffn_pre, w_gate_up, ffn_conv_w, ffn_conv_b, w_down, norm_ffn_post):
    weights = (norm_mix_pre, w_in, qkv_conv_w, a_log, dt_bias, gdn_norm_w, rel_bias, attn_norm_w, w_out,
               norm_mix_post, norm_ffn_pre, w_gate_up, ffn_conv_w, ffn_conv_b, w_down, norm_ffn_post)
    depth = w_in.shape[0]
    xp, xs = x_prompt, x_sample
    outs_p, outs_s = [], []
    for l in range(depth):
        lw = tuple(w[l] for w in weights)
        op, os_ = _layer(xp, xs, cache_band_k[l], cache_band_v[l], state_delta[l], state_qkv_conv[l],
                         state_ffn_conv[l], lw)
        xp, xs = op[0], os_[0]
        outs_p.append(op[1:])
        outs_s.append(os_[1:])
    stack = lambda outs, i: jnp.stack([o[i] for o in outs], axis=0)
    return (xp, xs) + tuple(stack(outs_p, i) for i in range(5)) + tuple(stack(outs_s, i) for i in range(5))
```

```python
import functools

import jax
import jax.numpy as jnp
from jax import lax
from jax.experimental import pallas as pl
from jax.experimental.pallas import tpu as pltpu

F32 = jnp.float32
BF16 = jnp.bfloat16

EPS = 1e-6
CHUNK = 64
GDN_HEADS = 4
GDN_DK = 128
GDN_DV = 128
GDN_CONV = 4
ATT_HEADS = 8
ATT_DH = 64
BAND_CHUNKS = 8
WINDOW = BAND_CHUNKS * CHUNK
MAX_REL = 128
FFN_CONV = 3

GDN_QK = GDN_HEADS * GDN_DK
GDN_QKV = GDN_HEADS * (2 * GDN_DK + GDN_DV)
GDN_Z = GDN_HEADS * GDN_DV
ATT_W = ATT_HEADS * ATT_DH
LANES = 128
SUBLANES = 8
VMEM_LIMIT = 56 * 1024 * 1024
ROW_TILE = 512
FFN_COLS = 2816
_FLAT_BIAS_BLOCKS = tuple(
    jb for jb in range(BAND_CHUNKS // 2 + 1)
    if jb * 2 * CHUNK >= CHUNK and (jb + 1) * 2 * CHUNK <= (BAND_CHUNKS + 1) * CHUNK
    and WINDOW - (jb + 1) * 2 * CHUNK + 1 >= MAX_REL)
LOG2E = 1.4426950408889634
Q_SCALE = ATT_DH ** -0.5 * LOG2E


def _dot(a, b):
    return jnp.dot(a, b, preferred_element_type=F32)


def _dot_nt(a, b):
    return lax.dot_general(a, b, (((1,), (1,)), ((), ())), preferred_element_type=F32)


def _dot_tn(a, b):
    return lax.dot_general(a, b, (((0,), (0,)), ((), ())), preferred_element_type=F32)


def _split3(x):
    x1 = x.astype(BF16)
    r1 = x - x1.astype(F32)
    x2 = r1.astype(BF16)
    x3 = (r1 - x2.astype(F32)).astype(BF16)
    return x1, x2, x3


def _dot_exact_lhs(a16, x):
    x1, x2, x3 = _split3(x)
    return _dot(a16, x1) + _dot(a16, x2) + _dot(a16, x3)


def _dot_exact_rhs(x, b16):
    x1, x2, x3 = _split3(x)
    return _dot(x1, b16) + _dot(x2, b16) + _dot(x3, b16)


def _sigmoid(x):
    return 0.5 + 0.5 * jnp.tanh(0.5 * x)


def _silu(x):
    h = 0.5 * x
    return h + h * jnp.tanh(h)


def _softplus(x):
    return jnp.maximum(x, 0.0) + jnp.log(1.0 + jnp.exp(-jnp.abs(x)))


def _rms(x):
    return x * lax.rsqrt(jnp.mean(x * x, axis=-1, keepdims=True) + EPS)


def _gelu_tanh(x):
    c = 0.7978845608028654
    return 0.5 * x * (1.0 + jnp.tanh(c * (x + 0.044715 * (x * x * x))))


def _const_spec(shape):
    n = len(shape)
    return pl.BlockSpec(shape, lambda *_: (0,) * n, pipeline_mode=pl.Buffered(1))


def _inproj_kernel(x_ref, nw_ref, w_ref, wv_ref, wkt_ref, wvt_ref, cw_ref, prev_ref,
                   act_ref, cst_ref, z_ref, ba_ref, q_ref, k_ref, vt_ref, kf_ref, vf_ref, carry,
                   *, keep_all, n_tiles, S, R, tps):
    x = x_ref[0]
    tm = x.shape[0]
    u = (_rms(x) * nw_ref[...]).astype(BF16)

    hist = GDN_CONV - 1
    if tps > 1:
        @pl.when(pl.program_id(1) % tps == 0)
        def _():
            for r in range(hist):
                carry[r:r + 1, :] = prev_ref[0, r]
    cw = cw_ref[...]
    slabs = [slice(c * GDN_QK, (c + 1) * GDN_QK) for c in range(GDN_QKV // GDN_QK)]

    def conv_silu(pre, cols):
        if tps > 1:
            history = lambda s, r, cols=cols: carry[r:r + 1, cols]
        else:
            history = lambda s, r, cols=cols: prev_ref[0, r, s:s + 1, cols]
        conv = cw[hist:hist + 1, cols] * pre
        for k in range(1, GDN_CONV):
            conv = conv + cw[hist - k:hist - k + 1, cols] * _delayed(
                pre, k, S, R, lambda s, r, k=k, history=history: history(s, hist - k + r))
        act_ref[0, :, cols] = _silu(conv)
        for s in range(S):
            for r in range(hist):
                row = pre[(s + 1) * R - hist + r:(s + 1) * R - hist + r + 1]
                if tps > 1:
                    carry[r:r + 1, cols] = row
                else:
                    cst_ref[0, r, s:s + 1, cols] = row

    p = _dot(u, w_ref[...])
    for cols in slabs:
        conv_silu(p[:, cols], cols)
    if tps > 1:
        for r in range(hist):
            cst_ref[0, r] = carry[r:r + 1, :]
    c0 = GDN_QKV + GDN_Z
    c1 = c0 + LANES
    z_ref[0] = p[:, GDN_QKV:c0]
    ba_ref[0] = p[:, c0:c1]
    q_ref[0] = (p[:, c1:c1 + ATT_W] * Q_SCALE).astype(BF16)
    k = p[:, c1 + ATT_W:]
    k_ref[0] = k.astype(BF16)
    vt = _dot_nt(wvt_ref[...], u)
    for jb in range(tm // LANES):
        vt_ref[0, jb] = vt[:, jb * LANES:(jb + 1) * LANES].astype(BF16)

    if keep_all:
        kf_ref[0] = k
        vf_ref[0] = _dot(u, wv_ref[...])
    else:
        @pl.when(pl.program_id(1) == n_tiles - 1)
        def _():
            kf_ref[0] = _dot_nt(wkt_ref[...], u).reshape(ATT_HEADS, ATT_DH, tm)
            vf_ref[0] = vt.reshape(ATT_HEADS, ATT_DH, tm)


def _inproj(x, nw, w, wv, wkt, wvt, cw, prev, *, keep_all):
    B, T, D = x.shape
    tm = ROW_TILE
    nt = T // tm
    n_seq = prev.shape[0]
    hist = GDN_CONV - 1
    R = min(B * T // n_seq, tm)
    S = tm // R
    tps = B * T // (n_seq * R)
    assert S == 1 or tps == 1
    grouped = (n_seq // S, hist, S, GDN_QKV)
    prev_g = prev.reshape(n_seq // S, S, hist, GDN_QKV).transpose(0, 2, 1, 3)
    groups_per_b = nt // tps
    st = pl.BlockSpec((1,) + grouped[1:], lambda b, i: (b * groups_per_b + i // tps, 0, 0, 0))
    row = lambda w: pl.BlockSpec((1, tm, w), lambda b, i: (b, i, 0))
    if keep_all:
        keep = row(ATT_W)
        keep_shape = (B, T, ATT_W)
    else:
        keep = pl.BlockSpec((1, ATT_HEADS, ATT_DH, tm), lambda b, i: (b, 0, 0, 0))
        keep_shape = (B, ATT_HEADS, ATT_DH, tm)
    consts = [nw, w, wv, wkt, wvt, cw]
    act, cst, *rest = pl.pallas_call(
        functools.partial(_inproj_kernel, keep_all=keep_all, n_tiles=nt, S=S, R=R, tps=tps),
        name="inproj",
        grid=(B, nt),
        in_specs=[row(D)] + [_const_spec(a.shape) for a in consts] + [st],
        out_specs=[row(GDN_QKV), st, row(GDN_Z), row(LANES), row(ATT_W), row(ATT_W),
                   pl.BlockSpec((1, tm // LANES, ATT_W, LANES), lambda b, i: (b, i, 0, 0)), keep, keep],
        out_shape=[jax.ShapeDtypeStruct((B, T, GDN_QKV), F32),
                   jax.ShapeDtypeStruct(grouped, F32),
                   jax.ShapeDtypeStruct((B, T, GDN_Z), F32),
                   jax.ShapeDtypeStruct((B, T, LANES), F32),
                   jax.ShapeDtypeStruct((B, T, ATT_W), BF16),
                   jax.ShapeDtypeStruct((B, T, ATT_W), BF16),
                   jax.ShapeDtypeStruct((B, T // LANES, ATT_W, LANES), BF16),
                   jax.ShapeDtypeStruct(keep_shape, F32),
                   jax.ShapeDtypeStruct(keep_shape, F32)],
        scratch_shapes=[pltpu.VMEM((hist, GDN_QKV), F32)],
        compiler_params=pltpu.CompilerParams(
            dimension_semantics=("arbitrary", "arbitrary"), vmem_limit_bytes=VMEM_LIMIT),
    )(x, *consts, prev_g)
    return (act, cst.transpose(0, 2, 1, 3).reshape(n_seq, hist, GDN_QKV), *rest)


def _bmm(a, b):
    return lax.dot_general(a, b, (((2,), (1,)), ((0,), (0,))), preferred_element_type=F32)


def _bmm_nt(a, b):
    return lax.dot_general(a, b, (((2,), (2,)), ((0,), (0,))), preferred_element_type=F32)


def _bmm_tn(a, b):
    return lax.dot_general(a, b, (((1,), (1,)), ((0,), (0,))), preferred_element_type=F32)


def _gdn_kernel(act_ref, z_ref, ba_ref, s0_ref, alog_ref, dtb_ref, gnw_ref,
                o_ref, sout_ref, s_scr, *, C, cps, bs, n_steps):
    j = pl.program_id(1)
    R = C * cps
    H = GDN_HEADS
    nh = bs * H

    @pl.when(j == 0)
    def _():
        s_scr[...] = s0_ref[...].reshape(nh, GDN_DK, GDN_DV)

    ri = lax.broadcasted_iota(jnp.int32, (C, C), 0)
    ci = lax.broadcasted_iota(jnp.int32, (C, C), 1)
    incl = ri >= ci
    strict = ri > ci
    rr = lax.broadcasted_iota(jnp.int32, (R, R), 0)
    cc = lax.broadcasted_iota(jnp.int32, (R, R), 1)
    cum16 = jnp.where((rr >= cc) & (rr // C == cc // C), 1.0, 0.0).astype(BF16)
    alog = alog_ref[...]
    dtb = dtb_ref[...]

    acts, sigs, Gs, GTs = [], [], [], []
    for s in range(bs):
        acts.append(act_ref.at[s])
        ba = ba_ref[s]
        sigs.append(_sigmoid(ba))
        G = _dot_exact_lhs(cum16, -jnp.exp(alog) * _softplus(ba + dtb))
        Gs.append(G)
        GTs.append(G.T)

    order = [(c, s, h) for c in range(cps) for s in range(bs) for h in range(H)]
    rows = lambda c: slice(c * C, (c + 1) * C)

    def tiles(slabs, col0, width):
        return jnp.stack([slabs[s][rows(c), col0 + h * width:col0 + (h + 1) * width] for c, s, h in order])

    q = tiles(acts, 0, GDN_DK)
    k = tiles(acts, GDN_QK, GDN_DK)
    v = tiles(acts, 2 * GDN_QK, GDN_DV)
    beta = tiles(sigs, 0, 1)
    Gc = tiles(Gs, H, 1)
    Gr = jnp.stack([GTs[s][H + h:H + h + 1, rows(c)] for c, s, h in order])
    Gl = Gc[:, C - 1:C, :]

    nb = len(order)
    li = lax.broadcasted_iota(jnp.int32, (2 * GDN_DK, 2 * GDN_DK), 0) // GDN_DK
    lj = lax.broadcasted_iota(jnp.int32, (2 * GDN_DK, 2 * GDN_DK), 1) // GDN_DK
    ones2 = jnp.where(li == lj, 1.0, 0.0).astype(BF16)
    sq = jnp.concatenate([q * q, k * k], axis=-1).reshape(nb * C, 2 * GDN_DK)
    sq_hi = sq.astype(BF16)
    sq_lo = (sq - sq_hi.astype(F32)).astype(BF16)
    norms = (_dot(sq_hi, ones2) + _dot(sq_lo, ones2)).reshape(nb, C, 2 * GDN_DK)
    qn = q * lax.rsqrt(norms[:, :, :GDN_DK] + EPS) * (GDN_DK ** -0.5)
    kn = k * lax.rsqrt(norms[:, :, GDN_DK:] + EPS)
    eG = jnp.exp(Gc)
    gam = jnp.where(incl, jnp.exp(jnp.where(incl, Gc - Gr, 0.0)), 0.0)
    kb = kn * beta
    kn16 = kn.astype(BF16)
    aq = _bmm_nt(jnp.concatenate([kb, qn], axis=1).astype(BF16), kn16)
    A = jnp.where(strict, aq[:, :C] * gam, 0.0)
    QK16 = (aq[:, C:] * gam).astype(BF16)

    n_joint = C.bit_length() - 2
    A16 = A.astype(BF16)
    N = -A
    Q = _bmm(A16, A16)
    for it in range(n_joint):
        Q16 = Q.astype(BF16)
        if it == n_joint - 1:
            N = N + Q + _bmm(N.astype(BF16), Q16)
        else:
            nq = _bmm(jnp.concatenate([N, Q], axis=1).astype(BF16), Q16)
            N = N + Q + nq[:, :C]
            Q = nq[:, C:]
    rhs = jnp.concatenate([v * beta, kb * eG], axis=-1)
    sol = rhs + _bmm(N.astype(BF16), rhs.astype(BF16))
    u = sol[:, :, :GDN_DV]
    wq16 = jnp.concatenate([sol[:, :, GDN_DV:], qn * eG], axis=1).astype(BF16)
    kg16 = (kn * jnp.exp(Gl - Gc)).astype(BF16)
    dl = jnp.exp(Gl)

    S = s_scr[...]
    o_parts = []
    for c in range(cps):
        sl = slice(c * nh, (c + 1) * nh)
        r = _bmm(wq16[sl], S.astype(BF16))
        vn16 = (u[sl] - r[:, :C]).astype(BF16)
        o_parts.append(r[:, C:] + _bmm(QK16[sl], vn16))
        S = S * dl[sl] + _bmm_tn(kg16[sl], vn16)
    s_scr[...] = S

    on = _rms(jnp.concatenate(o_parts, axis=0)) * gnw_ref[...]
    gates = []
    for s in range(bs):
        zs = z_ref[s]
        gates.append(_silu(zs))
    for idx, (c, s, h) in enumerate(order):
        cols = slice(h * GDN_DV, (h + 1) * GDN_DV)
        o_ref[s, rows(c), cols] = (on[idx] * gates[s][rows(c), cols]).astype(o_ref.dtype)

    @pl.when(j == n_steps - 1)
    def _():
        sout_ref[...] = S.reshape(bs, H, GDN_DK, GDN_DV)


def _gdn(act, z, ba, s0, alog, dtb, gnw, *, C, cps, bs):
    B, T, _ = act.shape
    R = C * cps
    n_steps = T // R
    row = lambda w: pl.BlockSpec((bs, R, w), lambda b, j: (b, j, 0))
    st = pl.BlockSpec((bs, GDN_HEADS, GDN_DK, GDN_DV), lambda b, j: (b, 0, 0, 0))
    return pl.pallas_call(
        functools.partial(_gdn_kernel, C=C, cps=cps, bs=bs, n_steps=n_steps),
        name="gdn",
        grid=(B // bs, n_steps),
        in_specs=[row(GDN_QKV), row(GDN_Z), row(LANES), st,
                  _const_spec(alog.shape), _const_spec(dtb.shape), _const_spec(gnw.shape)],
        out_specs=[row(GDN_Z), st],
        out_shape=[jax.ShapeDtypeStruct((B, T, GDN_Z), BF16),
                   jax.ShapeDtypeStruct((B, GDN_HEADS, GDN_DK, GDN_DV), F32)],
        scratch_shapes=[pltpu.VMEM((bs * GDN_HEADS, GDN_DK, GDN_DV), F32)],
        compiler_params=pltpu.CompilerParams(
            dimension_semantics=("arbitrary", "arbitrary"), vmem_limit_bytes=VMEM_LIMIT),
    )(act, z, ba, s0, alog, dtb, gnw)


def _attn_prompt_kernel(q_ref, k_ref, vt_ref, bias_ref, nw_ref, o_ref, *, qt):
    j = pl.program_id(1)
    tq = 2 * CHUNK
    n_kb = BAND_CHUNKS // 2 + 1
    n_pairs = ATT_HEADS // 2
    lane_head = lax.broadcasted_iota(jnp.int32, (tq, LANES), 1) // ATT_DH

    def tile(i, masked):
        m = j * qt + i
        q = q_ref[0, i * tq:(i + 1) * tq, :]
        wt = []
        for p in range(n_pairs):
            qp = q[:, p * LANES:(p + 1) * LANES]
            zero = jnp.zeros_like(qp)
            wt.append(jnp.concatenate([jnp.where(lane_head == 0, qp, zero),
                                       jnp.where(lane_head == 1, qp, zero)], axis=0))
        wt = jnp.stack(wt)
        firsts = [m - (n_kb - 1) + jb for jb in range(n_kb)]
        blks = [jnp.maximum(f, 0) for f in firsts]
        k_all = jnp.concatenate(
            [jnp.stack([k_ref[0, blk][:, p * LANES:(p + 1) * LANES] for p in range(n_pairs)]) for blk in blks],
            axis=1)
        v_all = jnp.concatenate([vt_ref[0, blk].reshape(n_pairs, LANES, tq) for blk in blks], axis=2)
        s_all = _bmm_nt(k_all, wt)
        s_parts = []
        for jb in range(n_kb):
            s = s_all[:, jb * tq:(jb + 1) * tq, :]
            if jb not in _FLAT_BIAS_BLOCKS:
                s = s + bias_ref[:, jb * tq:(jb + 1) * tq, :]
            if masked and jb < n_kb - 1:
                s = jnp.where(firsts[jb] >= 0, s, -jnp.inf)
            s_parts.append(s)
        st = jnp.concatenate(s_parts, axis=1)
        pt = jnp.exp2(st - jnp.max(st, axis=1, keepdims=True))
        inv = 1.0 / jnp.sum(pt, axis=1, keepdims=True)
        acc = _bmm(v_all, pt.astype(BF16))
        ot = jnp.concatenate([acc[:, :ATT_DH, :tq] * inv[:, :, :tq],
                              acc[:, ATT_DH:, tq:] * inv[:, :, tq:]], axis=1)
        oh = ot.reshape(ATT_HEADS, ATT_DH, tq)
        on = oh * lax.rsqrt(jnp.mean(oh * oh, axis=1, keepdims=True) + EPS) * nw_ref[...]
        on = on.reshape(n_pairs, LANES, tq)
        for p in range(n_pairs):
            o_ref[0, i * tq:(i + 1) * tq, p * LANES:(p + 1) * LANES] = on[p].T.astype(o_ref.dtype)

    near_start = j * qt < n_kb - 1

    @pl.when(near_start)
    def _():
        for i in range(qt):
            tile(i, True)

    @pl.when(jnp.logical_not(near_start))
    def _():
        for i in range(qt):
            tile(i, False)


def _attn_prompt(q, k, vt, bias, nw, *, qt):
    B, T, _ = q.shape
    tq = 2 * CHUNK
    n_steps = T // (qt * tq)
    k4 = k.reshape(B, T // tq, tq, ATT_W)
    row = pl.BlockSpec((1, qt * tq, ATT_W), lambda b, j: (b, j, 0))
    return pl.pallas_call(
        functools.partial(_attn_prompt_kernel, qt=qt),
        name="attn_prompt",
        grid=(B, n_steps),
        in_specs=[row,
                  pl.BlockSpec((1, T // tq, tq, ATT_W), lambda b, j: (b, 0, 0, 0)),
                  pl.BlockSpec((1, T // tq, ATT_W, tq), lambda b, j: (b, 0, 0, 0)),
                  _const_spec(bias.shape), _const_spec(nw.shape)],
        out_specs=row,
        out_shape=jax.ShapeDtypeStruct((B, T, ATT_W), BF16),
        compiler_params=pltpu.CompilerParams(
            dimension_semantics=("arbitrary", "arbitrary"), vmem_limit_bytes=VMEM_LIMIT),
    )(q, k4, vt, bias, nw)


def _attn_sample_kernel(q_ref, kc_ref, vc_ref, kn_ref, vn_ref, bc_ref, bn_ref, nw_ref, o_ref, *, n_seq):
    by_head = lambda a: jnp.stack([a[:, h * ATT_DH:(h + 1) * ATT_DH] for h in range(ATT_HEADS)])
    for s in range(n_seq):
        kc = kc_ref[s].astype(BF16)
        vc = vc_ref[s].astype(BF16)
        q = by_head(q_ref[s])
        kn = by_head(kn_ref[s])
        vn = by_head(vn_ref[s]).astype(BF16)
        s_c = _bmm(q, kc) + bc_ref[...]
        s_n = _bmm_nt(q, kn) + bn_ref[...]
        m = jnp.maximum(jnp.max(s_c, axis=-1, keepdims=True), jnp.max(s_n, axis=-1, keepdims=True))
        p_c = jnp.exp2(s_c - m)
        p_n = jnp.exp2(s_n - m)
        l = jnp.sum(p_c, axis=-1, keepdims=True) + jnp.sum(p_n, axis=-1, keepdims=True)
        o = (_bmm_nt(p_c.astype(BF16), vc) + _bmm(p_n.astype(BF16), vn)) * (1.0 / l)
        on = _rms(o) * nw_ref[...]
        o_ref[s] = jnp.concatenate([on[h] for h in range(ATT_HEADS)], axis=-1).astype(o_ref.dtype)


def _attn_sample(q, kc, vc, kn, vn, bias_c, bias_n, nw, *, n_seq):
    B, T, _ = q.shape
    new = pl.BlockSpec((n_seq, T, ATT_W), lambda b: (b, 0, 0))
    cache = pl.BlockSpec((n_seq,) + kc.shape[1:], lambda b: (b, 0, 0, 0))
    return pl.pallas_call(
        functools.partial(_attn_sample_kernel, n_seq=n_seq),
        name="attn_sample",
        grid=(B // n_seq,),
        in_specs=[new, cache, cache, new, new, _const_spec(bias_c.shape), _const_spec(bias_n.shape),
                  _const_spec(nw.shape)],
        out_specs=new,
        out_shape=jax.ShapeDtypeStruct((B, T, ATT_W), BF16),
        compiler_params=pltpu.CompilerParams(
            dimension_semantics=("arbitrary",), vmem_limit_bytes=VMEM_LIMIT),
    )(q, kc, vc, kn, vn, bias_c, bias_n, nw)


def _delayed(g, k, S, R, hist_row):
    rolled = pltpu.roll(g, k, axis=0)
    sub = lax.broadcasted_iota(jnp.int32, (SUBLANES, g.shape[1]), 0)
    parts = []
    for s in range(S):
        head = rolled[s * R:s * R + SUBLANES]
        for i in range(k):
            head = jnp.where(sub == i, hist_row(s, i), head)
        parts += [head, rolled[s * R + SUBLANES:(s + 1) * R]]
    return jnp.concatenate(parts, axis=0)


def _mix_ffn_kernel(x_ref, oa_ref, ob_ref, wo_ref, nmp_ref, nfp_ref, wg_ref, wu_ref, cw_ref, cb_ref,
                    wd_ref, nfo_ref, prev_ref, y_ref, st_ref, carry,
                    *, S, R, tps, d_ff):
    t = pl.program_id(0)
    hist = FFN_CONV - 1
    half = oa_ref.shape[-1]
    mix = _dot(oa_ref[...], wo_ref[:half, :]) + _dot(ob_ref[...], wo_ref[half:, :])
    x1 = x_ref[...] + _rms(mix) * nmp_ref[...]
    u2 = (_rms(x1) * nfp_ref[...]).astype(BF16)

    if tps > 1:
        @pl.when(t % tps == 0)
        def _():
            for i in range(hist):
                carry[i:i + 1, :] = prev_ref[0, i]

    fc = FFN_COLS
    n_fc = d_ff // fc
    acc = None
    proj = lambda c: (_dot(u2, wg_ref[:, c * fc:(c + 1) * fc]), _dot(u2, wu_ref[:, c * fc:(c + 1) * fc]))
    nxt = proj(0)
    for c in range(n_fc):
        cols = slice(c * fc, (c + 1) * fc)
        g, up = nxt
        if c + 1 < n_fc:
            nxt = proj(c + 1)
        if tps > 1:
            history = lambda s, i: carry[i:i + 1, cols]
        else:
            history = lambda s, i: prev_ref[0, i, s:s + 1, cols]
        cw = cw_ref[:, cols]
        conv = cb_ref[:, cols] + cw[hist:hist + 1] * g
        for k in range(1, FFN_CONV):
            conv = conv + cw[hist - k:hist - k + 1] * _delayed(
                g, k, S, R, lambda s, i, k=k: history(s, hist - k + i))
        for s in range(S):
            for i in range(hist):
                row = g[(s + 1) * R - hist + i:(s + 1) * R - hist + i + 1]
                if tps > 1:
                    carry[i:i + 1, cols] = row
                else:
                    st_ref[0, i, s:s + 1, cols] = row
        hid = (_gelu_tanh(conv) * up).astype(BF16)
        part = _dot(hid, wd_ref[cols, :])
        acc = part if acc is None else acc + part
    if tps > 1:
        for i in range(hist):
            st_ref[0, i] = carry[i:i + 1, :]
    y_ref[...] = x1 + _rms(acc) * nfo_ref[...]


def _mix_ffn(x, oa, ob, wo, nmp, nfp, wg, wu, cw, cb, wd, nfo, prev, *, S, R):
    M, D = x.shape
    d_ff = wg.shape[1]
    tm = S * R
    n_seq = prev.shape[0]
    hist = FFN_CONV - 1
    tps = M // (n_seq * R)
    assert S == 1 or tps == 1
    grouped = (n_seq // S, hist, S, d_ff)
    prev_g = prev.reshape(n_seq // S, S, hist, d_ff).transpose(0, 2, 1, 3)
    row = lambda w: pl.BlockSpec((tm, w), lambda t: (t, 0))
    st = pl.BlockSpec((1,) + grouped[1:], lambda t: (t // tps, 0, 0, 0))
    consts = [wo, nmp, nfp, wg, wu, cw, cb, wd, nfo]
    y, st_g = pl.pallas_call(
        functools.partial(_mix_ffn_kernel, S=S, R=R, tps=tps, d_ff=d_ff),
        name="mix_ffn",
        grid=(M // tm,),
        in_specs=[row(D), row(oa.shape[1]), row(ob.shape[1])] + [_const_spec(a.shape) for a in consts] + [st],
        out_specs=[row(D), st],
        out_shape=[jax.ShapeDtypeStruct((M, D), F32), jax.ShapeDtypeStruct(grouped, F32)],
        scratch_shapes=[pltpu.VMEM((hist, d_ff), F32)],
        compiler_params=pltpu.CompilerParams(
            dimension_semantics=("arbitrary",), vmem_limit_bytes=VMEM_LIMIT),
    )(x, oa, ob, *consts, prev_g)
    return y, st_g.transpose(0, 2, 1, 3).reshape(n_seq, hist, d_ff)


def _pad_rows_front(a, rows):
    return jnp.pad(a, ((0, 0), (rows - a.shape[1], 0), (0, 0)))


def _lane_row(vals, offset):
    return jnp.zeros((1, LANES), F32).at[0, offset:offset + vals.shape[0]].set(vals.astype(F32))


def _bias_kernel(r_ref, bp_ref, bc_ref, bn_ref, *, lc, ts):
    tq = 2 * CHUNK
    key = lax.broadcasted_iota(jnp.int32, (tq, tq), 0)
    query_chunk = lax.broadcasted_iota(jnp.int32, (tq, tq), 1) // CHUNK
    for h in range(r_ref.shape[0]):
        p, e = divmod(h, 2)
        for jb in range(r_ref.shape[1]):
            row = jnp.broadcast_to(r_ref[h, jb], (tq, 2 * tq))
            blk = pltpu.roll(row, 0, 1, stride=1, stride_axis=0)[:, :tq]
            key_in_band = jb * tq + key - query_chunk * CHUNK
            valid = (key_in_band >= 0) & (key_in_band < (BAND_CHUNKS + 1) * CHUNK)
            far = r_ref[h, 0][:, 0:1]
            bp_ref[p, jb * tq:(jb + 1) * tq, e * tq:(e + 1) * tq] = jnp.where(valid, blk - far, -jnp.inf)
            by_query = blk.T
            if (jb + 1) * tq <= lc:
                bc_ref[h, :, jb * tq:(jb + 1) * tq] = by_query[:ts]
            else:
                bn_ref[h] = by_query[:ts, :ts]


def _band_biases(table, lc, ts):
    n_heads = table.shape[0]
    tq = 2 * CHUNK
    n_kb = BAND_CHUNKS // 2 + 1
    assert lc == (n_kb - 1) * tq and ts <= tq
    starts = [WINDOW - tq * jb + MAX_REL - half * tq for jb in range(n_kb) for half in (0, 1)]
    pad_l = max(0, -min(starts))
    pad_r = max(0, max(starts) + tq - table.shape[1])
    ext = jnp.pad(table, ((0, 0), (pad_l, pad_r)), mode="edge")
    rows = jnp.concatenate([ext[:, s + pad_l:s + pad_l + tq] for s in starts], axis=1)
    rows = rows.reshape(n_heads, n_kb, 1, 2 * tq)
    return pl.pallas_call(
        functools.partial(_bias_kernel, lc=lc, ts=ts),
        name="band_bias",
        out_shape=[jax.ShapeDtypeStruct((n_heads // 2, n_kb * tq, 2 * tq), F32),
                   jax.ShapeDtypeStruct((n_heads, ts, lc), F32),
                   jax.ShapeDtypeStruct((n_heads, ts, ts), F32)],
    )(rows)


def _layer(xp, xs, cache_k, cache_v, s_delta, s_qkv, s_ffn, lw):
    (norm_mix_pre, w_in, qkv_conv_w, a_log, dt_bias, gdn_norm_w, rel_bias, attn_norm_w, w_out,
     norm_mix_post, norm_ffn_pre, w_gate_up, ffn_conv_w, ffn_conv_b, w_down, norm_ffn_post) = lw
    Bp, Tp, D = xp.shape
    Bs, Ts, _ = xs.shape
    d_ff = w_down.shape[0]

    c1 = GDN_QKV + GDN_Z
    c2 = c1 + 2 * GDN_HEADS
    w16 = w_in.astype(BF16)
    w_proj = jnp.concatenate([w16[:, :c1], jnp.pad(w16[:, c1:c2], ((0, 0), (0, LANES - 2 * GDN_HEADS))),
                              w16[:, c2:c2 + 2 * ATT_W]], axis=1)
    wv = w16[:, c2 + 2 * ATT_W:]
    wkt = w16[:, c2 + ATT_W:c2 + 2 * ATT_W].T
    wvt = wv.T
    nmix = norm_mix_pre.reshape(1, D)
    cw_qkv = jnp.pad(qkv_conv_w, ((0, SUBLANES - GDN_CONV), (0, 0)))
    alog = _lane_row(a_log, GDN_HEADS)
    dtb = _lane_row(dt_bias, GDN_HEADS)
    gnw = gdn_norm_w.reshape(1, GDN_DV)
    anw = attn_norm_w.astype(F32).reshape(1, ATT_DH)
    anw_col = jnp.broadcast_to(attn_norm_w.astype(F32)[:, None], (ATT_DH, 2 * CHUNK))
    lc = cache_k.shape[1]
    bias_p, bias_c, bias_n = _band_biases(rel_bias.astype(F32) * LOG2E, lc, Ts)
    wo = w_out.astype(BF16)
    w_gate = w_gate_up[:, :d_ff].astype(BF16)
    w_up = w_gate_up[:, d_ff:].astype(BF16)
    wd = w_down.astype(BF16)
    cw_ffn = jnp.pad(ffn_conv_w, ((0, SUBLANES - FFN_CONV), (0, 0)))
    cb = ffn_conv_b.reshape(1, d_ff)
    nmp = norm_mix_post.reshape(1, D)
    nfp = norm_ffn_pre.reshape(1, D)
    nfo = norm_ffn_post.reshape(1, D)

    def group(x, keep_all, gdn_prev, gdn_s0, gdn_c, gdn_cps, gdn_bs, ffn_prev, S, R, attn):
        B, T, _ = x.shape
        xi = x if not keep_all else x.reshape(1, B * T, D)
        act, qkv_state, z, ba, q, k, vt, kf, vf = _inproj(xi, nmix, w_proj, wv, wkt, wvt, cw_qkv, gdn_prev,
                                                          keep_all=keep_all)
        rs = lambda a: a.reshape(B, T, a.shape[-1])
        act, z, ba, q, k = map(rs, (act, z, ba, q, k))
        oa, s_new = _gdn(act, z, ba, gdn_s0, alog, dtb, gnw, C=gdn_c, cps=gdn_cps, bs=gdn_bs)
        ob = attn(q, k, vt, vf)
        y, ffn_state = _mix_ffn(x.reshape(B * T, D), oa.reshape(B * T, GDN_Z), ob.reshape(B * T, ATT_W),
                                wo, nmp, nfp, w_gate, w_up, cw_ffn, cb, wd, nfo, ffn_prev,
                                S=S, R=R)
        if keep_all:
            k_rows = kf.reshape(B, T, ATT_HEADS, ATT_DH)
            v_rows = vf.reshape(B, T, ATT_HEADS, ATT_DH)
        else:
            k_rows = jnp.transpose(kf, (0, 3, 1, 2))
            v_rows = jnp.transpose(vf, (0, 3, 1, 2))
        return y.reshape(B, T, D), k_rows, v_rows, s_new, qkv_state, ffn_state

    out_p = group(
        xp, False, jnp.zeros((Bp, GDN_CONV - 1, GDN_QKV), F32),
        jnp.zeros((Bp, GDN_HEADS, GDN_DK, GDN_DV), F32), CHUNK, 4, 4,
        jnp.zeros((Bp, FFN_CONV - 1, d_ff), F32), 1, ROW_TILE,
        lambda q, k, vt, vf: _attn_prompt(q, k, vt, bias_p, anw_col, qt=4))

    kc_t = jnp.transpose(cache_k, (0, 2, 3, 1))
    vc_t = jnp.transpose(cache_v, (0, 2, 3, 1))
    out_s = group(
        xs, True, s_qkv, s_delta, Ts, 1, 4, s_ffn, ROW_TILE // Ts, Ts,
        lambda q, k, vt, vf: _attn_sample(q, kc_t, vc_t, k, vf.reshape(Bs, Ts, ATT_W), bias_c, bias_n, anw,
                                          n_seq=2))
    return out_p, out_s


def kernel(x_prompt, x_sample, cache_band_k, cache_band_v, state_delta, state_qkv_conv, state_ffn_conv, norm_mix_pre, w_in, qkv_conv_w, a_log, dt_bias, gdn_norm_w, rel_bias, attn_norm_w, w_out, norm_mix_post, norm_ffn_pre, w_gate_up, ffn_conv_w, ffn_conv_b, w_down, norm_ffn_post):
    weights = (norm_mix_pre, w_in, qkv_conv_w, a_log, dt_bias, gdn_norm_w, rel_bias, attn_norm_w, w_out,
               norm_mix_post, norm_ffn_pre, w_gate_up, ffn_conv_w, ffn_conv_b, w_down, norm_ffn_post)
    depth = w_in.shape[0]
    xp, xs = x_prompt, x_sample
    outs_p, outs_s = [], []
    for l in range(depth):
        lw = tuple(w[l] for w in weights)
        op, os_ = _layer(xp, xs, cache_band_k[l], cache_band_v[l], state_delta[l], state_qkv_conv[l],
                         state_ffn_conv[l], lw)
        xp, xs = op[0], os_[0]
        outs_p.append(op[1:])
        outs_s.append(os_[1:])
    stack = lambda outs, i: jnp.stack([o[i] for o in outs], axis=0)
    return (xp, xs) + tuple(stack(outs_p, i) for i in range(5)) + tuple(stack(outs_s, i) for i in range(5))
```

```python
import functools

import jax
import jax.numpy as jnp
from jax import lax
from jax.experimental import pallas as pl
from jax.experimental.pallas import tpu as pltpu

F32 = jnp.float32
BF16 = jnp.bfloat16

EPS = 1e-6
CHUNK = 64
GDN_HEADS = 4
GDN_DK = 128
GDN_DV = 128
GDN_CONV = 4
ATT_HEADS = 8
ATT_DH = 64
BAND_CHUNKS = 8
WINDOW = BAND_CHUNKS * CHUNK
MAX_REL = 128
FFN_CONV = 3

GDN_QK = GDN_HEADS * GDN_DK
GDN_QKV = GDN_HEADS * (2 * GDN_DK + GDN_DV)
GDN_Z = GDN_HEADS * GDN_DV
ATT_W = ATT_HEADS * ATT_DH
LANES = 128
SUBLANES = 8
VMEM_LIMIT = 56 * 1024 * 1024
ROW_TILE = 512
_FLAT_BIAS_BLOCKS = tuple(
    jb for jb in range(BAND_CHUNKS // 2 + 1)
    if jb * 2 * CHUNK >= CHUNK and (jb + 1) * 2 * CHUNK <= (BAND_CHUNKS + 1) * CHUNK
    and WINDOW - (jb + 1) * 2 * CHUNK + 1 >= MAX_REL)
LOG2E = 1.4426950408889634
Q_SCALE = ATT_DH ** -0.5 * LOG2E


def _dot(a, b):
    return jnp.dot(a, b, preferred_element_type=F32)


def _dot_nt(a, b):
    return lax.dot_general(a, b, (((1,), (1,)), ((), ())), preferred_element_type=F32)


def _split3(x):
    x1 = x.astype(BF16)
    r1 = x - x1.astype(F32)
    x2 = r1.astype(BF16)
    x3 = (r1 - x2.astype(F32)).astype(BF16)
    return x1, x2, x3


def _dot_exact_lhs(a16, x):
    x1, x2, x3 = _split3(x)
    return _dot(a16, x1) + _dot(a16, x2) + _dot(a16, x3)


def _sigmoid(x):
    return 0.5 + 0.5 * jnp.tanh(0.5 * x)


def _silu(x):
    h = 0.5 * x
    return h + h * jnp.tanh(h)


def _softplus(x):
    return jnp.maximum(x, 0.0) + jnp.log(1.0 + jnp.exp(-jnp.abs(x)))


def _rms(x):
    return x * lax.rsqrt(jnp.mean(x * x, axis=-1, keepdims=True) + EPS)


def _gelu_tanh(x):
    c = 0.7978845608028654
    return 0.5 * x * (1.0 + jnp.tanh(c * (x + 0.044715 * (x * x * x))))


def _const_spec(shape):
    n = len(shape)
    return pl.BlockSpec(shape, lambda *_: (0,) * n, pipeline_mode=pl.Buffered(1))


def _inproj_kernel(x_ref, nw_ref, w_ref, wkt_ref, wvt_ref, cw_ref, prev_ref,
                   act_ref, cst_ref, z_ref, ba_ref, q_ref, k_ref, vt_ref, kf_ref, vf_ref, carry,
                   *, keep_all, n_tiles, S, R, tps):
    x = x_ref[0]
    tm = x.shape[0]
    u = (_rms(x) * nw_ref[...]).astype(BF16)

    hist = GDN_CONV - 1
    if tps > 1:
        @pl.when(pl.program_id(1) % tps == 0)
        def _():
            for r in range(hist):
                carry[r:r + 1, :] = prev_ref[0, r]
    cw = cw_ref[...]
    slabs = [slice(c * GDN_QK, (c + 1) * GDN_QK) for c in range(GDN_QKV // GDN_QK)]

    def conv_silu(pre, cols):
        if tps > 1:
            history = lambda s, r, cols=cols: carry[r:r + 1, cols]
        else:
            history = lambda s, r, cols=cols: prev_ref[0, r, s:s + 1, cols]
        conv = cw[hist:hist + 1, cols] * pre
        for k in range(1, GDN_CONV):
            conv = conv + cw[hist - k:hist - k + 1, cols] * _delayed(
                pre, k, S, R, lambda s, r, k=k, history=history: history(s, hist - k + r))
        act_ref[0, :, cols] = _silu(conv)
        for s in range(S):
            for r in range(hist):
                row = pre[(s + 1) * R - hist + r:(s + 1) * R - hist + r + 1]
                if tps > 1:
                    carry[r:r + 1, cols] = row
                else:
                    cst_ref[0, r, s:s + 1, cols] = row

    p = _dot_nt(u, w_ref[...])
    for cols in slabs:
        conv_silu(p[:, cols], cols)
    if tps > 1:
        for r in range(hist):
            cst_ref[0, r] = carry[r:r + 1, :]
    c0 = GDN_QKV + GDN_Z
    c1 = c0 + LANES
    z_ref[0] = p[:, GDN_QKV:c0]
    ba_ref[0] = p[:, c0:c1]
    q_ref[0] = (p[:, c1:c1 + ATT_W] * Q_SCALE).astype(BF16)
    k = p[:, c1 + ATT_W:]
    k_ref[0] = k.astype(BF16)
    vt = _dot_nt(wvt_ref[...], u)
    for jb in range(tm // LANES):
        vt_ref[0, jb] = vt[:, jb * LANES:(jb + 1) * LANES].astype(BF16)

    if keep_all:
        kf_ref[0] = k
        vf_ref[0] = _dot_nt(u, wvt_ref[...])
    else:
        @pl.when(pl.program_id(1) == n_tiles - 1)
        def _():
            kf_ref[0] = _dot_nt(wkt_ref[...], u).reshape(ATT_HEADS, ATT_DH, tm)
            vf_ref[0] = vt.reshape(ATT_HEADS, ATT_DH, tm)


def _inproj(x, nw, w, wkt, wvt, cw, prev, *, keep_all):
    B, T, D = x.shape
    tm = ROW_TILE
    nt = T // tm
    n_seq = prev.shape[0]
    hist = GDN_CONV - 1
    R = min(B * T // n_seq, tm)
    S = tm // R
    tps = B * T // (n_seq * R)
    assert S == 1 or tps == 1
    grouped = (n_seq // S, hist, S, GDN_QKV)
    prev_g = prev.reshape(n_seq // S, S, hist, GDN_QKV).transpose(0, 2, 1, 3)
    groups_per_b = nt // tps
    st = pl.BlockSpec((1,) + grouped[1:], lambda b, i: (b * groups_per_b + i // tps, 0, 0, 0))
    row = lambda w: pl.BlockSpec((1, tm, w), lambda b, i: (b, i, 0))
    if keep_all:
        keep = row(ATT_W)
        keep_shape = (B, T, ATT_W)
    else:
        keep = pl.BlockSpec((1, ATT_HEADS, ATT_DH, tm), lambda b, i: (b, 0, 0, 0))
        keep_shape = (B, ATT_HEADS, ATT_DH, tm)
    consts = [nw, w, wkt, wvt, cw]
    act, cst, *rest = pl.pallas_call(
        functools.partial(_inproj_kernel, keep_all=keep_all, n_tiles=nt, S=S, R=R, tps=tps),
        name="inproj",
        grid=(B, nt),
        in_specs=[row(D)] + [_const_spec(a.shape) for a in consts] + [st],
        out_specs=[row(GDN_QKV), st, row(GDN_Z), row(LANES), row(ATT_W), row(ATT_W),
                   pl.BlockSpec((1, tm // LANES, ATT_W, LANES), lambda b, i: (b, i, 0, 0)), keep, keep],
        out_shape=[jax.ShapeDtypeStruct((B, T, GDN_QKV), F32),
                   jax.ShapeDtypeStruct(grouped, F32),
                   jax.ShapeDtypeStruct((B, T, GDN_Z), F32),
                   jax.ShapeDtypeStruct((B, T, LANES), F32),
                   jax.ShapeDtypeStruct((B, T, ATT_W), BF16),
                   jax.ShapeDtypeStruct((B, T, ATT_W), BF16),
                   jax.ShapeDtypeStruct((B, T // LANES, ATT_W, LANES), BF16),
                   jax.ShapeDtypeStruct(keep_shape, F32),
                   jax.ShapeDtypeStruct(keep_shape, F32)],
        scratch_shapes=[pltpu.VMEM((hist, GDN_QKV), F32)],
        compiler_params=pltpu.CompilerParams(
            dimension_semantics=("arbitrary", "arbitrary"), vmem_limit_bytes=VMEM_LIMIT),
    )(x, *consts, prev_g)
    return (act, cst.transpose(0, 2, 1, 3).reshape(n_seq, hist, GDN_QKV), *rest)


def _bmm(a, b):
    return lax.dot_general(a, b, (((2,), (1,)), ((0,), (0,))), preferred_element_type=F32)


def _bmm_nt(a, b):
    return lax.dot_general(a, b, (((2,), (2,)), ((0,), (0,))), preferred_element_type=F32)


def _bmm_tn(a, b):
    return lax.dot_general(a, b, (((1,), (1,)), ((0,), (0,))), preferred_element_type=F32)


def _gdn_kernel(act_ref, z_ref, ba_ref, s0_ref, alog_ref, dtb_ref, gnw_ref,
                o_ref, sout_ref, s_scr, *, C, cps, bs, n_steps):
    j = pl.program_id(1)
    R = C * cps
    H = GDN_HEADS
    nh = bs * H

    @pl.when(j == 0)
    def _():
        s_scr[...] = s0_ref[...].reshape(nh, GDN_DK, GDN_DV)

    ri = lax.broadcasted_iota(jnp.int32, (C, C), 0)
    ci = lax.broadcasted_iota(jnp.int32, (C, C), 1)
    incl = ri >= ci
    strict = ri > ci
    rr = lax.broadcasted_iota(jnp.int32, (R, R), 0)
    cc = lax.broadcasted_iota(jnp.int32, (R, R), 1)
    cum16 = jnp.where((rr >= cc) & (rr // C == cc // C), 1.0, 0.0).astype(BF16)
    alog = alog_ref[...]
    dtb = dtb_ref[...]

    acts, sigs, Gs, GTs = [], [], [], []
    for s in range(bs):
        acts.append(act_ref.at[s])
        ba = ba_ref[s]
        sigs.append(_sigmoid(ba))
        G = _dot_exact_lhs(cum16, -jnp.exp(alog) * _softplus(ba + dtb))
        Gs.append(G)
        GTs.append(G.T)

    order = [(c, s, h) for c in range(cps) for s in range(bs) for h in range(H)]
    rows = lambda c: slice(c * C, (c + 1) * C)

    def tiles(slabs, col0, width):
        return jnp.stack([slabs[s][rows(c), col0 + h * width:col0 + (h + 1) * width] for c, s, h in order])

    q = tiles(acts, 0, GDN_DK)
    k = tiles(acts, GDN_QK, GDN_DK)
    v = tiles(acts, 2 * GDN_QK, GDN_DV)
    beta = tiles(sigs, 0, 1)
    Gc = tiles(Gs, H, 1)
    Gr = jnp.stack([GTs[s][H + h:H + h + 1, rows(c)] for c, s, h in order])
    Gl = Gc[:, C - 1:C, :]

    nb = len(order)
    li = lax.broadcasted_iota(jnp.int32, (2 * GDN_DK, 2 * GDN_DK), 0) // GDN_DK
    lj = lax.broadcasted_iota(jnp.int32, (2 * GDN_DK, 2 * GDN_DK), 1) // GDN_DK
    ones2 = jnp.where(li == lj, 1.0, 0.0).astype(BF16)
    sq = jnp.concatenate([q * q, k * k], axis=-1).reshape(nb * C, 2 * GDN_DK)
    sq_hi = sq.astype(BF16)
    sq_lo = (sq - sq_hi.astype(F32)).astype(BF16)
    norms = (_dot(sq_hi, ones2) + _dot(sq_lo, ones2)).reshape(nb, C, 2 * GDN_DK)
    qn = q * lax.rsqrt(norms[:, :, :GDN_DK] + EPS) * (GDN_DK ** -0.5)
    kn = k * lax.rsqrt(norms[:, :, GDN_DK:] + EPS)
    eG = jnp.exp(Gc)
    gam = jnp.where(incl, jnp.exp(jnp.where(incl, Gc - Gr, 0.0)), 0.0)
    kb = kn * beta
    kn16 = kn.astype(BF16)
    aq = _bmm_nt(jnp.concatenate([kb, qn], axis=1).astype(BF16), kn16)
    A = jnp.where(strict, aq[:, :C] * gam, 0.0)
    QK16 = (aq[:, C:] * gam).astype(BF16)

    n_joint = C.bit_length() - 2
    A16 = A.astype(BF16)
    N = -A
    Q = _bmm(A16, A16)
    for it in range(n_joint):
        Q16 = Q.astype(BF16)
        if it == n_joint - 1:
            N = N + Q + _bmm(N.astype(BF16), Q16)
        else:
            nq = _bmm(jnp.concatenate([N, Q], axis=1).astype(BF16), Q16)
            N = N + Q + nq[:, :C]
            Q = nq[:, C:]
    rhs = jnp.concatenate([v * beta, kb * eG], axis=-1)
    sol = rhs + _bmm(N.astype(BF16), rhs.astype(BF16))
    u = sol[:, :, :GDN_DV]
    wq16 = jnp.concatenate([sol[:, :, GDN_DV:], qn * eG], axis=1).astype(BF16)
    kg16 = (kn * jnp.exp(Gl - Gc)).astype(BF16)
    dl = jnp.exp(Gl)

    S = s_scr[...]
    o_parts = []
    for c in range(cps):
        sl = slice(c * nh, (c + 1) * nh)
        r = _bmm(wq16[sl], S.astype(BF16))
        vn16 = (u[sl] - r[:, :C]).astype(BF16)
        o_parts.append(r[:, C:] + _bmm(QK16[sl], vn16))
        S = S * dl[sl] + _bmm_tn(kg16[sl], vn16)
    s_scr[...] = S

    on = _rms(jnp.concatenate(o_parts, axis=0)) * gnw_ref[...]
    gates = []
    for s in range(bs):
        zs = z_ref[s]
        gates.append(_silu(zs))
    for idx, (c, s, h) in enumerate(order):
        cols = slice(h * GDN_DV, (h + 1) * GDN_DV)
        o_ref[s, rows(c), cols] = (on[idx] * gates[s][rows(c), cols]).astype(o_ref.dtype)

    @pl.when(j == n_steps - 1)
    def _():
        sout_ref[...] = S.reshape(bs, H, GDN_DK, GDN_DV)


def _gdn(act, z, ba, s0, alog, dtb, gnw, *, C, cps, bs):
    B, T, _ = act.shape
    R = C * cps
    assert B % bs == 0 and T % R == 0
    n_steps = T // R
    row = lambda w: pl.BlockSpec((bs, R, w), lambda b, j: (b, j, 0))
    st = pl.BlockSpec((bs, GDN_HEADS, GDN_DK, GDN_DV), lambda b, j: (b, 0, 0, 0))
    return pl.pallas_call(
        functools.partial(_gdn_kernel, C=C, cps=cps, bs=bs, n_steps=n_steps),
        name="gdn",
        grid=(B // bs, n_steps),
        in_specs=[row(GDN_QKV), row(GDN_Z), row(LANES), st,
                  _const_spec(alog.shape), _const_spec(dtb.shape), _const_spec(gnw.shape)],
        out_specs=[row(GDN_Z), st],
        out_shape=[jax.ShapeDtypeStruct((B, T, GDN_Z), BF16),
                   jax.ShapeDtypeStruct((B, GDN_HEADS, GDN_DK, GDN_DV), F32)],
        scratch_shapes=[pltpu.VMEM((bs * GDN_HEADS, GDN_DK, GDN_DV), F32)],
        compiler_params=pltpu.CompilerParams(
            dimension_semantics=("arbitrary", "arbitrary"), vmem_limit_bytes=VMEM_LIMIT),
    )(act, z, ba, s0, alog, dtb, gnw)


def _attn_prompt_kernel(q_ref, k_ref, vt_ref, bias_ref, nw_ref, o_ref, *, qt):
    j = pl.program_id(1)
    tq = 2 * CHUNK
    n_kb = BAND_CHUNKS // 2 + 1
    n_pairs = ATT_HEADS // 2
    lane_head = lax.broadcasted_iota(jnp.int32, (tq, LANES), 1) // ATT_DH

    def tile(i, masked):
        m = j * qt + i
        q = q_ref[0, i * tq:(i + 1) * tq, :]
        wt = []
        for p in range(n_pairs):
            qp = q[:, p * LANES:(p + 1) * LANES]
            zero = jnp.zeros_like(qp)
            wt.append(jnp.concatenate([jnp.where(lane_head == 0, qp, zero),
                                       jnp.where(lane_head == 1, qp, zero)], axis=0))
        wt = jnp.stack(wt)
        firsts = [m - (n_kb - 1) + jb for jb in range(n_kb)]
        blks = [jnp.maximum(f, 0) for f in firsts]
        k_all = jnp.concatenate(
            [jnp.stack([k_ref[0, blk][:, p * LANES:(p + 1) * LANES] for p in range(n_pairs)]) for blk in blks],
            axis=1)
        v_all = jnp.concatenate([vt_ref[0, blk].reshape(n_pairs, LANES, tq) for blk in blks], axis=2)
        s_all = _bmm_nt(k_all, wt)
        s_parts = []
        for jb in range(n_kb):
            s = s_all[:, jb * tq:(jb + 1) * tq, :]
            if jb not in _FLAT_BIAS_BLOCKS:
                s = s + bias_ref[:, jb * tq:(jb + 1) * tq, :]
            if masked and jb < n_kb - 1:
                s = jnp.where(firsts[jb] >= 0, s, -jnp.inf)
            s_parts.append(s)
        st = jnp.concatenate(s_parts, axis=1)
        pt = jnp.exp2(st - jnp.max(st, axis=1, keepdims=True))
        inv = 1.0 / jnp.sum(pt, axis=1, keepdims=True)
        acc = _bmm(v_all, pt.astype(BF16))
        ot = jnp.concatenate([acc[:, :ATT_DH, :tq] * inv[:, :, :tq],
                              acc[:, ATT_DH:, tq:] * inv[:, :, tq:]], axis=1)
        oh = ot.reshape(ATT_HEADS, ATT_DH, tq)
        on = oh * lax.rsqrt(jnp.mean(oh * oh, axis=1, keepdims=True) + EPS) * nw_ref[...]
        on = on.reshape(n_pairs, LANES, tq)
        for p in range(n_pairs):
            o_ref[0, i * tq:(i + 1) * tq, p * LANES:(p + 1) * LANES] = on[p].T.astype(o_ref.dtype)

    near_start = j * qt < n_kb - 1

    @pl.when(near_start)
    def _():
        for i in range(qt):
            tile(i, True)

    @pl.when(jnp.logical_not(near_start))
    def _():
        for i in range(qt):
            tile(i, False)


def _attn_prompt(q, k, vt, bias, nw, *, qt):
    B, T, _ = q.shape
    tq = 2 * CHUNK
    n_steps = T // (qt * tq)
    k4 = k.reshape(B, T // tq, tq, ATT_W)
    row = pl.BlockSpec((1, qt * tq, ATT_W), lambda b, j: (b, j, 0))
    return pl.pallas_call(
        functools.partial(_attn_prompt_kernel, qt=qt),
        name="attn_prompt",
        grid=(B, n_steps),
        in_specs=[row,
                  pl.BlockSpec((1, T // tq, tq, ATT_W), lambda b, j: (b, 0, 0, 0)),
                  pl.BlockSpec((1, T // tq, ATT_W, tq), lambda b, j: (b, 0, 0, 0)),
                  _const_spec(bias.shape), _const_spec(nw.shape)],
        out_specs=row,
        out_shape=jax.ShapeDtypeStruct((B, T, ATT_W), BF16),
        compiler_params=pltpu.CompilerParams(
            dimension_semantics=("arbitrary", "arbitrary"), vmem_limit_bytes=VMEM_LIMIT),
    )(q, k4, vt, bias, nw)


def _attn_sample_kernel(q_ref, kc_ref, vc_ref, kn_ref, vn_ref, bc_ref, bn_ref, nw_ref, o_ref, *, n_seq):
    by_head = lambda a: jnp.stack([a[:, h * ATT_DH:(h + 1) * ATT_DH] for h in range(ATT_HEADS)])
    for s in range(n_seq):
        kc = kc_ref[s].astype(BF16)
        vc = vc_ref[s].astype(BF16)
        q = by_head(q_ref[s])
        kn = by_head(kn_ref[s])
        vn = by_head(vn_ref[s]).astype(BF16)
        s_c = _bmm(q, kc) + bc_ref[...]
        s_n = _bmm_nt(q, kn) + bn_ref[...]
        m = jnp.maximum(jnp.max(s_c, axis=-1, keepdims=True), jnp.max(s_n, axis=-1, keepdims=True))
        p_c = jnp.exp2(s_c - m)
        p_n = jnp.exp2(s_n - m)
        l = jnp.sum(p_c, axis=-1, keepdims=True) + jnp.sum(p_n, axis=-1, keepdims=True)
        o = (_bmm_nt(p_c.astype(BF16), vc) + _bmm(p_n.astype(BF16), vn)) * (1.0 / l)
        on = _rms(o) * nw_ref[...]
        o_ref[s] = jnp.concatenate([on[h] for h in range(ATT_HEADS)], axis=-1).astype(o_ref.dtype)


def _attn_sample(q, kc, vc, kn, vn, bias_c, bias_n, nw, *, n_seq):
    B, T, _ = q.shape
    new = pl.BlockSpec((n_seq, T, ATT_W), lambda b: (b, 0, 0))
    cache = pl.BlockSpec((n_seq,) + kc.shape[1:], lambda b: (b, 0, 0, 0))
    return pl.pallas_call(
        functools.partial(_attn_sample_kernel, n_seq=n_seq),
        name="attn_sample",
        grid=(B // n_seq,),
        in_specs=[new, cache, cache, new, new, _const_spec(bias_c.shape), _const_spec(bias_n.shape),
                  _const_spec(nw.shape)],
        out_specs=new,
        out_shape=jax.ShapeDtypeStruct((B, T, ATT_W), BF16),
        compiler_params=pltpu.CompilerParams(
            dimension_semantics=("arbitrary",), vmem_limit_bytes=VMEM_LIMIT),
    )(q, kc, vc, kn, vn, bias_c, bias_n, nw)


def _delayed(g, k, S, R, hist_row):
    rolled = pltpu.roll(g, k, axis=0)
    sub = lax.broadcasted_iota(jnp.int32, (SUBLANES, g.shape[1]), 0)
    parts = []
    for s in range(S):
        head = rolled[s * R:s * R + SUBLANES]
        for i in range(k):
            head = jnp.where(sub == i, hist_row(s, i), head)
        parts += [head, rolled[s * R + SUBLANES:(s + 1) * R]]
    return jnp.concatenate(parts, axis=0)


def _mix_ffn_kernel(x_ref, oa_ref, ob_ref, wo_ref, nmp_ref, nfp_ref, wgu_ref, cw_ref, cb_ref,
                    wd_ref, nfo_ref, prev_ref, y_ref, st_ref, carry,
                    *, S, R, tps, d_ff):
    t = pl.program_id(0)
    hist = FFN_CONV - 1
    half = oa_ref.shape[-1]
    mix = _dot(oa_ref[...], wo_ref[:half, :]) + _dot(ob_ref[...], wo_ref[half:, :])
    x1 = x_ref[...] + _rms(mix) * nmp_ref[...]
    u2 = (_rms(x1) * nfp_ref[...]).astype(BF16)

    if tps > 1:
        @pl.when(t % tps == 0)
        def _():
            for i in range(hist):
                carry[i:i + 1, :] = prev_ref[0, i]

    gu = _dot(u2, wgu_ref[...])
    g = gu[:, :d_ff]
    up = gu[:, d_ff:]
    if tps > 1:
        history = lambda s, i: carry[i:i + 1, :]
    else:
        history = lambda s, i: prev_ref[0, i, s:s + 1, :]
    cw = cw_ref[...]
    conv = cb_ref[...] + cw[hist:hist + 1] * g
    for k in range(1, FFN_CONV):
        conv = conv + cw[hist - k:hist - k + 1] * _delayed(
            g, k, S, R, lambda s, i, k=k: history(s, hist - k + i))
    for s in range(S):
        for i in range(hist):
            row = g[(s + 1) * R - hist + i:(s + 1) * R - hist + i + 1]
            if tps > 1:
                carry[i:i + 1, :] = row
            else:
                st_ref[0, i, s:s + 1, :] = row
    if tps > 1:
        for i in range(hist):
            st_ref[0, i] = carry[i:i + 1, :]
    hid = (_gelu_tanh(conv) * up).astype(BF16)
    y_ref[...] = x1 + _rms(_dot(hid, wd_ref[...])) * nfo_ref[...]


def _mix_ffn(x, oa, ob, wo, nmp, nfp, wgu, cw, cb, wd, nfo, prev, *, S, R):
    M, D = x.shape
    d_ff = wd.shape[0]
    tm = S * R
    n_seq = prev.shape[0]
    hist = FFN_CONV - 1
    tps = M // (n_seq * R)
    assert S == 1 or tps == 1
    grouped = (n_seq // S, hist, S, d_ff)
    prev_g = prev.reshape(n_seq // S, S, hist, d_ff).transpose(0, 2, 1, 3)
    row = lambda w: pl.BlockSpec((tm, w), lambda t: (t, 0))
    st = pl.BlockSpec((1,) + grouped[1:], lambda t: (t // tps, 0, 0, 0))
    consts = [wo, nmp, nfp, wgu, cw, cb, wd, nfo]
    y, st_g = pl.pallas_call(
        functools.partial(_mix_ffn_kernel, S=S, R=R, tps=tps, d_ff=d_ff),
        name="mix_ffn",
        grid=(M // tm,),
        in_specs=[row(D), row(oa.shape[1]), row(ob.shape[1])] + [_const_spec(a.shape) for a in consts] + [st],
        out_specs=[row(D), st],
        out_shape=[jax.ShapeDtypeStruct((M, D), F32), jax.ShapeDtypeStruct(grouped, F32)],
        scratch_shapes=[pltpu.VMEM((hist, d_ff), F32)],
        compiler_params=pltpu.CompilerParams(
            dimension_semantics=("arbitrary",), vmem_limit_bytes=VMEM_LIMIT),
    )(x, oa, ob, *consts, prev_g)
    return y, st_g.transpose(0, 2, 1, 3).reshape(n_seq, hist, d_ff)


def _lane_row(vals, offset):
    return jnp.zeros((1, LANES), F32).at[0, offset:offset + vals.shape[0]].set(vals.astype(F32))


def _bias_kernel(r_ref, bp_ref, bc_ref, bn_ref, *, lc, ts):
    tq = 2 * CHUNK
    key = lax.broadcasted_iota(jnp.int32, (tq, tq), 0)
    query_chunk = lax.broadcasted_iota(jnp.int32, (tq, tq), 1) // CHUNK
    for h in range(r_ref.shape[0]):
        p, e = divmod(h, 2)
        for jb in range(r_ref.shape[1]):
            row = jnp.broadcast_to(r_ref[h, jb], (tq, 2 * tq))
            blk = pltpu.roll(row, 0, 1, stride=1, stride_axis=0)[:, :tq]
            key_in_band = jb * tq + key - query_chunk * CHUNK
            valid = (key_in_band >= 0) & (key_in_band < (BAND_CHUNKS + 1) * CHUNK)
            far = r_ref[h, 0][:, 0:1]
            bp_ref[p, jb * tq:(jb + 1) * tq, e * tq:(e + 1) * tq] = jnp.where(valid, blk - far, -jnp.inf)
            by_query = blk.T
            if (jb + 1) * tq <= lc:
                bc_ref[h, :, jb * tq:(jb + 1) * tq] = by_query[:ts]
            else:
                bn_ref[h] = by_query[:ts, :ts]


def _band_biases(table, lc, ts):
    n_heads = table.shape[0]
    tq = 2 * CHUNK
    n_kb = BAND_CHUNKS // 2 + 1
    assert lc == (n_kb - 1) * tq and ts <= tq
    starts = [WINDOW - tq * jb + MAX_REL - half * tq for jb in range(n_kb) for half in (0, 1)]
    pad_l = max(0, -min(starts))
    pad_r = max(0, max(starts) + tq - table.shape[1])
    ext = jnp.pad(table, ((0, 0), (pad_l, pad_r)), mode="edge")
    rows = jnp.concatenate([ext[:, s + pad_l:s + pad_l + tq] for s in starts], axis=1)
    rows = rows.reshape(n_heads, n_kb, 1, 2 * tq)
    return pl.pallas_call(
        functools.partial(_bias_kernel, lc=lc, ts=ts),
        name="band_bias",
        out_shape=[jax.ShapeDtypeStruct((n_heads // 2, n_kb * tq, 2 * tq), F32),
                   jax.ShapeDtypeStruct((n_heads, ts, lc), F32),
                   jax.ShapeDtypeStruct((n_heads, ts, ts), F32)],
    )(rows)


def _layer(xp, xs, cache_k, cache_v, s_delta, s_qkv, s_ffn, lw):
    (norm_mix_pre, w_in, qkv_conv_w, a_log, dt_bias, gdn_norm_w, rel_bias, attn_norm_w, w_out,
     norm_mix_post, norm_ffn_pre, w_gate_up, ffn_conv_w, ffn_conv_b, w_down, norm_ffn_post) = lw
    Bp, Tp, D = xp.shape
    Bs, Ts, _ = xs.shape
    d_ff = w_down.shape[0]

    c1 = GDN_QKV + GDN_Z
    c2 = c1 + 2 * GDN_HEADS
    wt16 = jnp.swapaxes(w_in, 0, 1).astype(BF16)
    w_proj = jnp.concatenate([wt16[:c1], jnp.pad(wt16[c1:c2], ((0, LANES - 2 * GDN_HEADS), (0, 0))),
                              wt16[c2:c2 + 2 * ATT_W]], axis=0)
    wkt = wt16[c2 + ATT_W:c2 + 2 * ATT_W]
    wvt = wt16[c2 + 2 * ATT_W:]
    nmix = norm_mix_pre.reshape(1, D)
    cw_qkv = jnp.pad(qkv_conv_w, ((0, SUBLANES - GDN_CONV), (0, 0)))
    alog = _lane_row(a_log, GDN_HEADS)
    dtb = _lane_row(dt_bias, GDN_HEADS)
    gnw = gdn_norm_w.reshape(1, GDN_DV)
    anw = attn_norm_w.astype(F32).reshape(1, ATT_DH)
    anw_col = jnp.broadcast_to(attn_norm_w.astype(F32)[:, None], (ATT_DH, 2 * CHUNK))
    lc = cache_k.shape[1]
    bias_p, bias_c, bias_n = _band_biases(rel_bias.astype(F32) * LOG2E, lc, Ts)
    wo = w_out.astype(BF16)
    wgu = w_gate_up.astype(BF16)
    wd = w_down.astype(BF16)
    cw_ffn = jnp.pad(ffn_conv_w, ((0, SUBLANES - FFN_CONV), (0, 0)))
    cb = ffn_conv_b.reshape(1, d_ff)
    nmp = norm_mix_post.reshape(1, D)
    nfp = norm_ffn_pre.reshape(1, D)
    nfo = norm_ffn_post.reshape(1, D)

    def group(x, keep_all, gdn_prev, gdn_s0, gdn_c, gdn_cps, gdn_bs, ffn_prev, S, R, attn):
        B, T, _ = x.shape
        xi = x if not keep_all else x.reshape(1, B * T, D)
        act, qkv_state, z, ba, q, k, vt, kf, vf = _inproj(xi, nmix, w_proj, wkt, wvt, cw_qkv, gdn_prev,
                                                          keep_all=keep_all)
        rs = lambda a: a.reshape(B, T, a.shape[-1])
        act, z, ba, q, k = map(rs, (act, z, ba, q, k))
        oa, s_new = _gdn(act, z, ba, gdn_s0, alog, dtb, gnw, C=gdn_c, cps=gdn_cps, bs=gdn_bs)
        ob = attn(q, k, vt, vf)
        y, ffn_state = _mix_ffn(x.reshape(B * T, D), oa.reshape(B * T, GDN_Z), ob.reshape(B * T, ATT_W),
                                wo, nmp, nfp, wgu, cw_ffn, cb, wd, nfo, ffn_prev,
                                S=S, R=R)
        if keep_all:
            k_rows = kf.reshape(B, T, ATT_HEADS, ATT_DH)
            v_rows = vf.reshape(B, T, ATT_HEADS, ATT_DH)
        else:
            k_rows = jnp.transpose(kf, (0, 3, 1, 2))
            v_rows = jnp.transpose(vf, (0, 3, 1, 2))
        return y.reshape(B, T, D), k_rows, v_rows, s_new, qkv_state, ffn_state

    out_p = group(
        xp, False, jnp.zeros((Bp, GDN_CONV - 1, GDN_QKV), F32),
        jnp.zeros((Bp, GDN_HEADS, GDN_DK, GDN_DV), F32), CHUNK, 4, 4,
        jnp.zeros((Bp, FFN_CONV - 1, d_ff), F32), 1, ROW_TILE,
        lambda q, k, vt, vf: _attn_prompt(q, k, vt, bias_p, anw_col, qt=4))

    kc_t = jnp.transpose(cache_k, (0, 2, 3, 1))
    vc_t = jnp.transpose(cache_v, (0, 2, 3, 1))
    out_s = group(
        xs, True, s_qkv, s_delta, Ts, 1, 4, s_ffn, ROW_TILE // Ts, Ts,
        lambda q, k, vt, vf: _attn_sample(q, kc_t, vc_t, k, vf.reshape(Bs, Ts, ATT_W), bias_c, bias_n, anw,
                                          n_seq=2))
    return out_p, out_s


def kernel(x_prompt, x_sample, cache_band_k, cache_band_v, state_delta, state_qkv_conv, state_ffn_conv, norm_mix_pre, w_in, qkv_conv_w, a_log, dt_bias, gdn_norm_w, rel_bias, attn_norm_w, w_out, norm_mix_post, norm_ffn_pre, w_gate_up, ffn_conv_w, ffn_conv_b, w_down, norm_ffn_post):
    weights = (norm_mix_pre, w_in, qkv_conv_w, a_log, dt_bias, gdn_norm_w, rel_bias, attn_norm_w, w_out,
               norm_mix_post, norm_ffn_pre, w_gate_up, ffn_conv_w, ffn_conv_b, w_down, norm_ffn_post)
    depth = w_in.shape[0]
    xp, xs = x_prompt, x_sample
    outs_p, outs_s = [], []
    for l in range(depth):
        lw = tuple(w[l] for w in weights)
        op, os_ = _layer(xp, xs, cache_band_k[l], cache_band_v[l], state_delta[l], state_qkv_conv[l],
                         state_ffn_conv[l], lw)
        xp, xs = op[0], os_[0]
        outs_p.append(op[1:])
        outs_s.append(os_[1:])
    stack = lambda outs, i: jnp.stack([o[i] for o in outs], axis=0)
    return (xp, xs) + tuple(stack(outs_p, i) for i in range(5)) + tuple(stack(outs_s, i) for i in range(5))
```

```python
import functools

import jax
import jax.numpy as jnp
from jax import lax
from jax.experimental import pallas as pl
from jax.experimental.pallas import tpu as pltpu

F32 = jnp.float32
BF16 = jnp.bfloat16

EPS = 1e-6
CHUNK = 64
GDN_HEADS = 4
GDN_DK = 128
GDN_DV = 128
GDN_CONV = 4
ATT_HEADS = 8
ATT_DH = 64
BAND_CHUNKS = 8
WINDOW = BAND_CHUNKS * CHUNK
MAX_REL = 128
FFN_CONV = 3

GDN_QK = GDN_HEADS * GDN_DK
GDN_QKV = GDN_HEADS * (2 * GDN_DK + GDN_DV)
GDN_Z = GDN_HEADS * GDN_DV
ATT_W = ATT_HEADS * ATT_DH
LANES = 128
SUBLANES = 8
VMEM_LIMIT = 56 * 1024 * 1024
ROW_TILE = 512
_FLAT_BIAS_BLOCKS = tuple(
    jb for jb in range(BAND_CHUNKS // 2 + 1)
    if jb * 2 * CHUNK >= CHUNK and (jb + 1) * 2 * CHUNK <= (BAND_CHUNKS + 1) * CHUNK
    and WINDOW - (jb + 1) * 2 * CHUNK + 1 >= MAX_REL)
LOG2E = 1.4426950408889634
Q_SCALE = ATT_DH ** -0.5 * LOG2E


def _dot(a, b):
    return jnp.dot(a, b, preferred_element_type=F32)


def _dot_nt(a, b):
    return lax.dot_general(a, b, (((1,), (1,)), ((), ())), preferred_element_type=F32)


def _split3(x):
    x1 = x.astype(BF16)
    r1 = x - x1.astype(F32)
    x2 = r1.astype(BF16)
    x3 = (r1 - x2.astype(F32)).astype(BF16)
    return x1, x2, x3


def _dot_exact_lhs(a16, x):
    x1, x2, x3 = _split3(x)
    return _dot(a16, x1) + _dot(a16, x2) + _dot(a16, x3)


def _sigmoid(x):
    return 0.5 + 0.5 * jnp.tanh(0.5 * x)


def _silu(x):
    h = 0.5 * x
    return h + h * jnp.tanh(h)


def _softplus(x):
    return jnp.maximum(x, 0.0) + jnp.log(1.0 + jnp.exp(-jnp.abs(x)))


def _rms(x):
    return x * lax.rsqrt(jnp.mean(x * x, axis=-1, keepdims=True) + EPS)


def _gelu_tanh(x):
    c = 0.7978845608028654
    h = 0.5 * x
    return h + h * jnp.tanh(x * (c + (0.044715 * c) * (x * x)))


def _const_spec(shape):
    n = len(shape)
    return pl.BlockSpec(shape, lambda *_: (0,) * n, pipeline_mode=pl.Buffered(1))


def _inproj_kernel(x_ref, nw_ref, w_ref, wkt_ref, wvt_ref, cw_ref, prev_ref,
                   act_ref, cst_ref, z_ref, ba_ref, q_ref, k_ref, vt_ref, kf_ref, vf_ref, carry,
                   *, keep_all, n_tiles, S, R, tps):
    x = x_ref[0]
    tm = x.shape[0]
    u = (_rms(x) * nw_ref[...]).astype(BF16)

    hist = GDN_CONV - 1
    if tps > 1:
        @pl.when(pl.program_id(1) % tps == 0)
        def _():
            for r in range(hist):
                carry[r:r + 1, :] = prev_ref[0, r]
    cw = cw_ref[...]
    slabs = [slice(c * GDN_QK, (c + 1) * GDN_QK) for c in range(GDN_QKV // GDN_QK)]

    def conv_silu(pre, cols):
        if tps > 1:
            history = lambda s, r, cols=cols: carry[r:r + 1, cols]
        else:
            history = lambda s, r, cols=cols: prev_ref[0, r, s:s + 1, cols]
        conv = cw[hist:hist + 1, cols] * pre
        for k in range(1, GDN_CONV):
            conv = conv + cw[hist - k:hist - k + 1, cols] * _delayed(
                pre, k, S, R, lambda s, r, k=k, history=history: history(s, hist - k + r))
        act_ref[0, :, cols] = _silu(conv)
        for s in range(S):
            for r in range(hist):
                row = pre[(s + 1) * R - hist + r:(s + 1) * R - hist + r + 1]
                if tps > 1:
                    carry[r:r + 1, cols] = row
                else:
                    cst_ref[0, r, s:s + 1, cols] = row

    p = _dot_nt(u, w_ref[...])
    for cols in slabs:
        conv_silu(p[:, cols], cols)
    if tps > 1:
        for r in range(hist):
            cst_ref[0, r] = carry[r:r + 1, :]
    c0 = GDN_QKV + GDN_Z
    c1 = c0 + LANES
    z_ref[0] = p[:, GDN_QKV:c0]
    ba_ref[0] = p[:, c0:c1]
    q_ref[0] = (p[:, c1:c1 + ATT_W] * Q_SCALE).astype(BF16)
    k = p[:, c1 + ATT_W:]
    k_ref[0] = k.astype(BF16)
    vt = _dot_nt(wvt_ref[...], u)
    for jb in range(tm // LANES):
        vt_ref[0, jb] = vt[:, jb * LANES:(jb + 1) * LANES].astype(BF16)

    if keep_all:
        kf_ref[0] = k
        vf_ref[0] = _dot_nt(u, wvt_ref[...])
    else:
        @pl.when(pl.program_id(1) == n_tiles - 1)
        def _():
            kf_ref[0] = _dot_nt(wkt_ref[...], u).reshape(ATT_HEADS, ATT_DH, tm)
            vf_ref[0] = vt.reshape(ATT_HEADS, ATT_DH, tm)


def _inproj(x, nw, w, wkt, wvt, cw, prev, *, keep_all):
    B, T, D = x.shape
    tm = ROW_TILE
    nt = T // tm
    n_seq = prev.shape[0]
    hist = GDN_CONV - 1
    R = min(B * T // n_seq, tm)
    S = tm // R
    tps = B * T // (n_seq * R)
    assert S == 1 or tps == 1
    grouped = (n_seq // S, hist, S, GDN_QKV)
    prev_g = prev.reshape(n_seq // S, S, hist, GDN_QKV).transpose(0, 2, 1, 3)
    groups_per_b = nt // tps
    st = pl.BlockSpec((1,) + grouped[1:], lambda b, i: (b * groups_per_b + i // tps, 0, 0, 0))
    row = lambda w: pl.BlockSpec((1, tm, w), lambda b, i: (b, i, 0))
    if keep_all:
        keep = row(ATT_W)
        keep_shape = (B, T, ATT_W)
    else:
        keep = pl.BlockSpec((1, ATT_HEADS, ATT_DH, tm), lambda b, i: (b, 0, 0, 0))
        keep_shape = (B, ATT_HEADS, ATT_DH, tm)
    consts = [nw, w, wkt, wvt, cw]
    act, cst, *rest = pl.pallas_call(
        functools.partial(_inproj_kernel, keep_all=keep_all, n_tiles=nt, S=S, R=R, tps=tps),
        name="inproj",
        grid=(B, nt),
        in_specs=[row(D)] + [_const_spec(a.shape) for a in consts] + [st],
        out_specs=[row(GDN_QKV), st, row(GDN_Z), row(LANES), row(ATT_W), row(ATT_W),
                   pl.BlockSpec((1, tm // LANES, ATT_W, LANES), lambda b, i: (b, i, 0, 0)), keep, keep],
        out_shape=[jax.ShapeDtypeStruct((B, T, GDN_QKV), F32),
                   jax.ShapeDtypeStruct(grouped, F32),
                   jax.ShapeDtypeStruct((B, T, GDN_Z), F32),
                   jax.ShapeDtypeStruct((B, T, LANES), F32),
                   jax.ShapeDtypeStruct((B, T, ATT_W), BF16),
                   jax.ShapeDtypeStruct((B, T, ATT_W), BF16),
                   jax.ShapeDtypeStruct((B, T // LANES, ATT_W, LANES), BF16),
                   jax.ShapeDtypeStruct(keep_shape, F32),
                   jax.ShapeDtypeStruct(keep_shape, F32)],
        scratch_shapes=[pltpu.VMEM((hist, GDN_QKV), F32)],
        compiler_params=pltpu.CompilerParams(
            dimension_semantics=("arbitrary", "arbitrary"), vmem_limit_bytes=VMEM_LIMIT),
    )(x, *consts, prev_g)
    return (act, cst.transpose(0, 2, 1, 3).reshape(n_seq, hist, GDN_QKV), *rest)


def _bmm(a, b):
    return lax.dot_general(a, b, (((2,), (1,)), ((0,), (0,))), preferred_element_type=F32)


def _bmm_nt(a, b):
    return lax.dot_general(a, b, (((2,), (2,)), ((0,), (0,))), preferred_element_type=F32)


def _bmm_tn(a, b):
    return lax.dot_general(a, b, (((1,), (1,)), ((0,), (0,))), preferred_element_type=F32)


def _gdn_kernel(act_ref, z_ref, ba_ref, s0_ref, alog_ref, dtb_ref, gnw_ref,
                o_ref, sout_ref, s_scr, *, C, cps, bs, n_steps):
    j = pl.program_id(1)
    R = C * cps
    H = GDN_HEADS
    nh = bs * H

    @pl.when(j == 0)
    def _():
        s_scr[...] = s0_ref[...].reshape(nh, GDN_DK, GDN_DV)

    ri = lax.broadcasted_iota(jnp.int32, (C, C), 0)
    ci = lax.broadcasted_iota(jnp.int32, (C, C), 1)
    incl = ri >= ci
    strict = ri > ci
    rr = lax.broadcasted_iota(jnp.int32, (R, R), 0)
    cc = lax.broadcasted_iota(jnp.int32, (R, R), 1)
    cum16 = jnp.where((rr >= cc) & (rr // C == cc // C), 1.0, 0.0).astype(BF16)
    alog = alog_ref[...]
    dtb = dtb_ref[...]

    acts, sigs, Gs, GTs = [], [], [], []
    for s in range(bs):
        acts.append(act_ref.at[s])
        ba = ba_ref[s]
        sigs.append(_sigmoid(ba))
        G = _dot_exact_lhs(cum16, -jnp.exp(alog) * _softplus(ba + dtb))
        Gs.append(G)
        GTs.append(G.T)

    order = [(c, s, h) for c in range(cps) for s in range(bs) for h in range(H)]
    rows = lambda c: slice(c * C, (c + 1) * C)

    def tiles(slabs, col0, width):
        return jnp.stack([slabs[s][rows(c), col0 + h * width:col0 + (h + 1) * width] for c, s, h in order])

    q = tiles(acts, 0, GDN_DK)
    k = tiles(acts, GDN_QK, GDN_DK)
    v = tiles(acts, 2 * GDN_QK, GDN_DV)
    beta = tiles(sigs, 0, 1)
    Gc = tiles(Gs, H, 1)
    Gr = jnp.stack([GTs[s][H + h:H + h + 1, rows(c)] for c, s, h in order])
    Gl = Gc[:, C - 1:C, :]

    nb = len(order)
    li = lax.broadcasted_iota(jnp.int32, (2 * GDN_DK, 2 * GDN_DK), 0) // GDN_DK
    lj = lax.broadcasted_iota(jnp.int32, (2 * GDN_DK, 2 * GDN_DK), 1) // GDN_DK
    ones2 = jnp.where(li == lj, 1.0, 0.0).astype(BF16)
    sq = jnp.concatenate([q * q, k * k], axis=-1).reshape(nb * C, 2 * GDN_DK)
    sq_hi = sq.astype(BF16)
    sq_lo = (sq - sq_hi.astype(F32)).astype(BF16)
    norms = (_dot(sq_hi, ones2) + _dot(sq_lo, ones2)).reshape(nb, C, 2 * GDN_DK)
    qn = q * lax.rsqrt(norms[:, :, :GDN_DK] + EPS) * (GDN_DK ** -0.5)
    kn = k * lax.rsqrt(norms[:, :, GDN_DK:] + EPS)
    eG = jnp.exp(Gc)
    gam = jnp.where(incl, jnp.exp(jnp.where(incl, Gc - Gr, 0.0)), 0.0)
    kb = kn * beta
    kn16 = kn.astype(BF16)
    aq = _bmm_nt(jnp.concatenate([kb, qn], axis=1).astype(BF16), kn16)
    A = jnp.where(strict, aq[:, :C] * gam, 0.0)
    QK16 = (aq[:, C:] * gam).astype(BF16)

    n_joint = C.bit_length() - 2
    A16 = A.astype(BF16)
    N = -A
    Q = _bmm(A16, A16)
    for it in range(n_joint):
        Q16 = Q.astype(BF16)
        if it == n_joint - 1:
            N = N + Q + _bmm(N.astype(BF16), Q16)
        else:
            nq = _bmm(jnp.concatenate([N, Q], axis=1).astype(BF16), Q16)
            N = N + Q + nq[:, :C]
            Q = nq[:, C:]
    rhs = jnp.concatenate([v * beta, kb * eG], axis=-1)
    sol = rhs + _bmm(N.astype(BF16), rhs.astype(BF16))
    u = sol[:, :, :GDN_DV]
    wq16 = jnp.concatenate([sol[:, :, GDN_DV:], qn * eG], axis=1).astype(BF16)
    kg16 = (kn * jnp.exp(Gl - Gc)).astype(BF16)
    dl = jnp.exp(Gl)

    S = s_scr[...]
    o_parts = []
    for c in range(cps):
        sl = slice(c * nh, (c + 1) * nh)
        r = _bmm(wq16[sl], S.astype(BF16))
        vn16 = (u[sl] - r[:, :C]).astype(BF16)
        o_parts.append(r[:, C:] + _bmm(QK16[sl], vn16))
        S = S * dl[sl] + _bmm_tn(kg16[sl], vn16)
    s_scr[...] = S

    on = _rms(jnp.concatenate(o_parts, axis=0)) * gnw_ref[...]
    gates = []
    for s in range(bs):
        zs = z_ref[s]
        gates.append(_silu(zs))
    for idx, (c, s, h) in enumerate(order):
        cols = slice(h * GDN_DV, (h + 1) * GDN_DV)
        o_ref[s, rows(c), cols] = (on[idx] * gates[s][rows(c), cols]).astype(o_ref.dtype)

    @pl.when(j == n_steps - 1)
    def _():
        sout_ref[...] = S.reshape(bs, H, GDN_DK, GDN_DV)


def _gdn(act, z, ba, s0, alog, dtb, gnw, *, C, cps, bs):
    B, T, _ = act.shape
    R = C * cps
    assert B % bs == 0 and T % R == 0
    n_steps = T // R
    row = lambda w: pl.BlockSpec((bs, R, w), lambda b, j: (b, j, 0))
    st = pl.BlockSpec((bs, GDN_HEADS, GDN_DK, GDN_DV), lambda b, j: (b, 0, 0, 0))
    return pl.pallas_call(
        functools.partial(_gdn_kernel, C=C, cps=cps, bs=bs, n_steps=n_steps),
        name="gdn",
        grid=(B // bs, n_steps),
        in_specs=[row(GDN_QKV), row(GDN_Z), row(LANES), st,
                  _const_spec(alog.shape), _const_spec(dtb.shape), _const_spec(gnw.shape)],
        out_specs=[row(GDN_Z), st],
        out_shape=[jax.ShapeDtypeStruct((B, T, GDN_Z), BF16),
                   jax.ShapeDtypeStruct((B, GDN_HEADS, GDN_DK, GDN_DV), F32)],
        scratch_shapes=[pltpu.VMEM((bs * GDN_HEADS, GDN_DK, GDN_DV), F32)],
        compiler_params=pltpu.CompilerParams(
            dimension_semantics=("arbitrary", "arbitrary"), vmem_limit_bytes=VMEM_LIMIT),
    )(act, z, ba, s0, alog, dtb, gnw)


def _attn_prompt_kernel(q_ref, k_ref, vt_ref, bias_ref, nw_ref, o_ref, *, qt):
    j = pl.program_id(1)
    tq = 2 * CHUNK
    n_kb = BAND_CHUNKS // 2 + 1
    n_pairs = ATT_HEADS // 2
    lane_head = lax.broadcasted_iota(jnp.int32, (tq, LANES), 1) // ATT_DH

    def tile(i, masked):
        m = j * qt + i
        q = q_ref[0, i * tq:(i + 1) * tq, :]
        wt = []
        for p in range(n_pairs):
            qp = q[:, p * LANES:(p + 1) * LANES]
            zero = jnp.zeros_like(qp)
            wt.append(jnp.concatenate([jnp.where(lane_head == 0, qp, zero),
                                       jnp.where(lane_head == 1, qp, zero)], axis=0))
        wt = jnp.stack(wt)
        firsts = [m - (n_kb - 1) + jb for jb in range(n_kb)]
        blks = [jnp.maximum(f, 0) for f in firsts]
        k_all = jnp.concatenate(
            [jnp.stack([k_ref[0, blk][:, p * LANES:(p + 1) * LANES] for p in range(n_pairs)]) for blk in blks],
            axis=1)
        v_all = jnp.concatenate([vt_ref[0, blk].reshape(n_pairs, LANES, tq) for blk in blks], axis=2)
        s_all = _bmm_nt(k_all, wt)
        s_parts = []
        for jb in range(n_kb):
            s = s_all[:, jb * tq:(jb + 1) * tq, :]
            if jb not in _FLAT_BIAS_BLOCKS:
                s = s + bias_ref[:, jb * tq:(jb + 1) * tq, :]
            if masked and jb < n_kb - 1:
                s = jnp.where(firsts[jb] >= 0, s, -jnp.inf)
            s_parts.append(s)
        st = jnp.concatenate(s_parts, axis=1)
        pt = jnp.exp2(st - jnp.max(st, axis=1, keepdims=True))
        inv = 1.0 / jnp.sum(pt, axis=1, keepdims=True)
        acc = _bmm(v_all, pt.astype(BF16))
        ot = jnp.concatenate([acc[:, :ATT_DH, :tq] * inv[:, :, :tq],
                              acc[:, ATT_DH:, tq:] * inv[:, :, tq:]], axis=1)
        oh = ot.reshape(ATT_HEADS, ATT_DH, tq)
        on = oh * lax.rsqrt(jnp.mean(oh * oh, axis=1, keepdims=True) + EPS) * nw_ref[...]
        on = on.reshape(n_pairs, LANES, tq)
        for p in range(n_pairs):
            o_ref[0, i * tq:(i + 1) * tq, p * LANES:(p + 1) * LANES] = on[p].T.astype(o_ref.dtype)

    near_start = j * qt < n_kb - 1

    @pl.when(near_start)
    def _():
        for i in range(qt):
            tile(i, True)

    @pl.when(jnp.logical_not(near_start))
    def _():
        for i in range(qt):
            tile(i, False)


def _attn_prompt(q, k, vt, bias, nw, *, qt):
    B, T, _ = q.shape
    tq = 2 * CHUNK
    n_steps = T // (qt * tq)
    k4 = k.reshape(B, T // tq, tq, ATT_W)
    row = pl.BlockSpec((1, qt * tq, ATT_W), lambda b, j: (b, j, 0))
    return pl.pallas_call(
        functools.partial(_attn_prompt_kernel, qt=qt),
        name="attn_prompt",
        grid=(B, n_steps),
        in_specs=[row,
                  pl.BlockSpec((1, T // tq, tq, ATT_W), lambda b, j: (b, 0, 0, 0)),
                  pl.BlockSpec((1, T // tq, ATT_W, tq), lambda b, j: (b, 0, 0, 0)),
                  _const_spec(bias.shape), _const_spec(nw.shape)],
        out_specs=row,
        out_shape=jax.ShapeDtypeStruct((B, T, ATT_W), BF16),
        compiler_params=pltpu.CompilerParams(
            dimension_semantics=("arbitrary", "arbitrary"), vmem_limit_bytes=VMEM_LIMIT),
    )(q, k4, vt, bias, nw)


def _attn_sample_kernel(q_ref, kc_ref, vc_ref, kn_ref, vn_ref, bc_ref, bn_ref, nw_ref, o_ref, *, n_seq):
    by_head = lambda a: jnp.stack([a[:, h * ATT_DH:(h + 1) * ATT_DH] for h in range(ATT_HEADS)])
    for s in range(n_seq):
        kc = kc_ref[s].astype(BF16)
        vc = vc_ref[s].astype(BF16)
        q = by_head(q_ref[s])
        kn = by_head(kn_ref[s])
        vn = by_head(vn_ref[s]).astype(BF16)
        s_c = _bmm(q, kc) + bc_ref[...]
        s_n = _bmm_nt(q, kn) + bn_ref[...]
        m = jnp.maximum(jnp.max(s_c, axis=-1, keepdims=True), jnp.max(s_n, axis=-1, keepdims=True))
        p_c = jnp.exp2(s_c - m)
        p_n = jnp.exp2(s_n - m)
        l = jnp.sum(p_c, axis=-1, keepdims=True) + jnp.sum(p_n, axis=-1, keepdims=True)
        o = (_bmm_nt(p_c.astype(BF16), vc) + _bmm(p_n.astype(BF16), vn)) * (1.0 / l)
        on = _rms(o) * nw_ref[...]
        o_ref[s] = jnp.concatenate([on[h] for h in range(ATT_HEADS)], axis=-1).astype(o_ref.dtype)


def _attn_sample(q, kc, vc, kn, vn, bias_c, bias_n, nw, *, n_seq):
    B, T, _ = q.shape
    new = pl.BlockSpec((n_seq, T, ATT_W), lambda b: (b, 0, 0))
    cache = pl.BlockSpec((n_seq,) + kc.shape[1:], lambda b: (b, 0, 0, 0))
    return pl.pallas_call(
        functools.partial(_attn_sample_kernel, n_seq=n_seq),
        name="attn_sample",
        grid=(B // n_seq,),
        in_specs=[new, cache, cache, new, new, _const_spec(bias_c.shape), _const_spec(bias_n.shape),
                  _const_spec(nw.shape)],
        out_specs=new,
        out_shape=jax.ShapeDtypeStruct((B, T, ATT_W), BF16),
        compiler_params=pltpu.CompilerParams(
            dimension_semantics=("arbitrary",), vmem_limit_bytes=VMEM_LIMIT),
    )(q, kc, vc, kn, vn, bias_c, bias_n, nw)


def _delayed(g, k, S, R, hist_row):
    rolled = pltpu.roll(g, k, axis=0)
    sub = lax.broadcasted_iota(jnp.int32, (SUBLANES, g.shape[1]), 0)
    parts = []
    for s in range(S):
        head = rolled[s * R:s * R + SUBLANES]
        for i in range(k):
            head = jnp.where(sub == i, hist_row(s, i), head)
        parts += [head, rolled[s * R + SUBLANES:(s + 1) * R]]
    return jnp.concatenate(parts, axis=0)


def _mix_ffn_kernel(x_ref, oa_ref, ob_ref, wo_ref, nmp_ref, nfp_ref, wgu_ref, cw_ref, cb_ref,
                    wd_ref, nfo_ref, prev_ref, y_ref, st_ref, carry,
                    *, S, R, tps, d_ff):
    t = pl.program_id(0)
    hist = FFN_CONV - 1
    half = oa_ref.shape[-1]
    mix = _dot(oa_ref[...], wo_ref[:half, :]) + _dot(ob_ref[...], wo_ref[half:, :])
    x1 = x_ref[...] + _rms(mix) * nmp_ref[...]
    u2 = (_rms(x1) * nfp_ref[...]).astype(BF16)

    if tps > 1:
        @pl.when(t % tps == 0)
        def _():
            for i in range(hist):
                carry[i:i + 1, :] = prev_ref[0, i]

    gu = _dot(u2, wgu_ref[...])
    g = gu[:, :d_ff]
    up = gu[:, d_ff:]
    if tps > 1:
        history = lambda s, i: carry[i:i + 1, :]
    else:
        history = lambda s, i: prev_ref[0, i, s:s + 1, :]
    cw = cw_ref[...]
    conv = cb_ref[...] + cw[hist:hist + 1] * g
    for k in range(1, FFN_CONV):
        conv = conv + cw[hist - k:hist - k + 1] * _delayed(
            g, k, S, R, lambda s, i, k=k: history(s, hist - k + i))
    for s in range(S):
        for i in range(hist):
            row = g[(s + 1) * R - hist + i:(s + 1) * R - hist + i + 1]
            if tps > 1:
                carry[i:i + 1, :] = row
            else:
                st_ref[0, i, s:s + 1, :] = row
    if tps > 1:
        for i in range(hist):
            st_ref[0, i] = carry[i:i + 1, :]
    hid = (_gelu_tanh(conv) * up).astype(BF16)
    y_ref[...] = x1 + _rms(_dot(hid, wd_ref[...])) * nfo_ref[...]


def _mix_ffn(x, oa, ob, wo, nmp, nfp, wgu, cw, cb, wd, nfo, prev, *, S, R):
    M, D = x.shape
    d_ff = wd.shape[0]
    tm = S * R
    n_seq = prev.shape[0]
    hist = FFN_CONV - 1
    tps = M // (n_seq * R)
    assert S == 1 or tps == 1
    grouped = (n_seq // S, hist, S, d_ff)
    prev_g = prev.reshape(n_seq // S, S, hist, d_ff).transpose(0, 2, 1, 3)
    row = lambda w: pl.BlockSpec((tm, w), lambda t: (t, 0))
    st = pl.BlockSpec((1,) + grouped[1:], lambda t: (t // tps, 0, 0, 0))
    consts = [wo, nmp, nfp, wgu, cw, cb, wd, nfo]
    y, st_g = pl.pallas_call(
        functools.partial(_mix_ffn_kernel, S=S, R=R, tps=tps, d_ff=d_ff),
        name="mix_ffn",
        grid=(M // tm,),
        in_specs=[row(D), row(oa.shape[1]), row(ob.shape[1])] + [_const_spec(a.shape) for a in consts] + [st],
        out_specs=[row(D), st],
        out_shape=[jax.ShapeDtypeStruct((M, D), F32), jax.ShapeDtypeStruct(grouped, F32)],
        scratch_shapes=[pltpu.VMEM((hist, d_ff), F32)],
        compiler_params=pltpu.CompilerParams(
            dimension_semantics=("arbitrary",), vmem_limit_bytes=VMEM_LIMIT),
    )(x, oa, ob, *consts, prev_g)
    return y, st_g.transpose(0, 2, 1, 3).reshape(n_seq, hist, d_ff)


def _lane_row(vals, offset):
    return jnp.zeros((1, LANES), F32).at[0, offset:offset + vals.shape[0]].set(vals.astype(F32))


def _bias_kernel(r_ref, bp_ref, bc_ref, bn_ref, *, lc, ts):
    tq = 2 * CHUNK
    key = lax.broadcasted_iota(jnp.int32, (tq, tq), 0)
    query_chunk = lax.broadcasted_iota(jnp.int32, (tq, tq), 1) // CHUNK
    for h in range(r_ref.shape[0]):
        p, e = divmod(h, 2)
        for jb in range(r_ref.shape[1]):
            row = jnp.broadcast_to(r_ref[h, jb], (tq, 2 * tq))
            blk = pltpu.roll(row, 0, 1, stride=1, stride_axis=0)[:, :tq]
            key_in_band = jb * tq + key - query_chunk * CHUNK
            valid = (key_in_band >= 0) & (key_in_band < (BAND_CHUNKS + 1) * CHUNK)
            far = r_ref[h, 0][:, 0:1]
            bp_ref[p, jb * tq:(jb + 1) * tq, e * tq:(e + 1) * tq] = jnp.where(valid, blk - far, -jnp.inf)
            by_query = blk.T
            if (jb + 1) * tq <= lc:
                bc_ref[h, :, jb * tq:(jb + 1) * tq] = by_query[:ts]
            else:
                bn_ref[h] = by_query[:ts, :ts]


def _band_biases(table, lc, ts):
    n_heads = table.shape[0]
    tq = 2 * CHUNK
    n_kb = BAND_CHUNKS // 2 + 1
    assert lc == (n_kb - 1) * tq and ts <= tq
    starts = [WINDOW - tq * jb + MAX_REL - half * tq for jb in range(n_kb) for half in (0, 1)]
    pad_l = max(0, -min(starts))
    pad_r = max(0, max(starts) + tq - table.shape[1])
    ext = jnp.pad(table, ((0, 0), (pad_l, pad_r)), mode="edge")
    rows = jnp.concatenate([ext[:, s + pad_l:s + pad_l + tq] for s in starts], axis=1)
    rows = rows.reshape(n_heads, n_kb, 1, 2 * tq)
    return pl.pallas_call(
        functools.partial(_bias_kernel, lc=lc, ts=ts),
        name="band_bias",
        out_shape=[jax.ShapeDtypeStruct((n_heads // 2, n_kb * tq, 2 * tq), F32),
                   jax.ShapeDtypeStruct((n_heads, ts, lc), F32),
                   jax.ShapeDtypeStruct((n_heads, ts, ts), F32)],
    )(rows)


def _layer(xp, xs, cache_k, cache_v, s_delta, s_qkv, s_ffn, lw):
    (norm_mix_pre, w_in, qkv_conv_w, a_log, dt_bias, gdn_norm_w, rel_bias, attn_norm_w, w_out,
     norm_mix_post, norm_ffn_pre, w_gate_up, ffn_conv_w, ffn_conv_b, w_down, norm_ffn_post) = lw
    Bp, Tp, D = xp.shape
    Bs, Ts, _ = xs.shape
    d_ff = w_down.shape[0]

    c1 = GDN_QKV + GDN_Z
    c2 = c1 + 2 * GDN_HEADS
    wt = jnp.swapaxes(w_in, 0, 1)
    w_proj = jnp.concatenate([wt[:c1], jnp.pad(wt[c1:c2], ((0, LANES - 2 * GDN_HEADS), (0, 0))),
                              wt[c2:c2 + 2 * ATT_W]], axis=0).astype(BF16)
    wkt = wt[c2 + ATT_W:c2 + 2 * ATT_W].astype(BF16)
    wvt = wt[c2 + 2 * ATT_W:].astype(BF16)
    nmix = norm_mix_pre.reshape(1, D)
    cw_qkv = jnp.pad(qkv_conv_w, ((0, SUBLANES - GDN_CONV), (0, 0)))
    alog = _lane_row(a_log, GDN_HEADS)
    dtb = _lane_row(dt_bias, GDN_HEADS)
    gnw = gdn_norm_w.reshape(1, GDN_DV)
    anw = attn_norm_w.astype(F32).reshape(1, ATT_DH)
    anw_col = jnp.broadcast_to(attn_norm_w.astype(F32)[:, None], (ATT_DH, 2 * CHUNK))
    lc = cache_k.shape[1]
    bias_p, bias_c, bias_n = _band_biases(rel_bias.astype(F32) * LOG2E, lc, Ts)
    wo = w_out.astype(BF16)
    wgu = w_gate_up.astype(BF16)
    wd = w_down.astype(BF16)
    cw_ffn = jnp.pad(ffn_conv_w, ((0, SUBLANES - FFN_CONV), (0, 0)))
    cb = ffn_conv_b.reshape(1, d_ff)
    nmp = norm_mix_post.reshape(1, D)
    nfp = norm_ffn_pre.reshape(1, D)
    nfo = norm_ffn_post.reshape(1, D)

    def group(x, keep_all, gdn_prev, gdn_s0, gdn_c, gdn_cps, gdn_bs, ffn_prev, S, R, attn):
        B, T, _ = x.shape
        xi = x if not keep_all else x.reshape(1, B * T, D)
        act, qkv_state, z, ba, q, k, vt, kf, vf = _inproj(xi, nmix, w_proj, wkt, wvt, cw_qkv, gdn_prev,
                                                          keep_all=keep_all)
        rs = lambda a: a.reshape(B, T, a.shape[-1])
        act, z, ba, q, k = map(rs, (act, z, ba, q, k))
        oa, s_new = _gdn(act, z, ba, gdn_s0, alog, dtb, gnw, C=gdn_c, cps=gdn_cps, bs=gdn_bs)
        ob = attn(q, k, vt, vf)
        y, ffn_state = _mix_ffn(x.reshape(B * T, D), oa.reshape(B * T, GDN_Z), ob.reshape(B * T, ATT_W),
                                wo, nmp, nfp, wgu, cw_ffn, cb, wd, nfo, ffn_prev,
                                S=S, R=R)
        if keep_all:
            k_rows = kf.reshape(B, T, ATT_HEADS, ATT_DH)
            v_rows = vf.reshape(B, T, ATT_HEADS, ATT_DH)
        else:
            k_rows = jnp.transpose(kf, (0, 3, 1, 2))
            v_rows = jnp.transpose(vf, (0, 3, 1, 2))
        return y.reshape(B, T, D), k_rows, v_rows, s_new, qkv_state, ffn_state

    out_p = group(
        xp, False, jnp.zeros((Bp, GDN_CONV - 1, GDN_QKV), F32),
        jnp.zeros((Bp, GDN_HEADS, GDN_DK, GDN_DV), F32), CHUNK, 4, 4,
        jnp.zeros((Bp, FFN_CONV - 1, d_ff), F32), 1, ROW_TILE,
        lambda q, k, vt, vf: _attn_prompt(q, k, vt, bias_p, anw_col, qt=4))

    kc_t = jnp.transpose(cache_k, (0, 2, 3, 1))
    vc_t = jnp.transpose(cache_v, (0, 2, 3, 1))
    out_s = group(
        xs, True, s_qkv, s_delta, Ts, 1, 4, s_ffn, ROW_TILE // Ts, Ts,
        lambda q, k, vt, vf: _attn_sample(q, kc_t, vc_t, k, vf.reshape(Bs, Ts, ATT_W), bias_c, bias_n, anw,
                                          n_seq=2))
    return out_p, out_s


def kernel(x_prompt, x_sample, cache_band_k, cache_band_v, state_delta, state_qkv_conv, state_ffn_conv, norm_mix_pre, w_in, qkv_conv_w, a_log, dt_bias, gdn_norm_w, rel_bias, attn_norm_w, w_out, norm_mix_post, norm_ffn_pre, w_gate_up, ffn_conv_w, ffn_conv_b, w_down, norm_ffn_post):
    weights = (norm_mix_pre, w_in, qkv_conv_w, a_log, dt_bias, gdn_norm_w, rel_bias, attn_norm_w, w_out,
               norm_mix_post, norm_ffn_pre, w_gate_up, ffn_conv_w, ffn_conv_b, w_down, norm_ffn_post)
    depth = w_in.shape[0]
    xp, xs = x_prompt, x_sample
    outs_p, outs_s = [], []
    for l in range(depth):
        lw = tuple(w[l] for w in weights)
        op, os_ = _layer(xp, xs, cache_band_k[l], cache_band_v[l], state_delta[l], state_qkv_conv[l],
                         state_ffn_conv[l], lw)
        xp, xs = op[0], os_[0]
        outs_p.append(op[1:])
        outs_s.append(os_[1:])
    stack = lambda outs, i: jnp.stack([o[i] for o in outs], axis=0)
    return (xp, xs) + tuple(stack(outs_p, i) for i in range(5)) + tuple(stack(outs_s, i) for i in range(5))
```

```python
import functools

import jax
import jax.numpy as jnp
from jax import lax
from jax.experimental import pallas as pl
from jax.experimental.pallas import tpu as pltpu

F32 = jnp.float32
BF16 = jnp.bfloat16

EPS = 1e-6
CHUNK = 64
GDN_HEADS = 4
GDN_DK = 128
GDN_DV = 128
GDN_CONV = 4
ATT_HEADS = 8
ATT_DH = 64
BAND_CHUNKS = 8
WINDOW = BAND_CHUNKS * CHUNK
MAX_REL = 128
FFN_CONV = 3

GDN_QK = GDN_HEADS * GDN_DK
GDN_QKV = GDN_HEADS * (2 * GDN_DK + GDN_DV)
GDN_Z = GDN_HEADS * GDN_DV
ATT_W = ATT_HEADS * ATT_DH
LANES = 128
SUBLANES = 8
VMEM_LIMIT = 56 * 1024 * 1024
ROW_TILE = 512
_FLAT_BIAS_BLOCKS = tuple(
    jb for jb in range(BAND_CHUNKS // 2 + 1)
    if jb * 2 * CHUNK >= CHUNK and (jb + 1) * 2 * CHUNK <= (BAND_CHUNKS + 1) * CHUNK
    and WINDOW - (jb + 1) * 2 * CHUNK + 1 >= MAX_REL)
LOG2E = 1.4426950408889634
Q_SCALE = ATT_DH ** -0.5 * LOG2E


def _dot(a, b):
    return jnp.dot(a, b, preferred_element_type=F32)


def _dot_nt(a, b):
    return lax.dot_general(a, b, (((1,), (1,)), ((), ())), preferred_element_type=F32)


def _split3(x):
    x1 = x.astype(BF16)
    r1 = x - x1.astype(F32)
    x2 = r1.astype(BF16)
    x3 = (r1 - x2.astype(F32)).astype(BF16)
    return x1, x2, x3


def _dot_exact_lhs(a16, x):
    x1, x2, x3 = _split3(x)
    return _dot(a16, x1) + _dot(a16, x2) + _dot(a16, x3)


def _sigmoid(x):
    return 0.5 + 0.5 * jnp.tanh(0.5 * x)


def _silu(x):
    h = 0.5 * x
    return h + h * jnp.tanh(h)


def _softplus(x):
    return jnp.maximum(x, 0.0) + jnp.log(1.0 + jnp.exp(-jnp.abs(x)))


def _rms(x):
    return x * lax.rsqrt(jnp.mean(x * x, axis=-1, keepdims=True) + EPS)


def _gelu_tanh(x):
    c = 0.7978845608028654
    h = 0.5 * x
    return h + h * jnp.tanh(x * (c + (0.044715 * c) * (x * x)))


def _const_spec(shape):
    n = len(shape)
    return pl.BlockSpec(shape, lambda *_: (0,) * n, pipeline_mode=pl.Buffered(1))


def _inproj_kernel(x_ref, nw_ref, w_ref, wkt_ref, wvt_ref, cw_ref, prev_ref,
                   act_ref, cst_ref, z_ref, ba_ref, q_ref, k_ref, vt_ref, kf_ref, vf_ref, carry, w_nn,
                   *, keep_all, n_tiles, S, R, tps):
    x = x_ref[0]
    tm = x.shape[0]
    u = (_rms(x) * nw_ref[...]).astype(BF16)

    hist = GDN_CONV - 1
    if tps > 1:
        @pl.when(pl.program_id(1) % tps == 0)
        def _():
            for r in range(hist):
                carry[r:r + 1, :] = prev_ref[0, r]
    cw = cw_ref[...]
    slabs = [slice(c * GDN_QK, (c + 1) * GDN_QK) for c in range(GDN_QKV // GDN_QK)]

    def conv_silu(pre, cols):
        if tps > 1:
            history = lambda s, r, cols=cols: carry[r:r + 1, cols]
        else:
            history = lambda s, r, cols=cols: prev_ref[0, r, s:s + 1, cols]
        conv = cw[hist:hist + 1, cols] * pre
        for k in range(1, GDN_CONV):
            conv = conv + cw[hist - k:hist - k + 1, cols] * _delayed(
                pre, k, S, R, lambda s, r, k=k, history=history: history(s, hist - k + r))
        act_ref[0, :, cols] = _silu(conv)
        for s in range(S):
            for r in range(hist):
                row = pre[(s + 1) * R - hist + r:(s + 1) * R - hist + r + 1]
                if tps > 1:
                    carry[r:r + 1, cols] = row
                else:
                    cst_ref[0, r, s:s + 1, cols] = row

    @pl.when((pl.program_id(0) == 0) & (pl.program_id(1) == 0))
    def _():
        for c in range(w_ref.shape[0] // LANES):
            w_nn[:, c * LANES:(c + 1) * LANES] = w_ref[c * LANES:(c + 1) * LANES, :].T

    p = _dot(u, w_nn[...])
    for cols in slabs:
        conv_silu(p[:, cols], cols)
    if tps > 1:
        for r in range(hist):
            cst_ref[0, r] = carry[r:r + 1, :]
    c0 = GDN_QKV + GDN_Z
    c1 = c0 + LANES
    z_ref[0] = p[:, GDN_QKV:c0]
    ba_ref[0] = p[:, c0:c1]
    q_ref[0] = (p[:, c1:c1 + ATT_W] * Q_SCALE).astype(BF16)
    k = p[:, c1 + ATT_W:]
    k_ref[0] = k.astype(BF16)
    vt = _dot_nt(wvt_ref[...], u)
    for jb in range(tm // LANES):
        vt_ref[0, jb] = vt[:, jb * LANES:(jb + 1) * LANES].astype(BF16)

    if keep_all:
        kf_ref[0] = k
        vf_ref[0] = _dot_nt(u, wvt_ref[...])
    else:
        @pl.when(pl.program_id(1) == n_tiles - 1)
        def _():
            kf_ref[0] = _dot_nt(wkt_ref[...], u).reshape(ATT_HEADS, ATT_DH, tm)
            vf_ref[0] = vt.reshape(ATT_HEADS, ATT_DH, tm)


def _inproj(x, nw, w, wkt, wvt, cw, prev, *, keep_all):
    B, T, D = x.shape
    tm = ROW_TILE
    nt = T // tm
    n_seq = prev.shape[0]
    hist = GDN_CONV - 1
    R = min(B * T // n_seq, tm)
    S = tm // R
    tps = B * T // (n_seq * R)
    assert S == 1 or tps == 1
    grouped = (n_seq // S, hist, S, GDN_QKV)
    prev_g = prev.reshape(n_seq // S, S, hist, GDN_QKV).transpose(0, 2, 1, 3)
    groups_per_b = nt // tps
    st = pl.BlockSpec((1,) + grouped[1:], lambda b, i: (b * groups_per_b + i // tps, 0, 0, 0))
    row = lambda w: pl.BlockSpec((1, tm, w), lambda b, i: (b, i, 0))
    if keep_all:
        keep = row(ATT_W)
        keep_shape = (B, T, ATT_W)
    else:
        keep = pl.BlockSpec((1, ATT_HEADS, ATT_DH, tm), lambda b, i: (b, 0, 0, 0))
        keep_shape = (B, ATT_HEADS, ATT_DH, tm)
    consts = [nw, w, wkt, wvt, cw]
    act, cst, *rest = pl.pallas_call(
        functools.partial(_inproj_kernel, keep_all=keep_all, n_tiles=nt, S=S, R=R, tps=tps),
        name="inproj",
        grid=(B, nt),
        in_specs=[row(D)] + [_const_spec(a.shape) for a in consts] + [st],
        out_specs=[row(GDN_QKV), st, row(GDN_Z), row(LANES), row(ATT_W), row(ATT_W),
                   pl.BlockSpec((1, tm // LANES, ATT_W, LANES), lambda b, i: (b, i, 0, 0)), keep, keep],
        out_shape=[jax.ShapeDtypeStruct((B, T, GDN_QKV), F32),
                   jax.ShapeDtypeStruct(grouped, F32),
                   jax.ShapeDtypeStruct((B, T, GDN_Z), F32),
                   jax.ShapeDtypeStruct((B, T, LANES), F32),
                   jax.ShapeDtypeStruct((B, T, ATT_W), BF16),
                   jax.ShapeDtypeStruct((B, T, ATT_W), BF16),
                   jax.ShapeDtypeStruct((B, T // LANES, ATT_W, LANES), BF16),
                   jax.ShapeDtypeStruct(keep_shape, F32),
                   jax.ShapeDtypeStruct(keep_shape, F32)],
        scratch_shapes=[pltpu.VMEM((hist, GDN_QKV), F32), pltpu.VMEM(w.shape[::-1], BF16)],
        compiler_params=pltpu.CompilerParams(
            dimension_semantics=("arbitrary", "arbitrary"), vmem_limit_bytes=VMEM_LIMIT),
    )(x, *consts, prev_g)
    return (act, cst.transpose(0, 2, 1, 3).reshape(n_seq, hist, GDN_QKV), *rest)


def _bmm(a, b):
    return lax.dot_general(a, b, (((2,), (1,)), ((0,), (0,))), preferred_element_type=F32)


def _bmm_nt(a, b):
    return lax.dot_general(a, b, (((2,), (2,)), ((0,), (0,))), preferred_element_type=F32)


def _bmm_tn(a, b):
    return lax.dot_general(a, b, (((1,), (1,)), ((0,), (0,))), preferred_element_type=F32)


def _gdn_kernel(act_ref, z_ref, ba_ref, s0_ref, alog_ref, dtb_ref, gnw_ref,
                o_ref, sout_ref, s_scr, *, C, cps, bs, n_steps):
    j = pl.program_id(1)
    R = C * cps
    H = GDN_HEADS
    nh = bs * H

    @pl.when(j == 0)
    def _():
        s_scr[...] = s0_ref[...].reshape(nh, GDN_DK, GDN_DV)

    ri = lax.broadcasted_iota(jnp.int32, (C, C), 0)
    ci = lax.broadcasted_iota(jnp.int32, (C, C), 1)
    incl = ri >= ci
    strict = ri > ci
    rr = lax.broadcasted_iota(jnp.int32, (R, R), 0)
    cc = lax.broadcasted_iota(jnp.int32, (R, R), 1)
    cum16 = jnp.where((rr >= cc) & (rr // C == cc // C), 1.0, 0.0).astype(BF16)
    alog = alog_ref[...]
    dtb = dtb_ref[...]

    acts, sigs, Gs, GTs = [], [], [], []
    for s in range(bs):
        acts.append(act_ref.at[s])
        ba = ba_ref[s]
        sigs.append(_sigmoid(ba))
        G = _dot_exact_lhs(cum16, -jnp.exp(alog) * _softplus(ba + dtb))
        Gs.append(G)
        GTs.append(G.T)

    order = [(c, s, h) for c in range(cps) for s in range(bs) for h in range(H)]
    rows = lambda c: slice(c * C, (c + 1) * C)

    def tiles(slabs, col0, width):
        return jnp.stack([slabs[s][rows(c), col0 + h * width:col0 + (h + 1) * width] for c, s, h in order])

    q = tiles(acts, 0, GDN_DK)
    k = tiles(acts, GDN_QK, GDN_DK)
    v = tiles(acts, 2 * GDN_QK, GDN_DV)
    beta = tiles(sigs, 0, 1)
    Gc = tiles(Gs, H, 1)
    Gr = jnp.stack([GTs[s][H + h:H + h + 1, rows(c)] for c, s, h in order])
    Gl = Gc[:, C - 1:C, :]

    nb = len(order)
    li = lax.broadcasted_iota(jnp.int32, (2 * GDN_DK, 2 * GDN_DK), 0) // GDN_DK
    lj = lax.broadcasted_iota(jnp.int32, (2 * GDN_DK, 2 * GDN_DK), 1) // GDN_DK
    ones2 = jnp.where(li == lj, 1.0, 0.0).astype(BF16)
    sq = jnp.concatenate([q * q, k * k], axis=-1).reshape(nb * C, 2 * GDN_DK)
    sq_hi = sq.astype(BF16)
    sq_lo = (sq - sq_hi.astype(F32)).astype(BF16)
    norms = (_dot(sq_hi, ones2) + _dot(sq_lo, ones2)).reshape(nb, C, 2 * GDN_DK)
    qn = q * lax.rsqrt(norms[:, :, :GDN_DK] + EPS) * (GDN_DK ** -0.5)
    kn = k * lax.rsqrt(norms[:, :, GDN_DK:] + EPS)
    eG = jnp.exp(Gc)
    gam = jnp.where(incl, jnp.exp(jnp.where(incl, Gc - Gr, 0.0)), 0.0)
    kb = kn * beta
    kn16 = kn.astype(BF16)
    aq = _bmm_nt(jnp.concatenate([kb, qn], axis=1).astype(BF16), kn16)
    A = jnp.where(strict, aq[:, :C] * gam, 0.0)
    QK16 = (aq[:, C:] * gam).astype(BF16)

    n_joint = C.bit_length() - 2
    A16 = A.astype(BF16)
    N = -A
    Q = _bmm(A16, A16)
    for it in range(n_joint):
        Q16 = Q.astype(BF16)
        if it == n_joint - 1:
            N = N + Q + _bmm(N.astype(BF16), Q16)
        else:
            nq = _bmm(jnp.concatenate([N, Q], axis=1).astype(BF16), Q16)
            N = N + Q + nq[:, :C]
            Q = nq[:, C:]
    rhs = jnp.concatenate([v * beta, kb * eG], axis=-1)
    sol = rhs + _bmm(N.astype(BF16), rhs.astype(BF16))
    u = sol[:, :, :GDN_DV]
    wq16 = jnp.concatenate([sol[:, :, GDN_DV:], qn * eG], axis=1).astype(BF16)
    kg16 = (kn * jnp.exp(Gl - Gc)).astype(BF16)
    dl = jnp.exp(Gl)

    S = s_scr[...]
    o_parts = []
    for c in range(cps):
        sl = slice(c * nh, (c + 1) * nh)
        r = _bmm(wq16[sl], S.astype(BF16))
        vn16 = (u[sl] - r[:, :C]).astype(BF16)
        o_parts.append(r[:, C:] + _bmm(QK16[sl], vn16))
        S = S * dl[sl] + _bmm_tn(kg16[sl], vn16)
    s_scr[...] = S

    on = _rms(jnp.concatenate(o_parts, axis=0)) * gnw_ref[...]
    gates = []
    for s in range(bs):
        zs = z_ref[s]
        gates.append(_silu(zs))
    for idx, (c, s, h) in enumerate(order):
        cols = slice(h * GDN_DV, (h + 1) * GDN_DV)
        o_ref[s, rows(c), cols] = (on[idx] * gates[s][rows(c), cols]).astype(o_ref.dtype)

    @pl.when(j == n_steps - 1)
    def _():
        sout_ref[...] = S.reshape(bs, H, GDN_DK, GDN_DV)


def _gdn(act, z, ba, s0, alog, dtb, gnw, *, C, cps, bs):
    B, T, _ = act.shape
    R = C * cps
    assert B % bs == 0 and T % R == 0
    n_steps = T // R
    row = lambda w: pl.BlockSpec((bs, R, w), lambda b, j: (b, j, 0))
    st = pl.BlockSpec((bs, GDN_HEADS, GDN_DK, GDN_DV), lambda b, j: (b, 0, 0, 0))
    return pl.pallas_call(
        functools.partial(_gdn_kernel, C=C, cps=cps, bs=bs, n_steps=n_steps),
        name="gdn",
        grid=(B // bs, n_steps),
        in_specs=[row(GDN_QKV), row(GDN_Z), row(LANES), st,
                  _const_spec(alog.shape), _const_spec(dtb.shape), _const_spec(gnw.shape)],
        out_specs=[row(GDN_Z), st],
        out_shape=[jax.ShapeDtypeStruct((B, T, GDN_Z), BF16),
                   jax.ShapeDtypeStruct((B, GDN_HEADS, GDN_DK, GDN_DV), F32)],
        scratch_shapes=[pltpu.VMEM((bs * GDN_HEADS, GDN_DK, GDN_DV), F32)],
        compiler_params=pltpu.CompilerParams(
            dimension_semantics=("arbitrary", "arbitrary"), vmem_limit_bytes=VMEM_LIMIT),
    )(act, z, ba, s0, alog, dtb, gnw)


def _attn_prompt_kernel(q_ref, k_ref, vt_ref, bias_ref, nw_ref, o_ref, *, qt):
    j = pl.program_id(1)
    tq = 2 * CHUNK
    n_kb = BAND_CHUNKS // 2 + 1
    n_pairs = ATT_HEADS // 2
    lane_head = lax.broadcasted_iota(jnp.int32, (tq, LANES), 1) // ATT_DH

    def tile(i, masked):
        m = j * qt + i
        q = q_ref[0, i * tq:(i + 1) * tq, :]
        wt = []
        for p in range(n_pairs):
            qp = q[:, p * LANES:(p + 1) * LANES]
            zero = jnp.zeros_like(qp)
            wt.append(jnp.concatenate([jnp.where(lane_head == 0, qp, zero),
                                       jnp.where(lane_head == 1, qp, zero)], axis=0))
        wt = jnp.stack(wt)
        firsts = [m - (n_kb - 1) + jb for jb in range(n_kb)]
        blks = [jnp.maximum(f, 0) for f in firsts]
        k_all = jnp.concatenate(
            [jnp.stack([k_ref[0, blk][:, p * LANES:(p + 1) * LANES] for p in range(n_pairs)]) for blk in blks],
            axis=1)
        v_all = jnp.concatenate([vt_ref[0, blk].reshape(n_pairs, LANES, tq) for blk in blks], axis=2)
        s_all = _bmm_nt(k_all, wt)
        s_parts = []
        for jb in range(n_kb):
            s = s_all[:, jb * tq:(jb + 1) * tq, :]
            if jb not in _FLAT_BIAS_BLOCKS:
                s = s + bias_ref[:, jb * tq:(jb + 1) * tq, :]
            if masked and jb < n_kb - 1:
                s = jnp.where(firsts[jb] >= 0, s, -jnp.inf)
            s_parts.append(s)
        st = jnp.concatenate(s_parts, axis=1)
        pt = jnp.exp2(st - jnp.max(st, axis=1, keepdims=True))
        inv = 1.0 / jnp.sum(pt, axis=1, keepdims=True)
        acc = _bmm(v_all, pt.astype(BF16))
        ot = jnp.concatenate([acc[:, :ATT_DH, :tq] * inv[:, :, :tq],
                              acc[:, ATT_DH:, tq:] * inv[:, :, tq:]], axis=1)
        oh = ot.reshape(ATT_HEADS, ATT_DH, tq)
        on = oh * lax.rsqrt(jnp.mean(oh * oh, axis=1, keepdims=True) + EPS) * nw_ref[...]
        on = on.reshape(n_pairs, LANES, tq)
        for p in range(n_pairs):
            o_ref[0, i * tq:(i + 1) * tq, p * LANES:(p + 1) * LANES] = on[p].T.astype(o_ref.dtype)

    near_start = j * qt < n_kb - 1

    @pl.when(near_start)
    def _():
        for i in range(qt):
            tile(i, True)

    @pl.when(jnp.logical_not(near_start))
    def _():
        for i in range(qt):
            tile(i, False)


def _attn_prompt(q, k, vt, bias, nw, *, qt):
    B, T, _ = q.shape
    tq = 2 * CHUNK
    n_steps = T // (qt * tq)
    k4 = k.reshape(B, T // tq, tq, ATT_W)
    row = pl.BlockSpec((1, qt * tq, ATT_W), lambda b, j: (b, j, 0))
    return pl.pallas_call(
        functools.partial(_attn_prompt_kernel, qt=qt),
        name="attn_prompt",
        grid=(B, n_steps),
        in_specs=[row,
                  pl.BlockSpec((1, T // tq, tq, ATT_W), lambda b, j: (b, 0, 0, 0)),
                  pl.BlockSpec((1, T // tq, ATT_W, tq), lambda b, j: (b, 0, 0, 0)),
                  _const_spec(bias.shape), _const_spec(nw.shape)],
        out_specs=row,
        out_shape=jax.ShapeDtypeStruct((B, T, ATT_W), BF16),
        compiler_params=pltpu.CompilerParams(
            dimension_semantics=("arbitrary", "arbitrary"), vmem_limit_bytes=VMEM_LIMIT),
    )(q, k4, vt, bias, nw)


def _attn_sample_kernel(q_ref, kc_ref, vc_ref, kn_ref, vn_ref, bc_ref, bn_ref, nw_ref, o_ref, *, n_seq):
    by_head = lambda a: jnp.stack([a[:, h * ATT_DH:(h + 1) * ATT_DH] for h in range(ATT_HEADS)])
    for s in range(n_seq):
        kc = kc_ref[s].astype(BF16)
        vc = vc_ref[s].astype(BF16)
        q = by_head(q_ref[s])
        kn = by_head(kn_ref[s])
        vn = by_head(vn_ref[s]).astype(BF16)
        s_c = _bmm(q, kc) + bc_ref[...]
        s_n = _bmm_nt(q, kn) + bn_ref[...]
        m = jnp.maximum(jnp.max(s_c, axis=-1, keepdims=True), jnp.max(s_n, axis=-1, keepdims=True))
        p_c = jnp.exp2(s_c - m)
        p_n = jnp.exp2(s_n - m)
        l = jnp.sum(p_c, axis=-1, keepdims=True) + jnp.sum(p_n, axis=-1, keepdims=True)
        o = (_bmm_nt(p_c.astype(BF16), vc) + _bmm(p_n.astype(BF16), vn)) * (1.0 / l)
        on = _rms(o) * nw_ref[...]
        o_ref[s] = jnp.concatenate([on[h] for h in range(ATT_HEADS)], axis=-1).astype(o_ref.dtype)


def _attn_sample(q, kc, vc, kn, vn, bias_c, bias_n, nw, *, n_seq):
    B, T, _ = q.shape
    new = pl.BlockSpec((n_seq, T, ATT_W), lambda b: (b, 0, 0))
    cache = pl.BlockSpec((n_seq,) + kc.shape[1:], lambda b: (b, 0, 0, 0))
    return pl.pallas_call(
        functools.partial(_attn_sample_kernel, n_seq=n_seq),
        name="attn_sample",
        grid=(B // n_seq,),
        in_specs=[new, cache, cache, new, new, _const_spec(bias_c.shape), _const_spec(bias_n.shape),
                  _const_spec(nw.shape)],
        out_specs=new,
        out_shape=jax.ShapeDtypeStruct((B, T, ATT_W), BF16),
        compiler_params=pltpu.CompilerParams(
            dimension_semantics=("arbitrary",), vmem_limit_bytes=VMEM_LIMIT),
    )(q, kc, vc, kn, vn, bias_c, bias_n, nw)


def _delayed(g, k, S, R, hist_row):
    rolled = pltpu.roll(g, k, axis=0)
    sub = lax.broadcasted_iota(jnp.int32, (SUBLANES, g.shape[1]), 0)
    parts = []
    for s in range(S):
        head = rolled[s * R:s * R + SUBLANES]
        for i in range(k):
            head = jnp.where(sub == i, hist_row(s, i), head)
        parts += [head, rolled[s * R + SUBLANES:(s + 1) * R]]
    return jnp.concatenate(parts, axis=0)


def _mix_ffn_kernel(x_ref, oa_ref, ob_ref, wo_ref, nmp_ref, nfp_ref, wgu_ref, cw_ref, cb_ref,
                    wd_ref, nfo_ref, prev_ref, y_ref, st_ref, carry,
                    *, S, R, tps, d_ff):
    t = pl.program_id(0)
    hist = FFN_CONV - 1
    half = oa_ref.shape[-1]
    mix = _dot(oa_ref[...], wo_ref[:half, :]) + _dot(ob_ref[...], wo_ref[half:, :])
    x1 = x_ref[...] + _rms(mix) * nmp_ref[...]
    u2 = (_rms(x1) * nfp_ref[...]).astype(BF16)

    if tps > 1:
        @pl.when(t % tps == 0)
        def _():
            for i in range(hist):
                carry[i:i + 1, :] = prev_ref[0, i]

    gu = _dot(u2, wgu_ref[...])
    g = gu[:, :d_ff]
    up = gu[:, d_ff:]
    if tps > 1:
        history = lambda s, i: carry[i:i + 1, :]
    else:
        history = lambda s, i: prev_ref[0, i, s:s + 1, :]
    cw = cw_ref[...]
    conv = cb_ref[...] + cw[hist:hist + 1] * g
    for k in range(1, FFN_CONV):
        conv = conv + cw[hist - k:hist - k + 1] * _delayed(
            g, k, S, R, lambda s, i, k=k: history(s, hist - k + i))
    for s in range(S):
        for i in range(hist):
            row = g[(s + 1) * R - hist + i:(s + 1) * R - hist + i + 1]
            if tps > 1:
                carry[i:i + 1, :] = row
            else:
                st_ref[0, i, s:s + 1, :] = row
    if tps > 1:
        for i in range(hist):
            st_ref[0, i] = carry[i:i + 1, :]
    hid = (_gelu_tanh(conv) * up).astype(BF16)
    y_ref[...] = x1 + _rms(_dot(hid, wd_ref[...])) * nfo_ref[...]


def _mix_ffn(x, oa, ob, wo, nmp, nfp, wgu, cw, cb, wd, nfo, prev, *, S, R):
    M, D = x.shape
    d_ff = wd.shape[0]
    tm = S * R
    n_seq = prev.shape[0]
    hist = FFN_CONV - 1
    tps = M // (n_seq * R)
    assert S == 1 or tps == 1
    grouped = (n_seq // S, hist, S, d_ff)
    prev_g = prev.reshape(n_seq // S, S, hist, d_ff).transpose(0, 2, 1, 3)
    row = lambda w: pl.BlockSpec((tm, w), lambda t: (t, 0))
    st = pl.BlockSpec((1,) + grouped[1:], lambda t: (t // tps, 0, 0, 0))
    consts = [wo, nmp, nfp, wgu, cw, cb, wd, nfo]
    y, st_g = pl.pallas_call(
        functools.partial(_mix_ffn_kernel, S=S, R=R, tps=tps, d_ff=d_ff),
        name="mix_ffn",
        grid=(M // tm,),
        in_specs=[row(D), row(oa.shape[1]), row(ob.shape[1])] + [_const_spec(a.shape) for a in consts] + [st],
        out_specs=[row(D), st],
        out_shape=[jax.ShapeDtypeStruct((M, D), F32), jax.ShapeDtypeStruct(grouped, F32)],
        scratch_shapes=[pltpu.VMEM((hist, d_ff), F32)],
        compiler_params=pltpu.CompilerParams(
            dimension_semantics=("arbitrary",), vmem_limit_bytes=VMEM_LIMIT),
    )(x, oa, ob, *consts, prev_g)
    return y, st_g.transpose(0, 2, 1, 3).reshape(n_seq, hist, d_ff)


def _lane_row(vals, offset):
    return jnp.zeros((1, LANES), F32).at[0, offset:offset + vals.shape[0]].set(vals.astype(F32))


def _bias_kernel(r_ref, bp_ref, bc_ref, bn_ref, *, lc, ts):
    tq = 2 * CHUNK
    key = lax.broadcasted_iota(jnp.int32, (tq, tq), 0)
    query_chunk = lax.broadcasted_iota(jnp.int32, (tq, tq), 1) // CHUNK
    for h in range(r_ref.shape[0]):
        p, e = divmod(h, 2)
        for jb in range(r_ref.shape[1]):
            row = jnp.broadcast_to(r_ref[h, jb], (tq, 2 * tq))
            blk = pltpu.roll(row, 0, 1, stride=1, stride_axis=0)[:, :tq]
            key_in_band = jb * tq + key - query_chunk * CHUNK
            valid = (key_in_band >= 0) & (key_in_band < (BAND_CHUNKS + 1) * CHUNK)
            far = r_ref[h, 0][:, 0:1]
            bp_ref[p, jb * tq:(jb + 1) * tq, e * tq:(e + 1) * tq] = jnp.where(valid, blk - far, -jnp.inf)
            by_query = blk.T
            if (jb + 1) * tq <= lc:
                bc_ref[h, :, jb * tq:(jb + 1) * tq] = by_query[:ts]
            else:
                bn_ref[h] = by_query[:ts, :ts]


def _band_biases(table, lc, ts):
    n_heads = table.shape[0]
    tq = 2 * CHUNK
    n_kb = BAND_CHUNKS // 2 + 1
    assert lc == (n_kb - 1) * tq and ts <= tq
    starts = [WINDOW - tq * jb + MAX_REL - half * tq for jb in range(n_kb) for half in (0, 1)]
    pad_l = max(0, -min(starts))
    pad_r = max(0, max(starts) + tq - table.shape[1])
    ext = jnp.pad(table, ((0, 0), (pad_l, pad_r)), mode="edge")
    rows = jnp.concatenate([ext[:, s + pad_l:s + pad_l + tq] for s in starts], axis=1)
    rows = rows.reshape(n_heads, n_kb, 1, 2 * tq)
    return pl.pallas_call(
        functools.partial(_bias_kernel, lc=lc, ts=ts),
        name="band_bias",
        out_shape=[jax.ShapeDtypeStruct((n_heads // 2, n_kb * tq, 2 * tq), F32),
                   jax.ShapeDtypeStruct((n_heads, ts, lc), F32),
                   jax.ShapeDtypeStruct((n_heads, ts, ts), F32)],
    )(rows)


def _layer(xp, xs, cache_k, cache_v, s_delta, s_qkv, s_ffn, lw):
    (norm_mix_pre, w_in, qkv_conv_w, a_log, dt_bias, gdn_norm_w, rel_bias, attn_norm_w, w_out,
     norm_mix_post, norm_ffn_pre, w_gate_up, ffn_conv_w, ffn_conv_b, w_down, norm_ffn_post) = lw
    Bp, Tp, D = xp.shape
    Bs, Ts, _ = xs.shape
    d_ff = w_down.shape[0]

    c1 = GDN_QKV + GDN_Z
    c2 = c1 + 2 * GDN_HEADS
    wt = jnp.swapaxes(w_in, 0, 1)
    w_proj = jnp.concatenate([wt[:c1], jnp.pad(wt[c1:c2], ((0, LANES - 2 * GDN_HEADS), (0, 0))),
                              wt[c2:c2 + 2 * ATT_W]], axis=0).astype(BF16)
    wkt = wt[c2 + ATT_W:c2 + 2 * ATT_W].astype(BF16)
    wvt = wt[c2 + 2 * ATT_W:].astype(BF16)
    nmix = norm_mix_pre.reshape(1, D)
    cw_qkv = jnp.pad(qkv_conv_w, ((0, SUBLANES - GDN_CONV), (0, 0)))
    alog = _lane_row(a_log, GDN_HEADS)
    dtb = _lane_row(dt_bias, GDN_HEADS)
    gnw = gdn_norm_w.reshape(1, GDN_DV)
    anw = attn_norm_w.astype(F32).reshape(1, ATT_DH)
    anw_col = jnp.broadcast_to(attn_norm_w.astype(F32)[:, None], (ATT_DH, 2 * CHUNK))
    lc = cache_k.shape[1]
    bias_p, bias_c, bias_n = _band_biases(rel_bias.astype(F32) * LOG2E, lc, Ts)
    wo = w_out.astype(BF16)
    wgu = w_gate_up.astype(BF16)
    wd = w_down.astype(BF16)
    cw_ffn = jnp.pad(ffn_conv_w, ((0, SUBLANES - FFN_CONV), (0, 0)))
    cb = ffn_conv_b.reshape(1, d_ff)
    nmp = norm_mix_post.reshape(1, D)
    nfp = norm_ffn_pre.reshape(1, D)
    nfo = norm_ffn_post.reshape(1, D)

    def group(x, keep_all, gdn_prev, gdn_s0, gdn_c, gdn_cps, gdn_bs, ffn_prev, S, R, attn):
        B, T, _ = x.shape
        xi = x if not keep_all else x.reshape(1, B * T, D)
        act, qkv_state, z, ba, q, k, vt, kf, vf = _inproj(xi, nmix, w_proj, wkt, wvt, cw_qkv, gdn_prev,
                                                          keep_all=keep_all)
        rs = lambda a: a.reshape(B, T, a.shape[-1])
        act, z, ba, q, k = map(rs, (act, z, ba, q, k))
        oa, s_new = _gdn(act, z, ba, gdn_s0, alog, dtb, gnw, C=gdn_c, cps=gdn_cps, bs=gdn_bs)
        ob = attn(q, k, vt, vf)
        y, ffn_state = _mix_ffn(x.reshape(B * T, D), oa.reshape(B * T, GDN_Z), ob.reshape(B * T, ATT_W),
                                wo, nmp, nfp, wgu, cw_ffn, cb, wd, nfo, ffn_prev,
                                S=S, R=R)
        if keep_all:
            k_rows = kf.reshape(B, T, ATT_HEADS, ATT_DH)
            v_rows = vf.reshape(B, T, ATT_HEADS, ATT_DH)
        else:
            k_rows = jnp.transpose(kf, (0, 3, 1, 2))
            v_rows = jnp.transpose(vf, (0, 3, 1, 2))
        return y.reshape(B, T, D), k_rows, v_rows, s_new, qkv_state, ffn_state

    out_p = group(
        xp, False, jnp.zeros((Bp, GDN_CONV - 1, GDN_QKV), F32),
        jnp.zeros((Bp, GDN_HEADS, GDN_DK, GDN_DV), F32), CHUNK, 4, 4,
        jnp.zeros((Bp, FFN_CONV - 1, d_ff), F32), 1, ROW_TILE,
        lambda q, k, vt, vf: _attn_prompt(q, k, vt, bias_p, anw_col, qt=4))

    kc_t = jnp.transpose(cache_k, (0, 2, 3, 1))
    vc_t = jnp.transpose(cache_v, (0, 2, 3, 1))
    out_s = group(
        xs, True, s_qkv, s_delta, Ts, 1, 4, s_ffn, ROW_TILE // Ts, Ts,
        lambda q, k, vt, vf: _attn_sample(q, kc_t, vc_t, k, vf.reshape(Bs, Ts, ATT_W), bias_c, bias_n, anw,
                                          n_seq=2))
    return out_p, out_s


def kernel(x_prompt, x_sample, cache_band_k, cache_band_v, state_delta, state_qkv_conv, state_ffn_conv, norm_mix_pre, w_in, qkv_conv_w, a_log, dt_bias, gdn_norm_w, rel_bias, attn_norm_w, w_out, norm_mix_post, norm_ffn_pre, w_gate_up, ffn_conv_w, ffn_conv_b, w_down, norm_ffn_post):
    weights = (norm_mix_pre, w_in, qkv_conv_w, a_log, dt_bias, gdn_norm_w, rel_bias, attn_norm_w, w_out,
               norm_mix_post, norm_ffn_pre, w_gate_up, ffn_conv_w, ffn_conv_b, w_down, norm_ffn_post)
    depth = w_in.shape[0]
    xp, xs = x_prompt, x_sample
    outs_p, outs_s = [], []
    for l in range(depth):
        lw = tuple(w[l] for w in weights)
        op, os_ = _layer(xp, xs, cache_band_k[l], cache_band_v[l], state_delta[l], state_qkv_conv[l],
                         state_ffn_conv[l], lw)
        xp, xs = op[0], os_[0]
        outs_p.append(op[1:])
        outs_s.append(os_[1:])
    stack = lambda outs, i: jnp.stack([o[i] for o in outs], axis=0)
    return (xp, xs) + tuple(stack(outs_p, i) for i in range(5)) + tuple(stack(outs_s, i) for i in range(5))
```

```python
import functools

import jax
import jax.numpy as jnp
from jax import lax
from jax.experimental import pallas as pl
from jax.experimental.pallas import tpu as pltpu

F32 = jnp.float32
BF16 = jnp.bfloat16

EPS = 1e-6
CHUNK = 64
GDN_HEADS = 4
GDN_DK = 128
GDN_DV = 128
GDN_CONV = 4
ATT_HEADS = 8
ATT_DH = 64
BAND_CHUNKS = 8
WINDOW = BAND_CHUNKS * CHUNK
MAX_REL = 128
FFN_CONV = 3

GDN_QK = GDN_HEADS * GDN_DK
GDN_QKV = GDN_HEADS * (2 * GDN_DK + GDN_DV)
GDN_Z = GDN_HEADS * GDN_DV
ATT_W = ATT_HEADS * ATT_DH
LANES = 128
SUBLANES = 8
VMEM_LIMIT = 56 * 1024 * 1024
ROW_TILE = 512
_FLAT_BIAS_BLOCKS = tuple(
    jb for jb in range(BAND_CHUNKS // 2 + 1)
    if jb * 2 * CHUNK >= CHUNK and (jb + 1) * 2 * CHUNK <= (BAND_CHUNKS + 1) * CHUNK
    and WINDOW - (jb + 1) * 2 * CHUNK + 1 >= MAX_REL)
LOG2E = 1.4426950408889634
Q_SCALE = ATT_DH ** -0.5 * LOG2E


def _dot(a, b):
    return jnp.dot(a, b, preferred_element_type=F32)


def _dot_nt(a, b):
    return lax.dot_general(a, b, (((1,), (1,)), ((), ())), preferred_element_type=F32)


def _split3(x):
    x1 = x.astype(BF16)
    r1 = x - x1.astype(F32)
    x2 = r1.astype(BF16)
    x3 = (r1 - x2.astype(F32)).astype(BF16)
    return x1, x2, x3


def _dot_exact_lhs(a16, x):
    x1, x2, x3 = _split3(x)
    return _dot(a16, x1) + _dot(a16, x2) + _dot(a16, x3)


def _sigmoid(x):
    return 0.5 + 0.5 * jnp.tanh(0.5 * x)


def _silu(x):
    h = 0.5 * x
    return h + h * jnp.tanh(h)


def _softplus(x):
    return jnp.maximum(x, 0.0) + jnp.log(1.0 + jnp.exp(-jnp.abs(x)))


def _rms(x):
    return x * lax.rsqrt(jnp.mean(x * x, axis=-1, keepdims=True) + EPS)


def _gelu_tanh(x):
    c = 0.7978845608028654
    h = 0.5 * x
    return h + h * jnp.tanh(x * (c + (0.044715 * c) * (x * x)))


def _const_spec(shape):
    n = len(shape)
    return pl.BlockSpec(shape, lambda *_: (0,) * n, pipeline_mode=pl.Buffered(1))


def _inproj_kernel(x_ref, nw_ref, wa_ref, wg_ref, wqk_ref, wkt_ref, wvt_ref, cw_ref, prev_ref,
                   act_ref, cst_ref, z_ref, ba_ref, q_ref, k_ref, vt_ref, kf_ref, vf_ref, carry, w_nn,
                   *, keep_all, n_tiles, S, R, tps):
    x = x_ref[0]
    tm = x.shape[0]
    u = (_rms(x) * nw_ref[...]).astype(BF16)

    hist = GDN_CONV - 1
    if tps > 1:
        @pl.when(pl.program_id(1) % tps == 0)
        def _():
            for r in range(hist):
                carry[r:r + 1, :] = prev_ref[0, r]
    cw = cw_ref[...]
    slabs = [slice(c * GDN_QK, (c + 1) * GDN_QK) for c in range(GDN_QKV // GDN_QK)]

    def conv_silu(pre, cols):
        if tps > 1:
            history = lambda s, r, cols=cols: carry[r:r + 1, cols]
        else:
            history = lambda s, r, cols=cols: prev_ref[0, r, s:s + 1, cols]
        conv = cw[hist:hist + 1, cols] * pre
        for k in range(1, GDN_CONV):
            conv = conv + cw[hist - k:hist - k + 1, cols] * _delayed(
                pre, k, S, R, lambda s, r, k=k, history=history: history(s, hist - k + r))
        act_ref[0, :, cols] = _silu(conv)
        for s in range(S):
            for r in range(hist):
                row = pre[(s + 1) * R - hist + r:(s + 1) * R - hist + r + 1]
                if tps > 1:
                    carry[r:r + 1, cols] = row
                else:
                    cst_ref[0, r, s:s + 1, cols] = row

    @pl.when((pl.program_id(0) == 0) & (pl.program_id(1) == 0))
    def _():
        col = 0
        for piece in (wa_ref, wg_ref, wqk_ref):
            for c in range(piece.shape[0] // LANES):
                w_nn[:, col:col + LANES] = piece[c * LANES:(c + 1) * LANES, :].T
                col += LANES

    p = _dot(u, w_nn[...])
    for cols in slabs:
        conv_silu(p[:, cols], cols)
    if tps > 1:
        for r in range(hist):
            cst_ref[0, r] = carry[r:r + 1, :]
    c0 = GDN_QKV + GDN_Z
    c1 = c0 + LANES
    z_ref[0] = p[:, GDN_QKV:c0]
    ba_ref[0] = p[:, c0:c1]
    q_ref[0] = (p[:, c1:c1 + ATT_W] * Q_SCALE).astype(BF16)
    k = p[:, c1 + ATT_W:]
    k_ref[0] = k.astype(BF16)
    vt = _dot_nt(wvt_ref[...], u)
    for jb in range(tm // LANES):
        vt_ref[0, jb] = vt[:, jb * LANES:(jb + 1) * LANES].astype(BF16)

    if keep_all:
        kf_ref[0] = k
        vf_ref[0] = _dot_nt(u, wvt_ref[...])
    else:
        @pl.when(pl.program_id(1) == n_tiles - 1)
        def _():
            kf_ref[0] = _dot_nt(wkt_ref[...], u).reshape(ATT_HEADS, ATT_DH, tm)
            vf_ref[0] = vt.reshape(ATT_HEADS, ATT_DH, tm)


def _inproj(x, nw, w_pieces, wkt, wvt, cw, prev, *, keep_all):
    B, T, D = x.shape
    tm = ROW_TILE
    nt = T // tm
    n_seq = prev.shape[0]
    hist = GDN_CONV - 1
    R = min(B * T // n_seq, tm)
    S = tm // R
    tps = B * T // (n_seq * R)
    assert S == 1 or tps == 1
    grouped = (n_seq // S, hist, S, GDN_QKV)
    prev_g = prev.reshape(n_seq // S, S, hist, GDN_QKV).transpose(0, 2, 1, 3)
    groups_per_b = nt // tps
    st = pl.BlockSpec((1,) + grouped[1:], lambda b, i: (b * groups_per_b + i // tps, 0, 0, 0))
    row = lambda w: pl.BlockSpec((1, tm, w), lambda b, i: (b, i, 0))
    if keep_all:
        keep = row(ATT_W)
        keep_shape = (B, T, ATT_W)
    else:
        keep = pl.BlockSpec((1, ATT_HEADS, ATT_DH, tm), lambda b, i: (b, 0, 0, 0))
        keep_shape = (B, ATT_HEADS, ATT_DH, tm)
    consts = [nw, *w_pieces, wkt, wvt, cw]
    n_proj = sum(w.shape[0] for w in w_pieces)
    act, cst, *rest = pl.pallas_call(
        functools.partial(_inproj_kernel, keep_all=keep_all, n_tiles=nt, S=S, R=R, tps=tps),
        name="inproj",
        grid=(B, nt),
        in_specs=[row(D)] + [_const_spec(a.shape) for a in consts] + [st],
        out_specs=[row(GDN_QKV), st, row(GDN_Z), row(LANES), row(ATT_W), row(ATT_W),
                   pl.BlockSpec((1, tm // LANES, ATT_W, LANES), lambda b, i: (b, i, 0, 0)), keep, keep],
        out_shape=[jax.ShapeDtypeStruct((B, T, GDN_QKV), F32),
                   jax.ShapeDtypeStruct(grouped, F32),
                   jax.ShapeDtypeStruct((B, T, GDN_Z), F32),
                   jax.ShapeDtypeStruct((B, T, LANES), F32),
                   jax.ShapeDtypeStruct((B, T, ATT_W), BF16),
                   jax.ShapeDtypeStruct((B, T, ATT_W), BF16),
                   jax.ShapeDtypeStruct((B, T // LANES, ATT_W, LANES), BF16),
                   jax.ShapeDtypeStruct(keep_shape, F32),
                   jax.ShapeDtypeStruct(keep_shape, F32)],
        scratch_shapes=[pltpu.VMEM((hist, GDN_QKV), F32), pltpu.VMEM((D, n_proj), BF16)],
        compiler_params=pltpu.CompilerParams(
            dimension_semantics=("arbitrary", "arbitrary"), vmem_limit_bytes=VMEM_LIMIT),
    )(x, *consts, prev_g)
    return (act, cst.transpose(0, 2, 1, 3).reshape(n_seq, hist, GDN_QKV), *rest)


def _bmm(a, b):
    return lax.dot_general(a, b, (((2,), (1,)), ((0,), (0,))), preferred_element_type=F32)


def _bmm_nt(a, b):
    return lax.dot_general(a, b, (((2,), (2,)), ((0,), (0,))), preferred_element_type=F32)


def _bmm_tn(a, b):
    return lax.dot_general(a, b, (((1,), (1,)), ((0,), (0,))), preferred_element_type=F32)


def _gdn_kernel(act_ref, z_ref, ba_ref, s0_ref, alog_ref, dtb_ref, gnw_ref,
                o_ref, sout_ref, s_scr, *, C, cps, bs, n_steps):
    j = pl.program_id(1)
    R = C * cps
    H = GDN_HEADS
    nh = bs * H

    @pl.when(j == 0)
    def _():
        s_scr[...] = s0_ref[...].reshape(nh, GDN_DK, GDN_DV)

    ri = lax.broadcasted_iota(jnp.int32, (C, C), 0)
    ci = lax.broadcasted_iota(jnp.int32, (C, C), 1)
    incl = ri >= ci
    strict = ri > ci
    rr = lax.broadcasted_iota(jnp.int32, (R, R), 0)
    cc = lax.broadcasted_iota(jnp.int32, (R, R), 1)
    cum16 = jnp.where((rr >= cc) & (rr // C == cc // C), 1.0, 0.0).astype(BF16)
    alog = alog_ref[...]
    dtb = dtb_ref[...]

    acts, sigs, Gs, GTs = [], [], [], []
    for s in range(bs):
        acts.append(act_ref.at[s])
        ba = ba_ref[s]
        sigs.append(_sigmoid(ba))
        G = _dot_exact_lhs(cum16, -jnp.exp(alog) * _softplus(ba + dtb))
        Gs.append(G)
        GTs.append(G.T)

    order = [(c, s, h) for c in range(cps) for s in range(bs) for h in range(H)]
    rows = lambda c: slice(c * C, (c + 1) * C)

    def tiles(slabs, col0, width):
        return jnp.stack([slabs[s][rows(c), col0 + h * width:col0 + (h + 1) * width] for c, s, h in order])

    q = tiles(acts, 0, GDN_DK)
    k = tiles(acts, GDN_QK, GDN_DK)
    v = tiles(acts, 2 * GDN_QK, GDN_DV)
    beta = tiles(sigs, 0, 1)
    Gc = tiles(Gs, H, 1)
    Gr = jnp.stack([GTs[s][H + h:H + h + 1, rows(c)] for c, s, h in order])
    Gl = Gc[:, C - 1:C, :]

    nb = len(order)
    li = lax.broadcasted_iota(jnp.int32, (2 * GDN_DK, 2 * GDN_DK), 0) // GDN_DK
    lj = lax.broadcasted_iota(jnp.int32, (2 * GDN_DK, 2 * GDN_DK), 1) // GDN_DK
    ones2 = jnp.where(li == lj, 1.0, 0.0).astype(BF16)
    sq = jnp.concatenate([q * q, k * k], axis=-1).reshape(nb * C, 2 * GDN_DK)
    sq_hi = sq.astype(BF16)
    sq_lo = (sq - sq_hi.astype(F32)).astype(BF16)
    norms = (_dot(sq_hi, ones2) + _dot(sq_lo, ones2)).reshape(nb, C, 2 * GDN_DK)
    qn = q * lax.rsqrt(norms[:, :, :GDN_DK] + EPS) * (GDN_DK ** -0.5)
    kn = k * lax.rsqrt(norms[:, :, GDN_DK:] + EPS)
    eG = jnp.exp(Gc)
    gam = jnp.where(incl, jnp.exp(jnp.where(incl, Gc - Gr, 0.0)), 0.0)
    kb = kn * beta
    kn16 = kn.astype(BF16)
    aq = _bmm_nt(jnp.concatenate([kb, qn], axis=1).astype(BF16), kn16)
    A = jnp.where(strict, aq[:, :C] * gam, 0.0)
    QK16 = (aq[:, C:] * gam).astype(BF16)

    n_joint = C.bit_length() - 2
    A16 = A.astype(BF16)
    N = -A
    Q = _bmm(A16, A16)
    for it in range(n_joint):
        Q16 = Q.astype(BF16)
        if it == n_joint - 1:
            N = N + Q + _bmm(N.astype(BF16), Q16)
        else:
            nq = _bmm(jnp.concatenate([N, Q], axis=1).astype(BF16), Q16)
            N = N + Q + nq[:, :C]
            Q = nq[:, C:]
    rhs = jnp.concatenate([v * beta, kb * eG], axis=-1)
    sol = rhs + _bmm(N.astype(BF16), rhs.astype(BF16))
    u = sol[:, :, :GDN_DV]
    wq16 = jnp.concatenate([sol[:, :, GDN_DV:], qn * eG], axis=1).astype(BF16)
    kg16 = (kn * jnp.exp(Gl - Gc)).astype(BF16)
    dl = jnp.exp(Gl)

    S = s_scr[...]
    o_parts = []
    for c in range(cps):
        sl = slice(c * nh, (c + 1) * nh)
        r = _bmm(wq16[sl], S.astype(BF16))
        vn16 = (u[sl] - r[:, :C]).astype(BF16)
        o_parts.append(r[:, C:] + _bmm(QK16[sl], vn16))
        S = S * dl[sl] + _bmm_tn(kg16[sl], vn16)
    s_scr[...] = S

    on = _rms(jnp.concatenate(o_parts, axis=0)) * gnw_ref[...]
    gates = []
    for s in range(bs):
        zs = z_ref[s]
        gates.append(_silu(zs))
    for idx, (c, s, h) in enumerate(order):
        cols = slice(h * GDN_DV, (h + 1) * GDN_DV)
        o_ref[s, rows(c), cols] = (on[idx] * gates[s][rows(c), cols]).astype(o_ref.dtype)

    @pl.when(j == n_steps - 1)
    def _():
        sout_ref[...] = S.reshape(bs, H, GDN_DK, GDN_DV)


def _gdn(act, z, ba, s0, alog, dtb, gnw, *, C, cps, bs):
    B, T, _ = act.shape
    R = C * cps
    assert B % bs == 0 and T % R == 0
    n_steps = T // R
    row = lambda w: pl.BlockSpec((bs, R, w), lambda b, j: (b, j, 0))
    st = pl.BlockSpec((bs, GDN_HEADS, GDN_DK, GDN_DV), lambda b, j: (b, 0, 0, 0))
    return pl.pallas_call(
        functools.partial(_gdn_kernel, C=C, cps=cps, bs=bs, n_steps=n_steps),
        name="gdn",
        grid=(B // bs, n_steps),
        in_specs=[row(GDN_QKV), row(GDN_Z), row(LANES), st,
                  _const_spec(alog.shape), _const_spec(dtb.shape), _const_spec(gnw.shape)],
        out_specs=[row(GDN_Z), st],
        out_shape=[jax.ShapeDtypeStruct((B, T, GDN_Z), BF16),
                   jax.ShapeDtypeStruct((B, GDN_HEADS, GDN_DK, GDN_DV), F32)],
        scratch_shapes=[pltpu.VMEM((bs * GDN_HEADS, GDN_DK, GDN_DV), F32)],
        compiler_params=pltpu.CompilerParams(
            dimension_semantics=("arbitrary", "arbitrary"), vmem_limit_bytes=VMEM_LIMIT),
    )(act, z, ba, s0, alog, dtb, gnw)


def _attn_prompt_kernel(q_ref, k_ref, vt_ref, bias_ref, nw_ref, o_ref, *, qt):
    j = pl.program_id(1)
    tq = 2 * CHUNK
    n_kb = BAND_CHUNKS // 2 + 1
    n_pairs = ATT_HEADS // 2
    lane_head = lax.broadcasted_iota(jnp.int32, (tq, LANES), 1) // ATT_DH

    def tile(i, masked):
        m = j * qt + i
        q = q_ref[0, i * tq:(i + 1) * tq, :]
        wt = []
        for p in range(n_pairs):
            qp = q[:, p * LANES:(p + 1) * LANES]
            zero = jnp.zeros_like(qp)
            wt.append(jnp.concatenate([jnp.where(lane_head == 0, qp, zero),
                                       jnp.where(lane_head == 1, qp, zero)], axis=0))
        wt = jnp.stack(wt)
        firsts = [m - (n_kb - 1) + jb for jb in range(n_kb)]
        blks = [jnp.maximum(f, 0) for f in firsts]
        k_all = jnp.concatenate(
            [jnp.stack([k_ref[0, blk][:, p * LANES:(p + 1) * LANES] for p in range(n_pairs)]) for blk in blks],
            axis=1)
        v_all = jnp.concatenate([vt_ref[0, blk].reshape(n_pairs, LANES, tq) for blk in blks], axis=2)
        s_all = _bmm_nt(k_all, wt)
        s_parts = []
        for jb in range(n_kb):
            s = s_all[:, jb * tq:(jb + 1) * tq, :]
            if jb not in _FLAT_BIAS_BLOCKS:
                s = s + bias_ref[:, jb * tq:(jb + 1) * tq, :]
            if masked and jb < n_kb - 1:
                s = jnp.where(firsts[jb] >= 0, s, -jnp.inf)
            s_parts.append(s)
        st = jnp.concatenate(s_parts, axis=1)
        pt16 = jnp.exp2(st - jnp.max(st, axis=1, keepdims=True)).astype(BF16)
        acc = _bmm(jnp.concatenate([v_all, jnp.ones((n_pairs, 2 * SUBLANES, n_kb * tq), BF16)], axis=1), pt16)
        inv = 1.0 / acc[:, LANES:LANES + 1, :]
        ot = jnp.concatenate([acc[:, :ATT_DH, :tq] * inv[:, :, :tq],
                              acc[:, ATT_DH:LANES, tq:] * inv[:, :, tq:]], axis=1)
        oh = ot.reshape(ATT_HEADS, ATT_DH, tq)
        on = oh * lax.rsqrt(jnp.mean(oh * oh, axis=1, keepdims=True) + EPS) * nw_ref[...]
        on = on.reshape(n_pairs, LANES, tq)
        for p in range(n_pairs):
            o_ref[0, i * tq:(i + 1) * tq, p * LANES:(p + 1) * LANES] = on[p].T.astype(o_ref.dtype)

    near_start = j * qt < n_kb - 1

    @pl.when(near_start)
    def _():
        for i in range(qt):
            tile(i, True)

    @pl.when(jnp.logical_not(near_start))
    def _():
        for i in range(qt):
            tile(i, False)


def _attn_prompt(q, k, vt, bias, nw, *, qt):
    B, T, _ = q.shape
    tq = 2 * CHUNK
    n_steps = T // (qt * tq)
    k4 = k.reshape(B, T // tq, tq, ATT_W)
    row = pl.BlockSpec((1, qt * tq, ATT_W), lambda b, j: (b, j, 0))
    return pl.pallas_call(
        functools.partial(_attn_prompt_kernel, qt=qt),
        name="attn_prompt",
        grid=(B, n_steps),
        in_specs=[row,
                  pl.BlockSpec((1, T // tq, tq, ATT_W), lambda b, j: (b, 0, 0, 0)),
                  pl.BlockSpec((1, T // tq, ATT_W, tq), lambda b, j: (b, 0, 0, 0)),
                  _const_spec(bias.shape), _const_spec(nw.shape)],
        out_specs=row,
        out_shape=jax.ShapeDtypeStruct((B, T, ATT_W), BF16),
        compiler_params=pltpu.CompilerParams(
            dimension_semantics=("arbitrary", "arbitrary"), vmem_limit_bytes=VMEM_LIMIT),
    )(q, k4, vt, bias, nw)


def _attn_sample_kernel(q_ref, kc_ref, vc_ref, kn_ref, vn_ref, bc_ref, bn_ref, nw_ref, o_ref, *, n_seq):
    by_head = lambda a: jnp.stack([a[:, h * ATT_DH:(h + 1) * ATT_DH] for h in range(ATT_HEADS)])
    for s in range(n_seq):
        kc = kc_ref[s].astype(BF16)
        vc = vc_ref[s].astype(BF16)
        q = by_head(q_ref[s])
        kn = by_head(kn_ref[s])
        vn = by_head(vn_ref[s]).astype(BF16)
        s_c = _bmm(q, kc) + bc_ref[...]
        s_n = _bmm_nt(q, kn) + bn_ref[...]
        m = jnp.maximum(jnp.max(s_c, axis=-1, keepdims=True), jnp.max(s_n, axis=-1, keepdims=True))
        p_c = jnp.exp2(s_c - m)
        p_n = jnp.exp2(s_n - m)
        l = jnp.sum(p_c, axis=-1, keepdims=True) + jnp.sum(p_n, axis=-1, keepdims=True)
        o = (_bmm_nt(p_c.astype(BF16), vc) + _bmm(p_n.astype(BF16), vn)) * (1.0 / l)
        on = _rms(o) * nw_ref[...]
        o_ref[s] = jnp.concatenate([on[h] for h in range(ATT_HEADS)], axis=-1).astype(o_ref.dtype)


def _attn_sample(q, kc, vc, kn, vn, bias_c, bias_n, nw, *, n_seq):
    B, T, _ = q.shape
    new = pl.BlockSpec((n_seq, T, ATT_W), lambda b: (b, 0, 0))
    cache = pl.BlockSpec((n_seq,) + kc.shape[1:], lambda b: (b, 0, 0, 0))
    return pl.pallas_call(
        functools.partial(_attn_sample_kernel, n_seq=n_seq),
        name="attn_sample",
        grid=(B // n_seq,),
        in_specs=[new, cache, cache, new, new, _const_spec(bias_c.shape), _const_spec(bias_n.shape),
                  _const_spec(nw.shape)],
        out_specs=new,
        out_shape=jax.ShapeDtypeStruct((B, T, ATT_W), BF16),
        compiler_params=pltpu.CompilerParams(
            dimension_semantics=("arbitrary",), vmem_limit_bytes=VMEM_LIMIT),
    )(q, kc, vc, kn, vn, bias_c, bias_n, nw)


def _delayed(g, k, S, R, hist_row):
    rolled = pltpu.roll(g, k, axis=0)
    sub = lax.broadcasted_iota(jnp.int32, (SUBLANES, g.shape[1]), 0)
    parts = []
    for s in range(S):
        head = rolled[s * R:s * R + SUBLANES]
        for i in range(k):
            head = jnp.where(sub == i, hist_row(s, i), head)
        parts += [head, rolled[s * R + SUBLANES:(s + 1) * R]]
    return jnp.concatenate(parts, axis=0)


def _mix_ffn_kernel(x_ref, oa_ref, ob_ref, wo_ref, nmp_ref, nfp_ref, wgu_ref, cw_ref, cb_ref,
                    wd_ref, nfo_ref, prev_ref, y_ref, st_ref, carry,
                    *, S, R, tps, d_ff):
    t = pl.program_id(0)
    hist = FFN_CONV - 1
    half = oa_ref.shape[-1]
    mix = _dot(oa_ref[...], wo_ref[:half, :]) + _dot(ob_ref[...], wo_ref[half:, :])
    x1 = x_ref[...] + _rms(mix) * nmp_ref[...]
    u2 = (_rms(x1) * nfp_ref[...]).astype(BF16)

    if tps > 1:
        @pl.when(t % tps == 0)
        def _():
            for i in range(hist):
                carry[i:i + 1, :] = prev_ref[0, i]

    gu = _dot(u2, wgu_ref[...])
    g = gu[:, :d_ff]
    up = gu[:, d_ff:]
    if tps > 1:
        history = lambda s, i: carry[i:i + 1, :]
    else:
        history = lambda s, i: prev_ref[0, i, s:s + 1, :]
    cw = cw_ref[...]
    conv = cb_ref[...] + cw[hist:hist + 1] * g
    for k in range(1, FFN_CONV):
        conv = conv + cw[hist - k:hist - k + 1] * _delayed(
            g, k, S, R, lambda s, i, k=k: history(s, hist - k + i))
    for s in range(S):
        for i in range(hist):
            row = g[(s + 1) * R - hist + i:(s + 1) * R - hist + i + 1]
            if tps > 1:
                carry[i:i + 1, :] = row
            else:
                st_ref[0, i, s:s + 1, :] = row
    if tps > 1:
        for i in range(hist):
            st_ref[0, i] = carry[i:i + 1, :]
    hid = (_gelu_tanh(conv) * up).astype(BF16)
    y_ref[...] = x1 + _rms(_dot(hid, wd_ref[...])) * nfo_ref[...]


def _mix_ffn(x, oa, ob, wo, nmp, nfp, wgu, cw, cb, wd, nfo, prev, *, S, R):
    M, D = x.shape
    d_ff = wd.shape[0]
    tm = S * R
    n_seq = prev.shape[0]
    hist = FFN_CONV - 1
    tps = M // (n_seq * R)
    assert S == 1 or tps == 1
    grouped = (n_seq // S, hist, S, d_ff)
    prev_g = prev.reshape(n_seq // S, S, hist, d_ff).transpose(0, 2, 1, 3)
    row = lambda w: pl.BlockSpec((tm, w), lambda t: (t, 0))
    st = pl.BlockSpec((1,) + grouped[1:], lambda t: (t // tps, 0, 0, 0))
    consts = [wo, nmp, nfp, wgu, cw, cb, wd, nfo]
    y, st_g = pl.pallas_call(
        functools.partial(_mix_ffn_kernel, S=S, R=R, tps=tps, d_ff=d_ff),
        name="mix_ffn",
        grid=(M // tm,),
        in_specs=[row(D), row(oa.shape[1]), row(ob.shape[1])] + [_const_spec(a.shape) for a in consts] + [st],
        out_specs=[row(D), st],
        out_shape=[jax.ShapeDtypeStruct((M, D), F32), jax.ShapeDtypeStruct(grouped, F32)],
        scratch_shapes=[pltpu.VMEM((hist, d_ff), F32)],
        compiler_params=pltpu.CompilerParams(
            dimension_semantics=("arbitrary",), vmem_limit_bytes=VMEM_LIMIT),
    )(x, oa, ob, *consts, prev_g)
    return y, st_g.transpose(0, 2, 1, 3).reshape(n_seq, hist, d_ff)


def _lane_row(vals, offset):
    return jnp.zeros((1, LANES), F32).at[0, offset:offset + vals.shape[0]].set(vals.astype(F32))


def _bias_kernel(r_ref, bp_ref, bc_ref, bn_ref, *, lc, ts):
    tq = 2 * CHUNK
    key = lax.broadcasted_iota(jnp.int32, (tq, tq), 0)
    query_chunk = lax.broadcasted_iota(jnp.int32, (tq, tq), 1) // CHUNK
    for h in range(r_ref.shape[0]):
        p, e = divmod(h, 2)
        for jb in range(r_ref.shape[1]):
            row = jnp.broadcast_to(r_ref[h, jb], (tq, 2 * tq))
            blk = pltpu.roll(row, 0, 1, stride=1, stride_axis=0)[:, :tq]
            key_in_band = jb * tq + key - query_chunk * CHUNK
            valid = (key_in_band >= 0) & (key_in_band < (BAND_CHUNKS + 1) * CHUNK)
            far = r_ref[h, 0][:, 0:1]
            bp_ref[p, jb * tq:(jb + 1) * tq, e * tq:(e + 1) * tq] = jnp.where(valid, blk - far, -jnp.inf)
            by_query = blk.T
            if (jb + 1) * tq <= lc:
                bc_ref[h, :, jb * tq:(jb + 1) * tq] = by_query[:ts]
            else:
                bn_ref[h] = by_query[:ts, :ts]


def _band_biases(table, lc, ts):
    n_heads = table.shape[0]
    tq = 2 * CHUNK
    n_kb = BAND_CHUNKS // 2 + 1
    assert lc == (n_kb - 1) * tq and ts <= tq
    starts = [WINDOW - tq * jb + MAX_REL - half * tq for jb in range(n_kb) for half in (0, 1)]
    pad_l = max(0, -min(starts))
    pad_r = max(0, max(starts) + tq - table.shape[1])
    ext = jnp.pad(table, ((0, 0), (pad_l, pad_r)), mode="edge")
    rows = jnp.concatenate([ext[:, s + pad_l:s + pad_l + tq] for s in starts], axis=1)
    rows = rows.reshape(n_heads, n_kb, 1, 2 * tq)
    return pl.pallas_call(
        functools.partial(_bias_kernel, lc=lc, ts=ts),
        name="band_bias",
        out_shape=[jax.ShapeDtypeStruct((n_heads // 2, n_kb * tq, 2 * tq), F32),
                   jax.ShapeDtypeStruct((n_heads, ts, lc), F32),
                   jax.ShapeDtypeStruct((n_heads, ts, ts), F32)],
    )(rows)


def _layer(xp, xs, cache_k, cache_v, s_delta, s_qkv, s_ffn, lw):
    (norm_mix_pre, w_in, qkv_conv_w, a_log, dt_bias, gdn_norm_w, rel_bias, attn_norm_w, w_out,
     norm_mix_post, norm_ffn_pre, w_gate_up, ffn_conv_w, ffn_conv_b, w_down, norm_ffn_post) = lw
    Bp, Tp, D = xp.shape
    Bs, Ts, _ = xs.shape
    d_ff = w_down.shape[0]

    c1 = GDN_QKV + GDN_Z
    c2 = c1 + 2 * GDN_HEADS
    wt = jnp.swapaxes(w_in, 0, 1).astype(BF16)
    w_proj = (wt[:c1], jnp.pad(wt[c1:c2], ((0, LANES - 2 * GDN_HEADS), (0, 0))), wt[c2:c2 + 2 * ATT_W])
    wkt = wt[c2 + ATT_W:c2 + 2 * ATT_W]
    wvt = wt[c2 + 2 * ATT_W:]
    nmix = norm_mix_pre.reshape(1, D)
    cw_qkv = jnp.pad(qkv_conv_w, ((0, SUBLANES - GDN_CONV), (0, 0)))
    alog = _lane_row(a_log, GDN_HEADS)
    dtb = _lane_row(dt_bias, GDN_HEADS)
    gnw = gdn_norm_w.reshape(1, GDN_DV)
    anw = attn_norm_w.astype(F32).reshape(1, ATT_DH)
    anw_col = jnp.broadcast_to(attn_norm_w.astype(F32)[:, None], (ATT_DH, 2 * CHUNK))
    lc = cache_k.shape[1]
    bias_p, bias_c, bias_n = _band_biases(rel_bias.astype(F32) * LOG2E, lc, Ts)
    wo = w_out.astype(BF16)
    wgu = w_gate_up.astype(BF16)
    wd = w_down.astype(BF16)
    cw_ffn = jnp.pad(ffn_conv_w, ((0, SUBLANES - FFN_CONV), (0, 0)))
    cb = ffn_conv_b.reshape(1, d_ff)
    nmp = norm_mix_post.reshape(1, D)
    nfp = norm_ffn_pre.reshape(1, D)
    nfo = norm_ffn_post.reshape(1, D)

    def group(x, keep_all, gdn_prev, gdn_s0, gdn_c, gdn_cps, gdn_bs, ffn_prev, S, R, attn):
        B, T, _ = x.shape
        xi = x if not keep_all else x.reshape(1, B * T, D)
        act, qkv_state, z, ba, q, k, vt, kf, vf = _inproj(xi, nmix, w_proj, wkt, wvt, cw_qkv, gdn_prev,
                                                          keep_all=keep_all)
        rs = lambda a: a.reshape(B, T, a.shape[-1])
        act, z, ba, q, k = map(rs, (act, z, ba, q, k))
        oa, s_new = _gdn(act, z, ba, gdn_s0, alog, dtb, gnw, C=gdn_c, cps=gdn_cps, bs=gdn_bs)
        ob = attn(q, k, vt, vf)
        y, ffn_state = _mix_ffn(x.reshape(B * T, D), oa.reshape(B * T, GDN_Z), ob.reshape(B * T, ATT_W),
                                wo, nmp, nfp, wgu, cw_ffn, cb, wd, nfo, ffn_prev,
                                S=S, R=R)
        if keep_all:
            k_rows = kf.reshape(B, T, ATT_HEADS, ATT_DH)
            v_rows = vf.reshape(B, T, ATT_HEADS, ATT_DH)
        else:
            k_rows = jnp.transpose(kf, (0, 3, 1, 2))
            v_rows = jnp.transpose(vf, (0, 3, 1, 2))
        return y.reshape(B, T, D), k_rows, v_rows, s_new, qkv_state, ffn_state

    out_p = group(
        xp, False, jnp.zeros((Bp, GDN_CONV - 1, GDN_QKV), F32),
        jnp.zeros((Bp, GDN_HEADS, GDN_DK, GDN_DV), F32), CHUNK, 4, 4,
        jnp.zeros((Bp, FFN_CONV - 1, d_ff), F32), 1, ROW_TILE,
        lambda q, k, vt, vf: _attn_prompt(q, k, vt, bias_p, anw_col, qt=4))

    kc_t = jnp.transpose(cache_k, (0, 2, 3, 1))
    vc_t = jnp.transpose(cache_v, (0, 2, 3, 1))
    out_s = group(
        xs, True, s_qkv, s_delta, Ts, 1, 4, s_ffn, ROW_TILE // Ts, Ts,
        lambda q, k, vt, vf: _attn_sample(q, kc_t, vc_t, k, vf.reshape(Bs, Ts, ATT_W), bias_c, bias_n, anw,
                                          n_seq=2))
    return out_p, out_s


def kernel(x_prompt, x_sample, cache_band_k, cache_band_v, state_delta, state_qkv_conv, state_ffn_conv, norm_mix_pre, w_in, qkv_conv_w, a_log, dt_bias, gdn_norm_w, rel_bias, attn_norm_w, w_out, norm_mix_post, norm_ffn_pre, w_gate_up, ffn_conv_w, ffn_conv_b, w_down, norm_ffn_post):
    weights = (norm_mix_pre, w_in, qkv_conv_w, a_log, dt_bias, gdn_norm_w, rel_bias, attn_norm_w, w_out,
               norm_mix_post, norm_ffn_pre, w_gate_up, ffn_conv_w, ffn_conv_b, w_down, norm_ffn_post)
    depth = w_in.shape[0]
    xp, xs = x_prompt, x_sample
    outs_p, outs_s = [], []
    for l in range(depth):
        lw = tuple(w[l] for w in weights)
        op, os_ = _layer(xp, xs, cache_band_k[l], cache_band_v[l], state_delta[l], state_qkv_conv[l],
                         state_ffn_conv[l], lw)
        xp, xs = op[0], os_[0]
        outs_p.append(op[1:])
        outs_s.append(os_[1:])
    stack = lambda outs, i: jnp.stack([o[i] for o in outs], axis=0)
    return (xp, xs) + tuple(stack(outs_p, i) for i in range(5)) + tuple(stack(outs_s, i) for i in range(5))
```

```python
import functools

import jax
import jax.numpy as jnp
from jax import lax
from jax.experimental import pallas as pl
from jax.experimental.pallas import tpu as pltpu

F32 = jnp.float32
BF16 = jnp.bfloat16

EPS = 1e-6
CHUNK = 64
GDN_HEADS = 4
GDN_DK = 128
GDN_DV = 128
GDN_CONV = 4
ATT_HEADS = 8
ATT_DH = 64
BAND_CHUNKS = 8
WINDOW = BAND_CHUNKS * CHUNK
MAX_REL = 128
FFN_CONV = 3

GDN_QK = GDN_HEADS * GDN_DK
GDN_QKV = GDN_HEADS * (2 * GDN_DK + GDN_DV)
GDN_Z = GDN_HEADS * GDN_DV
ATT_W = ATT_HEADS * ATT_DH
LANES = 128
SUBLANES = 8
VMEM_LIMIT = 56 * 1024 * 1024
ROW_TILE = 512
_FLAT_BIAS_BLOCKS = tuple(
    jb for jb in range(BAND_CHUNKS // 2 + 1)
    if jb * 2 * CHUNK >= CHUNK and (jb + 1) * 2 * CHUNK <= (BAND_CHUNKS + 1) * CHUNK
    and WINDOW - (jb + 1) * 2 * CHUNK + 1 >= MAX_REL)
LOG2E = 1.4426950408889634
Q_SCALE = ATT_DH ** -0.5 * LOG2E


def _dot(a, b):
    return jnp.dot(a, b, preferred_element_type=F32)


def _dot_nt(a, b):
    return lax.dot_general(a, b, (((1,), (1,)), ((), ())), preferred_element_type=F32)


def _split3(x):
    x1 = x.astype(BF16)
    r1 = x - x1.astype(F32)
    x2 = r1.astype(BF16)
    x3 = (r1 - x2.astype(F32)).astype(BF16)
    return x1, x2, x3


def _dot_exact_lhs(a16, x):
    x1, x2, x3 = _split3(x)
    return _dot(a16, x1) + _dot(a16, x2) + _dot(a16, x3)


def _sigmoid(x):
    return 0.5 + 0.5 * jnp.tanh(0.5 * x)


def _silu(x):
    h = 0.5 * x
    return h + h * jnp.tanh(h)


def _softplus(x):
    return jnp.maximum(x, 0.0) + jnp.log(1.0 + jnp.exp(-jnp.abs(x)))


def _rms(x):
    return x * lax.rsqrt(jnp.mean(x * x, axis=-1, keepdims=True) + EPS)


def _gelu_tanh(x):
    c = 0.7978845608028654
    h = 0.5 * x
    return h + h * jnp.tanh(x * (c + (0.044715 * c) * (x * x)))


def _const_spec(shape):
    n = len(shape)
    return pl.BlockSpec(shape, lambda *_: (0,) * n, pipeline_mode=pl.Buffered(1))


def _inproj_kernel(x_ref, nw_ref, wa_ref, wg_ref, wqk_ref, wkt_ref, wvt_ref, cw_ref, prev_ref,
                   act_ref, cst_ref, z_ref, ba_ref, q_ref, k_ref, vt_ref, kf_ref, vf_ref, carry, w_nn,
                   *, keep_all, n_tiles, S, R, tps):
    x = x_ref[0]
    tm = x.shape[0]
    u = (_rms(x) * nw_ref[...]).astype(BF16)

    hist = GDN_CONV - 1
    if tps > 1:
        @pl.when(pl.program_id(1) % tps == 0)
        def _():
            for r in range(hist):
                carry[r:r + 1, :] = prev_ref[0, r]
    cw = cw_ref[...]
    slabs = [slice(c * GDN_QK, (c + 1) * GDN_QK) for c in range(GDN_QKV // GDN_QK)]

    def conv_silu(pre, cols):
        if tps > 1:
            history = lambda s, r, cols=cols: carry[r:r + 1, cols]
        else:
            history = lambda s, r, cols=cols: prev_ref[0, r, s:s + 1, cols]
        conv = cw[hist:hist + 1, cols] * pre
        for k in range(1, GDN_CONV):
            conv = conv + cw[hist - k:hist - k + 1, cols] * _delayed(
                pre, k, S, R, lambda s, r, k=k, history=history: history(s, hist - k + r))
        act_ref[0, :, cols] = _silu(conv)
        for s in range(S):
            for r in range(hist):
                row = pre[(s + 1) * R - hist + r:(s + 1) * R - hist + r + 1]
                if tps > 1:
                    carry[r:r + 1, cols] = row
                else:
                    cst_ref[0, r, s:s + 1, cols] = row

    @pl.when((pl.program_id(0) == 0) & (pl.program_id(1) == 0))
    def _():
        col = 0
        for piece in (wa_ref, wg_ref, wqk_ref):
            for c in range(piece.shape[0] // LANES):
                w_nn[:, col:col + LANES] = piece[c * LANES:(c + 1) * LANES, :].T
                col += LANES

    p = _dot(u, w_nn[...])
    for cols in slabs:
        conv_silu(p[:, cols], cols)
    if tps > 1:
        for r in range(hist):
            cst_ref[0, r] = carry[r:r + 1, :]
    c0 = GDN_QKV + GDN_Z
    c1 = c0 + LANES
    z_ref[0] = p[:, GDN_QKV:c0]
    ba_ref[0] = p[:, c0:c1]
    q_ref[0] = (p[:, c1:c1 + ATT_W] * Q_SCALE).astype(BF16)
    k = p[:, c1 + ATT_W:]
    k_ref[0] = k.astype(BF16)
    vt = _dot_nt(wvt_ref[...], u)
    for jb in range(tm // LANES):
        vt_ref[0, jb] = vt[:, jb * LANES:(jb + 1) * LANES].astype(BF16)

    if keep_all:
        kf_ref[0] = k
        vf_ref[0] = _dot_nt(u, wvt_ref[...])
    else:
        @pl.when(pl.program_id(1) == n_tiles - 1)
        def _():
            kf_ref[0] = _dot_nt(wkt_ref[...], u).reshape(ATT_HEADS, ATT_DH, tm)
            vf_ref[0] = vt.reshape(ATT_HEADS, ATT_DH, tm)


def _inproj(x, nw, w_pieces, wkt, wvt, cw, prev, *, keep_all):
    B, T, D = x.shape
    tm = ROW_TILE
    nt = T // tm
    n_seq = prev.shape[0]
    hist = GDN_CONV - 1
    R = min(B * T // n_seq, tm)
    S = tm // R
    tps = B * T // (n_seq * R)
    assert S == 1 or tps == 1
    grouped = (n_seq // S, hist, S, GDN_QKV)
    prev_g = prev.reshape(n_seq // S, S, hist, GDN_QKV).transpose(0, 2, 1, 3)
    groups_per_b = nt // tps
    st = pl.BlockSpec((1,) + grouped[1:], lambda b, i: (b * groups_per_b + i // tps, 0, 0, 0))
    row = lambda w: pl.BlockSpec((1, tm, w), lambda b, i: (b, i, 0))
    if keep_all:
        keep = row(ATT_W)
        keep_shape = (B, T, ATT_W)
    else:
        keep = pl.BlockSpec((1, ATT_HEADS, ATT_DH, tm), lambda b, i: (b, 0, 0, 0))
        keep_shape = (B, ATT_HEADS, ATT_DH, tm)
    consts = [nw, *w_pieces, wkt, wvt, cw]
    n_proj = sum(w.shape[0] for w in w_pieces)
    act, cst, *rest = pl.pallas_call(
        functools.partial(_inproj_kernel, keep_all=keep_all, n_tiles=nt, S=S, R=R, tps=tps),
        name="inproj",
        grid=(B, nt),
        in_specs=[row(D)] + [_const_spec(a.shape) for a in consts] + [st],
        out_specs=[row(GDN_QKV), st, row(GDN_Z), row(LANES), row(ATT_W), row(ATT_W),
                   pl.BlockSpec((1, tm // LANES, ATT_W, LANES), lambda b, i: (b, i, 0, 0)), keep, keep],
        out_shape=[jax.ShapeDtypeStruct((B, T, GDN_QKV), F32),
                   jax.ShapeDtypeStruct(grouped, F32),
                   jax.ShapeDtypeStruct((B, T, GDN_Z), F32),
                   jax.ShapeDtypeStruct((B, T, LANES), F32),
                   jax.ShapeDtypeStruct((B, T, ATT_W), BF16),
                   jax.ShapeDtypeStruct((B, T, ATT_W), BF16),
                   jax.ShapeDtypeStruct((B, T // LANES, ATT_W, LANES), BF16),
                   jax.ShapeDtypeStruct(keep_shape, F32),
                   jax.ShapeDtypeStruct(keep_shape, F32)],
        scratch_shapes=[pltpu.VMEM((hist, GDN_QKV), F32), pltpu.VMEM((D, n_proj), BF16)],
        compiler_params=pltpu.CompilerParams(
            dimension_semantics=("arbitrary", "arbitrary"), vmem_limit_bytes=VMEM_LIMIT),
    )(x, *consts, prev_g)
    return (act, cst.transpose(0, 2, 1, 3).reshape(n_seq, hist, GDN_QKV), *rest)


def _bmm(a, b):
    return lax.dot_general(a, b, (((2,), (1,)), ((0,), (0,))), preferred_element_type=F32)


def _bmm_nt(a, b):
    return lax.dot_general(a, b, (((2,), (2,)), ((0,), (0,))), preferred_element_type=F32)


def _bmm_tn(a, b):
    return lax.dot_general(a, b, (((1,), (1,)), ((0,), (0,))), preferred_element_type=F32)


def _gdn_kernel(act_ref, z_ref, ba_ref, s0_ref, alog_ref, dtb_ref, gnw_ref,
                o_ref, sout_ref, s_scr, *, C, cps, bs, n_steps):
    j = pl.program_id(1)
    R = C * cps
    H = GDN_HEADS
    nh = bs * H

    @pl.when(j == 0)
    def _():
        s_scr[...] = s0_ref[...].reshape(nh, GDN_DK, GDN_DV)

    ri = lax.broadcasted_iota(jnp.int32, (C, C), 0)
    ci = lax.broadcasted_iota(jnp.int32, (C, C), 1)
    incl = ri >= ci
    strict = ri > ci
    rr = lax.broadcasted_iota(jnp.int32, (R, R), 0)
    cc = lax.broadcasted_iota(jnp.int32, (R, R), 1)
    cum16 = jnp.where((rr >= cc) & (rr // C == cc // C), 1.0, 0.0).astype(BF16)
    alog = alog_ref[...]
    dtb = dtb_ref[...]

    acts, sigs, Gs, GTs = [], [], [], []
    for s in range(bs):
        acts.append(act_ref.at[s])
        ba = ba_ref[s]
        sigs.append(_sigmoid(ba))
        G = _dot_exact_lhs(cum16, -jnp.exp(alog) * _softplus(ba + dtb))
        Gs.append(G)
        GTs.append(G.T)

    order = [(c, s, h) for c in range(cps) for s in range(bs) for h in range(H)]
    rows = lambda c: slice(c * C, (c + 1) * C)

    def tiles(slabs, col0, width):
        return jnp.stack([slabs[s][rows(c), col0 + h * width:col0 + (h + 1) * width] for c, s, h in order])

    q = tiles(acts, 0, GDN_DK)
    k = tiles(acts, GDN_QK, GDN_DK)
    v = tiles(acts, 2 * GDN_QK, GDN_DV)
    beta = tiles(sigs, 0, 1)
    Gc = tiles(Gs, H, 1)
    Gr = jnp.stack([GTs[s][H + h:H + h + 1, rows(c)] for c, s, h in order])
    Gl = Gc[:, C - 1:C, :]

    nb = len(order)
    li = lax.broadcasted_iota(jnp.int32, (2 * GDN_DK, 2 * GDN_DK), 0) // GDN_DK
    lj = lax.broadcasted_iota(jnp.int32, (2 * GDN_DK, 2 * GDN_DK), 1) // GDN_DK
    ones2 = jnp.where(li == lj, 1.0, 0.0).astype(BF16)
    sq = jnp.concatenate([q * q, k * k], axis=-1).reshape(nb * C, 2 * GDN_DK)
    sq_hi = sq.astype(BF16)
    sq_lo = (sq - sq_hi.astype(F32)).astype(BF16)
    norms = (_dot(sq_hi, ones2) + _dot(sq_lo, ones2)).reshape(nb, C, 2 * GDN_DK)
    qn = q * lax.rsqrt(norms[:, :, :GDN_DK] + EPS) * (GDN_DK ** -0.5)
    kn = k * lax.rsqrt(norms[:, :, GDN_DK:] + EPS)
    eG = jnp.exp(Gc)
    gam = jnp.where(incl, jnp.exp(jnp.where(incl, Gc - Gr, 0.0)), 0.0)
    kb = kn * beta
    kn16 = kn.astype(BF16)
    aq = _bmm_nt(jnp.concatenate([kb, qn], axis=1).astype(BF16), kn16)
    A = jnp.where(strict, aq[:, :C] * gam, 0.0)
    QK16 = (aq[:, C:] * gam).astype(BF16)

    n_joint = C.bit_length() - 2
    A16 = A.astype(BF16)
    N = -A
    Q = _bmm(A16, A16)
    for it in range(n_joint):
        Q16 = Q.astype(BF16)
        if it == n_joint - 1:
            N = N + Q + _bmm(N.astype(BF16), Q16)
        else:
            nq = _bmm(jnp.concatenate([N, Q], axis=1).astype(BF16), Q16)
            N = N + Q + nq[:, :C]
            Q = nq[:, C:]
    rhs = jnp.concatenate([v * beta, kb * eG], axis=-1)
    sol = rhs + _bmm(N.astype(BF16), rhs.astype(BF16))
    u = sol[:, :, :GDN_DV]
    wq16 = jnp.concatenate([sol[:, :, GDN_DV:], qn * eG], axis=1).astype(BF16)
    kg16 = (kn * jnp.exp(Gl - Gc)).astype(BF16)
    dl = jnp.exp(Gl)

    S = s_scr[...]
    o_parts = []
    for c in range(cps):
        sl = slice(c * nh, (c + 1) * nh)
        r = _bmm(wq16[sl], S.astype(BF16))
        vn16 = (u[sl] - r[:, :C]).astype(BF16)
        o_parts.append(r[:, C:] + _bmm(QK16[sl], vn16))
        S = S * dl[sl] + _bmm_tn(kg16[sl], vn16)
    s_scr[...] = S

    on = _rms(jnp.concatenate(o_parts, axis=0)) * gnw_ref[...]
    gates = []
    for s in range(bs):
        zs = z_ref[s]
        gates.append(_silu(zs))
    for idx, (c, s, h) in enumerate(order):
        cols = slice(h * GDN_DV, (h + 1) * GDN_DV)
        o_ref[s, rows(c), cols] = (on[idx] * gates[s][rows(c), cols]).astype(o_ref.dtype)

    @pl.when(j == n_steps - 1)
    def _():
        sout_ref[...] = S.reshape(bs, H, GDN_DK, GDN_DV)


def _gdn(act, z, ba, s0, alog, dtb, gnw, *, C, cps, bs):
    B, T, _ = act.shape
    R = C * cps
    assert B % bs == 0 and T % R == 0
    n_steps = T // R
    row = lambda w: pl.BlockSpec((bs, R, w), lambda b, j: (b, j, 0))
    st = pl.BlockSpec((bs, GDN_HEADS, GDN_DK, GDN_DV), lambda b, j: (b, 0, 0, 0))
    return pl.pallas_call(
        functools.partial(_gdn_kernel, C=C, cps=cps, bs=bs, n_steps=n_steps),
        name="gdn",
        grid=(B // bs, n_steps),
        in_specs=[row(GDN_QKV), row(GDN_Z), row(LANES), st,
                  _const_spec(alog.shape), _const_spec(dtb.shape), _const_spec(gnw.shape)],
        out_specs=[row(GDN_Z), st],
        out_shape=[jax.ShapeDtypeStruct((B, T, GDN_Z), BF16),
                   jax.ShapeDtypeStruct((B, GDN_HEADS, GDN_DK, GDN_DV), F32)],
        scratch_shapes=[pltpu.VMEM((bs * GDN_HEADS, GDN_DK, GDN_DV), F32)],
        compiler_params=pltpu.CompilerParams(
            dimension_semantics=("arbitrary", "arbitrary"), vmem_limit_bytes=VMEM_LIMIT),
    )(act, z, ba, s0, alog, dtb, gnw)


def _attn_prompt_kernel(q_ref, k_ref, vt_ref, bias_ref, nw_ref, o_ref, *, qt):
    j = pl.program_id(1)
    tq = 2 * CHUNK
    n_kb = BAND_CHUNKS // 2 + 1
    n_pairs = ATT_HEADS // 2
    lane_head = lax.broadcasted_iota(jnp.int32, (tq, LANES), 1) // ATT_DH

    def tile(i, masked):
        m = j * qt + i
        q = q_ref[0, i * tq:(i + 1) * tq, :]
        wt = []
        for p in range(n_pairs):
            qp = q[:, p * LANES:(p + 1) * LANES]
            zero = jnp.zeros_like(qp)
            wt.append(jnp.concatenate([jnp.where(lane_head == 0, qp, zero),
                                       jnp.where(lane_head == 1, qp, zero)], axis=0))
        wt = jnp.stack(wt)
        firsts = [m - (n_kb - 1) + jb for jb in range(n_kb)]
        blks = [jnp.maximum(f, 0) for f in firsts]
        k_all = jnp.concatenate(
            [jnp.stack([k_ref[0, blk][:, p * LANES:(p + 1) * LANES] for p in range(n_pairs)]) for blk in blks],
            axis=1)
        v_all = jnp.concatenate([vt_ref[0, blk].reshape(n_pairs, LANES, tq) for blk in blks], axis=2)
        s_all = _bmm_nt(k_all, wt)
        s_parts = []
        for jb in range(n_kb):
            s = s_all[:, jb * tq:(jb + 1) * tq, :]
            if jb not in _FLAT_BIAS_BLOCKS:
                s = s + bias_ref[:, jb * tq:(jb + 1) * tq, :]
            if masked and jb < n_kb - 1:
                s = jnp.where(firsts[jb] >= 0, s, -jnp.inf)
            s_parts.append(s)
        st = jnp.concatenate(s_parts, axis=1)
        pt16 = jnp.exp2(st - jnp.max(st, axis=1, keepdims=True)).astype(BF16)
        acc = _bmm(jnp.concatenate([v_all, jnp.ones((n_pairs, 2 * SUBLANES, n_kb * tq), BF16)], axis=1), pt16)
        inv = 1.0 / acc[:, LANES:LANES + 1, :]
        ot = jnp.concatenate([acc[:, :ATT_DH, :tq] * inv[:, :, :tq],
                              acc[:, ATT_DH:LANES, tq:] * inv[:, :, tq:]], axis=1)
        oh = ot.reshape(ATT_HEADS, ATT_DH, tq)
        on = oh * lax.rsqrt(jnp.mean(oh * oh, axis=1, keepdims=True) + EPS) * nw_ref[...]
        on = on.reshape(n_pairs, LANES, tq)
        for p in range(n_pairs):
            o_ref[0, i * tq:(i + 1) * tq, p * LANES:(p + 1) * LANES] = on[p].T.astype(o_ref.dtype)

    near_start = j * qt < n_kb - 1

    @pl.when(near_start)
    def _():
        for i in range(qt):
            tile(i, True)

    @pl.when(jnp.logical_not(near_start))
    def _():
        for i in range(qt):
            tile(i, False)


def _attn_prompt(q, k, vt, bias, nw, *, qt):
    B, T, _ = q.shape
    tq = 2 * CHUNK
    n_steps = T // (qt * tq)
    k4 = k.reshape(B, T // tq, tq, ATT_W)
    row = pl.BlockSpec((1, qt * tq, ATT_W), lambda b, j: (b, j, 0))
    return pl.pallas_call(
        functools.partial(_attn_prompt_kernel, qt=qt),
        name="attn_prompt",
        grid=(B, n_steps),
        in_specs=[row,
                  pl.BlockSpec((1, T // tq, tq, ATT_W), lambda b, j: (b, 0, 0, 0)),
                  pl.BlockSpec((1, T // tq, ATT_W, tq), lambda b, j: (b, 0, 0, 0)),
                  _const_spec(bias.shape), _const_spec(nw.shape)],
        out_specs=row,
        out_shape=jax.ShapeDtypeStruct((B, T, ATT_W), BF16),
        compiler_params=pltpu.CompilerParams(
            dimension_semantics=("arbitrary", "arbitrary"), vmem_limit_bytes=VMEM_LIMIT),
    )(q, k4, vt, bias, nw)


def _attn_sample_kernel(q_ref, kc_ref, vc_ref, kn_ref, vn_ref, bc_ref, bn_ref, nw_ref, o_ref, *, n_seq):
    by_head = lambda a: jnp.stack([a[:, h * ATT_DH:(h + 1) * ATT_DH] for h in range(ATT_HEADS)])
    for s in range(n_seq):
        kc = kc_ref[s].astype(BF16)
        vc = vc_ref[s].astype(BF16)
        q = by_head(q_ref[s])
        kn = by_head(kn_ref[s])
        vn = by_head(vn_ref[s]).astype(BF16)
        s_c = _bmm(q, kc) + bc_ref[...]
        s_n = _bmm_nt(q, kn) + bn_ref[...]
        m = jnp.maximum(jnp.max(s_c, axis=-1, keepdims=True), jnp.max(s_n, axis=-1, keepdims=True))
        p_c = jnp.exp2(s_c - m)
        p_n = jnp.exp2(s_n - m)
        l = jnp.sum(p_c, axis=-1, keepdims=True) + jnp.sum(p_n, axis=-1, keepdims=True)
        o = (_bmm_nt(p_c.astype(BF16), vc) + _bmm(p_n.astype(BF16), vn)) * (1.0 / l)
        on = _rms(o) * nw_ref[...]
        o_ref[s] = jnp.concatenate([on[h] for h in range(ATT_HEADS)], axis=-1).astype(o_ref.dtype)


def _attn_sample(q, kc, vc, kn, vn, bias_c, bias_n, nw, *, n_seq):
    B, T, _ = q.shape
    new = pl.BlockSpec((n_seq, T, ATT_W), lambda b: (b, 0, 0))
    cache = pl.BlockSpec((n_seq,) + kc.shape[1:], lambda b: (b, 0, 0, 0))
    return pl.pallas_call(
        functools.partial(_attn_sample_kernel, n_seq=n_seq),
        name="attn_sample",
        grid=(B // n_seq,),
        in_specs=[new, cache, cache, new, new, _const_spec(bias_c.shape), _const_spec(bias_n.shape),
                  _const_spec(nw.shape)],
        out_specs=new,
        out_shape=jax.ShapeDtypeStruct((B, T, ATT_W), BF16),
        compiler_params=pltpu.CompilerParams(
            dimension_semantics=("arbitrary",), vmem_limit_bytes=VMEM_LIMIT),
    )(q, kc, vc, kn, vn, bias_c, bias_n, nw)


def _delayed(g, k, S, R, hist_row):
    rolled = pltpu.roll(g, k, axis=0)
    sub = lax.broadcasted_iota(jnp.int32, (SUBLANES, g.shape[1]), 0)
    parts = []
    for s in range(S):
        head = rolled[s * R:s * R + SUBLANES]
        for i in range(k):
            head = jnp.where(sub == i, hist_row(s, i), head)
        parts += [head, rolled[s * R + SUBLANES:(s + 1) * R]]
    return jnp.concatenate(parts, axis=0)


def _mix_ffn_kernel(x_ref, oa_ref, ob_ref, wo_ref, nmp_ref, nfp_ref, wgu_ref, cw_ref, cb_ref,
                    wd_ref, nfo_ref, prev_ref, y_ref, st_ref, carry,
                    *, S, R, tps, d_ff):
    t = pl.program_id(0)
    hist = FFN_CONV - 1
    half = oa_ref.shape[-1]
    mix = _dot(oa_ref[...], wo_ref[:half, :]) + _dot(ob_ref[...], wo_ref[half:, :])
    x1 = x_ref[...] + _rms(mix) * nmp_ref[...]
    u2 = (_rms(x1) * nfp_ref[...]).astype(BF16)

    if tps > 1:
        @pl.when(t % tps == 0)
        def _():
            for i in range(hist):
                carry[i:i + 1, :] = prev_ref[0, i]

    gu = _dot(u2, wgu_ref[...])
    g = gu[:, :d_ff]
    up = gu[:, d_ff:]
    if tps > 1:
        history = lambda s, i: carry[i:i + 1, :]
    else:
        history = lambda s, i: prev_ref[0, i, s:s + 1, :]
    cw = cw_ref[...]
    conv = cb_ref[...] + cw[hist:hist + 1] * g
    for k in range(1, FFN_CONV):
        conv = conv + cw[hist - k:hist - k + 1] * _delayed(
            g, k, S, R, lambda s, i, k=k: history(s, hist - k + i))
    for s in range(S):
        for i in range(hist):
            row = g[(s + 1) * R - hist + i:(s + 1) * R - hist + i + 1]
            if tps > 1:
                carry[i:i + 1, :] = row
            else:
                st_ref[0, i, s:s + 1, :] = row
    if tps > 1:
        for i in range(hist):
            st_ref[0, i] = carry[i:i + 1, :]
    hid = (_gelu_tanh(conv) * up).astype(BF16)
    y_ref[...] = x1 + _rms(_dot(hid, wd_ref[...])) * nfo_ref[...]


def _mix_ffn(x, oa, ob, wo, nmp, nfp, wgu, cw, cb, wd, nfo, prev, *, S, R):
    M, D = x.shape
    d_ff = wd.shape[0]
    tm = S * R
    n_seq = prev.shape[0]
    hist = FFN_CONV - 1
    tps = M // (n_seq * R)
    assert S == 1 or tps == 1
    grouped = (n_seq // S, hist, S, d_ff)
    prev_g = prev.reshape(n_seq // S, S, hist, d_ff).transpose(0, 2, 1, 3)
    row = lambda w: pl.BlockSpec((tm, w), lambda t: (t, 0))
    st = pl.BlockSpec((1,) + grouped[1:], lambda t: (t // tps, 0, 0, 0))
    consts = [wo, nmp, nfp, wgu, cw, cb, wd, nfo]
    y, st_g = pl.pallas_call(
        functools.partial(_mix_ffn_kernel, S=S, R=R, tps=tps, d_ff=d_ff),
        name="mix_ffn",
        grid=(M // tm,),
        in_specs=[row(D), row(oa.shape[1]), row(ob.shape[1])] + [_const_spec(a.shape) for a in consts] + [st],
        out_specs=[row(D), st],
        out_shape=[jax.ShapeDtypeStruct((M, D), F32), jax.ShapeDtypeStruct(grouped, F32)],
        scratch_shapes=[pltpu.VMEM((hist, d_ff), F32)],
        compiler_params=pltpu.CompilerParams(
            dimension_semantics=("arbitrary",), vmem_limit_bytes=VMEM_LIMIT),
    )(x, oa, ob, *consts, prev_g)
    return y, st_g.transpose(0, 2, 1, 3).reshape(n_seq, hist, d_ff)


def _lane_row(vals, offset):
    return jnp.zeros((1, LANES), F32).at[0, offset:offset + vals.shape[0]].set(vals.astype(F32))


def _bias_kernel(r_ref, bp_ref, bc_ref, bn_ref, *, lc, ts):
    tq = 2 * CHUNK
    key = lax.broadcasted_iota(jnp.int32, (tq, tq), 0)
    query_chunk = lax.broadcasted_iota(jnp.int32, (tq, tq), 1) // CHUNK
    for h in range(r_ref.shape[0]):
        p, e = divmod(h, 2)
        for jb in range(r_ref.shape[1]):
            row = jnp.broadcast_to(r_ref[h, jb], (tq, 2 * tq))
            blk = pltpu.roll(row, 0, 1, stride=1, stride_axis=0)[:, :tq]
            key_in_band = jb * tq + key - query_chunk * CHUNK
            valid = (key_in_band >= 0) & (key_in_band < (BAND_CHUNKS + 1) * CHUNK)
            far = r_ref[h, 0][:, 0:1]
            bp_ref[p, jb * tq:(jb + 1) * tq, e * tq:(e + 1) * tq] = jnp.where(valid, blk - far, -jnp.inf)
            by_query = blk.T
            if (jb + 1) * tq <= lc:
                bc_ref[h, :, jb * tq:(jb + 1) * tq] = by_query[:ts]
            else:
                bn_ref[h] = by_query[:ts, :ts]


def _band_biases(table, lc, ts):
    n_heads = table.shape[0]
    tq = 2 * CHUNK
    n_kb = BAND_CHUNKS // 2 + 1
    assert lc == (n_kb - 1) * tq and ts <= tq
    starts = [WINDOW - tq * jb + MAX_REL - half * tq for jb in range(n_kb) for half in (0, 1)]
    pad_l = max(0, -min(starts))
    pad_r = max(0, max(starts) + tq - table.shape[1])
    ext = jnp.pad(table, ((0, 0), (pad_l, pad_r)), mode="edge")
    rows = jnp.concatenate([ext[:, s + pad_l:s + pad_l + tq] for s in starts], axis=1)
    rows = rows.reshape(n_heads, n_kb, 1, 2 * tq)
    return pl.pallas_call(
        functools.partial(_bias_kernel, lc=lc, ts=ts),
        name="band_bias",
        out_shape=[jax.ShapeDtypeStruct((n_heads // 2, n_kb * tq, 2 * tq), F32),
                   jax.ShapeDtypeStruct((n_heads, ts, lc), F32),
                   jax.ShapeDtypeStruct((n_heads, ts, ts), F32)],
    )(rows)


def _layer(xp, xs, cache_k, cache_v, s_delta, s_qkv, s_ffn, lw):
    (norm_mix_pre, w_in, qkv_conv_w, a_log, dt_bias, gdn_norm_w, rel_bias, attn_norm_w, w_out,
     norm_mix_post, norm_ffn_pre, w_gate_up, ffn_conv_w, ffn_conv_b, w_down, norm_ffn_post) = lw
    Bp, Tp, D = xp.shape
    Bs, Ts, _ = xs.shape
    d_ff = w_down.shape[0]

    c1 = GDN_QKV + GDN_Z
    c2 = c1 + 2 * GDN_HEADS
    wt = jnp.swapaxes(w_in, 0, 1).astype(BF16)
    w_proj = (wt[:c1], jnp.pad(wt[c1:c2], ((0, LANES - 2 * GDN_HEADS), (0, 0))), wt[c2:c2 + 2 * ATT_W])
    wkt = wt[c2 + ATT_W:c2 + 2 * ATT_W]
    wvt = wt[c2 + 2 * ATT_W:]
    nmix = norm_mix_pre.reshape(1, D)
    cw_qkv = jnp.pad(qkv_conv_w, ((0, SUBLANES - GDN_CONV), (0, 0)))
    alog = _lane_row(a_log, GDN_HEADS)
    dtb = _lane_row(dt_bias, GDN_HEADS)
    gnw = gdn_norm_w.reshape(1, GDN_DV)
    anw = attn_norm_w.astype(F32).reshape(1, ATT_DH)
    anw_col = jnp.broadcast_to(attn_norm_w.astype(F32)[:, None], (ATT_DH, 2 * CHUNK))
    lc = cache_k.shape[1]
    bias_p, bias_c, bias_n = _band_biases(rel_bias.astype(F32) * LOG2E, lc, Ts)
    wo = w_out.astype(BF16)
    wgu = w_gate_up.astype(BF16)
    wd = w_down.astype(BF16)
    cw_ffn = jnp.pad(ffn_conv_w, ((0, SUBLANES - FFN_CONV), (0, 0)))
    cb = ffn_conv_b.reshape(1, d_ff)
    nmp = norm_mix_post.reshape(1, D)
    nfp = norm_ffn_pre.reshape(1, D)
    nfo = norm_ffn_post.reshape(1, D)

    def group(x, keep_all, gdn_prev, gdn_s0, gdn_c, gdn_cps, gdn_bs, ffn_prev, S, R, attn):
        B, T, _ = x.shape
        xi = x if not keep_all else x.reshape(1, B * T, D)
        act, qkv_state, z, ba, q, k, vt, kf, vf = _inproj(xi, nmix, w_proj, wkt, wvt, cw_qkv, gdn_prev,
                                                          keep_all=keep_all)
        rs = lambda a: a.reshape(B, T, a.shape[-1])
        act, z, ba, q, k = map(rs, (act, z, ba, q, k))
        oa, s_new = _gdn(act, z, ba, gdn_s0, alog, dtb, gnw, C=gdn_c, cps=gdn_cps, bs=gdn_bs)
        ob = attn(q, k, vt, vf)
        y, ffn_state = _mix_ffn(x.reshape(B * T, D), oa.reshape(B * T, GDN_Z), ob.reshape(B * T, ATT_W),
                                wo, nmp, nfp, wgu, cw_ffn, cb, wd, nfo, ffn_prev,
                                S=S, R=R)
        if keep_all:
            k_rows = kf.reshape(B, T, ATT_HEADS, ATT_DH)
            v_rows = vf.reshape(B, T, ATT_HEADS, ATT_DH)
        else:
            k_rows = jnp.transpose(kf, (0, 3, 1, 2))
            v_rows = jnp.transpose(vf, (0, 3, 1, 2))
        return y.reshape(B, T, D), k_rows, v_rows, s_new, qkv_state, ffn_state

    out_p = group(
        xp, False, jnp.zeros((Bp, GDN_CONV - 1, GDN_QKV), F32),
        jnp.zeros((Bp, GDN_HEADS, GDN_DK, GDN_DV), F32), CHUNK, 4, 4,
        jnp.zeros((Bp, FFN_CONV - 1, d_ff), F32), 1, ROW_TILE,
        lambda q, k, vt, vf: _attn_prompt(q, k, vt, bias_p, anw_col, qt=4))

    kc_t = jnp.transpose(cache_k, (0, 2, 3, 1))
    vc_t = jnp.transpose(cache_v, (0, 2, 3, 1))
    out_s = group(
        xs, True, s_qkv, s_delta, Ts, 1, 8, s_ffn, ROW_TILE // Ts, Ts,
        lambda q, k, vt, vf: _attn_sample(q, kc_t, vc_t, k, vf.reshape(Bs, Ts, ATT_W), bias_c, bias_n, anw,
                                          n_seq=4))
    return out_p, out_s


def kernel(x_prompt, x_sample, cache_band_k, cache_band_v, state_delta, state_qkv_conv, state_ffn_conv, norm_mix_pre, w_in, qkv_conv_w, a_log, dt_bias, gdn_norm_w, rel_bias, attn_norm_w, w_out, norm_mix_post, norm_ffn_pre, w_gate_up, ffn_conv_w, ffn_conv_b, w_down, norm_ffn_post):
    weights = (norm_mix_pre, w_in, qkv_conv_w, a_log, dt_bias, gdn_norm_w, rel_bias, attn_norm_w, w_out,
               norm_mix_post, norm_ffn_pre, w_gate_up, ffn_conv_w, ffn_conv_b, w_down, norm_ffn_post)
    depth = w_in.shape[0]
    xp, xs = x_prompt, x_sample
    outs_p, outs_s = [], []
    for l in range(depth):
        lw = tuple(w[l] for w in weights)
        op, os_ = _layer(xp, xs, cache_band_k[l], cache_band_v[l], state_delta[l], state_qkv_conv[l],
                         state_ffn_conv[l], lw)
        xp, xs = op[0], os_[0]
        outs_p.append(op[1:])
        outs_s.append(os_[1:])
    stack = lambda outs, i: jnp.stack([o[i] for o in outs], axis=0)
    return (xp, xs) + tuple(stack(outs_p, i) for i in range(5)) + tuple(stack(outs_s, i) for i in range(5))
```

```python
import functools

import jax
import jax.numpy as jnp
from jax import lax
from jax.experimental import pallas as pl
from jax.experimental.pallas import tpu as pltpu

F32 = jnp.float32
BF16 = jnp.bfloat16

EPS = 1e-6
CHUNK = 64
GDN_HEADS = 4
GDN_DK = 128
GDN_DV = 128
GDN_CONV = 4
ATT_HEADS = 8
ATT_DH = 64
BAND_CHUNKS = 8
WINDOW = BAND_CHUNKS * CHUNK
MAX_REL = 128
FFN_CONV = 3

GDN_QK = GDN_HEADS * GDN_DK
GDN_QKV = GDN_HEADS * (2 * GDN_DK + GDN_DV)
GDN_Z = GDN_HEADS * GDN_DV
ATT_W = ATT_HEADS * ATT_DH
LANES = 128
SUBLANES = 8
VMEM_LIMIT = 56 * 1024 * 1024
ROW_TILE = 512
_FLAT_BIAS_BLOCKS = tuple(
    jb for jb in range(BAND_CHUNKS // 2 + 1)
    if jb * 2 * CHUNK >= CHUNK and (jb + 1) * 2 * CHUNK <= (BAND_CHUNKS + 1) * CHUNK
    and WINDOW - (jb + 1) * 2 * CHUNK + 1 >= MAX_REL)
LOG2E = 1.4426950408889634
Q_SCALE = ATT_DH ** -0.5 * LOG2E


def _dot(a, b):
    return jnp.dot(a, b, preferred_element_type=F32)


def _dot_nt(a, b):
    return lax.dot_general(a, b, (((1,), (1,)), ((), ())), preferred_element_type=F32)


def _split3(x):
    x1 = x.astype(BF16)
    r1 = x - x1.astype(F32)
    x2 = r1.astype(BF16)
    x3 = (r1 - x2.astype(F32)).astype(BF16)
    return x1, x2, x3


def _dot_exact_lhs(a16, x):
    x1, x2, x3 = _split3(x)
    return _dot(a16, x1) + _dot(a16, x2) + _dot(a16, x3)


def _sigmoid(x):
    return 0.5 + 0.5 * jnp.tanh(0.5 * x)


def _silu_of_half(h):
    return h + h * jnp.tanh(h)


def _silu(x):
    return _silu_of_half(0.5 * x)


def _softplus(x):
    return jnp.maximum(x, 0.0) + jnp.log(1.0 + jnp.exp(-jnp.abs(x)))


def _rms(x):
    return x * lax.rsqrt(jnp.mean(x * x, axis=-1, keepdims=True) + EPS)


def _twice_gelu_tanh(x):
    c = 0.7978845608028654
    return x + x * jnp.tanh(x * (c + (0.044715 * c) * (x * x)))


def _const_spec(shape):
    n = len(shape)
    return pl.BlockSpec(shape, lambda *_: (0,) * n, pipeline_mode=pl.Buffered(1))


def _inproj_kernel(x_ref, nw_ref, wa_ref, wg_ref, wqk_ref, wkt_ref, wvt_ref, cw_ref, prev_ref,
                   act_ref, cst_ref, z_ref, ba_ref, q_ref, k_ref, vt_ref, kf_ref, vf_ref, carry, w_nn,
                   *, keep_all, n_tiles, S, R, tps):
    x = x_ref[0]
    tm = x.shape[0]
    u = (_rms(x) * nw_ref[...]).astype(BF16)

    hist = GDN_CONV - 1
    if tps > 1:
        @pl.when(pl.program_id(1) % tps == 0)
        def _():
            for r in range(hist):
                carry[r:r + 1, :] = prev_ref[0, r]
    cw = cw_ref[...]
    slabs = [slice(c * GDN_QK, (c + 1) * GDN_QK) for c in range(GDN_QKV // GDN_QK)]

    def conv_silu(pre, cols):
        if tps > 1:
            history = lambda s, r, cols=cols: carry[r:r + 1, cols]
        else:
            history = lambda s, r, cols=cols: prev_ref[0, r, s:s + 1, cols]
        half_conv = cw[hist:hist + 1, cols] * pre
        for k in range(1, GDN_CONV):
            half_conv = half_conv + cw[hist - k:hist - k + 1, cols] * _delayed(
                pre, k, S, R, lambda s, r, k=k, history=history: history(s, hist - k + r))
        act_ref[0, :, cols] = _silu_of_half(half_conv)
        for s in range(S):
            for r in range(hist):
                row = pre[(s + 1) * R - hist + r:(s + 1) * R - hist + r + 1]
                if tps > 1:
                    carry[r:r + 1, cols] = row
                else:
                    cst_ref[0, r, s:s + 1, cols] = row

    @pl.when((pl.program_id(0) == 0) & (pl.program_id(1) == 0))
    def _():
        col = 0
        for piece in (wa_ref, wg_ref, wqk_ref):
            for c in range(piece.shape[0] // LANES):
                w_nn[:, col:col + LANES] = piece[c * LANES:(c + 1) * LANES, :].T
                col += LANES

    p = _dot(u, w_nn[...])
    for cols in slabs:
        conv_silu(p[:, cols], cols)
    if tps > 1:
        for r in range(hist):
            cst_ref[0, r] = carry[r:r + 1, :]
    c0 = GDN_QKV + GDN_Z
    c1 = c0 + LANES
    z_ref[0] = p[:, GDN_QKV:c0]
    ba_ref[0] = p[:, c0:c1]
    q_ref[0] = (p[:, c1:c1 + ATT_W] * Q_SCALE).astype(BF16)
    k = p[:, c1 + ATT_W:]
    k_ref[0] = k.astype(BF16)
    vt = _dot_nt(wvt_ref[...], u)
    for jb in range(tm // LANES):
        vt_ref[0, jb] = vt[:, jb * LANES:(jb + 1) * LANES].astype(BF16)

    if keep_all:
        kf_ref[0] = k
        vf_ref[0] = _dot_nt(u, wvt_ref[...])
    else:
        @pl.when(pl.program_id(1) == n_tiles - 1)
        def _():
            kf_ref[0] = _dot_nt(wkt_ref[...], u).reshape(ATT_HEADS, ATT_DH, tm)
            vf_ref[0] = vt.reshape(ATT_HEADS, ATT_DH, tm)


def _inproj(x, nw, w_pieces, wkt, wvt, cw, prev, *, keep_all):
    B, T, D = x.shape
    tm = ROW_TILE
    nt = T // tm
    n_seq = prev.shape[0]
    hist = GDN_CONV - 1
    R = min(B * T // n_seq, tm)
    S = tm // R
    tps = B * T // (n_seq * R)
    assert S == 1 or tps == 1
    grouped = (n_seq // S, hist, S, GDN_QKV)
    prev_g = prev.reshape(n_seq // S, S, hist, GDN_QKV).transpose(0, 2, 1, 3)
    groups_per_b = nt // tps
    st = pl.BlockSpec((1,) + grouped[1:], lambda b, i: (b * groups_per_b + i // tps, 0, 0, 0))
    row = lambda w: pl.BlockSpec((1, tm, w), lambda b, i: (b, i, 0))
    if keep_all:
        keep = row(ATT_W)
        keep_shape = (B, T, ATT_W)
    else:
        keep = pl.BlockSpec((1, ATT_HEADS, ATT_DH, tm), lambda b, i: (b, 0, 0, 0))
        keep_shape = (B, ATT_HEADS, ATT_DH, tm)
    consts = [nw, *w_pieces, wkt, wvt, cw]
    n_proj = sum(w.shape[0] for w in w_pieces)
    act, cst, *rest = pl.pallas_call(
        functools.partial(_inproj_kernel, keep_all=keep_all, n_tiles=nt, S=S, R=R, tps=tps),
        name="inproj",
        grid=(B, nt),
        in_specs=[row(D)] + [_const_spec(a.shape) for a in consts] + [st],
        out_specs=[row(GDN_QKV), st, row(GDN_Z), row(LANES), row(ATT_W), row(ATT_W),
                   pl.BlockSpec((1, tm // LANES, ATT_W, LANES), lambda b, i: (b, i, 0, 0)), keep, keep],
        out_shape=[jax.ShapeDtypeStruct((B, T, GDN_QKV), F32),
                   jax.ShapeDtypeStruct(grouped, F32),
                   jax.ShapeDtypeStruct((B, T, GDN_Z), F32),
                   jax.ShapeDtypeStruct((B, T, LANES), F32),
                   jax.ShapeDtypeStruct((B, T, ATT_W), BF16),
                   jax.ShapeDtypeStruct((B, T, ATT_W), BF16),
                   jax.ShapeDtypeStruct((B, T // LANES, ATT_W, LANES), BF16),
                   jax.ShapeDtypeStruct(keep_shape, F32),
                   jax.ShapeDtypeStruct(keep_shape, F32)],
        scratch_shapes=[pltpu.VMEM((hist, GDN_QKV), F32), pltpu.VMEM((D, n_proj), BF16)],
        compiler_params=pltpu.CompilerParams(
            dimension_semantics=("arbitrary", "arbitrary"), vmem_limit_bytes=VMEM_LIMIT),
    )(x, *consts, prev_g)
    return (act, cst.transpose(0, 2, 1, 3).reshape(n_seq, hist, GDN_QKV), *rest)


def _bmm(a, b):
    return lax.dot_general(a, b, (((2,), (1,)), ((0,), (0,))), preferred_element_type=F32)


def _bmm_nt(a, b):
    return lax.dot_general(a, b, (((2,), (2,)), ((0,), (0,))), preferred_element_type=F32)


def _bmm_tn(a, b):
    return lax.dot_general(a, b, (((1,), (1,)), ((0,), (0,))), preferred_element_type=F32)


def _gdn_kernel(act_ref, z_ref, ba_ref, s0_ref, alog_ref, dtb_ref, gnw_ref,
                o_ref, sout_ref, s_scr, *, C, cps, bs, n_steps):
    j = pl.program_id(1)
    R = C * cps
    H = GDN_HEADS
    nh = bs * H

    @pl.when(j == 0)
    def _():
        s_scr[...] = s0_ref[...].reshape(nh, GDN_DK, GDN_DV)

    ri = lax.broadcasted_iota(jnp.int32, (C, C), 0)
    ci = lax.broadcasted_iota(jnp.int32, (C, C), 1)
    incl = ri >= ci
    strict = ri > ci
    rr = lax.broadcasted_iota(jnp.int32, (R, R), 0)
    cc = lax.broadcasted_iota(jnp.int32, (R, R), 1)
    cum16 = jnp.where((rr >= cc) & (rr // C == cc // C), 1.0, 0.0).astype(BF16)
    alog = alog_ref[...]
    dtb = dtb_ref[...]

    acts, sigs, Gs, GTs = [], [], [], []
    for s in range(bs):
        acts.append(act_ref.at[s])
        ba = ba_ref[s]
        sigs.append(_sigmoid(ba))
        G = _dot_exact_lhs(cum16, -jnp.exp(alog) * _softplus(ba + dtb))
        Gs.append(G)
        GTs.append(G.T)

    order = [(c, s, h) for c in range(cps) for s in range(bs) for h in range(H)]
    rows = lambda c: slice(c * C, (c + 1) * C)

    def tiles(slabs, col0, width):
        return jnp.stack([slabs[s][rows(c), col0 + h * width:col0 + (h + 1) * width] for c, s, h in order])

    q = tiles(acts, 0, GDN_DK)
    k = tiles(acts, GDN_QK, GDN_DK)
    v = tiles(acts, 2 * GDN_QK, GDN_DV)
    beta = tiles(sigs, 0, 1)
    Gc = tiles(Gs, H, 1)
    Gr = jnp.stack([GTs[s][H + h:H + h + 1, rows(c)] for c, s, h in order])
    Gl = Gc[:, C - 1:C, :]

    nb = len(order)
    li = lax.broadcasted_iota(jnp.int32, (2 * GDN_DK, 2 * GDN_DK), 0) // GDN_DK
    lj = lax.broadcasted_iota(jnp.int32, (2 * GDN_DK, 2 * GDN_DK), 1) // GDN_DK
    ones2 = jnp.where(li == lj, 1.0, 0.0).astype(BF16)
    sq = jnp.concatenate([q * q, k * k], axis=-1).reshape(nb * C, 2 * GDN_DK)
    sq_hi = sq.astype(BF16)
    sq_lo = (sq - sq_hi.astype(F32)).astype(BF16)
    norms = (_dot(sq_hi, ones2) + _dot(sq_lo, ones2)).reshape(nb, C, 2 * GDN_DK)
    qn = q * lax.rsqrt(norms[:, :, :GDN_DK] + EPS) * (GDN_DK ** -0.5)
    kn = k * lax.rsqrt(norms[:, :, GDN_DK:] + EPS)
    eG = jnp.exp(Gc)
    gam = jnp.where(incl, jnp.exp(jnp.where(incl, Gc - Gr, 0.0)), 0.0)
    kb = kn * beta
    kn16 = kn.astype(BF16)
    aq = _bmm_nt(jnp.concatenate([kb, qn], axis=1).astype(BF16), kn16)
    A = jnp.where(strict, aq[:, :C] * gam, 0.0)
    QK16 = (aq[:, C:] * gam).astype(BF16)

    n_joint = C.bit_length() - 2
    A16 = A.astype(BF16)
    N = -A
    Q = _bmm(A16, A16)
    for it in range(n_joint):
        Q16 = Q.astype(BF16)
        if it == n_joint - 1:
            N = N + Q + _bmm(N.astype(BF16), Q16)
        else:
            nq = _bmm(jnp.concatenate([N, Q], axis=1).astype(BF16), Q16)
            N = N + Q + nq[:, :C]
            Q = nq[:, C:]
    rhs = jnp.concatenate([v * beta, kb * eG], axis=-1)
    sol = rhs + _bmm(N.astype(BF16), rhs.astype(BF16))
    u = sol[:, :, :GDN_DV]
    wq16 = jnp.concatenate([sol[:, :, GDN_DV:], qn * eG], axis=1).astype(BF16)
    kg16 = (kn * jnp.exp(Gl - Gc)).astype(BF16)
    dl = jnp.exp(Gl)

    S = s_scr[...]
    o_parts = []
    for c in range(cps):
        sl = slice(c * nh, (c + 1) * nh)
        r = _bmm(wq16[sl], S.astype(BF16))
        vn16 = (u[sl] - r[:, :C]).astype(BF16)
        o_parts.append(r[:, C:] + _bmm(QK16[sl], vn16))
        S = S * dl[sl] + _bmm_tn(kg16[sl], vn16)
    s_scr[...] = S

    on = _rms(jnp.concatenate(o_parts, axis=0)) * gnw_ref[...]
    gates = []
    for s in range(bs):
        zs = z_ref[s]
        gates.append(_silu(zs))
    for idx, (c, s, h) in enumerate(order):
        cols = slice(h * GDN_DV, (h + 1) * GDN_DV)
        o_ref[s, rows(c), cols] = (on[idx] * gates[s][rows(c), cols]).astype(o_ref.dtype)

    @pl.when(j == n_steps - 1)
    def _():
        sout_ref[...] = S.reshape(bs, H, GDN_DK, GDN_DV)


def _gdn(act, z, ba, s0, alog, dtb, gnw, *, C, cps, bs):
    B, T, _ = act.shape
    R = C * cps
    assert B % bs == 0 and T % R == 0
    n_steps = T // R
    row = lambda w: pl.BlockSpec((bs, R, w), lambda b, j: (b, j, 0))
    st = pl.BlockSpec((bs, GDN_HEADS, GDN_DK, GDN_DV), lambda b, j: (b, 0, 0, 0))
    return pl.pallas_call(
        functools.partial(_gdn_kernel, C=C, cps=cps, bs=bs, n_steps=n_steps),
        name="gdn",
        grid=(B // bs, n_steps),
        in_specs=[row(GDN_QKV), row(GDN_Z), row(LANES), st,
                  _const_spec(alog.shape), _const_spec(dtb.shape), _const_spec(gnw.shape)],
        out_specs=[row(GDN_Z), st],
        out_shape=[jax.ShapeDtypeStruct((B, T, GDN_Z), BF16),
                   jax.ShapeDtypeStruct((B, GDN_HEADS, GDN_DK, GDN_DV), F32)],
        scratch_shapes=[pltpu.VMEM((bs * GDN_HEADS, GDN_DK, GDN_DV), F32)],
        compiler_params=pltpu.CompilerParams(
            dimension_semantics=("arbitrary", "arbitrary"), vmem_limit_bytes=VMEM_LIMIT),
    )(act, z, ba, s0, alog, dtb, gnw)


def _attn_prompt_kernel(q_ref, k_ref, vt_ref, bias_ref, nw_ref, o_ref, *, qt):
    j = pl.program_id(1)
    tq = 2 * CHUNK
    n_kb = BAND_CHUNKS // 2 + 1
    n_pairs = ATT_HEADS // 2
    lane_head = lax.broadcasted_iota(jnp.int32, (tq, LANES), 1) // ATT_DH

    def tile(i, masked):
        m = j * qt + i
        q = q_ref[0, i * tq:(i + 1) * tq, :]
        wt = []
        for p in range(n_pairs):
            qp = q[:, p * LANES:(p + 1) * LANES]
            zero = jnp.zeros_like(qp)
            wt.append(jnp.concatenate([jnp.where(lane_head == 0, qp, zero),
                                       jnp.where(lane_head == 1, qp, zero)], axis=0))
        wt = jnp.stack(wt)
        firsts = [m - (n_kb - 1) + jb for jb in range(n_kb)]
        blks = [jnp.maximum(f, 0) for f in firsts]
        k_all = jnp.concatenate(
            [jnp.stack([k_ref[0, blk][:, p * LANES:(p + 1) * LANES] for p in range(n_pairs)]) for blk in blks],
            axis=1)
        v_all = jnp.concatenate([vt_ref[0, blk].reshape(n_pairs, LANES, tq) for blk in blks], axis=2)
        s_all = _bmm_nt(k_all, wt)
        s_parts = []
        for jb in range(n_kb):
            s = s_all[:, jb * tq:(jb + 1) * tq, :]
            if jb not in _FLAT_BIAS_BLOCKS:
                s = s + bias_ref[:, jb * tq:(jb + 1) * tq, :]
            if masked and jb < n_kb - 1:
                s = jnp.where(firsts[jb] >= 0, s, -jnp.inf)
            s_parts.append(s)
        st = jnp.concatenate(s_parts, axis=1)
        pt16 = jnp.exp2(st - jnp.max(st, axis=1, keepdims=True)).astype(BF16)
        acc = _bmm(jnp.concatenate([v_all, jnp.ones((n_pairs, 2 * SUBLANES, n_kb * tq), BF16)], axis=1), pt16)
        inv = 1.0 / acc[:, LANES:LANES + 1, :]
        ot = jnp.concatenate([acc[:, :ATT_DH, :tq] * inv[:, :, :tq],
                              acc[:, ATT_DH:LANES, tq:] * inv[:, :, tq:]], axis=1)
        oh = ot.reshape(ATT_HEADS, ATT_DH, tq)
        on = oh * lax.rsqrt(jnp.mean(oh * oh, axis=1, keepdims=True) + EPS) * nw_ref[...]
        on = on.reshape(n_pairs, LANES, tq)
        for p in range(n_pairs):
            o_ref[0, i * tq:(i + 1) * tq, p * LANES:(p + 1) * LANES] = on[p].T.astype(o_ref.dtype)

    near_start = j * qt < n_kb - 1

    @pl.when(near_start)
    def _():
        for i in range(qt):
            tile(i, True)

    @pl.when(jnp.logical_not(near_start))
    def _():
        for i in range(qt):
            tile(i, False)


def _attn_prompt(q, k, vt, bias, nw, *, qt):
    B, T, _ = q.shape
    tq = 2 * CHUNK
    n_steps = T // (qt * tq)
    k4 = k.reshape(B, T // tq, tq, ATT_W)
    row = pl.BlockSpec((1, qt * tq, ATT_W), lambda b, j: (b, j, 0))
    return pl.pallas_call(
        functools.partial(_attn_prompt_kernel, qt=qt),
        name="attn_prompt",
        grid=(B, n_steps),
        in_specs=[row,
                  pl.BlockSpec((1, T // tq, tq, ATT_W), lambda b, j: (b, 0, 0, 0)),
                  pl.BlockSpec((1, T // tq, ATT_W, tq), lambda b, j: (b, 0, 0, 0)),
                  _const_spec(bias.shape), _const_spec(nw.shape)],
        out_specs=row,
        out_shape=jax.ShapeDtypeStruct((B, T, ATT_W), BF16),
        compiler_params=pltpu.CompilerParams(
            dimension_semantics=("arbitrary", "arbitrary"), vmem_limit_bytes=VMEM_LIMIT),
    )(q, k4, vt, bias, nw)


def _attn_sample_kernel(q_ref, kc_ref, vc_ref, kn_ref, vn_ref, bc_ref, bn_ref, nw_ref, o_ref, *, n_seq):
    by_head = lambda a: jnp.stack([a[:, h * ATT_DH:(h + 1) * ATT_DH] for h in range(ATT_HEADS)])
    for s in range(n_seq):
        kc = kc_ref[s].astype(BF16)
        vc = vc_ref[s].astype(BF16)
        q = by_head(q_ref[s])
        kn = by_head(kn_ref[s])
        vn = by_head(vn_ref[s]).astype(BF16)
        s_c = _bmm(q, kc) + bc_ref[...]
        s_n = _bmm_nt(q, kn) + bn_ref[...]
        m = jnp.maximum(jnp.max(s_c, axis=-1, keepdims=True), jnp.max(s_n, axis=-1, keepdims=True))
        p_c = jnp.exp2(s_c - m)
        p_n = jnp.exp2(s_n - m)
        l = jnp.sum(p_c, axis=-1, keepdims=True) + jnp.sum(p_n, axis=-1, keepdims=True)
        o = (_bmm_nt(p_c.astype(BF16), vc) + _bmm(p_n.astype(BF16), vn)) * (1.0 / l)
        on = _rms(o) * nw_ref[...]
        o_ref[s] = jnp.concatenate([on[h] for h in range(ATT_HEADS)], axis=-1).astype(o_ref.dtype)


def _attn_sample(q, kc, vc, kn, vn, bias_c, bias_n, nw, *, n_seq):
    B, T, _ = q.shape
    new = pl.BlockSpec((n_seq, T, ATT_W), lambda b: (b, 0, 0))
    cache = pl.BlockSpec((n_seq,) + kc.shape[1:], lambda b: (b, 0, 0, 0))
    return pl.pallas_call(
        functools.partial(_attn_sample_kernel, n_seq=n_seq),
        name="attn_sample",
        grid=(B // n_seq,),
        in_specs=[new, cache, cache, new, new, _const_spec(bias_c.shape), _const_spec(bias_n.shape),
                  _const_spec(nw.shape)],
        out_specs=new,
        out_shape=jax.ShapeDtypeStruct((B, T, ATT_W), BF16),
        compiler_params=pltpu.CompilerParams(
            dimension_semantics=("arbitrary",), vmem_limit_bytes=VMEM_LIMIT),
    )(q, kc, vc, kn, vn, bias_c, bias_n, nw)


def _delayed(g, k, S, R, hist_row):
    rolled = pltpu.roll(g, k, axis=0)
    sub = lax.broadcasted_iota(jnp.int32, (SUBLANES, g.shape[1]), 0)
    parts = []
    for s in range(S):
        head = rolled[s * R:s * R + SUBLANES]
        for i in range(k):
            head = jnp.where(sub == i, hist_row(s, i), head)
        parts += [head, rolled[s * R + SUBLANES:(s + 1) * R]]
    return jnp.concatenate(parts, axis=0)


def _mix_ffn_kernel(x_ref, oa_ref, ob_ref, wo_ref, nmp_ref, nfp_ref, wgu_ref, cw_ref, cb_ref,
                    wd_ref, nfo_ref, prev_ref, y_ref, st_ref, carry,
                    *, S, R, tps, d_ff):
    t = pl.program_id(0)
    hist = FFN_CONV - 1
    half = oa_ref.shape[-1]
    mix = _dot(oa_ref[...], wo_ref[:half, :]) + _dot(ob_ref[...], wo_ref[half:, :])
    x1 = x_ref[...] + _rms(mix) * nmp_ref[...]
    u2 = (_rms(x1) * nfp_ref[...]).astype(BF16)

    if tps > 1:
        @pl.when(t % tps == 0)
        def _():
            for i in range(hist):
                carry[i:i + 1, :] = prev_ref[0, i]

    gu = _dot(u2, wgu_ref[...])
    g = gu[:, :d_ff]
    up = gu[:, d_ff:]
    if tps > 1:
        history = lambda s, i: carry[i:i + 1, :]
    else:
        history = lambda s, i: prev_ref[0, i, s:s + 1, :]
    cw = cw_ref[...]
    conv = cb_ref[...] + cw[hist:hist + 1] * g
    for k in range(1, FFN_CONV):
        conv = conv + cw[hist - k:hist - k + 1] * _delayed(
            g, k, S, R, lambda s, i, k=k: history(s, hist - k + i))
    for s in range(S):
        for i in range(hist):
            row = g[(s + 1) * R - hist + i:(s + 1) * R - hist + i + 1]
            if tps > 1:
                carry[i:i + 1, :] = row
            else:
                st_ref[0, i, s:s + 1, :] = row
    if tps > 1:
        for i in range(hist):
            st_ref[0, i] = carry[i:i + 1, :]
    hid = (_twice_gelu_tanh(conv) * up).astype(BF16)
    y_ref[...] = x1 + _rms(_dot(hid, wd_ref[...])) * nfo_ref[...]


def _mix_ffn(x, oa, ob, wo, nmp, nfp, wgu, cw, cb, wd, nfo, prev, *, S, R):
    M, D = x.shape
    d_ff = wd.shape[0]
    tm = S * R
    n_seq = prev.shape[0]
    hist = FFN_CONV - 1
    tps = M // (n_seq * R)
    assert S == 1 or tps == 1
    grouped = (n_seq // S, hist, S, d_ff)
    prev_g = prev.reshape(n_seq // S, S, hist, d_ff).transpose(0, 2, 1, 3)
    row = lambda w: pl.BlockSpec((tm, w), lambda t: (t, 0))
    st = pl.BlockSpec((1,) + grouped[1:], lambda t: (t // tps, 0, 0, 0))
    consts = [wo, nmp, nfp, wgu, cw, cb, wd, nfo]
    y, st_g = pl.pallas_call(
        functools.partial(_mix_ffn_kernel, S=S, R=R, tps=tps, d_ff=d_ff),
        name="mix_ffn",
        grid=(M // tm,),
        in_specs=[row(D), row(oa.shape[1]), row(ob.shape[1])] + [_const_spec(a.shape) for a in consts] + [st],
        out_specs=[row(D), st],
        out_shape=[jax.ShapeDtypeStruct((M, D), F32), jax.ShapeDtypeStruct(grouped, F32)],
        scratch_shapes=[pltpu.VMEM((hist, d_ff), F32)],
        compiler_params=pltpu.CompilerParams(
            dimension_semantics=("arbitrary",), vmem_limit_bytes=VMEM_LIMIT),
    )(x, oa, ob, *consts, prev_g)
    return y, st_g.transpose(0, 2, 1, 3).reshape(n_seq, hist, d_ff)


def _lane_row(vals, offset):
    return jnp.zeros((1, LANES), F32).at[0, offset:offset + vals.shape[0]].set(vals.astype(F32))


def _bias_kernel(r_ref, bp_ref, bc_ref, bn_ref, *, lc, ts):
    tq = 2 * CHUNK
    key = lax.broadcasted_iota(jnp.int32, (tq, tq), 0)
    query_chunk = lax.broadcasted_iota(jnp.int32, (tq, tq), 1) // CHUNK
    for h in range(r_ref.shape[0]):
        p, e = divmod(h, 2)
        for jb in range(r_ref.shape[1]):
            row = jnp.broadcast_to(r_ref[h, jb], (tq, 2 * tq))
            blk = pltpu.roll(row, 0, 1, stride=1, stride_axis=0)[:, :tq]
            key_in_band = jb * tq + key - query_chunk * CHUNK
            valid = (key_in_band >= 0) & (key_in_band < (BAND_CHUNKS + 1) * CHUNK)
            far = r_ref[h, 0][:, 0:1]
            bp_ref[p, jb * tq:(jb + 1) * tq, e * tq:(e + 1) * tq] = jnp.where(valid, blk - far, -jnp.inf)
            by_query = blk.T
            if (jb + 1) * tq <= lc:
                bc_ref[h, :, jb * tq:(jb + 1) * tq] = by_query[:ts]
            else:
                bn_ref[h] = by_query[:ts, :ts]


def _band_biases(table, lc, ts):
    n_heads = table.shape[0]
    tq = 2 * CHUNK
    n_kb = BAND_CHUNKS // 2 + 1
    assert lc == (n_kb - 1) * tq and ts <= tq
    starts = [WINDOW - tq * jb + MAX_REL - half * tq for jb in range(n_kb) for half in (0, 1)]
    pad_l = max(0, -min(starts))
    pad_r = max(0, max(starts) + tq - table.shape[1])
    ext = jnp.pad(table, ((0, 0), (pad_l, pad_r)), mode="edge")
    rows = jnp.concatenate([ext[:, s + pad_l:s + pad_l + tq] for s in starts], axis=1)
    rows = rows.reshape(n_heads, n_kb, 1, 2 * tq)
    return pl.pallas_call(
        functools.partial(_bias_kernel, lc=lc, ts=ts),
        name="band_bias",
        out_shape=[jax.ShapeDtypeStruct((n_heads // 2, n_kb * tq, 2 * tq), F32),
                   jax.ShapeDtypeStruct((n_heads, ts, lc), F32),
                   jax.ShapeDtypeStruct((n_heads, ts, ts), F32)],
    )(rows)


def _layer(xp, xs, cache_k, cache_v, s_delta, s_qkv, s_ffn, lw):
    (norm_mix_pre, w_in, qkv_conv_w, a_log, dt_bias, gdn_norm_w, rel_bias, attn_norm_w, w_out,
     norm_mix_post, norm_ffn_pre, w_gate_up, ffn_conv_w, ffn_conv_b, w_down, norm_ffn_post) = lw
    Bp, Tp, D = xp.shape
    Bs, Ts, _ = xs.shape
    d_ff = w_down.shape[0]

    c1 = GDN_QKV + GDN_Z
    c2 = c1 + 2 * GDN_HEADS
    wt = jnp.swapaxes(w_in, 0, 1).astype(BF16)
    w_proj = (wt[:c1], jnp.pad(wt[c1:c2], ((0, LANES - 2 * GDN_HEADS), (0, 0))), wt[c2:c2 + 2 * ATT_W])
    wkt = wt[c2 + ATT_W:c2 + 2 * ATT_W]
    wvt = wt[c2 + 2 * ATT_W:]
    nmix = norm_mix_pre.reshape(1, D)
    cw_qkv = jnp.pad(0.5 * qkv_conv_w, ((0, SUBLANES - GDN_CONV), (0, 0)))
    alog = _lane_row(a_log, GDN_HEADS)
    dtb = _lane_row(dt_bias, GDN_HEADS)
    gnw = gdn_norm_w.reshape(1, GDN_DV)
    anw = attn_norm_w.astype(F32).reshape(1, ATT_DH)
    anw_col = jnp.broadcast_to(attn_norm_w.astype(F32)[:, None], (ATT_DH, 2 * CHUNK))
    lc = cache_k.shape[1]
    bias_p, bias_c, bias_n = _band_biases(rel_bias.astype(F32) * LOG2E, lc, Ts)
    wo = w_out.astype(BF16)
    wgu = jnp.concatenate([w_gate_up[:, :d_ff], 0.5 * w_gate_up[:, d_ff:]], axis=1).astype(BF16)
    wd = w_down.astype(BF16)
    cw_ffn = jnp.pad(ffn_conv_w, ((0, SUBLANES - FFN_CONV), (0, 0)))
    cb = ffn_conv_b.reshape(1, d_ff)
    nmp = norm_mix_post.reshape(1, D)
    nfp = norm_ffn_pre.reshape(1, D)
    nfo = norm_ffn_post.reshape(1, D)

    def group(x, keep_all, gdn_prev, gdn_s0, gdn_c, gdn_cps, gdn_bs, ffn_prev, S, R, attn):
        B, T, _ = x.shape
        xi = x if not keep_all else x.reshape(1, B * T, D)
        act, qkv_state, z, ba, q, k, vt, kf, vf = _inproj(xi, nmix, w_proj, wkt, wvt, cw_qkv, gdn_prev,
                                                          keep_all=keep_all)
        rs = lambda a: a.reshape(B, T, a.shape[-1])
        act, z, ba, q, k = map(rs, (act, z, ba, q, k))
        oa, s_new = _gdn(act, z, ba, gdn_s0, alog, dtb, gnw, C=gdn_c, cps=gdn_cps, bs=gdn_bs)
        ob = attn(q, k, vt, vf)
        y, ffn_state = _mix_ffn(x.reshape(B * T, D), oa.reshape(B * T, GDN_Z), ob.reshape(B * T, ATT_W),
                                wo, nmp, nfp, wgu, cw_ffn, cb, wd, nfo, ffn_prev,
                                S=S, R=R)
        if keep_all:
            k_rows = kf.reshape(B, T, ATT_HEADS, ATT_DH)
            v_rows = vf.reshape(B, T, ATT_HEADS, ATT_DH)
        else:
            k_rows = jnp.transpose(kf, (0, 3, 1, 2))
            v_rows = jnp.transpose(vf, (0, 3, 1, 2))
        return y.reshape(B, T, D), k_rows, v_rows, s_new, qkv_state, ffn_state

    out_p = group(
        xp, False, jnp.zeros((Bp, GDN_CONV - 1, GDN_QKV), F32),
        jnp.zeros((Bp, GDN_HEADS, GDN_DK, GDN_DV), F32), CHUNK, 4, 4,
        jnp.zeros((Bp, FFN_CONV - 1, d_ff), F32), 1, ROW_TILE,
        lambda q, k, vt, vf: _attn_prompt(q, k, vt, bias_p, anw_col, qt=4))

    kc_t = jnp.transpose(cache_k, (0, 2, 3, 1))
    vc_t = jnp.transpose(cache_v, (0, 2, 3, 1))
    out_s = group(
        xs, True, s_qkv, s_delta, Ts, 1, 8, s_ffn, ROW_TILE // Ts, Ts,
        lambda q, k, vt, vf: _attn_sample(q, kc_t, vc_t, k, vf.reshape(Bs, Ts, ATT_W), bias_c, bias_n, anw,
                                          n_seq=4))
    return out_p, out_s


def kernel(x_prompt, x_sample, cache_band_k, cache_band_v, state_delta, state_qkv_conv, state_ffn_conv, norm_mix_pre, w_in, qkv_conv_w, a_log, dt_bias, gdn_norm_w, rel_bias, attn_norm_w, w_out, norm_mix_post, norm_ffn_pre, w_gate_up, ffn_conv_w, ffn_conv_b, w_down, norm_ffn_post):
    weights = (norm_mix_pre, w_in, qkv_conv_w, a_log, dt_bias, gdn_norm_w, rel_bias, attn_norm_w, w_out,
               norm_mix_post, norm_ffn_pre, w_gate_up, ffn_conv_w, ffn_conv_b, w_down, norm_ffn_post)
    depth = w_in.shape[0]
    xp, xs = x_prompt, x_sample
    outs_p, outs_s = [], []
    for l in range(depth):
        lw = tuple(w[l] for w in weights)
        op, os_ = _layer(xp, xs, cache_band_k[l], cache_band_v[l], state_delta[l], state_qkv_conv[l],
                         state_ffn_conv[l], lw)
        xp, xs = op[0], os_[0]
        outs_p.append(op[1:])
        outs_s.append(os_[1:])
    stack = lambda outs, i: jnp.stack([o[i] for o in outs], axis=0)
    return (xp, xs) + tuple(stack(outs_p, i) for i in range(5)) + tuple(stack(outs_s, i) for i in range(5))
```

```python
import functools

import jax
import jax.numpy as jnp
from jax import lax
from jax.experimental import pallas as pl
from jax.experimental.pallas import tpu as pltpu

F32 = jnp.float32
BF16 = jnp.bfloat16

EPS = 1e-6
CHUNK = 64
GDN_HEADS = 4
GDN_DK = 128
GDN_DV = 128
GDN_CONV = 4
ATT_HEADS = 8
ATT_DH = 64
BAND_CHUNKS = 8
WINDOW = BAND_CHUNKS * CHUNK
MAX_REL = 128
FFN_CONV = 3

GDN_QK = GDN_HEADS * GDN_DK
GDN_QKV = GDN_HEADS * (2 * GDN_DK + GDN_DV)
GDN_Z = GDN_HEADS * GDN_DV
ATT_W = ATT_HEADS * ATT_DH
LANES = 128
SUBLANES = 8
VMEM_LIMIT = 56 * 1024 * 1024
ROW_TILE = 512
_FLAT_BIAS_BLOCKS = tuple(
    jb for jb in range(BAND_CHUNKS // 2 + 1)
    if jb * 2 * CHUNK >= CHUNK and (jb + 1) * 2 * CHUNK <= (BAND_CHUNKS + 1) * CHUNK
    and WINDOW - (jb + 1) * 2 * CHUNK + 1 >= MAX_REL)
LOG2E = 1.4426950408889634
Q_SCALE = ATT_DH ** -0.5 * LOG2E


def _dot(a, b):
    return jnp.dot(a, b, preferred_element_type=F32)


def _dot_nt(a, b):
    return lax.dot_general(a, b, (((1,), (1,)), ((), ())), preferred_element_type=F32)


def _split3(x):
    x1 = x.astype(BF16)
    r1 = x - x1.astype(F32)
    x2 = r1.astype(BF16)
    x3 = (r1 - x2.astype(F32)).astype(BF16)
    return x1, x2, x3


def _dot_exact_lhs(a16, x):
    x1, x2, x3 = _split3(x)
    return _dot(a16, x1) + _dot(a16, x2) + _dot(a16, x3)


def _sigmoid(x):
    return 0.5 + 0.5 * jnp.tanh(0.5 * x)


def _silu_of_half(h):
    return h + h * jnp.tanh(h)


def _silu(x):
    return _silu_of_half(0.5 * x)


def _softplus(x):
    return jnp.maximum(x, 0.0) + jnp.log(1.0 + jnp.exp(-jnp.abs(x)))


def _rms(x):
    return x * lax.rsqrt(jnp.mean(x * x, axis=-1, keepdims=True) + EPS)


def _gelu_tanh(x):
    c = 0.7978845608028654
    h = 0.5 * x
    return h + h * jnp.tanh(x * (c + (0.044715 * c) * (x * x)))


def _const_spec(shape):
    n = len(shape)
    return pl.BlockSpec(shape, lambda *_: (0,) * n, pipeline_mode=pl.Buffered(1))


def _inproj_kernel(x_ref, nw_ref, wa_ref, wg_ref, wqk_ref, wkt_ref, wvt_ref, cw_ref, prev_ref,
                   act_ref, cst_ref, z_ref, ba_ref, q_ref, k_ref, vt_ref, kf_ref, vf_ref, carry, w_nn,
                   *, keep_all, n_tiles, S, R, tps):
    x = x_ref[0]
    tm = x.shape[0]
    u = (_rms(x) * nw_ref[...]).astype(BF16)

    hist = GDN_CONV - 1
    if tps > 1:
        @pl.when(pl.program_id(1) % tps == 0)
        def _():
            for r in range(hist):
                carry[r:r + 1, :] = prev_ref[0, r]
    cw = cw_ref[...]
    slabs = [slice(c * GDN_QK, (c + 1) * GDN_QK) for c in range(GDN_QKV // GDN_QK)]

    def conv_silu(pre, cols):
        if tps > 1:
            history = lambda s, r, cols=cols: carry[r:r + 1, cols]
        else:
            history = lambda s, r, cols=cols: prev_ref[0, r, s:s + 1, cols]
        half_conv = cw[hist:hist + 1, cols] * pre
        for k in range(1, GDN_CONV):
            half_conv = half_conv + cw[hist - k:hist - k + 1, cols] * _delayed(
                pre, k, S, R, lambda s, r, k=k, history=history: history(s, hist - k + r))
        act_ref[0, :, cols] = _silu_of_half(half_conv)
        for s in range(S):
            for r in range(hist):
                row = pre[(s + 1) * R - hist + r:(s + 1) * R - hist + r + 1]
                if tps > 1:
                    carry[r:r + 1, cols] = row
                else:
                    cst_ref[0, r, s:s + 1, cols] = row

    @pl.when((pl.program_id(0) == 0) & (pl.program_id(1) == 0))
    def _():
        col = 0
        for piece in (wa_ref, wg_ref, wqk_ref):
            for c in range(piece.shape[0] // LANES):
                w_nn[:, col:col + LANES] = piece[c * LANES:(c + 1) * LANES, :].T
                col += LANES

    p = _dot(u, w_nn[...])
    for cols in slabs:
        conv_silu(p[:, cols], cols)
    if tps > 1:
        for r in range(hist):
            cst_ref[0, r] = carry[r:r + 1, :]
    c0 = GDN_QKV + GDN_Z
    c1 = c0 + LANES
    z_ref[0] = p[:, GDN_QKV:c0]
    ba_ref[0] = p[:, c0:c1]
    q_ref[0] = (p[:, c1:c1 + ATT_W] * Q_SCALE).astype(BF16)
    k = p[:, c1 + ATT_W:]
    k_ref[0] = k.astype(BF16)
    vt = _dot_nt(wvt_ref[...], u)
    for jb in range(tm // LANES):
        vt_ref[0, jb] = vt[:, jb * LANES:(jb + 1) * LANES].astype(BF16)

    if keep_all:
        kf_ref[0] = k
        vf_ref[0] = _dot_nt(u, wvt_ref[...])
    else:
        @pl.when(pl.program_id(1) == n_tiles - 1)
        def _():
            kf_ref[0] = _dot_nt(wkt_ref[...], u).reshape(ATT_HEADS, ATT_DH, tm)
            vf_ref[0] = vt.reshape(ATT_HEADS, ATT_DH, tm)


def _inproj(x, nw, w_pieces, wkt, wvt, cw, prev, *, keep_all):
    B, T, D = x.shape
    tm = ROW_TILE
    nt = T // tm
    n_seq = prev.shape[0]
    hist = GDN_CONV - 1
    R = min(B * T // n_seq, tm)
    S = tm // R
    tps = B * T // (n_seq * R)
    assert S == 1 or tps == 1
    grouped = (n_seq // S, hist, S, GDN_QKV)
    prev_g = prev.reshape(n_seq // S, S, hist, GDN_QKV).transpose(0, 2, 1, 3)
    groups_per_b = nt // tps
    st = pl.BlockSpec((1,) + grouped[1:], lambda b, i: (b * groups_per_b + i // tps, 0, 0, 0))
    row = lambda w: pl.BlockSpec((1, tm, w), lambda b, i: (b, i, 0))
    if keep_all:
        keep = row(ATT_W)
        keep_shape = (B, T, ATT_W)
    else:
        keep = pl.BlockSpec((1, ATT_HEADS, ATT_DH, tm), lambda b, i: (b, 0, 0, 0))
        keep_shape = (B, ATT_HEADS, ATT_DH, tm)
    consts = [nw, *w_pieces, wkt, wvt, cw]
    n_proj = sum(w.shape[0] for w in w_pieces)
    act, cst, *rest = pl.pallas_call(
        functools.partial(_inproj_kernel, keep_all=keep_all, n_tiles=nt, S=S, R=R, tps=tps),
        name="inproj",
        grid=(B, nt),
        in_specs=[row(D)] + [_const_spec(a.shape) for a in consts] + [st],
        out_specs=[row(GDN_QKV), st, row(GDN_Z), row(LANES), row(ATT_W), row(ATT_W),
                   pl.BlockSpec((1, tm // LANES, ATT_W, LANES), lambda b, i: (b, i, 0, 0)), keep, keep],
        out_shape=[jax.ShapeDtypeStruct((B, T, GDN_QKV), F32),
                   jax.ShapeDtypeStruct(grouped, F32),
                   jax.ShapeDtypeStruct((B, T, GDN_Z), F32),
                   jax.ShapeDtypeStruct((B, T, LANES), F32),
                   jax.ShapeDtypeStruct((B, T, ATT_W), BF16),
                   jax.ShapeDtypeStruct((B, T, ATT_W), BF16),
                   jax.ShapeDtypeStruct((B, T // LANES, ATT_W, LANES), BF16),
                   jax.ShapeDtypeStruct(keep_shape, F32),
                   jax.ShapeDtypeStruct(keep_shape, F32)],
        scratch_shapes=[pltpu.VMEM((hist, GDN_QKV), F32), pltpu.VMEM((D, n_proj), BF16)],
        compiler_params=pltpu.CompilerParams(
            dimension_semantics=("arbitrary", "arbitrary"), vmem_limit_bytes=VMEM_LIMIT),
    )(x, *consts, prev_g)
    return (act, cst.transpose(0, 2, 1, 3).reshape(n_seq, hist, GDN_QKV), *rest)


def _bmm(a, b):
    return lax.dot_general(a, b, (((2,), (1,)), ((0,), (0,))), preferred_element_type=F32)


def _bmm_nt(a, b):
    return lax.dot_general(a, b, (((2,), (2,)), ((0,), (0,))), preferred_element_type=F32)


def _bmm_tn(a, b):
    return lax.dot_general(a, b, (((1,), (1,)), ((0,), (0,))), preferred_element_type=F32)


def _gdn_kernel(act_ref, z_ref, ba_ref, s0_ref, alog_ref, dtb_ref, gnw_ref,
                o_ref, sout_ref, s_scr, *, C, cps, bs, n_steps):
    j = pl.program_id(1)
    R = C * cps
    H = GDN_HEADS
    nh = bs * H

    @pl.when(j == 0)
    def _():
        s_scr[...] = s0_ref[...].reshape(nh, GDN_DK, GDN_DV)

    ri = lax.broadcasted_iota(jnp.int32, (C, C), 0)
    ci = lax.broadcasted_iota(jnp.int32, (C, C), 1)
    incl = ri >= ci
    strict = ri > ci
    rr = lax.broadcasted_iota(jnp.int32, (R, R), 0)
    cc = lax.broadcasted_iota(jnp.int32, (R, R), 1)
    cum16 = jnp.where((rr >= cc) & (rr // C == cc // C), 1.0, 0.0).astype(BF16)
    alog = alog_ref[...]
    dtb = dtb_ref[...]

    acts, sigs, Gs, GTs = [], [], [], []
    for s in range(bs):
        acts.append(act_ref.at[s])
        ba = ba_ref[s]
        sigs.append(_sigmoid(ba))
        G = _dot_exact_lhs(cum16, -jnp.exp(alog) * _softplus(ba + dtb))
        Gs.append(G)
        GTs.append(G.T)

    order = [(c, s, h) for c in range(cps) for s in range(bs) for h in range(H)]
    rows = lambda c: slice(c * C, (c + 1) * C)

    def tiles(slabs, col0, width):
        return jnp.stack([slabs[s][rows(c), col0 + h * width:col0 + (h + 1) * width] for c, s, h in order])

    q = tiles(acts, 0, GDN_DK)
    k = tiles(acts, GDN_QK, GDN_DK)
    v = tiles(acts, 2 * GDN_QK, GDN_DV)
    beta = tiles(sigs, 0, 1)
    Gc = tiles(Gs, H, 1)
    Gr = jnp.stack([GTs[s][H + h:H + h + 1, rows(c)] for c, s, h in order])
    Gl = Gc[:, C - 1:C, :]

    nb = len(order)
    li = lax.broadcasted_iota(jnp.int32, (2 * GDN_DK, 2 * GDN_DK), 0) // GDN_DK
    lj = lax.broadcasted_iota(jnp.int32, (2 * GDN_DK, 2 * GDN_DK), 1) // GDN_DK
    ones2 = jnp.where(li == lj, 1.0, 0.0).astype(BF16)
    sq = jnp.concatenate([q * q, k * k], axis=-1).reshape(nb * C, 2 * GDN_DK)
    sq_hi = sq.astype(BF16)
    sq_lo = (sq - sq_hi.astype(F32)).astype(BF16)
    norms = (_dot(sq_hi, ones2) + _dot(sq_lo, ones2)).reshape(nb, C, 2 * GDN_DK)
    qn = q * lax.rsqrt(norms[:, :, :GDN_DK] + EPS) * (GDN_DK ** -0.5)
    kn = k * lax.rsqrt(norms[:, :, GDN_DK:] + EPS)
    eG = jnp.exp(Gc)
    gam = jnp.where(incl, jnp.exp(jnp.where(incl, Gc - Gr, 0.0)), 0.0)
    kb = kn * beta
    kn16 = kn.astype(BF16)
    aq = _bmm_nt(jnp.concatenate([kb, qn], axis=1).astype(BF16), kn16)
    A = jnp.where(strict, aq[:, :C] * gam, 0.0)
    QK16 = (aq[:, C:] * gam).astype(BF16)

    n_joint = C.bit_length() - 2
    A16 = A.astype(BF16)
    N = -A
    Q = _bmm(A16, A16)
    for it in range(n_joint):
        Q16 = Q.astype(BF16)
        if it == n_joint - 1:
            N = N + Q + _bmm(N.astype(BF16), Q16)
        else:
            nq = _bmm(jnp.concatenate([N, Q], axis=1).astype(BF16), Q16)
            N = N + Q + nq[:, :C]
            Q = nq[:, C:]
    rhs = jnp.concatenate([v * beta, kb * eG], axis=-1)
    sol = rhs + _bmm(N.astype(BF16), rhs.astype(BF16))
    u = sol[:, :, :GDN_DV]
    wq16 = jnp.concatenate([sol[:, :, GDN_DV:], qn * eG], axis=1).astype(BF16)
    kg16 = (kn * jnp.exp(Gl - Gc)).astype(BF16)
    dl = jnp.exp(Gl)

    S = s_scr[...]
    o_parts = []
    for c in range(cps):
        sl = slice(c * nh, (c + 1) * nh)
        r = _bmm(wq16[sl], S.astype(BF16))
        vn16 = (u[sl] - r[:, :C]).astype(BF16)
        o_parts.append(r[:, C:] + _bmm(QK16[sl], vn16))
        S = S * dl[sl] + _bmm_tn(kg16[sl], vn16)
    s_scr[...] = S

    on = _rms(jnp.concatenate(o_parts, axis=0)) * gnw_ref[...]
    gates = []
    for s in range(bs):
        zs = z_ref[s]
        gates.append(_silu(zs))
    for idx, (c, s, h) in enumerate(order):
        cols = slice(h * GDN_DV, (h + 1) * GDN_DV)
        o_ref[s, rows(c), cols] = (on[idx] * gates[s][rows(c), cols]).astype(o_ref.dtype)

    @pl.when(j == n_steps - 1)
    def _():
        sout_ref[...] = S.reshape(bs, H, GDN_DK, GDN_DV)


def _gdn(act, z, ba, s0, alog, dtb, gnw, *, C, cps, bs):
    B, T, _ = act.shape
    R = C * cps
    assert B % bs == 0 and T % R == 0
    n_steps = T // R
    row = lambda w: pl.BlockSpec((bs, R, w), lambda b, j: (b, j, 0))
    st = pl.BlockSpec((bs, GDN_HEADS, GDN_DK, GDN_DV), lambda b, j: (b, 0, 0, 0))
    return pl.pallas_call(
        functools.partial(_gdn_kernel, C=C, cps=cps, bs=bs, n_steps=n_steps),
        name="gdn",
        grid=(B // bs, n_steps),
        in_specs=[row(GDN_QKV), row(GDN_Z), row(LANES), st,
                  _const_spec(alog.shape), _const_spec(dtb.shape), _const_spec(gnw.shape)],
        out_specs=[row(GDN_Z), st],
        out_shape=[jax.ShapeDtypeStruct((B, T, GDN_Z), BF16),
                   jax.ShapeDtypeStruct((B, GDN_HEADS, GDN_DK, GDN_DV), F32)],
        scratch_shapes=[pltpu.VMEM((bs * GDN_HEADS, GDN_DK, GDN_DV), F32)],
        compiler_params=pltpu.CompilerParams(
            dimension_semantics=("arbitrary", "arbitrary"), vmem_limit_bytes=VMEM_LIMIT),
    )(act, z, ba, s0, alog, dtb, gnw)


def _attn_prompt_kernel(q_ref, k_ref, vt_ref, bias_ref, nw_ref, o_ref, *, qt):
    j = pl.program_id(1)
    tq = 2 * CHUNK
    n_kb = BAND_CHUNKS // 2 + 1
    n_pairs = ATT_HEADS // 2
    lane_head = lax.broadcasted_iota(jnp.int32, (tq, LANES), 1) // ATT_DH

    def tile(i, masked):
        m = j * qt + i
        q = q_ref[0, i * tq:(i + 1) * tq, :]
        wt = []
        for p in range(n_pairs):
            qp = q[:, p * LANES:(p + 1) * LANES]
            zero = jnp.zeros_like(qp)
            wt.append(jnp.concatenate([jnp.where(lane_head == 0, qp, zero),
                                       jnp.where(lane_head == 1, qp, zero)], axis=0))
        wt = jnp.stack(wt)
        firsts = [m - (n_kb - 1) + jb for jb in range(n_kb)]
        blks = [jnp.maximum(f, 0) for f in firsts]
        k_all = jnp.concatenate(
            [jnp.stack([k_ref[0, blk][:, p * LANES:(p + 1) * LANES] for p in range(n_pairs)]) for blk in blks],
            axis=1)
        v_all = jnp.concatenate([vt_ref[0, blk].reshape(n_pairs, LANES, tq) for blk in blks], axis=2)
        s_all = _bmm_nt(k_all, wt)
        s_parts = []
        for jb in range(n_kb):
            s = s_all[:, jb * tq:(jb + 1) * tq, :]
            if jb not in _FLAT_BIAS_BLOCKS:
                s = s + bias_ref[:, jb * tq:(jb + 1) * tq, :]
            if masked and jb < n_kb - 1:
                s = jnp.where(firsts[jb] >= 0, s, -jnp.inf)
            s_parts.append(s)
        st = jnp.concatenate(s_parts, axis=1)
        pt16 = jnp.exp2(st - jnp.max(st, axis=1, keepdims=True)).astype(BF16)
        acc = _bmm(jnp.concatenate([v_all, jnp.ones((n_pairs, 2 * SUBLANES, n_kb * tq), BF16)], axis=1), pt16)
        inv = 1.0 / acc[:, LANES:LANES + 1, :]
        ot = jnp.concatenate([acc[:, :ATT_DH, :tq] * inv[:, :, :tq],
                              acc[:, ATT_DH:LANES, tq:] * inv[:, :, tq:]], axis=1)
        oh = ot.reshape(ATT_HEADS, ATT_DH, tq)
        on = oh * lax.rsqrt(jnp.mean(oh * oh, axis=1, keepdims=True) + EPS) * nw_ref[...]
        on = on.reshape(n_pairs, LANES, tq)
        for p in range(n_pairs):
            o_ref[0, i * tq:(i + 1) * tq, p * LANES:(p + 1) * LANES] = on[p].T.astype(o_ref.dtype)

    near_start = j * qt < n_kb - 1

    @pl.when(near_start)
    def _():
        for i in range(qt):
            tile(i, True)

    @pl.when(jnp.logical_not(near_start))
    def _():
        for i in range(qt):
            tile(i, False)


def _attn_prompt(q, k, vt, bias, nw, *, qt):
    B, T, _ = q.shape
    tq = 2 * CHUNK
    n_steps = T // (qt * tq)
    k4 = k.reshape(B, T // tq, tq, ATT_W)
    row = pl.BlockSpec((1, qt * tq, ATT_W), lambda b, j: (b, j, 0))
    return pl.pallas_call(
        functools.partial(_attn_prompt_kernel, qt=qt),
        name="attn_prompt",
        grid=(B, n_steps),
        in_specs=[row,
                  pl.BlockSpec((1, T // tq, tq, ATT_W), lambda b, j: (b, 0, 0, 0)),
                  pl.BlockSpec((1, T // tq, ATT_W, tq), lambda b, j: (b, 0, 0, 0)),
                  _const_spec(bias.shape), _const_spec(nw.shape)],
        out_specs=row,
        out_shape=jax.ShapeDtypeStruct((B, T, ATT_W), BF16),
        compiler_params=pltpu.CompilerParams(
            dimension_semantics=("arbitrary", "arbitrary"), vmem_limit_bytes=VMEM_LIMIT),
    )(q, k4, vt, bias, nw)


def _attn_sample_kernel(q_ref, kc_ref, vc_ref, kn_ref, vn_ref, bc_ref, bn_ref, nw_ref, o_ref, *, n_seq):
    by_head = lambda a: jnp.stack([a[:, h * ATT_DH:(h + 1) * ATT_DH] for h in range(ATT_HEADS)])
    for s in range(n_seq):
        kc = kc_ref[s].astype(BF16)
        vc = vc_ref[s].astype(BF16)
        q = by_head(q_ref[s])
        kn = by_head(kn_ref[s])
        vn = by_head(vn_ref[s]).astype(BF16)
        s_c = _bmm(q, kc) + bc_ref[...]
        s_n = _bmm_nt(q, kn) + bn_ref[...]
        m = jnp.maximum(jnp.max(s_c, axis=-1, keepdims=True), jnp.max(s_n, axis=-1, keepdims=True))
        p_c = jnp.exp2(s_c - m)
        p_n = jnp.exp2(s_n - m)
        l = jnp.sum(p_c, axis=-1, keepdims=True) + jnp.sum(p_n, axis=-1, keepdims=True)
        o = (_bmm_nt(p_c.astype(BF16), vc) + _bmm(p_n.astype(BF16), vn)) * (1.0 / l)
        on = _rms(o) * nw_ref[...]
        o_ref[s] = jnp.concatenate([on[h] for h in range(ATT_HEADS)], axis=-1).astype(o_ref.dtype)


def _attn_sample(q, kc, vc, kn, vn, bias_c, bias_n, nw, *, n_seq):
    B, T, _ = q.shape
    new = pl.BlockSpec((n_seq, T, ATT_W), lambda b: (b, 0, 0))
    cache = pl.BlockSpec((n_seq,) + kc.shape[1:], lambda b: (b, 0, 0, 0))
    return pl.pallas_call(
        functools.partial(_attn_sample_kernel, n_seq=n_seq),
        name="attn_sample",
        grid=(B // n_seq,),
        in_specs=[new, cache, cache, new, new, _const_spec(bias_c.shape), _const_spec(bias_n.shape),
                  _const_spec(nw.shape)],
        out_specs=new,
        out_shape=jax.ShapeDtypeStruct((B, T, ATT_W), BF16),
        compiler_params=pltpu.CompilerParams(
            dimension_semantics=("arbitrary",), vmem_limit_bytes=VMEM_LIMIT),
    )(q, kc, vc, kn, vn, bias_c, bias_n, nw)


def _delayed(g, k, S, R, hist_row):
    rolled = pltpu.roll(g, k, axis=0)
    sub = lax.broadcasted_iota(jnp.int32, (SUBLANES, g.shape[1]), 0)
    parts = []
    for s in range(S):
        head = rolled[s * R:s * R + SUBLANES]
        for i in range(k):
            head = jnp.where(sub == i, hist_row(s, i), head)
        parts += [head, rolled[s * R + SUBLANES:(s + 1) * R]]
    return jnp.concatenate(parts, axis=0)


def _mix_ffn_kernel(x_ref, oa_ref, ob_ref, wo_ref, nmp_ref, nfp_ref, wgu_ref, cw_ref, cb_ref,
                    wd_ref, nfo_ref, prev_ref, y_ref, st_ref, carry,
                    *, S, R, tps, d_ff):
    t = pl.program_id(0)
    hist = FFN_CONV - 1
    half = oa_ref.shape[-1]
    mix = _dot(oa_ref[...], wo_ref[:half, :]) + _dot(ob_ref[...], wo_ref[half:, :])
    x1 = x_ref[...] + _rms(mix) * nmp_ref[...]
    u2 = (_rms(x1) * nfp_ref[...]).astype(BF16)

    if tps > 1:
        @pl.when(t % tps == 0)
        def _():
            for i in range(hist):
                carry[i:i + 1, :] = prev_ref[0, i]

    gu = _dot(u2, wgu_ref[...])
    g = gu[:, :d_ff]
    up = gu[:, d_ff:]
    if tps > 1:
        history = lambda s, i: carry[i:i + 1, :]
    else:
        history = lambda s, i: prev_ref[0, i, s:s + 1, :]
    cw = cw_ref[...]
    conv = cb_ref[...] + cw[hist:hist + 1] * g
    for k in range(1, FFN_CONV):
        conv = conv + cw[hist - k:hist - k + 1] * _delayed(
            g, k, S, R, lambda s, i, k=k: history(s, hist - k + i))
    for s in range(S):
        for i in range(hist):
            row = g[(s + 1) * R - hist + i:(s + 1) * R - hist + i + 1]
            if tps > 1:
                carry[i:i + 1, :] = row
            else:
                st_ref[0, i, s:s + 1, :] = row
    if tps > 1:
        for i in range(hist):
            st_ref[0, i] = carry[i:i + 1, :]
    hid = (_gelu_tanh(conv) * up).astype(BF16)
    y_ref[...] = x1 + _rms(_dot(hid, wd_ref[...])) * nfo_ref[...]


def _mix_ffn(x, oa, ob, wo, nmp, nfp, wgu, cw, cb, wd, nfo, prev, *, S, R):
    M, D = x.shape
    d_ff = wd.shape[0]
    tm = S * R
    n_seq = prev.shape[0]
    hist = FFN_CONV - 1
    tps = M // (n_seq * R)
    assert S == 1 or tps == 1
    grouped = (n_seq // S, hist, S, d_ff)
    prev_g = prev.reshape(n_seq // S, S, hist, d_ff).transpose(0, 2, 1, 3)
    row = lambda w: pl.BlockSpec((tm, w), lambda t: (t, 0))
    st = pl.BlockSpec((1,) + grouped[1:], lambda t: (t // tps, 0, 0, 0))
    consts = [wo, nmp, nfp, wgu, cw, cb, wd, nfo]
    y, st_g = pl.pallas_call(
        functools.partial(_mix_ffn_kernel, S=S, R=R, tps=tps, d_ff=d_ff),
        name="mix_ffn",
        grid=(M // tm,),
        in_specs=[row(D), row(oa.shape[1]), row(ob.shape[1])] + [_const_spec(a.shape) for a in consts] + [st],
        out_specs=[row(D), st],
        out_shape=[jax.ShapeDtypeStruct((M, D), F32), jax.ShapeDtypeStruct(grouped, F32)],
        scratch_shapes=[pltpu.VMEM((hist, d_ff), F32)],
        compiler_params=pltpu.CompilerParams(
            dimension_semantics=("arbitrary",), vmem_limit_bytes=VMEM_LIMIT),
    )(x, oa, ob, *consts, prev_g)
    return y, st_g.transpose(0, 2, 1, 3).reshape(n_seq, hist, d_ff)


def _lane_row(vals, offset):
    return jnp.zeros((1, LANES), F32).at[0, offset:offset + vals.shape[0]].set(vals.astype(F32))


def _bias_kernel(r_ref, bp_ref, bc_ref, bn_ref, *, lc, ts):
    tq = 2 * CHUNK
    key = lax.broadcasted_iota(jnp.int32, (tq, tq), 0)
    query_chunk = lax.broadcasted_iota(jnp.int32, (tq, tq), 1) // CHUNK
    for h in range(r_ref.shape[0]):
        p, e = divmod(h, 2)
        for jb in range(r_ref.shape[1]):
            row = jnp.broadcast_to(r_ref[h, jb], (tq, 2 * tq))
            blk = pltpu.roll(row, 0, 1, stride=1, stride_axis=0)[:, :tq]
            key_in_band = jb * tq + key - query_chunk * CHUNK
            valid = (key_in_band >= 0) & (key_in_band < (BAND_CHUNKS + 1) * CHUNK)
            far = r_ref[h, 0][:, 0:1]
            bp_ref[p, jb * tq:(jb + 1) * tq, e * tq:(e + 1) * tq] = jnp.where(valid, blk - far, -jnp.inf)
            by_query = blk.T
            if (jb + 1) * tq <= lc:
                bc_ref[h, :, jb * tq:(jb + 1) * tq] = by_query[:ts]
            else:
                bn_ref[h] = by_query[:ts, :ts]


def _band_biases(table, lc, ts):
    n_heads = table.shape[0]
    tq = 2 * CHUNK
    n_kb = BAND_CHUNKS // 2 + 1
    assert lc == (n_kb - 1) * tq and ts <= tq
    starts = [WINDOW - tq * jb + MAX_REL - half * tq for jb in range(n_kb) for half in (0, 1)]
    pad_l = max(0, -min(starts))
    pad_r = max(0, max(starts) + tq - table.shape[1])
    ext = jnp.pad(table, ((0, 0), (pad_l, pad_r)), mode="edge")
    rows = jnp.concatenate([ext[:, s + pad_l:s + pad_l + tq] for s in starts], axis=1)
    rows = rows.reshape(n_heads, n_kb, 1, 2 * tq)
    return pl.pallas_call(
        functools.partial(_bias_kernel, lc=lc, ts=ts),
        name="band_bias",
        out_shape=[jax.ShapeDtypeStruct((n_heads // 2, n_kb * tq, 2 * tq), F32),
                   jax.ShapeDtypeStruct((n_heads, ts, lc), F32),
                   jax.ShapeDtypeStruct((n_heads, ts, ts), F32)],
    )(rows)


def _layer(xp, xs, cache_k, cache_v, s_delta, s_qkv, s_ffn, lw):
    (norm_mix_pre, w_in, qkv_conv_w, a_log, dt_bias, gdn_norm_w, rel_bias, attn_norm_w, w_out,
     norm_mix_post, norm_ffn_pre, w_gate_up, ffn_conv_w, ffn_conv_b, w_down, norm_ffn_post) = lw
    Bp, Tp, D = xp.shape
    Bs, Ts, _ = xs.shape
    d_ff = w_down.shape[0]

    c1 = GDN_QKV + GDN_Z
    c2 = c1 + 2 * GDN_HEADS
    wt = jnp.swapaxes(w_in, 0, 1).astype(BF16)
    w_proj = (wt[:c1], jnp.pad(wt[c1:c2], ((0, LANES - 2 * GDN_HEADS), (0, 0))), wt[c2:c2 + 2 * ATT_W])
    wkt = wt[c2 + ATT_W:c2 + 2 * ATT_W]
    wvt = wt[c2 + 2 * ATT_W:]
    nmix = norm_mix_pre.reshape(1, D)
    cw_qkv = jnp.pad(0.5 * qkv_conv_w, ((0, SUBLANES - GDN_CONV), (0, 0)))
    alog = _lane_row(a_log, GDN_HEADS)
    dtb = _lane_row(dt_bias, GDN_HEADS)
    gnw = gdn_norm_w.reshape(1, GDN_DV)
    anw = attn_norm_w.astype(F32).reshape(1, ATT_DH)
    anw_col = jnp.broadcast_to(attn_norm_w.astype(F32)[:, None], (ATT_DH, 2 * CHUNK))
    lc = cache_k.shape[1]
    bias_p, bias_c, bias_n = _band_biases(rel_bias.astype(F32) * LOG2E, lc, Ts)
    wo = w_out.astype(BF16)
    wgu = w_gate_up.astype(BF16)
    wd = w_down.astype(BF16)
    cw_ffn = jnp.pad(ffn_conv_w, ((0, SUBLANES - FFN_CONV), (0, 0)))
    cb = ffn_conv_b.reshape(1, d_ff)
    nmp = norm_mix_post.reshape(1, D)
    nfp = norm_ffn_pre.reshape(1, D)
    nfo = norm_ffn_post.reshape(1, D)

    def group(x, keep_all, gdn_prev, gdn_s0, gdn_c, gdn_cps, gdn_bs, ffn_prev, S, R, attn):
        B, T, _ = x.shape
        xi = x if not keep_all else x.reshape(1, B * T, D)
        act, qkv_state, z, ba, q, k, vt, kf, vf = _inproj(xi, nmix, w_proj, wkt, wvt, cw_qkv, gdn_prev,
                                                          keep_all=keep_all)
        rs = lambda a: a.reshape(B, T, a.shape[-1])
        act, z, ba, q, k = map(rs, (act, z, ba, q, k))
        oa, s_new = _gdn(act, z, ba, gdn_s0, alog, dtb, gnw, C=gdn_c, cps=gdn_cps, bs=gdn_bs)
        ob = attn(q, k, vt, vf)
        y, ffn_state = _mix_ffn(x.reshape(B * T, D), oa.reshape(B * T, GDN_Z), ob.reshape(B * T, ATT_W),
                                wo, nmp, nfp, wgu, cw_ffn, cb, wd, nfo, ffn_prev,
                                S=S, R=R)
        if keep_all:
            k_rows = kf.reshape(B, T, ATT_HEADS, ATT_DH)
            v_rows = vf.reshape(B, T, ATT_HEADS, ATT_DH)
        else:
            k_rows = jnp.transpose(kf, (0, 3, 1, 2))
            v_rows = jnp.transpose(vf, (0, 3, 1, 2))
        return y.reshape(B, T, D), k_rows, v_rows, s_new, qkv_state, ffn_state

    out_p = group(
        xp, False, jnp.zeros((Bp, GDN_CONV - 1, GDN_QKV), F32),
        jnp.zeros((Bp, GDN_HEADS, GDN_DK, GDN_DV), F32), CHUNK, 4, 4,
        jnp.zeros((Bp, FFN_CONV - 1, d_ff), F32), 1, ROW_TILE,
        lambda q, k, vt, vf: _attn_prompt(q, k, vt, bias_p, anw_col, qt=4))

    kc_t = jnp.transpose(cache_k, (0, 2, 3, 1))
    vc_t = jnp.transpose(cache_v, (0, 2, 3, 1))
    out_s = group(
        xs, True, s_qkv, s_delta, Ts, 1, 8, s_ffn, ROW_TILE // Ts, Ts,
        lambda q, k, vt, vf: _attn_sample(q, kc_t, vc_t, k, vf.reshape(Bs, Ts, ATT_W), bias_c, bias_n, anw,
                                          n_seq=4))
    return out_p, out_s


def kernel(x_prompt, x_sample, cache_band_k, cache_band_v, state_delta, state_qkv_conv, state_ffn_conv, norm_mix_pre, w_in, qkv_conv_w, a_log, dt_bias, gdn_norm_w, rel_bias, attn_norm_w, w_out, norm_mix_post, norm_ffn_pre, w_gate_up, ffn_conv_w, ffn_conv_b, w_down, norm_ffn_post):
    weights = (norm_mix_pre, w_in, qkv_conv_w, a_log, dt_bias, gdn_norm_w, rel_bias, attn_norm_w, w_out,
               norm_mix_post, norm_ffn_pre, w_gate_up, ffn_conv_w, ffn_conv_b, w_down, norm_ffn_post)
    depth = w_in.shape[0]
    xp, xs = x_prompt, x_sample
    outs_p, outs_s = [], []
    for l in range(depth):
        lw = tuple(w[l] for w in weights)
        op, os_ = _layer(xp, xs, cache_band_k[l], cache_band_v[l], state_delta[l], state_qkv_conv[l],
                         state_ffn_conv[l], lw)
        xp, xs = op[0], os_[0]
        outs_p.append(op[1:])
        outs_s.append(os_[1:])
    stack = lambda outs, i: jnp.stack([o[i] for o in outs], axis=0)
    return (xp, xs) + tuple(stack(outs_p, i) for i in range(5)) + tuple(stack(outs_s, i) for i in range(5))
```

```python
import functools

import jax
import jax.numpy as jnp
from jax import lax
from jax.experimental import pallas as pl
from jax.experimental.pallas import tpu as pltpu

F32 = jnp.float32
BF16 = jnp.bfloat16

EPS = 1e-6
CHUNK = 64
GDN_HEADS = 4
GDN_DK = 128
GDN_DV = 128
GDN_CONV = 4
ATT_HEADS = 8
ATT_DH = 64
BAND_CHUNKS = 8
WINDOW = BAND_CHUNKS * CHUNK
MAX_REL = 128
FFN_CONV = 3

GDN_QK = GDN_HEADS * GDN_DK
GDN_QKV = GDN_HEADS * (2 * GDN_DK + GDN_DV)
GDN_Z = GDN_HEADS * GDN_DV
ATT_W = ATT_HEADS * ATT_DH
LANES = 128
SUBLANES = 8
VMEM_LIMIT = 56 * 1024 * 1024
ROW_TILE = 512
_FLAT_BIAS_BLOCKS = tuple(
    jb for jb in range(BAND_CHUNKS // 2 + 1)
    if jb * 2 * CHUNK >= CHUNK and (jb + 1) * 2 * CHUNK <= (BAND_CHUNKS + 1) * CHUNK
    and WINDOW - (jb + 1) * 2 * CHUNK + 1 >= MAX_REL)
LOG2E = 1.4426950408889634
Q_SCALE = ATT_DH ** -0.5 * LOG2E


def _dot(a, b):
    return jnp.dot(a, b, preferred_element_type=F32)


def _dot_nt(a, b):
    return lax.dot_general(a, b, (((1,), (1,)), ((), ())), preferred_element_type=F32)


def _split3(x):
    x1 = x.astype(BF16)
    r1 = x - x1.astype(F32)
    x2 = r1.astype(BF16)
    x3 = (r1 - x2.astype(F32)).astype(BF16)
    return x1, x2, x3


def _dot_exact_lhs(a16, x):
    x1, x2, x3 = _split3(x)
    return _dot(a16, x1) + _dot(a16, x2) + _dot(a16, x3)


def _sigmoid(x):
    return 0.5 + 0.5 * jnp.tanh(0.5 * x)


def _silu_of_half(h):
    return h + h * jnp.tanh(h)


def _silu(x):
    return _silu_of_half(0.5 * x)


def _softplus(x):
    return jnp.maximum(x, 0.0) + jnp.log(1.0 + jnp.exp(-jnp.abs(x)))


def _rms(x):
    return x * lax.rsqrt(jnp.mean(x * x, axis=-1, keepdims=True) + EPS)


def _gelu_tanh(x):
    c = 0.7978845608028654
    h = 0.5 * x
    return h + h * jnp.tanh(x * (c + (0.044715 * c) * (x * x)))


def _const_spec(shape):
    n = len(shape)
    return pl.BlockSpec(shape, lambda *_: (0,) * n, pipeline_mode=pl.Buffered(1))


def _inproj_kernel(x_ref, nw_ref, wa_ref, wg_ref, wqk_ref, wkt_ref, wvt_ref, cw_ref, prev_ref,
                   act_ref, cst_ref, z_ref, ba_ref, q_ref, k_ref, vt_ref, kf_ref, vf_ref, carry, w_nn,
                   *, keep_all, n_tiles, S, R, tps):
    x = x_ref[0]
    tm = x.shape[0]
    u = (_rms(x) * nw_ref[...]).astype(BF16)

    hist = GDN_CONV - 1
    if tps > 1:
        @pl.when(pl.program_id(1) % tps == 0)
        def _():
            for r in range(hist):
                carry[r:r + 1, :] = prev_ref[0, r]
    cw = cw_ref[...]
    slabs = [slice(c * GDN_QK, (c + 1) * GDN_QK) for c in range(GDN_QKV // GDN_QK)]

    def conv_silu(pre, cols):
        if tps > 1:
            history = lambda s, r, cols=cols: carry[r:r + 1, cols]
        else:
            history = lambda s, r, cols=cols: prev_ref[0, r, s:s + 1, cols]
        half_conv = cw[hist:hist + 1, cols] * pre
        for k in range(1, GDN_CONV):
            half_conv = half_conv + cw[hist - k:hist - k + 1, cols] * _delayed(
                pre, k, S, R, lambda s, r, k=k, history=history: history(s, hist - k + r))
        act_ref[0, :, cols] = _silu_of_half(half_conv)
        for s in range(S):
            for r in range(hist):
                row = pre[(s + 1) * R - hist + r:(s + 1) * R - hist + r + 1]
                if tps > 1:
                    carry[r:r + 1, cols] = row
                else:
                    cst_ref[0, r, s:s + 1, cols] = row

    @pl.when((pl.program_id(0) == 0) & (pl.program_id(1) == 0))
    def _():
        col = 0
        for piece in (wa_ref, wg_ref, wqk_ref):
            for c in range(piece.shape[0] // LANES):
                w_nn[:, col:col + LANES] = piece[c * LANES:(c + 1) * LANES, :].T
                col += LANES

    p = _dot(u, w_nn[...])
    for cols in slabs:
        conv_silu(p[:, cols], cols)
    if tps > 1:
        for r in range(hist):
            cst_ref[0, r] = carry[r:r + 1, :]
    c0 = GDN_QKV + GDN_Z
    c1 = c0 + LANES
    z_ref[0] = p[:, GDN_QKV:c0]
    ba_ref[0] = p[:, c0:c1]
    q_ref[0] = (p[:, c1:c1 + ATT_W] * Q_SCALE).astype(BF16)
    k = p[:, c1 + ATT_W:]
    k_ref[0] = k.astype(BF16)
    vt = _dot_nt(wvt_ref[...], u)
    for jb in range(tm // LANES):
        vt_ref[0, jb] = vt[:, jb * LANES:(jb + 1) * LANES].astype(BF16)

    if keep_all:
        kf_ref[0] = k
        vf_ref[0] = _dot_nt(u, wvt_ref[...])
    else:
        @pl.when(pl.program_id(1) == n_tiles - 1)
        def _():
            kf_ref[0] = _dot_nt(wkt_ref[...], u).reshape(ATT_HEADS, ATT_DH, tm)
            vf_ref[0] = vt.reshape(ATT_HEADS, ATT_DH, tm)


def _inproj(x, nw, w_pieces, wkt, wvt, cw, prev, *, keep_all):
    B, T, D = x.shape
    tm = ROW_TILE
    nt = T // tm
    n_seq = prev.shape[0]
    hist = GDN_CONV - 1
    R = min(B * T // n_seq, tm)
    S = tm // R
    tps = B * T // (n_seq * R)
    assert S == 1 or tps == 1
    grouped = (n_seq // S, hist, S, GDN_QKV)
    prev_g = prev.reshape(n_seq // S, S, hist, GDN_QKV).transpose(0, 2, 1, 3)
    groups_per_b = nt // tps
    st = pl.BlockSpec((1,) + grouped[1:], lambda b, i: (b * groups_per_b + i // tps, 0, 0, 0))
    row = lambda w: pl.BlockSpec((1, tm, w), lambda b, i: (b, i, 0))
    if keep_all:
        keep = row(ATT_W)
        keep_shape = (B, T, ATT_W)
    else:
        keep = pl.BlockSpec((1, ATT_HEADS, ATT_DH, tm), lambda b, i: (b, 0, 0, 0))
        keep_shape = (B, ATT_HEADS, ATT_DH, tm)
    consts = [nw, *w_pieces, wkt, wvt, cw]
    n_proj = sum(w.shape[0] for w in w_pieces)
    act, cst, *rest = pl.pallas_call(
        functools.partial(_inproj_kernel, keep_all=keep_all, n_tiles=nt, S=S, R=R, tps=tps),
        name="inproj",
        grid=(B, nt),
        in_specs=[row(D)] + [_const_spec(a.shape) for a in consts] + [st],
        out_specs=[row(GDN_QKV), st, row(GDN_Z), row(LANES), row(ATT_W), row(ATT_W),
                   pl.BlockSpec((1, tm // LANES, ATT_W, LANES), lambda b, i: (b, i, 0, 0)), keep, keep],
        out_shape=[jax.ShapeDtypeStruct((B, T, GDN_QKV), F32),
                   jax.ShapeDtypeStruct(grouped, F32),
                   jax.ShapeDtypeStruct((B, T, GDN_Z), F32),
                   jax.ShapeDtypeStruct((B, T, LANES), F32),
                   jax.ShapeDtypeStruct((B, T, ATT_W), BF16),
                   jax.ShapeDtypeStruct((B, T, ATT_W), BF16),
                   jax.ShapeDtypeStruct((B, T // LANES, ATT_W, LANES), BF16),
                   jax.ShapeDtypeStruct(keep_shape, F32),
                   jax.ShapeDtypeStruct(keep_shape, F32)],
        scratch_shapes=[pltpu.VMEM((hist, GDN_QKV), F32), pltpu.VMEM((D, n_proj), BF16)],
        compiler_params=pltpu.CompilerParams(
            dimension_semantics=("arbitrary", "arbitrary"), vmem_limit_bytes=VMEM_LIMIT),
    )(x, *consts, prev_g)
    return (act, cst.transpose(0, 2, 1, 3).reshape(n_seq, hist, GDN_QKV), *rest)


def _bmm(a, b):
    return lax.dot_general(a, b, (((2,), (1,)), ((0,), (0,))), preferred_element_type=F32)


def _bmm_nt(a, b):
    return lax.dot_general(a, b, (((2,), (2,)), ((0,), (0,))), preferred_element_type=F32)


def _bmm_tn(a, b):
    return lax.dot_general(a, b, (((1,), (1,)), ((0,), (0,))), preferred_element_type=F32)


def _gdn_kernel(act_ref, z_ref, ba_ref, s0_ref, alog_ref, dtb_ref, gnw_ref,
                o_ref, sout_ref, s_scr, *, C, cps, bs, n_steps):
    j = pl.program_id(1)
    R = C * cps
    H = GDN_HEADS
    nh = bs * H

    @pl.when(j == 0)
    def _():
        s_scr[...] = s0_ref[...].reshape(nh, GDN_DK, GDN_DV)

    ri = lax.broadcasted_iota(jnp.int32, (C, C), 0)
    ci = lax.broadcasted_iota(jnp.int32, (C, C), 1)
    incl = ri >= ci
    strict = ri > ci
    rr = lax.broadcasted_iota(jnp.int32, (R, R), 0)
    cc = lax.broadcasted_iota(jnp.int32, (R, R), 1)
    cum16 = jnp.where((rr >= cc) & (rr // C == cc // C), 1.0, 0.0).astype(BF16)
    alog = alog_ref[...]
    dtb = dtb_ref[...]

    acts, sigs, Gs, GTs = [], [], [], []
    for s in range(bs):
        acts.append(act_ref.at[s])
        ba = ba_ref[s]
        sigs.append(_sigmoid(ba))
        G = _dot_exact_lhs(cum16, -jnp.exp(alog) * _softplus(ba + dtb))
        Gs.append(G)
        GTs.append(G.T)

    order = [(c, s, h) for c in range(cps) for s in range(bs) for h in range(H)]
    rows = lambda c: slice(c * C, (c + 1) * C)

    def tiles(slabs, col0, width):
        return jnp.stack([slabs[s][rows(c), col0 + h * width:col0 + (h + 1) * width] for c, s, h in order])

    q = tiles(acts, 0, GDN_DK)
    k = tiles(acts, GDN_QK, GDN_DK)
    v = tiles(acts, 2 * GDN_QK, GDN_DV)
    beta = tiles(sigs, 0, 1)
    Gc = tiles(Gs, H, 1)
    Gr = jnp.stack([GTs[s][H + h:H + h + 1, rows(c)] for c, s, h in order])
    Gl = Gc[:, C - 1:C, :]

    nb = len(order)
    li = lax.broadcasted_iota(jnp.int32, (2 * GDN_DK, 2 * GDN_DK), 0) // GDN_DK
    lj = lax.broadcasted_iota(jnp.int32, (2 * GDN_DK, 2 * GDN_DK), 1) // GDN_DK
    ones2 = jnp.where(li == lj, 1.0, 0.0).astype(BF16)
    sq = jnp.concatenate([q * q, k * k], axis=-1).reshape(nb * C, 2 * GDN_DK)
    sq_hi = sq.astype(BF16)
    sq_lo = (sq - sq_hi.astype(F32)).astype(BF16)
    norms = (_dot(sq_hi, ones2) + _dot(sq_lo, ones2)).reshape(nb, C, 2 * GDN_DK)
    qn = q * lax.rsqrt(norms[:, :, :GDN_DK] + EPS) * (GDN_DK ** -0.5)
    kn = k * lax.rsqrt(norms[:, :, GDN_DK:] + EPS)
    eG = jnp.exp(Gc)
    gam = jnp.where(incl, jnp.exp(jnp.where(incl, Gc - Gr, 0.0)), 0.0)
    kb = kn * beta
    kn16 = kn.astype(BF16)
    aq = _bmm_nt(jnp.concatenate([kb, qn], axis=1).astype(BF16), kn16)
    A = jnp.where(strict, aq[:, :C] * gam, 0.0)
    QK16 = (aq[:, C:] * gam).astype(BF16)

    n_joint = C.bit_length() - 2
    A16 = A.astype(BF16)
    N = -A
    Q = _bmm(A16, A16)
    for it in range(n_joint):
        Q16 = Q.astype(BF16)
        if it == n_joint - 1:
            N = N + Q + _bmm(N.astype(BF16), Q16)
        else:
            nq = _bmm(jnp.concatenate([N, Q], axis=1).astype(BF16), Q16)
            N = N + Q + nq[:, :C]
            Q = nq[:, C:]
    rhs = jnp.concatenate([v * beta, kb * eG], axis=-1)
    sol = rhs + _bmm(N.astype(BF16), rhs.astype(BF16))
    u = sol[:, :, :GDN_DV]
    wq16 = jnp.concatenate([sol[:, :, GDN_DV:], qn * eG], axis=1).astype(BF16)
    kg16 = (kn * jnp.exp(Gl - Gc)).astype(BF16)
    dl = jnp.exp(Gl)

    S = s_scr[...]
    o_parts = []
    for c in range(cps):
        sl = slice(c * nh, (c + 1) * nh)
        r = _bmm(wq16[sl], S.astype(BF16))
        vn16 = (u[sl] - r[:, :C]).astype(BF16)
        o_parts.append(r[:, C:] + _bmm(QK16[sl], vn16))
        S = S * dl[sl] + _bmm_tn(kg16[sl], vn16)
    s_scr[...] = S

    on = _rms(jnp.concatenate(o_parts, axis=0)) * gnw_ref[...]
    gates = []
    for s in range(bs):
        zs = z_ref[s]
        gates.append(_silu(zs))
    for idx, (c, s, h) in enumerate(order):
        cols = slice(h * GDN_DV, (h + 1) * GDN_DV)
        o_ref[s, rows(c), cols] = (on[idx] * gates[s][rows(c), cols]).astype(o_ref.dtype)

    @pl.when(j == n_steps - 1)
    def _():
        sout_ref[...] = S.reshape(bs, H, GDN_DK, GDN_DV)


def _gdn(act, z, ba, s0, alog, dtb, gnw, *, C, cps, bs):
    B, T, _ = act.shape
    R = C * cps
    assert B % bs == 0 and T % R == 0
    n_steps = T // R
    row = lambda w: pl.BlockSpec((bs, R, w), lambda b, j: (b, j, 0))
    st = pl.BlockSpec((bs, GDN_HEADS, GDN_DK, GDN_DV), lambda b, j: (b, 0, 0, 0))
    return pl.pallas_call(
        functools.partial(_gdn_kernel, C=C, cps=cps, bs=bs, n_steps=n_steps),
        name="gdn",
        grid=(B // bs, n_steps),
        in_specs=[row(GDN_QKV), row(GDN_Z), row(LANES), st,
                  _const_spec(alog.shape), _const_spec(dtb.shape), _const_spec(gnw.shape)],
        out_specs=[row(GDN_Z), st],
        out_shape=[jax.ShapeDtypeStruct((B, T, GDN_Z), BF16),
                   jax.ShapeDtypeStruct((B, GDN_HEADS, GDN_DK, GDN_DV), F32)],
        scratch_shapes=[pltpu.VMEM((bs * GDN_HEADS, GDN_DK, GDN_DV), F32)],
        compiler_params=pltpu.CompilerParams(
            dimension_semantics=("arbitrary", "arbitrary"), vmem_limit_bytes=VMEM_LIMIT),
    )(act, z, ba, s0, alog, dtb, gnw)


def _attn_prompt_kernel(q_ref, k_ref, vt_ref, bias_ref, nw_ref, o_ref, *, qt):
    j = pl.program_id(1)
    tq = 2 * CHUNK
    n_kb = BAND_CHUNKS // 2 + 1
    n_pairs = ATT_HEADS // 2
    lane_head = lax.broadcasted_iota(jnp.int32, (tq, LANES), 1) // ATT_DH

    def tile(i, masked):
        m = j * qt + i
        q = q_ref[0, i * tq:(i + 1) * tq, :]
        wt = []
        for p in range(n_pairs):
            qp = q[:, p * LANES:(p + 1) * LANES]
            zero = jnp.zeros_like(qp)
            wt.append(jnp.concatenate([jnp.where(lane_head == 0, qp, zero),
                                       jnp.where(lane_head == 1, qp, zero)], axis=0))
        wt = jnp.stack(wt)
        firsts = [m - (n_kb - 1) + jb for jb in range(n_kb)]
        blks = [jnp.maximum(f, 0) for f in firsts]
        k_all = jnp.concatenate(
            [jnp.stack([k_ref[0, blk][:, p * LANES:(p + 1) * LANES] for p in range(n_pairs)]) for blk in blks],
            axis=1)
        v_all = jnp.concatenate([vt_ref[0, blk].reshape(n_pairs, LANES, tq) for blk in blks], axis=2)
        s_all = _bmm_nt(k_all, wt)
        s_parts = []
        for jb in range(n_kb):
            s = s_all[:, jb * tq:(jb + 1) * tq, :]
            if jb not in _FLAT_BIAS_BLOCKS:
                s = s + bias_ref[:, jb * tq:(jb + 1) * tq, :]
            if masked and jb < n_kb - 1:
                s = jnp.where(firsts[jb] >= 0, s, -jnp.inf)
            s_parts.append(s)
        st = jnp.concatenate(s_parts, axis=1)
        pt16 = jnp.exp2(st - jnp.max(st, axis=1, keepdims=True)).astype(BF16)
        acc = _bmm(jnp.concatenate([v_all, jnp.ones((n_pairs, 2 * SUBLANES, n_kb * tq), BF16)], axis=1), pt16)
        inv = 1.0 / acc[:, LANES:LANES + 1, :]
        ot = jnp.concatenate([acc[:, :ATT_DH, :tq] * inv[:, :, :tq],
                              acc[:, ATT_DH:LANES, tq:] * inv[:, :, tq:]], axis=1)
        oh = ot.reshape(ATT_HEADS, ATT_DH, tq)
        on = oh * lax.rsqrt(jnp.mean(oh * oh, axis=1, keepdims=True) + EPS) * nw_ref[...]
        on = on.reshape(n_pairs, LANES, tq)
        for p in range(n_pairs):
            o_ref[0, i * tq:(i + 1) * tq, p * LANES:(p + 1) * LANES] = on[p].T.astype(o_ref.dtype)

    near_start = j * qt < n_kb - 1

    @pl.when(near_start)
    def _():
        for i in range(qt):
            tile(i, True)

    @pl.when(jnp.logical_not(near_start))
    def _():
        for i in range(qt):
            tile(i, False)


def _attn_prompt(q, k, vt, bias, nw, *, qt):
    B, T, _ = q.shape
    tq = 2 * CHUNK
    n_steps = T // (qt * tq)
    k4 = k.reshape(B, T // tq, tq, ATT_W)
    row = pl.BlockSpec((1, qt * tq, ATT_W), lambda b, j: (b, j, 0))
    return pl.pallas_call(
        functools.partial(_attn_prompt_kernel, qt=qt),
        name="attn_prompt",
        grid=(B, n_steps),
        in_specs=[row,
                  pl.BlockSpec((1, T // tq, tq, ATT_W), lambda b, j: (b, 0, 0, 0)),
                  pl.BlockSpec((1, T // tq, ATT_W, tq), lambda b, j: (b, 0, 0, 0)),
                  _const_spec(bias.shape), _const_spec(nw.shape)],
        out_specs=row,
        out_shape=jax.ShapeDtypeStruct((B, T, ATT_W), BF16),
        compiler_params=pltpu.CompilerParams(
            dimension_semantics=("arbitrary", "arbitrary"), vmem_limit_bytes=VMEM_LIMIT),
    )(q, k4, vt, bias, nw)


def _attn_sample_kernel(q_ref, kc_ref, vc_ref, kn_ref, vn_ref, bc_ref, bn_ref, nw_ref, o_ref, *, n_seq):
    by_head = lambda a: jnp.stack([a[:, h * ATT_DH:(h + 1) * ATT_DH] for h in range(ATT_HEADS)])
    for s in range(n_seq):
        kc = kc_ref[s].astype(BF16)
        vc = vc_ref[s].astype(BF16)
        q = by_head(q_ref[s])
        kn = by_head(kn_ref[s])
        vn = by_head(vn_ref[s]).astype(BF16)
        s_c = _bmm(q, kc) + bc_ref[...]
        s_n = _bmm_nt(q, kn) + bn_ref[...]
        m = jnp.maximum(jnp.max(s_c, axis=-1, keepdims=True), jnp.max(s_n, axis=-1, keepdims=True))
        p_c = jnp.exp2(s_c - m)
        p_n = jnp.exp2(s_n - m)
        l = jnp.sum(p_c, axis=-1, keepdims=True) + jnp.sum(p_n, axis=-1, keepdims=True)
        o = (_bmm_nt(p_c.astype(BF16), vc) + _bmm(p_n.astype(BF16), vn)) * (1.0 / l)
        on = _rms(o) * nw_ref[...]
        o_ref[s] = jnp.concatenate([on[h] for h in range(ATT_HEADS)], axis=-1).astype(o_ref.dtype)


def _attn_sample(q, kc, vc, kn, vn, bias_c, bias_n, nw, *, n_seq):
    B, T, _ = q.shape
    new = pl.BlockSpec((n_seq, T, ATT_W), lambda b: (b, 0, 0))
    cache = pl.BlockSpec((n_seq,) + kc.shape[1:], lambda b: (b, 0, 0, 0))
    return pl.pallas_call(
        functools.partial(_attn_sample_kernel, n_seq=n_seq),
        name="attn_sample",
        grid=(B // n_seq,),
        in_specs=[new, cache, cache, new, new, _const_spec(bias_c.shape), _const_spec(bias_n.shape),
                  _const_spec(nw.shape)],
        out_specs=new,
        out_shape=jax.ShapeDtypeStruct((B, T, ATT_W), BF16),
        compiler_params=pltpu.CompilerParams(
            dimension_semantics=("arbitrary",), vmem_limit_bytes=VMEM_LIMIT),
    )(q, kc, vc, kn, vn, bias_c, bias_n, nw)


def _delayed(g, k, S, R, hist_row):
    rolled = pltpu.roll(g, k, axis=0)
    sub = lax.broadcasted_iota(jnp.int32, (SUBLANES, g.shape[1]), 0)
    parts = []
    for s in range(S):
        head = rolled[s * R:s * R + SUBLANES]
        for i in range(k):
            head = jnp.where(sub == i, hist_row(s, i), head)
        parts += [head, rolled[s * R + SUBLANES:(s + 1) * R]]
    return jnp.concatenate(parts, axis=0)


def _mix_ffn_kernel(x_ref, oa_ref, ob_ref, wo_ref, nmp_ref, nfp_ref, wgu_ref, cw_ref, cb_ref,
                    wd_ref, nfo_ref, prev_ref, y_ref, st_ref, carry,
                    *, S, R, tps, d_ff):
    t = pl.program_id(0)
    hist = FFN_CONV - 1
    mix = _dot(jnp.concatenate([oa_ref[...], ob_ref[...]], axis=1), wo_ref[...])
    x1 = x_ref[...] + _rms(mix) * nmp_ref[...]
    u2 = (_rms(x1) * nfp_ref[...]).astype(BF16)

    if tps > 1:
        @pl.when(t % tps == 0)
        def _():
            for i in range(hist):
                carry[i:i + 1, :] = prev_ref[0, i]

    gu = _dot(u2, wgu_ref[...])
    g = gu[:, :d_ff]
    up = gu[:, d_ff:]
    if tps > 1:
        history = lambda s, i: carry[i:i + 1, :]
    else:
        history = lambda s, i: prev_ref[0, i, s:s + 1, :]
    cw = cw_ref[...]
    conv = cb_ref[...] + cw[hist:hist + 1] * g
    for k in range(1, FFN_CONV):
        conv = conv + cw[hist - k:hist - k + 1] * _delayed(
            g, k, S, R, lambda s, i, k=k: history(s, hist - k + i))
    for s in range(S):
        for i in range(hist):
            row = g[(s + 1) * R - hist + i:(s + 1) * R - hist + i + 1]
            if tps > 1:
                carry[i:i + 1, :] = row
            else:
                st_ref[0, i, s:s + 1, :] = row
    if tps > 1:
        for i in range(hist):
            st_ref[0, i] = carry[i:i + 1, :]
    hid = (_gelu_tanh(conv) * up).astype(BF16)
    y_ref[...] = x1 + _rms(_dot(hid, wd_ref[...])) * nfo_ref[...]


def _mix_ffn(x, oa, ob, wo, nmp, nfp, wgu, cw, cb, wd, nfo, prev, *, S, R):
    M, D = x.shape
    d_ff = wd.shape[0]
    tm = S * R
    n_seq = prev.shape[0]
    hist = FFN_CONV - 1
    tps = M // (n_seq * R)
    assert S == 1 or tps == 1
    grouped = (n_seq // S, hist, S, d_ff)
    prev_g = prev.reshape(n_seq // S, S, hist, d_ff).transpose(0, 2, 1, 3)
    row = lambda w: pl.BlockSpec((tm, w), lambda t: (t, 0))
    st = pl.BlockSpec((1,) + grouped[1:], lambda t: (t // tps, 0, 0, 0))
    consts = [wo, nmp, nfp, wgu, cw, cb, wd, nfo]
    y, st_g = pl.pallas_call(
        functools.partial(_mix_ffn_kernel, S=S, R=R, tps=tps, d_ff=d_ff),
        name="mix_ffn",
        grid=(M // tm,),
        in_specs=[row(D), row(oa.shape[1]), row(ob.shape[1])] + [_const_spec(a.shape) for a in consts] + [st],
        out_specs=[row(D), st],
        out_shape=[jax.ShapeDtypeStruct((M, D), F32), jax.ShapeDtypeStruct(grouped, F32)],
        scratch_shapes=[pltpu.VMEM((hist, d_ff), F32)],
        compiler_params=pltpu.CompilerParams(
            dimension_semantics=("arbitrary",), vmem_limit_bytes=VMEM_LIMIT),
    )(x, oa, ob, *consts, prev_g)
    return y, st_g.transpose(0, 2, 1, 3).reshape(n_seq, hist, d_ff)


def _lane_row(vals, offset):
    return jnp.zeros((1, LANES), F32).at[0, offset:offset + vals.shape[0]].set(vals.astype(F32))


def _bias_kernel(r_ref, bp_ref, bc_ref, bn_ref, *, lc, ts):
    tq = 2 * CHUNK
    key = lax.broadcasted_iota(jnp.int32, (tq, tq), 0)
    query_chunk = lax.broadcasted_iota(jnp.int32, (tq, tq), 1) // CHUNK
    for h in range(r_ref.shape[0]):
        p, e = divmod(h, 2)
        for jb in range(r_ref.shape[1]):
            row = jnp.broadcast_to(r_ref[h, jb], (tq, 2 * tq))
            blk = pltpu.roll(row, 0, 1, stride=1, stride_axis=0)[:, :tq]
            key_in_band = jb * tq + key - query_chunk * CHUNK
            valid = (key_in_band >= 0) & (key_in_band < (BAND_CHUNKS + 1) * CHUNK)
            far = r_ref[h, 0][:, 0:1]
            bp_ref[p, jb * tq:(jb + 1) * tq, e * tq:(e + 1) * tq] = jnp.where(valid, blk - far, -jnp.inf)
            by_query = blk.T
            if (jb + 1) * tq <= lc:
                bc_ref[h, :, jb * tq:(jb + 1) * tq] = by_query[:ts]
            else:
                bn_ref[h] = by_query[:ts, :ts]


def _band_biases(table, lc, ts):
    n_heads = table.shape[0]
    tq = 2 * CHUNK
    n_kb = BAND_CHUNKS // 2 + 1
    assert lc == (n_kb - 1) * tq and ts <= tq
    starts = [WINDOW - tq * jb + MAX_REL - half * tq for jb in range(n_kb) for half in (0, 1)]
    pad_l = max(0, -min(starts))
    pad_r = max(0, max(starts) + tq - table.shape[1])
    ext = jnp.pad(table, ((0, 0), (pad_l, pad_r)), mode="edge")
    rows = jnp.concatenate([ext[:, s + pad_l:s + pad_l + tq] for s in starts], axis=1)
    rows = rows.reshape(n_heads, n_kb, 1, 2 * tq)
    return pl.pallas_call(
        functools.partial(_bias_kernel, lc=lc, ts=ts),
        name="band_bias",
        out_shape=[jax.ShapeDtypeStruct((n_heads // 2, n_kb * tq, 2 * tq), F32),
                   jax.ShapeDtypeStruct((n_heads, ts, lc), F32),
                   jax.ShapeDtypeStruct((n_heads, ts, ts), F32)],
    )(rows)


def _layer(xp, xs, cache_k, cache_v, s_delta, s_qkv, s_ffn, lw):
    (norm_mix_pre, w_in, qkv_conv_w, a_log, dt_bias, gdn_norm_w, rel_bias, attn_norm_w, w_out,
     norm_mix_post, norm_ffn_pre, w_gate_up, ffn_conv_w, ffn_conv_b, w_down, norm_ffn_post) = lw
    Bp, Tp, D = xp.shape
    Bs, Ts, _ = xs.shape
    d_ff = w_down.shape[0]

    c1 = GDN_QKV + GDN_Z
    c2 = c1 + 2 * GDN_HEADS
    wt = jnp.swapaxes(w_in, 0, 1).astype(BF16)
    w_proj = (wt[:c1], jnp.pad(wt[c1:c2], ((0, LANES - 2 * GDN_HEADS), (0, 0))), wt[c2:c2 + 2 * ATT_W])
    wkt = wt[c2 + ATT_W:c2 + 2 * ATT_W]
    wvt = wt[c2 + 2 * ATT_W:]
    nmix = norm_mix_pre.reshape(1, D)
    cw_qkv = jnp.pad(0.5 * qkv_conv_w, ((0, SUBLANES - GDN_CONV), (0, 0)))
    alog = _lane_row(a_log, GDN_HEADS)
    dtb = _lane_row(dt_bias, GDN_HEADS)
    gnw = gdn_norm_w.reshape(1, GDN_DV)
    anw = attn_norm_w.astype(F32).reshape(1, ATT_DH)
    anw_col = jnp.broadcast_to(attn_norm_w.astype(F32)[:, None], (ATT_DH, 2 * CHUNK))
    lc = cache_k.shape[1]
    bias_p, bias_c, bias_n = _band_biases(rel_bias.astype(F32) * LOG2E, lc, Ts)
    wo = w_out.astype(BF16)
    wgu = w_gate_up.astype(BF16)
    wd = w_down.astype(BF16)
    cw_ffn = jnp.pad(ffn_conv_w, ((0, SUBLANES - FFN_CONV), (0, 0)))
    cb = ffn_conv_b.reshape(1, d_ff)
    nmp = norm_mix_post.reshape(1, D)
    nfp = norm_ffn_pre.reshape(1, D)
    nfo = norm_ffn_post.reshape(1, D)

    def group(x, keep_all, gdn_prev, gdn_s0, gdn_c, gdn_cps, gdn_bs, ffn_prev, S, R, attn):
        B, T, _ = x.shape
        xi = x if not keep_all else x.reshape(1, B * T, D)
        act, qkv_state, z, ba, q, k, vt, kf, vf = _inproj(xi, nmix, w_proj, wkt, wvt, cw_qkv, gdn_prev,
                                                          keep_all=keep_all)
        rs = lambda a: a.reshape(B, T, a.shape[-1])
        act, z, ba, q, k = map(rs, (act, z, ba, q, k))
        oa, s_new = _gdn(act, z, ba, gdn_s0, alog, dtb, gnw, C=gdn_c, cps=gdn_cps, bs=gdn_bs)
        ob = attn(q, k, vt, vf)
        y, ffn_state = _mix_ffn(x.reshape(B * T, D), oa.reshape(B * T, GDN_Z), ob.reshape(B * T, ATT_W),
                                wo, nmp, nfp, wgu, cw_ffn, cb, wd, nfo, ffn_prev,
                                S=S, R=R)
        if keep_all:
            k_rows = kf.reshape(B, T, ATT_HEADS, ATT_DH)
            v_rows = vf.reshape(B, T, ATT_HEADS, ATT_DH)
        else:
            k_rows = jnp.transpose(kf, (0, 3, 1, 2))
            v_rows = jnp.transpose(vf, (0, 3, 1, 2))
        return y.reshape(B, T, D), k_rows, v_rows, s_new, qkv_state, ffn_state

    out_p = group(
        xp, False, jnp.zeros((Bp, GDN_CONV - 1, GDN_QKV), F32),
        jnp.zeros((Bp, GDN_HEADS, GDN_DK, GDN_DV), F32), CHUNK, 4, 4,
        jnp.zeros((Bp, FFN_CONV - 1, d_ff), F32), 1, ROW_TILE,
        lambda q, k, vt, vf: _attn_prompt(q, k, vt, bias_p, anw_col, qt=8))

    kc_t = jnp.transpose(cache_k, (0, 2, 3, 1))
    vc_t = jnp.transpose(cache_v, (0, 2, 3, 1))
    out_s = group(
        xs, True, s_qkv, s_delta, Ts, 1, 8, s_ffn, ROW_TILE // Ts, Ts,
        lambda q, k, vt, vf: _attn_sample(q, kc_t, vc_t, k, vf.reshape(Bs, Ts, ATT_W), bias_c, bias_n, anw,
                                          n_seq=4))
    return out_p, out_s


def kernel(x_prompt, x_sample, cache_band_k, cache_band_v, state_delta, state_qkv_conv, state_ffn_conv, norm_mix_pre, w_in, qkv_conv_w, a_log, dt_bias, gdn_norm_w, rel_bias, attn_norm_w, w_out, norm_mix_post, norm_ffn_pre, w_gate_up, ffn_conv_w, ffn_conv_b, w_down, norm_ffn_post):
    weights = (norm_mix_pre, w_in, qkv_conv_w, a_log, dt_bias, gdn_norm_w, rel_bias, attn_norm_w, w_out,
               norm_mix_post, norm_ffn_pre, w_gate_up, ffn_conv_w, ffn_conv_b, w_down, norm_ffn_post)
    depth = w_in.shape[0]
    xp, xs = x_prompt, x_sample
    outs_p, outs_s = [], []
    for l in range(depth):
        lw = tuple(w[l] for w in weights)
        op, os_ = _layer(xp, xs, cache_band_k[l], cache_band_v[l], state_delta[l], state_qkv_conv[l],
                         state_ffn_conv[l], lw)
        xp, xs = op[0], os_[0]
        outs_p.append(op[1:])
        outs_s.append(os_[1:])
    stack = lambda outs, i: jnp.stack([o[i] for o in outs], axis=0)
    return (xp, xs) + tuple(stack(outs_p, i) for i in range(5)) + tuple(stack(outs_s, i) for i in range(5))
```

```python
import functools

import jax
import jax.numpy as jnp
from jax import lax
from jax.experimental import pallas as pl
from jax.experimental.pallas import tpu as pltpu

F32 = jnp.float32
BF16 = jnp.bfloat16

EPS = 1e-6
CHUNK = 64
GDN_HEADS = 4
GDN_DK = 128
GDN_DV = 128
GDN_CONV = 4
ATT_HEADS = 8
ATT_DH = 64
BAND_CHUNKS = 8
WINDOW = BAND_CHUNKS * CHUNK
MAX_REL = 128
FFN_CONV = 3

GDN_QK = GDN_HEADS * GDN_DK
GDN_QKV = GDN_HEADS * (2 * GDN_DK + GDN_DV)
GDN_Z = GDN_HEADS * GDN_DV
ATT_W = ATT_HEADS * ATT_DH
LANES = 128
SUBLANES = 8
VMEM_LIMIT = 56 * 1024 * 1024
ROW_TILE = 512
_FLAT_BIAS_BLOCKS = tuple(
    jb for jb in range(BAND_CHUNKS // 2 + 1)
    if jb * 2 * CHUNK >= CHUNK and (jb + 1) * 2 * CHUNK <= (BAND_CHUNKS + 1) * CHUNK
    and WINDOW - (jb + 1) * 2 * CHUNK + 1 >= MAX_REL)
LOG2E = 1.4426950408889634
Q_SCALE = ATT_DH ** -0.5 * LOG2E


def _dot(a, b):
    return jnp.dot(a, b, preferred_element_type=F32)


def _dot_nt(a, b):
    return lax.dot_general(a, b, (((1,), (1,)), ((), ())), preferred_element_type=F32)


def _split3(x):
    x1 = x.astype(BF16)
    r1 = x - x1.astype(F32)
    x2 = r1.astype(BF16)
    x3 = (r1 - x2.astype(F32)).astype(BF16)
    return x1, x2, x3


def _dot_exact_lhs(a16, x):
    x1, x2, x3 = _split3(x)
    return _dot(a16, x1) + _dot(a16, x2) + _dot(a16, x3)


def _sigmoid(x):
    return 0.5 + 0.5 * jnp.tanh(0.5 * x)


def _silu_of_half(h):
    return h + h * jnp.tanh(h)


def _silu(x):
    return _silu_of_half(0.5 * x)


def _softplus(x):
    return jnp.maximum(x, 0.0) + jnp.log(1.0 + jnp.exp(-jnp.abs(x)))


def _rms(x):
    return x * lax.rsqrt(jnp.mean(x * x, axis=-1, keepdims=True) + EPS)


def _gelu_tanh(x):
    c = 0.7978845608028654
    h = 0.5 * x
    return h + h * jnp.tanh(x * (c + (0.044715 * c) * (x * x)))


def _const_spec(shape):
    n = len(shape)
    return pl.BlockSpec(shape, lambda *_: (0,) * n, pipeline_mode=pl.Buffered(1))


def _inproj_kernel(x_ref, nw_ref, wa_ref, wg_ref, wqk_ref, wkt_ref, wvt_ref, cw_ref, prev_ref,
                   act_ref, cst_ref, z_ref, ba_ref, q_ref, k_ref, vt_ref, kf_ref, vf_ref, carry, w_nn,
                   *, keep_all, n_tiles, S, R, tps):
    x = x_ref[0]
    tm = x.shape[0]
    u = (_rms(x) * nw_ref[...]).astype(BF16)

    hist = GDN_CONV - 1
    if tps > 1:
        @pl.when(pl.program_id(1) % tps == 0)
        def _():
            for r in range(hist):
                carry[r:r + 1, :] = prev_ref[0, r]
    cw = cw_ref[...]
    slabs = [slice(c * GDN_QK, (c + 1) * GDN_QK) for c in range(GDN_QKV // GDN_QK)]

    def conv_silu(pre, cols):
        if tps > 1:
            history = lambda s, r, cols=cols: carry[r:r + 1, cols]
        else:
            history = lambda s, r, cols=cols: prev_ref[0, r, s:s + 1, cols]
        half_conv = cw[hist:hist + 1, cols] * pre
        for k in range(1, GDN_CONV):
            half_conv = half_conv + cw[hist - k:hist - k + 1, cols] * _delayed(
                pre, k, S, R, lambda s, r, k=k, history=history: history(s, hist - k + r))
        act_ref[0, :, cols] = _silu_of_half(half_conv)
        for s in range(S):
            for r in range(hist):
                row = pre[(s + 1) * R - hist + r:(s + 1) * R - hist + r + 1]
                if tps > 1:
                    carry[r:r + 1, cols] = row
                else:
                    cst_ref[0, r, s:s + 1, cols] = row

    @pl.when((pl.program_id(0) == 0) & (pl.program_id(1) == 0))
    def _():
        col = 0
        for piece in (wa_ref, wg_ref, wqk_ref):
            for c in range(piece.shape[0] // LANES):
                w_nn[:, col:col + LANES] = piece[c * LANES:(c + 1) * LANES, :].T
                col += LANES

    p = _dot(u, w_nn[...])
    for cols in slabs:
        conv_silu(p[:, cols], cols)
    if tps > 1:
        for r in range(hist):
            cst_ref[0, r] = carry[r:r + 1, :]
    c0 = GDN_QKV + GDN_Z
    c1 = c0 + LANES
    z_ref[0] = p[:, GDN_QKV:c0]
    ba_ref[0] = p[:, c0:c1]
    q_ref[0] = (p[:, c1:c1 + ATT_W] * Q_SCALE).astype(BF16)
    k = p[:, c1 + ATT_W:]
    k_ref[0] = k.astype(BF16)
    vt = _dot_nt(wvt_ref[...], u)
    for jb in range(tm // LANES):
        vt_ref[0, jb] = vt[:, jb * LANES:(jb + 1) * LANES].astype(BF16)

    if keep_all:
        kf_ref[0] = k
        vf_ref[0] = _dot_nt(u, wvt_ref[...])
    else:
        @pl.when(pl.program_id(1) == n_tiles - 1)
        def _():
            kf_ref[0] = _dot_nt(wkt_ref[...], u).reshape(ATT_HEADS, ATT_DH, tm)
            vf_ref[0] = vt.reshape(ATT_HEADS, ATT_DH, tm)


def _inproj(x, nw, w_pieces, wkt, wvt, cw, prev, *, keep_all):
    B, T, D = x.shape
    tm = ROW_TILE
    nt = T // tm
    n_seq = prev.shape[0]
    hist = GDN_CONV - 1
    R = min(B * T // n_seq, tm)
    S = tm // R
    tps = B * T // (n_seq * R)
    assert S == 1 or tps == 1
    grouped = (n_seq // S, hist, S, GDN_QKV)
    prev_g = prev.reshape(n_seq // S, S, hist, GDN_QKV).transpose(0, 2, 1, 3)
    groups_per_b = nt // tps
    st = pl.BlockSpec((1,) + grouped[1:], lambda b, i: (b * groups_per_b + i // tps, 0, 0, 0))
    row = lambda w: pl.BlockSpec((1, tm, w), lambda b, i: (b, i, 0))
    if keep_all:
        keep = row(ATT_W)
        keep_shape = (B, T, ATT_W)
    else:
        keep = pl.BlockSpec((1, ATT_HEADS, ATT_DH, tm), lambda b, i: (b, 0, 0, 0))
        keep_shape = (B, ATT_HEADS, ATT_DH, tm)
    consts = [nw, *w_pieces, wkt, wvt, cw]
    n_proj = sum(w.shape[0] for w in w_pieces)
    act, cst, *rest = pl.pallas_call(
        functools.partial(_inproj_kernel, keep_all=keep_all, n_tiles=nt, S=S, R=R, tps=tps),
        name="inproj",
        grid=(B, nt),
        in_specs=[row(D)] + [_const_spec(a.shape) for a in consts] + [st],
        out_specs=[row(GDN_QKV), st, row(GDN_Z), row(LANES), row(ATT_W), row(ATT_W),
                   pl.BlockSpec((1, tm // LANES, ATT_W, LANES), lambda b, i: (b, i, 0, 0)), keep, keep],
        out_shape=[jax.ShapeDtypeStruct((B, T, GDN_QKV), F32),
                   jax.ShapeDtypeStruct(grouped, F32),
                   jax.ShapeDtypeStruct((B, T, GDN_Z), F32),
                   jax.ShapeDtypeStruct((B, T, LANES), F32),
                   jax.ShapeDtypeStruct((B, T, ATT_W), BF16),
                   jax.ShapeDtypeStruct((B, T, ATT_W), BF16),
                   jax.ShapeDtypeStruct((B, T // LANES, ATT_W, LANES), BF16),
                   jax.ShapeDtypeStruct(keep_shape, F32),
                   jax.ShapeDtypeStruct(keep_shape, F32)],
        scratch_shapes=[pltpu.VMEM((hist, GDN_QKV), F32), pltpu.VMEM((D, n_proj), BF16)],
        compiler_params=pltpu.CompilerParams(
            dimension_semantics=("arbitrary", "arbitrary"), vmem_limit_bytes=VMEM_LIMIT),
    )(x, *consts, prev_g)
    return (act, cst.transpose(0, 2, 1, 3).reshape(n_seq, hist, GDN_QKV), *rest)


def _bmm(a, b):
    return lax.dot_general(a, b, (((2,), (1,)), ((0,), (0,))), preferred_element_type=F32)


def _bmm_nt(a, b):
    return lax.dot_general(a, b, (((2,), (2,)), ((0,), (0,))), preferred_element_type=F32)


def _bmm_tn(a, b):
    return lax.dot_general(a, b, (((1,), (1,)), ((0,), (0,))), preferred_element_type=F32)


def _gdn_kernel(act_ref, z_ref, ba_ref, s0_ref, alog_ref, dtb_ref, gnw_ref,
                o_ref, sout_ref, s_scr, *, C, cps, bs, n_steps):
    j = pl.program_id(1)
    R = C * cps
    H = GDN_HEADS
    nh = bs * H

    @pl.when(j == 0)
    def _():
        s_scr[...] = s0_ref[...].reshape(nh, GDN_DK, GDN_DV)

    ri = lax.broadcasted_iota(jnp.int32, (C, C), 0)
    ci = lax.broadcasted_iota(jnp.int32, (C, C), 1)
    incl = ri >= ci
    strict = ri > ci
    rr = lax.broadcasted_iota(jnp.int32, (R, R), 0)
    cc = lax.broadcasted_iota(jnp.int32, (R, R), 1)
    cum16 = jnp.where((rr >= cc) & (rr // C == cc // C), 1.0, 0.0).astype(BF16)
    alog = alog_ref[...]
    dtb = dtb_ref[...]

    acts, sigs, Gs, GTs = [], [], [], []
    for s in range(bs):
        acts.append(act_ref.at[s])
        ba = ba_ref[s]
        sigs.append(_sigmoid(ba))
        G = _dot_exact_lhs(cum16, -jnp.exp(alog) * _softplus(ba + dtb))
        Gs.append(G)
        GTs.append(G.T)

    order = [(c, s, h) for c in range(cps) for s in range(bs) for h in range(H)]
    rows = lambda c: slice(c * C, (c + 1) * C)

    def tiles(slabs, col0, width):
        return jnp.stack([slabs[s][rows(c), col0 + h * width:col0 + (h + 1) * width] for c, s, h in order])

    q = tiles(acts, 0, GDN_DK)
    k = tiles(acts, GDN_QK, GDN_DK)
    v = tiles(acts, 2 * GDN_QK, GDN_DV)
    beta = tiles(sigs, 0, 1)
    Gc = tiles(Gs, H, 1)
    Gr = jnp.stack([GTs[s][H + h:H + h + 1, rows(c)] for c, s, h in order])
    Gl = Gc[:, C - 1:C, :]

    nb = len(order)
    li = lax.broadcasted_iota(jnp.int32, (2 * GDN_DK, 2 * GDN_DK), 0) // GDN_DK
    lj = lax.broadcasted_iota(jnp.int32, (2 * GDN_DK, 2 * GDN_DK), 1) // GDN_DK
    ones2 = jnp.where(li == lj, 1.0, 0.0).astype(BF16)
    sq = jnp.concatenate([q * q, k * k], axis=-1).reshape(nb * C, 2 * GDN_DK)
    sq_hi = sq.astype(BF16)
    sq_lo = (sq - sq_hi.astype(F32)).astype(BF16)
    norms = (_dot(sq_hi, ones2) + _dot(sq_lo, ones2)).reshape(nb, C, 2 * GDN_DK)
    qn = q * lax.rsqrt(norms[:, :, :GDN_DK] + EPS) * (GDN_DK ** -0.5)
    kn = k * lax.rsqrt(norms[:, :, GDN_DK:] + EPS)
    eG = jnp.exp(Gc)
    gam = jnp.where(incl, jnp.exp(Gc - Gr), 0.0)
    kb = kn * beta
    kn16 = kn.astype(BF16)
    aq = _bmm_nt(jnp.concatenate([kb, qn], axis=1).astype(BF16), kn16)
    A = jnp.where(strict, aq[:, :C] * gam, 0.0)
    QK16 = (aq[:, C:] * gam).astype(BF16)

    n_joint = C.bit_length() - 2
    A16 = A.astype(BF16)
    N = -A
    Q = _bmm(A16, A16)
    for it in range(n_joint):
        Q16 = Q.astype(BF16)
        if it == n_joint - 1:
            N = N + Q + _bmm(N.astype(BF16), Q16)
        else:
            nq = _bmm(jnp.concatenate([N, Q], axis=1).astype(BF16), Q16)
            N = N + Q + nq[:, :C]
            Q = nq[:, C:]
    rhs = jnp.concatenate([v * beta, kb * eG], axis=-1)
    sol = rhs + _bmm(N.astype(BF16), rhs.astype(BF16))
    u = sol[:, :, :GDN_DV]
    wq16 = jnp.concatenate([sol[:, :, GDN_DV:], qn * eG], axis=1).astype(BF16)
    kg16 = (kn * jnp.exp(Gl - Gc)).astype(BF16)
    dl = jnp.exp(Gl)

    S = s_scr[...]
    o_parts = []
    for c in range(cps):
        sl = slice(c * nh, (c + 1) * nh)
        r = _bmm(wq16[sl], S.astype(BF16))
        vn16 = (u[sl] - r[:, :C]).astype(BF16)
        o_parts.append(r[:, C:] + _bmm(QK16[sl], vn16))
        S = S * dl[sl] + _bmm_tn(kg16[sl], vn16)
    s_scr[...] = S

    on = _rms(jnp.concatenate(o_parts, axis=0)) * gnw_ref[...]
    gates = []
    for s in range(bs):
        zs = z_ref[s]
        gates.append(_silu(zs))
    for idx, (c, s, h) in enumerate(order):
        cols = slice(h * GDN_DV, (h + 1) * GDN_DV)
        o_ref[s, rows(c), cols] = (on[idx] * gates[s][rows(c), cols]).astype(o_ref.dtype)

    @pl.when(j == n_steps - 1)
    def _():
        sout_ref[...] = S.reshape(bs, H, GDN_DK, GDN_DV)


def _gdn(act, z, ba, s0, alog, dtb, gnw, *, C, cps, bs):
    B, T, _ = act.shape
    R = C * cps
    assert B % bs == 0 and T % R == 0
    n_steps = T // R
    row = lambda w: pl.BlockSpec((bs, R, w), lambda b, j: (b, j, 0))
    st = pl.BlockSpec((bs, GDN_HEADS, GDN_DK, GDN_DV), lambda b, j: (b, 0, 0, 0))
    return pl.pallas_call(
        functools.partial(_gdn_kernel, C=C, cps=cps, bs=bs, n_steps=n_steps),
        name="gdn",
        grid=(B // bs, n_steps),
        in_specs=[row(GDN_QKV), row(GDN_Z), row(LANES), st,
                  _const_spec(alog.shape), _const_spec(dtb.shape), _const_spec(gnw.shape)],
        out_specs=[row(GDN_Z), st],
        out_shape=[jax.ShapeDtypeStruct((B, T, GDN_Z), BF16),
                   jax.ShapeDtypeStruct((B, GDN_HEADS, GDN_DK, GDN_DV), F32)],
        scratch_shapes=[pltpu.VMEM((bs * GDN_HEADS, GDN_DK, GDN_DV), F32)],
        compiler_params=pltpu.CompilerParams(
            dimension_semantics=("arbitrary", "arbitrary"), vmem_limit_bytes=VMEM_LIMIT),
    )(act, z, ba, s0, alog, dtb, gnw)


def _attn_prompt_kernel(q_ref, k_ref, vt_ref, bias_ref, nw_ref, o_ref, *, qt):
    j = pl.program_id(1)
    tq = 2 * CHUNK
    n_kb = BAND_CHUNKS // 2 + 1
    n_pairs = ATT_HEADS // 2
    lane_head = lax.broadcasted_iota(jnp.int32, (tq, LANES), 1) // ATT_DH

    def tile(i, masked):
        m = j * qt + i
        q = q_ref[0, i * tq:(i + 1) * tq, :]
        wt = []
        for p in range(n_pairs):
            qp = q[:, p * LANES:(p + 1) * LANES]
            zero = jnp.zeros_like(qp)
            wt.append(jnp.concatenate([jnp.where(lane_head == 0, qp, zero),
                                       jnp.where(lane_head == 1, qp, zero)], axis=0))
        wt = jnp.stack(wt)
        firsts = [m - (n_kb - 1) + jb for jb in range(n_kb)]
        blks = [jnp.maximum(f, 0) for f in firsts]
        k_all = jnp.concatenate(
            [jnp.stack([k_ref[0, blk][:, p * LANES:(p + 1) * LANES] for p in range(n_pairs)]) for blk in blks],
            axis=1)
        v_all = jnp.concatenate([vt_ref[0, blk].reshape(n_pairs, LANES, tq) for blk in blks], axis=2)
        s_all = _bmm_nt(k_all, wt)
        s_parts = []
        for jb in range(n_kb):
            s = s_all[:, jb * tq:(jb + 1) * tq, :]
            if jb not in _FLAT_BIAS_BLOCKS:
                s = s + bias_ref[:, jb * tq:(jb + 1) * tq, :]
            if masked and jb < n_kb - 1:
                s = jnp.where(firsts[jb] >= 0, s, -jnp.inf)
            s_parts.append(s)
        st = jnp.concatenate(s_parts, axis=1)
        pt16 = jnp.exp2(st - jnp.max(st, axis=1, keepdims=True)).astype(BF16)
        acc = _bmm(jnp.concatenate([v_all, jnp.ones((n_pairs, 2 * SUBLANES, n_kb * tq), BF16)], axis=1), pt16)
        inv = 1.0 / acc[:, LANES:LANES + 1, :]
        ot = jnp.concatenate([acc[:, :ATT_DH, :tq] * inv[:, :, :tq],
                              acc[:, ATT_DH:LANES, tq:] * inv[:, :, tq:]], axis=1)
        oh = ot.reshape(ATT_HEADS, ATT_DH, tq)
        on = oh * lax.rsqrt(jnp.mean(oh * oh, axis=1, keepdims=True) + EPS) * nw_ref[...]
        on = on.reshape(n_pairs, LANES, tq)
        for p in range(n_pairs):
            o_ref[0, i * tq:(i + 1) * tq, p * LANES:(p + 1) * LANES] = on[p].T.astype(o_ref.dtype)

    near_start = j * qt < n_kb - 1

    @pl.when(near_start)
    def _():
        for i in range(qt):
            tile(i, True)

    @pl.when(jnp.logical_not(near_start))
    def _():
        for i in range(qt):
            tile(i, False)


def _attn_prompt(q, k, vt, bias, nw, *, qt):
    B, T, _ = q.shape
    tq = 2 * CHUNK
    n_steps = T // (qt * tq)
    k4 = k.reshape(B, T // tq, tq, ATT_W)
    row = pl.BlockSpec((1, qt * tq, ATT_W), lambda b, j: (b, j, 0))
    return pl.pallas_call(
        functools.partial(_attn_prompt_kernel, qt=qt),
        name="attn_prompt",
        grid=(B, n_steps),
        in_specs=[row,
                  pl.BlockSpec((1, T // tq, tq, ATT_W), lambda b, j: (b, 0, 0, 0)),
                  pl.BlockSpec((1, T // tq, ATT_W, tq), lambda b, j: (b, 0, 0, 0)),
                  _const_spec(bias.shape), _const_spec(nw.shape)],
        out_specs=row,
        out_shape=jax.ShapeDtypeStruct((B, T, ATT_W), BF16),
        compiler_params=pltpu.CompilerParams(
            dimension_semantics=("arbitrary", "arbitrary"), vmem_limit_bytes=VMEM_LIMIT),
    )(q, k4, vt, bias, nw)


def _attn_sample_kernel(q_ref, kc_ref, vc_ref, kn_ref, vn_ref, bc_ref, bn_ref, nw_ref, o_ref, *, n_seq):
    by_head = lambda a: jnp.stack([a[:, h * ATT_DH:(h + 1) * ATT_DH] for h in range(ATT_HEADS)])
    for s in range(n_seq):
        kc = kc_ref[s].astype(BF16)
        vc = vc_ref[s].astype(BF16)
        q = by_head(q_ref[s])
        kn = by_head(kn_ref[s])
        vn = by_head(vn_ref[s]).astype(BF16)
        s_c = _bmm(q, kc) + bc_ref[...]
        s_n = _bmm_nt(q, kn) + bn_ref[...]
        m = jnp.maximum(jnp.max(s_c, axis=-1, keepdims=True), jnp.max(s_n, axis=-1, keepdims=True))
        p_c = jnp.exp2(s_c - m)
        p_n = jnp.exp2(s_n - m)
        l = jnp.sum(p_c, axis=-1, keepdims=True) + jnp.sum(p_n, axis=-1, keepdims=True)
        o = (_bmm_nt(p_c.astype(BF16), vc) + _bmm(p_n.astype(BF16), vn)) * (1.0 / l)
        on = _rms(o) * nw_ref[...]
        o_ref[s] = jnp.concatenate([on[h] for h in range(ATT_HEADS)], axis=-1).astype(o_ref.dtype)


def _attn_sample(q, kc, vc, kn, vn, bias_c, bias_n, nw, *, n_seq):
    B, T, _ = q.shape
    new = pl.BlockSpec((n_seq, T, ATT_W), lambda b: (b, 0, 0))
    cache = pl.BlockSpec((n_seq,) + kc.shape[1:], lambda b: (b, 0, 0, 0))
    return pl.pallas_call(
        functools.partial(_attn_sample_kernel, n_seq=n_seq),
        name="attn_sample",
        grid=(B // n_seq,),
        in_specs=[new, cache, cache, new, new, _const_spec(bias_c.shape), _const_spec(bias_n.shape),
                  _const_spec(nw.shape)],
        out_specs=new,
        out_shape=jax.ShapeDtypeStruct((B, T, ATT_W), BF16),
        compiler_params=pltpu.CompilerParams(
            dimension_semantics=("arbitrary",), vmem_limit_bytes=VMEM_LIMIT),
    )(q, kc, vc, kn, vn, bias_c, bias_n, nw)


def _delayed(g, k, S, R, hist_row):
    rolled = pltpu.roll(g, k, axis=0)
    sub = lax.broadcasted_iota(jnp.int32, (SUBLANES, g.shape[1]), 0)
    parts = []
    for s in range(S):
        head = rolled[s * R:s * R + SUBLANES]
        for i in range(k):
            head = jnp.where(sub == i, hist_row(s, i), head)
        parts += [head, rolled[s * R + SUBLANES:(s + 1) * R]]
    return jnp.concatenate(parts, axis=0)


def _mix_ffn_kernel(x_ref, oa_ref, ob_ref, wo_ref, nmp_ref, nfp_ref, wgu_ref, cw_ref, cb_ref,
                    wd_ref, nfo_ref, prev_ref, y_ref, st_ref, carry,
                    *, S, R, tps, d_ff):
    t = pl.program_id(0)
    hist = FFN_CONV - 1
    mix = _dot(jnp.concatenate([oa_ref[...], ob_ref[...]], axis=1), wo_ref[...])
    x1 = x_ref[...] + _rms(mix) * nmp_ref[...]
    u2 = (_rms(x1) * nfp_ref[...]).astype(BF16)

    if tps > 1:
        @pl.when(t % tps == 0)
        def _():
            for i in range(hist):
                carry[i:i + 1, :] = prev_ref[0, i]

    gu = _dot(u2, wgu_ref[...])
    g = gu[:, :d_ff]
    up = gu[:, d_ff:]
    if tps > 1:
        history = lambda s, i: carry[i:i + 1, :]
    else:
        history = lambda s, i: prev_ref[0, i, s:s + 1, :]
    cw = cw_ref[...]
    conv = cb_ref[...] + cw[hist:hist + 1] * g
    for k in range(1, FFN_CONV):
        conv = conv + cw[hist - k:hist - k + 1] * _delayed(
            g, k, S, R, lambda s, i, k=k: history(s, hist - k + i))
    for s in range(S):
        for i in range(hist):
            row = g[(s + 1) * R - hist + i:(s + 1) * R - hist + i + 1]
            if tps > 1:
                carry[i:i + 1, :] = row
            else:
                st_ref[0, i, s:s + 1, :] = row
    if tps > 1:
        for i in range(hist):
            st_ref[0, i] = carry[i:i + 1, :]
    hid = (_gelu_tanh(conv) * up).astype(BF16)
    y_ref[...] = x1 + _rms(_dot(hid, wd_ref[...])) * nfo_ref[...]


def _mix_ffn(x, oa, ob, wo, nmp, nfp, wgu, cw, cb, wd, nfo, prev, *, S, R):
    M, D = x.shape
    d_ff = wd.shape[0]
    tm = S * R
    n_seq = prev.shape[0]
    hist = FFN_CONV - 1
    tps = M // (n_seq * R)
    assert S == 1 or tps == 1
    grouped = (n_seq // S, hist, S, d_ff)
    prev_g = prev.reshape(n_seq // S, S, hist, d_ff).transpose(0, 2, 1, 3)
    row = lambda w: pl.BlockSpec((tm, w), lambda t: (t, 0))
    st = pl.BlockSpec((1,) + grouped[1:], lambda t: (t // tps, 0, 0, 0))
    consts = [wo, nmp, nfp, wgu, cw, cb, wd, nfo]
    y, st_g = pl.pallas_call(
        functools.partial(_mix_ffn_kernel, S=S, R=R, tps=tps, d_ff=d_ff),
        name="mix_ffn",
        grid=(M // tm,),
        in_specs=[row(D), row(oa.shape[1]), row(ob.shape[1])] + [_const_spec(a.shape) for a in consts] + [st],
        out_specs=[row(D), st],
        out_shape=[jax.ShapeDtypeStruct((M, D), F32), jax.ShapeDtypeStruct(grouped, F32)],
        scratch_shapes=[pltpu.VMEM((hist, d_ff), F32)],
        compiler_params=pltpu.CompilerParams(
            dimension_semantics=("arbitrary",), vmem_limit_bytes=VMEM_LIMIT),
    )(x, oa, ob, *consts, prev_g)
    return y, st_g.transpose(0, 2, 1, 3).reshape(n_seq, hist, d_ff)


def _lane_row(vals, offset):
    return jnp.zeros((1, LANES), F32).at[0, offset:offset + vals.shape[0]].set(vals.astype(F32))


def _bias_kernel(r_ref, bp_ref, bc_ref, bn_ref, *, lc, ts):
    tq = 2 * CHUNK
    key = lax.broadcasted_iota(jnp.int32, (tq, tq), 0)
    query_chunk = lax.broadcasted_iota(jnp.int32, (tq, tq), 1) // CHUNK
    for h in range(r_ref.shape[0]):
        p, e = divmod(h, 2)
        for jb in range(r_ref.shape[1]):
            row = jnp.broadcast_to(r_ref[h, jb], (tq, 2 * tq))
            blk = pltpu.roll(row, 0, 1, stride=1, stride_axis=0)[:, :tq]
            key_in_band = jb * tq + key - query_chunk * CHUNK
            valid = (key_in_band >= 0) & (key_in_band < (BAND_CHUNKS + 1) * CHUNK)
            far = r_ref[h, 0][:, 0:1]
            bp_ref[p, jb * tq:(jb + 1) * tq, e * tq:(e + 1) * tq] = jnp.where(valid, blk - far, -jnp.inf)
            by_query = blk.T
            if (jb + 1) * tq <= lc:
                bc_ref[h, :, jb * tq:(jb + 1) * tq] = by_query[:ts]
            else:
                bn_ref[h] = by_query[:ts, :ts]


def _band_biases(table, lc, ts):
    n_heads = table.shape[0]
    tq = 2 * CHUNK
    n_kb = BAND_CHUNKS // 2 + 1
    assert lc == (n_kb - 1) * tq and ts <= tq
    starts = [WINDOW - tq * jb + MAX_REL - half * tq for jb in range(n_kb) for half in (0, 1)]
    pad_l = max(0, -min(starts))
    pad_r = max(0, max(starts) + tq - table.shape[1])
    ext = jnp.pad(table, ((0, 0), (pad_l, pad_r)), mode="edge")
    rows = jnp.concatenate([ext[:, s + pad_l:s + pad_l + tq] for s in starts], axis=1)
    rows = rows.reshape(n_heads, n_kb, 1, 2 * tq)
    return pl.pallas_call(
        functools.partial(_bias_kernel, lc=lc, ts=ts),
        name="band_bias",
        out_shape=[jax.ShapeDtypeStruct((n_heads // 2, n_kb * tq, 2 * tq), F32),
                   jax.ShapeDtypeStruct((n_heads, ts, lc), F32),
                   jax.ShapeDtypeStruct((n_heads, ts, ts), F32)],
    )(rows)


def _layer(xp, xs, cache_k, cache_v, s_delta, s_qkv, s_ffn, lw):
    (norm_mix_pre, w_in, qkv_conv_w, a_log, dt_bias, gdn_norm_w, rel_bias, attn_norm_w, w_out,
     norm_mix_post, norm_ffn_pre, w_gate_up, ffn_conv_w, ffn_conv_b, w_down, norm_ffn_post) = lw
    Bp, Tp, D = xp.shape
    Bs, Ts, _ = xs.shape
    d_ff = w_down.shape[0]

    c1 = GDN_QKV + GDN_Z
    c2 = c1 + 2 * GDN_HEADS
    wt = jnp.swapaxes(w_in, 0, 1).astype(BF16)
    w_proj = (wt[:c1], jnp.pad(wt[c1:c2], ((0, LANES - 2 * GDN_HEADS), (0, 0))), wt[c2:c2 + 2 * ATT_W])
    wkt = wt[c2 + ATT_W:c2 + 2 * ATT_W]
    wvt = wt[c2 + 2 * ATT_W:]
    nmix = norm_mix_pre.reshape(1, D)
    cw_qkv = jnp.pad(0.5 * qkv_conv_w, ((0, SUBLANES - GDN_CONV), (0, 0)))
    alog = _lane_row(a_log, GDN_HEADS)
    dtb = _lane_row(dt_bias, GDN_HEADS)
    gnw = gdn_norm_w.reshape(1, GDN_DV)
    anw = attn_norm_w.astype(F32).reshape(1, ATT_DH)
    anw_col = jnp.broadcast_to(attn_norm_w.astype(F32)[:, None], (ATT_DH, 2 * CHUNK))
    lc = cache_k.shape[1]
    bias_p, bias_c, bias_n = _band_biases(rel_bias.astype(F32) * LOG2E, lc, Ts)
    wo = w_out.astype(BF16)
    wgu = w_gate_up.astype(BF16)
    wd = w_down.astype(BF16)
    cw_ffn = jnp.pad(ffn_conv_w, ((0, SUBLANES - FFN_CONV), (0, 0)))
    cb = ffn_conv_b.reshape(1, d_ff)
    nmp = norm_mix_post.reshape(1, D)
    nfp = norm_ffn_pre.reshape(1, D)
    nfo = norm_ffn_post.reshape(1, D)

    def group(x, keep_all, gdn_prev, gdn_s0, gdn_c, gdn_cps, gdn_bs, ffn_prev, S, R, attn):
        B, T, _ = x.shape
        xi = x if not keep_all else x.reshape(1, B * T, D)
        act, qkv_state, z, ba, q, k, vt, kf, vf = _inproj(xi, nmix, w_proj, wkt, wvt, cw_qkv, gdn_prev,
                                                          keep_all=keep_all)
        rs = lambda a: a.reshape(B, T, a.shape[-1])
        act, z, ba, q, k = map(rs, (act, z, ba, q, k))
        oa, s_new = _gdn(act, z, ba, gdn_s0, alog, dtb, gnw, C=gdn_c, cps=gdn_cps, bs=gdn_bs)
        ob = attn(q, k, vt, vf)
        y, ffn_state = _mix_ffn(x.reshape(B * T, D), oa.reshape(B * T, GDN_Z), ob.reshape(B * T, ATT_W),
                                wo, nmp, nfp, wgu, cw_ffn, cb, wd, nfo, ffn_prev,
                                S=S, R=R)
        if keep_all:
            k_rows = kf.reshape(B, T, ATT_HEADS, ATT_DH)
            v_rows = vf.reshape(B, T, ATT_HEADS, ATT_DH)
        else:
            k_rows = jnp.transpose(kf, (0, 3, 1, 2))
            v_rows = jnp.transpose(vf, (0, 3, 1, 2))
        return y.reshape(B, T, D), k_rows, v_rows, s_new, qkv_state, ffn_state

    out_p = group(
        xp, False, jnp.zeros((Bp, GDN_CONV - 1, GDN_QKV), F32),
        jnp.zeros((Bp, GDN_HEADS, GDN_DK, GDN_DV), F32), CHUNK, 4, 4,
        jnp.zeros((Bp, FFN_CONV - 1, d_ff), F32), 1, ROW_TILE,
        lambda q, k, vt, vf: _attn_prompt(q, k, vt, bias_p, anw_col, qt=8))

    kc_t = jnp.transpose(cache_k, (0, 2, 3, 1))
    vc_t = jnp.transpose(cache_v, (0, 2, 3, 1))
    out_s = group(
        xs, True, s_qkv, s_delta, Ts, 1, 8, s_ffn, ROW_TILE // Ts, Ts,
        lambda q, k, vt, vf: _attn_sample(q, kc_t, vc_t, k, vf.reshape(Bs, Ts, ATT_W), bias_c, bias_n, anw,
                                          n_seq=8))
    return out_p, out_s


def kernel(x_prompt, x_sample, cache_band_k, cache_band_v, state_delta, state_qkv_conv, state_ffn_conv, norm_mix_pre, w_in, qkv_conv_w, a_log, dt_bias, gdn_norm_w, rel_bias, attn_norm_w, w_out, norm_mix_post, norm_ffn_pre, w_gate_up, ffn_conv_w, ffn_conv_b, w_down, norm_ffn_post):
    weights = (norm_mix_pre, w_in, qkv_conv_w, a_log, dt_bias, gdn_norm_w, rel_bias, attn_norm_w, w_out,
               norm_mix_post, norm_ffn_pre, w_gate_up, ffn_conv_w, ffn_conv_b, w_down, norm_ffn_post)
    depth = w_in.shape[0]
    xp, xs = x_prompt, x_sample
    outs_p, outs_s = [], []
    for l in range(depth):
        lw = tuple(w[l] for w in weights)
        op, os_ = _layer(xp, xs, cache_band_k[l], cache_band_v[l], state_delta[l], state_qkv_conv[l],
                         state_ffn_conv[l], lw)
        xp, xs = op[0], os_[0]
        outs_p.append(op[1:])
        outs_s.append(os_[1:])
    stack = lambda outs, i: jnp.stack([o[i] for o in outs], axis=0)
    return (xp, xs) + tuple(stack(outs_p, i) for i in range(5)) + tuple(stack(outs_s, i) for i in range(5))
```
